```python
import math
import jax, jax.numpy as jnp
from jax import lax
import numpy as np

D_MODEL = 1024
BATCH = 8
SEQ = 2048
DEPTH = 1

SSM_GROUP_CH = 16
SSM_WIDTH = D_MODEL // 2
SSM_GROUPS = SSM_WIDTH // SSM_GROUP_CH
SSM_STATE = 64
DT_MIN = 1e-3
DT_MAX = 1e-1
GMLP_WIDTH = D_MODEL // 2
GMLP_HEADS = 8
GMLP_HEAD_DIM = GMLP_WIDTH // GMLP_HEADS
CHUNK = 128
PROJ_WIDTH = SSM_WIDTH + 2 * GMLP_WIDTH + 2 * D_MODEL
N_EXPERTS = 32
TOP_K = 4
D_EXPERT = D_MODEL
SWIGLU_LIMIT = 7.0
SWIGLU_ALPHA = 1.702
ROUTE_BLOCK = 128
N_MOD = 6
EPS = 1e-6

kernel_name = 'hybrid_s5_gmlp_moe_block'


def rms_norm(x, g):
    xf = x.astype(jnp.float32)
    y = xf * lax.rsqrt(jnp.mean(xf * xf, axis=-1, keepdims=True) + EPS)
    return (y * g.astype(jnp.float32)).astype(x.dtype)


def modulate(h, shift, scale):
    return h * (1 + scale[:, None, :]) + shift[:, None, :]


def s5_branch(u, a_re, a_im, log_dt, b_re, b_im, c_re, c_im, d_skip, glu_w, glu_b):
    bsz, seq, _ = u.shape
    uf = u.astype(jnp.float32).reshape(bsz, seq, SSM_GROUPS, SSM_GROUP_CH)
    lam = lax.complex(a_re.astype(jnp.float32), a_im.astype(jnp.float32))
    dt = jnp.exp(log_dt.astype(jnp.float32))[:, None]
    lam_bar = jnp.exp(lam * dt)
    b = lax.complex(b_re.astype(jnp.float32), b_im.astype(jnp.float32))
    b_bar = ((lam_bar - 1) / lam)[..., None] * b
    bu = jnp.einsum('bsgh,gph->bsgp', uf, b_bar)
    decay = jnp.broadcast_to(lam_bar, bu.shape)

    def combine(left, right):
        a_l, s_l = left
        a_r, s_r = right
        return a_r * a_l, a_r * s_l + s_r

    _, states = lax.associative_scan(combine, (decay, bu), axis=1)
    cm = lax.complex(c_re.astype(jnp.float32), c_im.astype(jnp.float32))
    y = jnp.einsum('bsgp,ghp->bsgh', states, cm).real
    y = y + d_skip.astype(jnp.float32).reshape(SSM_GROUPS, SSM_GROUP_CH) * uf
    y = y.reshape(bsz, seq, SSM_WIDTH)
    z = jax.nn.gelu(y)
    out = z * jax.nn.sigmoid(z @ glu_w.astype(jnp.float32) + glu_b.astype(jnp.float32))
    return out.astype(u.dtype)


def gmlp_branch(zuv, ln_g, ln_b, ws, bs):
    bsz, seq, _ = zuv.shape
    z = jax.nn.gelu(zuv)
    u, v = jnp.split(z, 2, axis=-1)
    vf = v.astype(jnp.float32)
    mu = jnp.mean(vf, axis=-1, keepdims=True)
    var = jnp.mean(jnp.square(vf - mu), axis=-1, keepdims=True)
    vn = (vf - mu) * lax.rsqrt(var + EPS) * ln_g.astype(jnp.float32) + ln_b.astype(jnp.float32)
    vn = vn.reshape(bsz, seq // CHUNK, CHUNK, GMLP_HEADS, GMLP_HEAD_DIM)
    causal = jnp.tril(jnp.ones((CHUNK, CHUNK), jnp.float32))
    w = ws.astype(jnp.float32) * causal
    mixed = jnp.einsum('hts,bnshc->bnthc', w, vn) + bs.astype(jnp.float32).T[:, :, None]
    return u * mixed.reshape(bsz, seq, GMLP_WIDTH).astype(u.dtype)


def moe(h, router_w, router_b, w_in, b_in, w_out, b_out):
    bsz, seq, d = h.shape
    tok = h.reshape(-1, d)
    n_tok = tok.shape[0]
    logits = (tok @ router_w + router_b).astype(jnp.float32)
    top_val, top_idx = lax.top_k(logits, TOP_K)
    weights = jax.nn.softmax(top_val, axis=-1)
    n_assign = n_tok * TOP_K
    flat_e = top_idx.reshape(-1)
    flat_tok = jnp.repeat(jnp.arange(n_tok, dtype=jnp.int32), TOP_K)
    flat_w = weights.reshape(-1)
    order = jnp.argsort(flat_e)
    sorted_e = flat_e[order]
    counts = jnp.bincount(flat_e, length=N_EXPERTS)
    start = jnp.cumsum(counts) - counts
    padded = (counts + ROUTE_BLOCK - 1) // ROUTE_BLOCK * ROUTE_BLOCK
    pad_end = jnp.cumsum(padded)
    pad_start = pad_end - padded
    rank = jnp.arange(n_assign, dtype=jnp.int32) - start[sorted_e]
    dest = pad_start[sorted_e] + rank
    n_rows = (n_assign + ROUTE_BLOCK - 1) // ROUTE_BLOCK * ROUTE_BLOCK + N_EXPERTS * ROUTE_BLOCK
    n_blocks = n_rows // ROUTE_BLOCK
    row_tok = jnp.zeros((n_rows,), jnp.int32).at[dest].set(flat_tok[order])
    row_w = jnp.zeros((n_rows,), jnp.float32).at[dest].set(flat_w[order])
    blk_start = jnp.arange(n_blocks, dtype=jnp.int32) * ROUTE_BLOCK
    blk_e = jnp.minimum(jnp.searchsorted(pad_end, blk_start, side='right'), N_EXPERTS - 1)

    def expert_block(args):
        rows, e = args
        xb = tok[rows]
        gu = xb @ w_in[e] + b_in[e]
        gate, up = gu[:, :D_EXPERT], gu[:, D_EXPERT:]
        gate = jnp.minimum(gate, SWIGLU_LIMIT)
        up = jnp.clip(up, -SWIGLU_LIMIT, SWIGLU_LIMIT)
        act = (up + 1) * (gate * jax.nn.sigmoid(SWIGLU_ALPHA * gate))
        return act @ w_out[e] + b_out[e]

    ys = lax.map(expert_block, (row_tok.reshape(n_blocks, ROUTE_BLOCK), blk_e))
    ys = ys.reshape(n_rows, d)
    ys = ys * row_w[:, None].astype(ys.dtype)
    out = jnp.zeros_like(tok).at[row_tok].add(ys.astype(tok.dtype))
    return out.reshape(bsz, seq, d)


def setup_inputs(seed: int = 0) -> dict:
    key = jax.random.key(seed)
    ks = jax.random.split(key, 32)
    f32 = jnp.float32
    nrm = lambda k, shape, s: jax.random.normal(k, shape, f32) * s
    L, D, G, P, H = DEPTH, D_MODEL, SSM_GROUPS, SSM_STATE, SSM_GROUP_CH
    n_idx = jnp.arange(P, dtype=f32)
    return {
        'x': nrm(ks[0], (BATCH, SEQ, D), 1.0),
        'c': nrm(ks[1], (BATCH, D), 1.0),
        'ada_w': nrm(ks[2], (L, D, N_MOD * D), 0.5 * D ** -0.5),
        'ada_b': nrm(ks[3], (L, N_MOD * D), 0.02),
        'norm1_g': 1.0 + nrm(ks[4], (L, D), 0.02),
        'w_in': nrm(ks[5], (L, D, PROJ_WIDTH), D ** -0.5),
        'ssm_a_re': -0.5 + nrm(ks[6], (L, G, P), 0.01),
        'ssm_a_im': math.pi * n_idx + nrm(ks[7], (L, G, P), 0.01),
        'ssm_log_dt': jax.random.uniform(ks[8], (L, G), f32, math.log(DT_MIN), math.log(DT_MAX)),
        'ssm_b_re': nrm(ks[9], (L, G, P, H), (2 * H) ** -0.5),
        'ssm_b_im': nrm(ks[10], (L, G, P, H), (2 * H) ** -0.5),
        'ssm_c_re': nrm(ks[11], (L, G, H, P), P ** -0.5),
        'ssm_c_im': nrm(ks[12], (L, G, H, P), P ** -0.5),
        'ssm_d': nrm(ks[13], (L, SSM_WIDTH), 1.0),
        'ssm_glu_w': nrm(ks[14], (L, SSM_WIDTH, SSM_WIDTH), SSM_WIDTH ** -0.5),
        'ssm_glu_b': nrm(ks[15], (L, SSM_WIDTH), 0.02),
        'w_branch_a': nrm(ks[16], (L, SSM_WIDTH, D), SSM_WIDTH ** -0.5),
        'gmlp_ln_g': 1.0 + nrm(ks[17], (L, GMLP_WIDTH), 0.02),
        'gmlp_ln_b': nrm(ks[18], (L, GMLP_WIDTH), 0.02),
        'gmlp_ws': nrm(ks[19], (L, GMLP_HEADS, CHUNK, CHUNK), CHUNK ** -0.5),
        'gmlp_bs': 1.0 + nrm(ks[20], (L, GMLP_HEADS, CHUNK), 0.02),
        'w_branch_b': nrm(ks[21], (L, GMLP_WIDTH, D), GMLP_WIDTH ** -0.5),
        'w_out': nrm(ks[22], (L, D, D), D ** -0.5),
        'norm2_g': 1.0 + nrm(ks[23], (L, D), 0.02),
        'router_w': nrm(ks[24], (L, D, N_EXPERTS), D ** -0.5),
        'router_b': nrm(ks[25], (L, N_EXPERTS), 0.01),
        'moe_w_in': nrm(ks[26], (L, N_EXPERTS, D, 2 * D_EXPERT), D ** -0.5),
        'moe_b_in': nrm(ks[27], (L, N_EXPERTS, 2 * D_EXPERT), 0.02),
        'moe_w_out': nrm(ks[28], (L, N_EXPERTS, D_EXPERT, D), D_EXPERT ** -0.5),
        'moe_b_out': nrm(ks[29], (L, N_EXPERTS, D), 0.02),
        'final_g': 1.0 + nrm(ks[30], (D,), 0.02),
    }


def reference(x, c, ada_w, ada_b, norm1_g, w_in, ssm_a_re, ssm_a_im, ssm_log_dt, ssm_b_re, ssm_b_im, ssm_c_re, ssm_c_im, ssm_d, ssm_glu_w, ssm_glu_b, w_branch_a, gmlp_ln_g, gmlp_ln_b, gmlp_ws, gmlp_bs, w_branch_b, w_out, norm2_g, router_w, router_b, moe_w_in, moe_b_in, moe_w_out, moe_b_out, final_g):
    split_at = [SSM_WIDTH, SSM_WIDTH + 2 * GMLP_WIDTH, SSM_WIDTH + 2 * GMLP_WIDTH + D_MODEL]
    for layer in range(DEPTH):
        mod = jax.nn.silu(c) @ ada_w[layer] + ada_b[layer]
        shift1, scale1, gate1, shift2, scale2, gate2 = jnp.split(mod, N_MOD, axis=-1)
        h = modulate(rms_norm(x, norm1_g[layer]), shift1, scale1)
        proj = h @ w_in[layer]
        u_ssm, zuv, g_a, g_b = jnp.split(proj, split_at, axis=-1)
        y_a = s5_branch(u_ssm, ssm_a_re[layer], ssm_a_im[layer], ssm_log_dt[layer], ssm_b_re[layer], ssm_b_im[layer], ssm_c_re[layer], ssm_c_im[layer], ssm_d[layer], ssm_glu_w[layer], ssm_glu_b[layer]) @ w_branch_a[layer]
        y_b = gmlp_branch(zuv, gmlp_ln_g[layer], gmlp_ln_b[layer], gmlp_ws[layer], gmlp_bs[layer]) @ w_branch_b[layer]
        merged = jax.nn.sigmoid(g_a) * y_a + jax.nn.sigmoid(g_b) * y_b
        x = x + gate1[:, None, :] * (merged @ w_out[layer])
        h = modulate(rms_norm(x, norm2_g[layer]), shift2, scale2)
        x = x + gate2[:, None, :] * moe(h, router_w[layer], router_b[layer], moe_w_in[layer], moe_b_in[layer], moe_w_out[layer], moe_b_out[layer])
    return rms_norm(x, final_g)
```

```python
import functools
import math

import jax
import jax.numpy as jnp
from jax import lax
from jax.experimental import pallas as pl
from jax.experimental.pallas import tpu as pltpu

F32 = jnp.float32
BF16 = jnp.bfloat16

D = 1024
B = 8
S = 2048
T = B * S
SSM_W = 512
SSM_G = 32
SSM_H = 16
SSM_P = 64
N_PACK = 4
PACK_G = SSM_G // N_PACK
GM_W = 512
GM_HEADS = 8
GM_HD = 64
CHUNK = 128
N_E = 32
TOP_K = 4
D_E = 1024
LIMIT = 7.0
ALPHA = 1.702
EPS = 1e-6

TS_PROJ = 512
L_SSM = 64
R_SSM = L_SSM * B
TS_MIX = 256
TT_ROUTE = 512
TM = 256
N_ROWS = T * TOP_K + N_E * TM
N_BLOCKS = N_ROWS // TM
TD = 256
TC = 128
VMEM_LIMIT = 56 * 1024 * 1024


def _sigmoid(v):
    return 1.0 / (1.0 + jnp.exp(-v))


def _gelu(v):
    return 0.5 * v * (1.0 + jnp.tanh(math.sqrt(2.0 / math.pi) * (v + 0.044715 * v * v * v)))


def _rms(v):
    return v * lax.rsqrt(jnp.mean(v * v, axis=-1, keepdims=True) + EPS)


def _mod_kernel(c_ref, w_ref, b_ref, o_ref):
    cv = c_ref[...]
    sv = cv * _sigmoid(cv)
    o_ref[...] = jnp.dot(sv, w_ref[...], preferred_element_type=F32,
                         precision=lax.Precision.HIGHEST) + b_ref[...]


def _mod_call(c, ada_w, ada_b):
    n = ada_w.shape[1]
    return pl.pallas_call(
        _mod_kernel,
        grid=(n // D,),
        in_specs=[pl.BlockSpec((B, D), lambda j: (0, 0)),
                  pl.BlockSpec((D, D), lambda j: (0, j)),
                  pl.BlockSpec((1, D), lambda j: (0, j))],
        out_specs=pl.BlockSpec((B, D), lambda j: (0, j)),
        out_shape=jax.ShapeDtypeStruct((B, n), F32),
        name="adaln_mod",
    )(c, ada_w, ada_b.reshape(1, n))


def _proj_kernel(x_ref, g_ref, shift_ref, scale_ref, w_ref, u_ref, zuv_ref, ga_ref, gb_ref):
    h = _rms(x_ref[0]) * g_ref[...]
    h = h * (1.0 + scale_ref[0]) + shift_ref[0]
    hb = h.astype(BF16)
    u_ref[...] = jnp.dot(hb, w_ref[:, 0:SSM_W], preferred_element_type=F32)
    zuv_ref[0] = jnp.dot(hb, w_ref[:, SSM_W:SSM_W + 2 * GM_W], preferred_element_type=F32)
    ga_ref[0] = jnp.dot(hb, w_ref[:, SSM_W + 2 * GM_W:SSM_W + 2 * GM_W + D], preferred_element_type=F32)
    gb_ref[0] = jnp.dot(hb, w_ref[:, SSM_W + 2 * GM_W + D:], preferred_element_type=F32)


def _proj_call(x, norm_g, mod3, w_in_bf):
    pw = w_in_bf.shape[1]
    tok_spec = pl.BlockSpec((1, TS_PROJ, D), lambda b, s: (b, s, 0))
    return pl.pallas_call(
        _proj_kernel,
        grid=(B, S // TS_PROJ),
        in_specs=[tok_spec,
                  pl.BlockSpec((1, D), lambda b, s: (0, 0)),
                  pl.BlockSpec((1, 1, D), lambda b, s: (b, 0, 0)),
                  pl.BlockSpec((1, 1, D), lambda b, s: (b, 0, 1)),
                  pl.BlockSpec((D, pw), lambda b, s: (0, 0))],
        out_specs=[pl.BlockSpec((TS_PROJ, SSM_W), lambda b, s: (s, b)),
                   tok_spec, tok_spec, tok_spec],
        out_shape=[jax.ShapeDtypeStruct((S, B * SSM_W), F32),
                   jax.ShapeDtypeStruct((B, S, D), F32),
                   jax.ShapeDtypeStruct((B, S, D), F32),
                   jax.ShapeDtypeStruct((B, S, D), F32)],
        compiler_params=pltpu.CompilerParams(vmem_limit_bytes=VMEM_LIMIT),
        name="norm_proj",
    )(x, norm_g.reshape(1, D), mod3, mod3, w_in_bf)


def _s5_kernel(u_ref, bm_ref, cre_ref, cim_ref, are_ref, aim_ref, d_ref, gw_ref, gb_ref, wa_ref,
               o_ref, sre, sim, st_re, st_im):
    @pl.when(pl.program_id(0) == 0)
    def _():
        st_re[...] = jnp.zeros_like(st_re)
        st_im[...] = jnp.zeros_like(st_im)

    u = u_ref[...]
    ub = u.astype(BF16)
    half = PACK_G * SSM_P
    for k in range(N_PACK):
        bu = jnp.dot(ub[:, 128 * k:128 * (k + 1)], bm_ref[k], preferred_element_type=F32)
        sre[k] = bu[:, :half]
        sim[k] = bu[:, half:]

    for k in range(N_PACK):
        ar = are_ref[k]
        ai = aim_ref[k]

        def body(t, carry, k=k, ar=ar, ai=ai):
            r, m = carry
            off = pl.multiple_of(t * B, B)
            nr = ar * r - ai * m + sre[k, pl.ds(off, B), :]
            ni = ar * m + ai * r + sim[k, pl.ds(off, B), :]
            sre[k, pl.ds(off, B), :] = nr
            sim[k, pl.ds(off, B), :] = ni
            return nr, ni

        r, m = lax.fori_loop(0, L_SSM, body, (st_re[k], st_im[k]), unroll=4)
        st_re[k] = r
        st_im[k] = m

    ys = []
    for k in range(N_PACK):
        yk = jnp.dot(sre[k].astype(BF16), cre_ref[k], preferred_element_type=F32)
        yk = yk + jnp.dot(sim[k].astype(BF16), cim_ref[k], preferred_element_type=F32)
        ys.append(yk)
    y = jnp.concatenate(ys, axis=1) + d_ref[...] * u
    z = _gelu(y)
    gl = jnp.dot(z.astype(BF16), gw_ref[...], preferred_element_type=F32) + gb_ref[...]
    out = z * _sigmoid(gl)
    o_ref[...] = jnp.dot(out.astype(BF16), wa_ref[...], preferred_element_type=F32)


def _s5_call(u_tb, bm, cre, cim, are, aim, d_skip, glu_w, glu_b, w_a):
    half = PACK_G * SSM_P
    full = lambda *shape: pl.BlockSpec(shape, lambda i: (0,) * len(shape))
    return pl.pallas_call(
        _s5_kernel,
        grid=(S // L_SSM,),
        in_specs=[pl.BlockSpec((R_SSM, SSM_W), lambda i: (i, 0)),
                  full(N_PACK, 128, 2 * half),
                  full(N_PACK, half, 128),
                  full(N_PACK, half, 128),
                  full(N_PACK, B, half),
                  full(N_PACK, B, half),
                  full(1, SSM_W),
                  full(SSM_W, SSM_W),
                  full(1, SSM_W),
                  full(SSM_W, D)],
        out_specs=pl.BlockSpec((R_SSM, D), lambda i: (i, 0)),
        out_shape=jax.ShapeDtypeStruct((S * B, D), F32),
        scratch_shapes=[pltpu.VMEM((N_PACK, R_SSM, half), F32),
                        pltpu.VMEM((N_PACK, R_SSM, half), F32),
                        pltpu.VMEM((N_PACK, B, half), F32),
                        pltpu.VMEM((N_PACK, B, half), F32)],
        compiler_params=pltpu.CompilerParams(dimension_semantics=("arbitrary",),
                                             vmem_limit_bytes=VMEM_LIMIT),
        name="s5_branch",
    )(u_tb, bm, cre, cim, are, aim, d_skip, glu_w, glu_b, w_a)


def _s5_params(a_re, a_im, log_dt, b_re, b_im, c_re, c_im):
    dt = jnp.exp(log_dt)[:, None]
    mag = jnp.exp(a_re * dt)
    lr = mag * jnp.cos(a_im * dt)
    li = mag * jnp.sin(a_im * dt)
    den = a_re * a_re + a_im * a_im
    cr = ((lr - 1.0) * a_re + li * a_im) / den
    ci = (li * a_re - (lr - 1.0) * a_im) / den
    bbr = cr[..., None] * b_re - ci[..., None] * b_im
    bbi = cr[..., None] * b_im + ci[..., None] * b_re
    eye = jnp.eye(PACK_G, dtype=F32)
    half = PACK_G * SSM_P

    def pack_b(m):
        m4 = m.reshape(N_PACK, PACK_G, SSM_P, SSM_H)
        return jnp.einsum('kgph,gj->kghjp', m4, eye).reshape(N_PACK, PACK_G * SSM_H, half)

    def pack_c(m):
        m4 = m.reshape(N_PACK, PACK_G, SSM_H, SSM_P)
        return jnp.einsum('kghp,gj->kgpjh', m4, eye).reshape(N_PACK, half, PACK_G * SSM_H)

    bm = jnp.concatenate([pack_b(bbr), pack_b(bbi)], axis=-1).astype(BF16)
    cre = pack_c(c_re).astype(BF16)
    cim = (-pack_c(c_im)).astype(BF16)
    are = jnp.broadcast_to(lr.reshape(N_PACK, 1, half), (N_PACK, B, half))
    aim = jnp.broadcast_to(li.reshape(N_PACK, 1, half), (N_PACK, B, half))
    return bm, cre, cim, are, aim


def _mix_kernel(zuv_ref, ga_ref, gb_ref, ya_ref, x_ref, gate1_ref, shift2_ref, scale2_ref,
                lng_ref, lnb_ref, ws_ref, bias_ref, wbb_ref, wo_ref, n2g_ref, rw_ref, rb_ref,
                x1_ref, h2_ref, lg_ref):
    z = _gelu(zuv_ref[0])
    u = z[:, :GM_W]
    v = z[:, GM_W:]
    mu = jnp.mean(v, axis=-1, keepdims=True)
    vc = v - mu
    var = jnp.mean(vc * vc, axis=-1, keepdims=True)
    vn = vc * lax.rsqrt(var + EPS) * lng_ref[...] + lnb_ref[...]

    row = lax.broadcasted_iota(jnp.int32, (CHUNK, 2 * CHUNK), 0)
    col = lax.broadcasted_iota(jnp.int32, (CHUNK, 2 * CHUNK), 1)
    causal = (col % CHUNK) <= row
    lane = lax.broadcasted_iota(jnp.int32, (CHUNK, 2 * GM_HD), 1)
    first = lane < GM_HD
    wpairs = [jnp.where(causal, ws_ref[j], 0.0).astype(BF16) for j in range(GM_HEADS // 2)]
    chunks = []
    for n in range(TS_MIX // CHUNK):
        cols = []
        for j in range(GM_HEADS // 2):
            vp = vn[n * CHUNK:(n + 1) * CHUNK, 2 * GM_HD * j:2 * GM_HD * (j + 1)]
            rhs = jnp.concatenate([jnp.where(first, vp, 0.0), jnp.where(first, 0.0, vp)], axis=0)
            cols.append(jnp.dot(wpairs[j], rhs.astype(BF16), preferred_element_type=F32))
        chunks.append(jnp.concatenate(cols, axis=1) + bias_ref[...])
    mixed = jnp.concatenate(chunks, axis=0)
    gm = u * mixed
    yb = jnp.dot(gm.astype(BF16), wbb_ref[...], preferred_element_type=F32)
    merged = _sigmoid(ga_ref[0]) * ya_ref[...] + _sigmoid(gb_ref[0]) * yb
    o = jnp.dot(merged.astype(BF16), wo_ref[...], preferred_element_type=F32)
    x1 = x_ref[0] + gate1_ref[0] * o
    x1_ref[0] = x1
    h2 = _rms(x1) * n2g_ref[...]
    h2 = h2 * (1.0 + scale2_ref[0]) + shift2_ref[0]
    h2_ref[0] = h2
    lg_ref[0] = jnp.dot(h2, rw_ref[...], preferred_element_type=F32,
                        precision=lax.Precision.HIGHEST) + rb_ref[...]


def _mix_call(zuv, ga, gb, ya2d, x, mod3, ln_g, ln_b, ws_pairs, bias_full, wbb, wo, n2g, rw, rb):
    tok_spec = pl.BlockSpec((1, TS_MIX, D), lambda b, s: (b, s, 0))
    full = lambda *shape: pl.BlockSpec(shape, lambda b, s: (0,) * len(shape))
    mod_spec = lambda j: pl.BlockSpec((1, 1, D), lambda b, s: (b, 0, j))
    return pl.pallas_call(
        _mix_kernel,
        grid=(B, S // TS_MIX),
        in_specs=[tok_spec, tok_spec, tok_spec,
                  pl.BlockSpec((TS_MIX, D), lambda b, s: (s, b)),
                  tok_spec,
                  mod_spec(2), mod_spec(3), mod_spec(4),
                  full(1, GM_W), full(1, GM_W),
                  full(GM_HEADS // 2, CHUNK, 2 * CHUNK),
                  full(CHUNK, GM_W),
                  full(GM_W, D), full(D, D), full(1, D),
                  full(D, N_E), full(1, N_E)],
        out_specs=[tok_spec, tok_spec,
                   pl.BlockSpec((1, TS_MIX, N_E), lambda b, s: (b, s, 0))],
        out_shape=[jax.ShapeDtypeStruct((B, S, D), F32),
                   jax.ShapeDtypeStruct((B, S, D), F32),
                   jax.ShapeDtypeStruct((B, S, N_E), F32)],
        compiler_params=pltpu.CompilerParams(vmem_limit_bytes=VMEM_LIMIT),
        name="gmlp_merge_norm2",
    )(zuv, ga, gb, ya2d, x, mod3, mod3, mod3, ln_g, ln_b, ws_pairs, bias_full, wbb, wo, n2g, rw, rb)


def _route_kernel(lg_ref, dest_ref, wt_ref, cnt_ref, carry, pstart):
    p = pl.program_id(0)
    i = pl.program_id(1)
    l = lg_ref[...]
    lane = lax.broadcasted_iota(jnp.int32, l.shape, 1).astype(F32)
    sels, vals = [], []
    for _ in range(TOP_K):
        m = jnp.max(l, axis=-1, keepdims=True)
        idx = jnp.min(jnp.where(l == m, lane, float(N_E)), axis=-1, keepdims=True)
        sel = lane == idx
        sels.append(sel)
        vals.append(m)
        l = jnp.where(sel, -jnp.inf, l)
    member = sels[0].astype(F32)
    for k in range(1, TOP_K):
        member = member + sels[k].astype(F32)
    tile_cnt = jnp.sum(member, axis=0, keepdims=True)

    @pl.when((p == 0) & (i == 0))
    def _():
        carry[...] = jnp.zeros_like(carry)

    @pl.when(p == 0)
    def _():
        carry[...] += tile_cnt

    @pl.when((p == 1) & (i == 0))
    def _():
        cnt = carry[...]
        cnt_ref[...] = jnp.broadcast_to(cnt, cnt_ref.shape)
        nblk = jnp.right_shift(cnt.astype(jnp.int32) + (TM - 1), TM.bit_length() - 1)
        nb = jnp.broadcast_to(nblk.astype(F32), (8, N_E)).astype(BF16)
        r = lax.broadcasted_iota(jnp.int32, (N_E, N_E), 0)
        c = lax.broadcasted_iota(jnp.int32, (N_E, N_E), 1)
        tri = (r < c).astype(BF16)
        excl = jnp.dot(nb, tri, preferred_element_type=F32)
        pstart[...] = excl[0:1] * float(TM)
        carry[...] = jnp.zeros_like(carry)

    @pl.when(p == 1)
    def _():
        tt = l.shape[0]
        r = lax.broadcasted_iota(jnp.int32, (tt, tt), 0)
        c = lax.broadcasted_iota(jnp.int32, (tt, tt), 1)
        strict = (c < r).astype(BF16)
        before = jnp.dot(strict, member.astype(BF16), preferred_element_type=F32)
        base = pstart[...] + carry[...] + before
        denom = jnp.zeros_like(vals[0])
        exps = []
        for k in range(TOP_K):
            e = jnp.exp(vals[k] - vals[0])
            exps.append(e)
            denom = denom + e
        dest = jnp.zeros(l.shape, F32)
        wt = jnp.zeros(l.shape, F32)
        for k in range(TOP_K):
            dk = jnp.sum(jnp.where(sels[k], base, 0.0), axis=-1, keepdims=True)
            dest = jnp.where(lane == float(k), dk, dest)
            wt = jnp.where(lane == float(k), exps[k] / denom, wt)
        dest_ref[...] = dest.astype(jnp.int32)
        wt_ref[...] = wt
        carry[...] += tile_cnt


def _route_call(logits):
    nt = T // TT_ROUTE
    return pl.pallas_call(
        _route_kernel,
        grid=(2, nt),
        in_specs=[pl.BlockSpec((TT_ROUTE, N_E), lambda p, i: (i, 0))],
        out_specs=[pl.BlockSpec((TT_ROUTE, N_E), lambda p, i: (p * i, 0)),
                   pl.BlockSpec((TT_ROUTE, N_E), lambda p, i: (p * i, 0)),
                   pl.BlockSpec((8, N_E), lambda p, i: (0, 0))],
        out_shape=[jax.ShapeDtypeStruct((T, N_E), jnp.int32),
                   jax.ShapeDtypeStruct((T, N_E), F32),
                   jax.ShapeDtypeStruct((8, N_E), F32)],
        scratch_shapes=[pltpu.VMEM((1, N_E), F32), pltpu.VMEM((1, N_E), F32)],
        compiler_params=pltpu.CompilerParams(dimension_semantics=("arbitrary", "arbitrary")),
        name="route",
    )(logits)


def _dispatch_kernel(pend_ref, dest_ref, h_ref, xs_ref, dsm, zbuf, sem_s, sem_z, sem_r):
    i = pl.program_id(0)

    @pl.when(i == 0)
    def _():
        zbuf[...] = jnp.zeros_like(zbuf)
        for e in range(N_E):
            prev = pend_ref[e - 1] if e > 0 else 0
            end = pend_ref[e]

            @pl.when(end > prev)
            def _():
                start = pl.multiple_of(end - TM, TM)
                cp = pltpu.make_async_copy(zbuf, xs_ref.at[pl.ds(start, TM)], sem_z)
                cp.start()
                cp.wait()

    cp = pltpu.make_async_copy(dest_ref.at[0, 0], dsm, sem_s)
    cp.start()
    cp.wait()

    def issue(j, carry):
        for k in range(TOP_K):
            d = dsm[j * TOP_K + k]
            pltpu.make_async_copy(h_ref.at[pl.ds(j, 1)], xs_ref.at[pl.ds(d, 1)], sem_r).start()
        return carry

    lax.fori_loop(0, TD, issue, 0, unroll=8)

    def drain(j, carry):
        pltpu.make_async_copy(h_ref.at[pl.ds(0, 1)], xs_ref.at[pl.ds(0, 1)], sem_r).wait()
        return carry

    lax.fori_loop(0, TD * TOP_K, drain, 0, unroll=8)


def _dispatch_call(pad_end, dest_tiles, h2):
    grid_spec = pltpu.PrefetchScalarGridSpec(
        num_scalar_prefetch=1,
        grid=(T // TD,),
        in_specs=[pl.BlockSpec((1, 1, TD * TOP_K), lambda i, pe: (i, 0, 0)),
                  pl.BlockSpec((TD, D), lambda i, pe: (i, 0))],
        out_specs=pl.BlockSpec(memory_space=pl.ANY),
        scratch_shapes=[pltpu.SMEM((TD * TOP_K,), jnp.int32),
                        pltpu.VMEM((TM, D), F32),
                        pltpu.SemaphoreType.DMA,
                        pltpu.SemaphoreType.DMA,
                        pltpu.SemaphoreType.DMA],
    )
    return pl.pallas_call(
        _dispatch_kernel,
        grid_spec=grid_spec,
        out_shape=jax.ShapeDtypeStruct((N_ROWS, D), F32),
        compiler_params=pltpu.CompilerParams(dimension_semantics=("arbitrary",)),
        name="dispatch",
    )(pad_end, dest_tiles, h2)


def _moe_kernel(be_ref, bf_ref, nv_ref, xs_ref, wi_ref, bi_ref, wo_ref, bo_ref, ys_ref, wi_bf, wo_bf):
    i = pl.program_id(0)

    @pl.when((i < nv_ref[0]) & (bf_ref[i] == 1))
    def _():
        wi_bf[...] = wi_ref[0].astype(BF16)
        wo_bf[...] = wo_ref[0].astype(BF16)

    @pl.when(i < nv_ref[0])
    def _():
        xb = xs_ref[...].astype(BF16)
        gu = jnp.dot(xb, wi_bf[...], preferred_element_type=F32) + bi_ref[0]
        gate = jnp.minimum(gu[:, :D_E], LIMIT)
        up = jnp.clip(gu[:, D_E:], -LIMIT, LIMIT)
        act = (up + 1.0) * (gate * _sigmoid(ALPHA * gate))
        ys_ref[...] = jnp.dot(act.astype(BF16), wo_bf[...], preferred_element_type=F32) + bo_ref[0]


def _moe_call(blk_e, blk_first, n_valid, xs, w_in, b_in, w_out, b_out):
    def row_map(i, be, bf, nv):
        return (jnp.minimum(i, nv[0] - 1), 0)

    def exp_map(i, be, bf, nv):
        return (be[i], 0, 0)

    grid_spec = pltpu.PrefetchScalarGridSpec(
        num_scalar_prefetch=3,
        grid=(N_BLOCKS,),
        in_specs=[pl.BlockSpec((TM, D), row_map),
                  pl.BlockSpec((1, D, 2 * D_E), exp_map),
                  pl.BlockSpec((1, 1, 2 * D_E), exp_map),
                  pl.BlockSpec((1, D_E, D), exp_map),
                  pl.BlockSpec((1, 1, D), exp_map)],
        out_specs=pl.BlockSpec((TM, D), row_map),
        scratch_shapes=[pltpu.VMEM((D, 2 * D_E), BF16),
                        pltpu.VMEM((D_E, D), BF16)],
    )
    return pl.pallas_call(
        _moe_kernel,
        grid_spec=grid_spec,
        out_shape=jax.ShapeDtypeStruct((N_ROWS, D), F32),
        compiler_params=pltpu.CompilerParams(dimension_semantics=("arbitrary",),
                                             vmem_limit_bytes=VMEM_LIMIT),
        name="moe_experts",
    )(blk_e, blk_first, n_valid, xs, w_in, b_in, w_out, b_out)


def _combine_kernel(dest_ref, ys_ref, wt_ref, x1_ref, gate2_ref, fg_ref, o_ref, dsm, buf, sem_s, sem_r):
    cp = pltpu.make_async_copy(dest_ref.at[0, 0], dsm, sem_s)
    cp.start()
    cp.wait()

    def issue(j, carry):
        for k in range(TOP_K):
            d = dsm[j * TOP_K + k]
            pltpu.make_async_copy(ys_ref.at[pl.ds(d, 1)], buf.at[k, pl.ds(j, 1)], sem_r).start()
        return carry

    lax.fori_loop(0, TC, issue, 0, unroll=8)

    def drain(j, carry):
        pltpu.make_async_copy(ys_ref.at[pl.ds(0, 1)], buf.at[0, pl.ds(0, 1)], sem_r).wait()
        return carry

    lax.fori_loop(0, TC * TOP_K, drain, 0, unroll=8)

    wt = wt_ref[...]
    acc = wt[:, 0:1] * buf[0]
    for k in range(1, TOP_K):
        acc = acc + wt[:, k:k + 1] * buf[k]
    x2 = x1_ref[...] + gate2_ref[0] * acc
    o_ref[...] = _rms(x2) * fg_ref[...]


def _combine_call(dest_tiles, ys, wt, x1, mod3, final_g):
    per_b = S // TC
    return pl.pallas_call(
        _combine_kernel,
        grid=(T // TC,),
        in_specs=[pl.BlockSpec((1, 1, TC * TOP_K), lambda i: (i, 0, 0)),
                  pl.BlockSpec(memory_space=pl.ANY),
                  pl.BlockSpec((TC, N_E), lambda i: (i, 0)),
                  pl.BlockSpec((TC, D), lambda i: (i, 0)),
                  pl.BlockSpec((1, 1, D), lambda i: (i // per_b, 0, 5)),
                  pl.BlockSpec((1, D), lambda i: (0, 0))],
        out_specs=pl.BlockSpec((TC, D), lambda i: (i, 0)),
        out_shape=jax.ShapeDtypeStruct((T, D), F32),
        scratch_shapes=[pltpu.SMEM((TC * TOP_K,), jnp.int32),
                        pltpu.VMEM((TOP_K, TC, D), F32),
                        pltpu.SemaphoreType.DMA,
                        pltpu.SemaphoreType.DMA],
        compiler_params=pltpu.CompilerParams(dimension_semantics=("arbitrary",)),
        name="combine_norm",
    )(dest_tiles, ys, wt, x1, mod3, final_g)


def kernel(x, c, ada_w, ada_b, norm1_g, w_in, ssm_a_re, ssm_a_im, ssm_log_dt, ssm_b_re, ssm_b_im, ssm_c_re, ssm_c_im, ssm_d, ssm_glu_w, ssm_glu_b, w_branch_a, gmlp_ln_g, gmlp_ln_b, gmlp_ws, gmlp_bs, w_branch_b, w_out, norm2_g, router_w, router_b, moe_w_in, moe_b_in, moe_w_out, moe_b_out, final_g):
    depth = ada_w.shape[0]
    assert depth == 1, "the final rms_norm is fused into the combine kernel of the only layer"
    for layer in range(depth):
        mod = _mod_call(c, ada_w[layer], ada_b[layer])
        mod3 = mod.reshape(B, 1, 6 * D)

        u_tb, zuv, ga, gb = _proj_call(x, norm1_g[layer], mod3, w_in[layer].astype(BF16))

        bm, cre, cim, are, aim = _s5_params(ssm_a_re[layer], ssm_a_im[layer], ssm_log_dt[layer],
                                            ssm_b_re[layer], ssm_b_im[layer],
                                            ssm_c_re[layer], ssm_c_im[layer])
        ya = _s5_call(u_tb.reshape(S * B, SSM_W), bm, cre, cim, are, aim,
                      ssm_d[layer].reshape(1, SSM_W), ssm_glu_w[layer].astype(BF16),
                      ssm_glu_b[layer].reshape(1, SSM_W), w_branch_a[layer].astype(BF16))

        ws = gmlp_ws[layer]
        ws_pairs = jnp.concatenate([ws[0::2], ws[1::2]], axis=-1)
        bias_full = jnp.repeat(gmlp_bs[layer].T, GM_HD, axis=1)
        x1, h2, logits = _mix_call(
            zuv, ga, gb, ya.reshape(S, B * D), x, mod3,
            gmlp_ln_g[layer].reshape(1, GM_W), gmlp_ln_b[layer].reshape(1, GM_W),
            ws_pairs, bias_full, w_branch_b[layer].astype(BF16), w_out[layer].astype(BF16),
            norm2_g[layer].reshape(1, D), router_w[layer], router_b[layer].reshape(1, N_E))

        dest, wt, cnt = _route_call(logits.reshape(T, N_E))
        counts = cnt[0].astype(jnp.int32)
        nblk = (counts + TM - 1) // TM
        blk_end = jnp.cumsum(nblk)
        pad_end = (blk_end * TM).astype(jnp.int32)
        blk_ids = jnp.arange(N_BLOCKS, dtype=jnp.int32)
        blk_e = jnp.sum((blk_end[None, :] <= blk_ids[:, None]).astype(jnp.int32), axis=1)
        blk_e = jnp.minimum(blk_e, N_E - 1)
        blk_first = jnp.concatenate([jnp.ones((1,), jnp.int32),
                                     (blk_e[1:] != blk_e[:-1]).astype(jnp.int32)])
        n_valid = blk_end[-1:].astype(jnp.int32)
        dest_flat = dest[:, :TOP_K].reshape(T * TOP_K)

        xs = _dispatch_call(pad_end, dest_flat.reshape(T // TD, 1, TD * TOP_K), h2.reshape(T, D))
        ys = _moe_call(blk_e, blk_first, n_valid, xs, moe_w_in[layer],
                       moe_b_in[layer].reshape(N_E, 1, 2 * D_E), moe_w_out[layer],
                       moe_b_out[layer].reshape(N_E, 1, D))
        x = _combine_call(dest_flat.reshape(T // TC, 1, TC * TOP_K), ys, wt, x1.reshape(T, D),
                          mod3, final_g.reshape(1, D)).reshape(B, S, D)
    return x
```

```python
import functools
import math

import jax
import jax.numpy as jnp
from jax import lax
from jax.experimental import pallas as pl
from jax.experimental.pallas import tpu as pltpu

F32 = jnp.float32
BF16 = jnp.bfloat16

D = 1024
B = 8
S = 2048
T = B * S
SSM_W = 512
SSM_G = 32
SSM_H = 16
SSM_P = 64
N_PACK = 4
PACK_G = SSM_G // N_PACK
GM_W = 512
GM_HEADS = 8
GM_HD = 64
CHUNK = 128
N_E = 32
TOP_K = 4
D_E = 1024
LIMIT = 7.0
ALPHA = 1.702
EPS = 1e-6

TS_PROJ = 512
L_SSM = 64
R_SSM = L_SSM * B
TS_MIX = 256
TT = 256
NT = T // TT
RS = TOP_K * TT
TM = 256
N_ROWS = T * TOP_K + N_E * TM
N_BLOCKS = N_ROWS // TM
LANES = 128
SUB = D // LANES
PIECE = 8
VMEM_LIMIT = 56 * 1024 * 1024


def _sigmoid(v):
    return 1.0 / (1.0 + jnp.exp(-v))


def _gelu(v):
    return 0.5 * v * (1.0 + jnp.tanh(math.sqrt(2.0 / math.pi) * (v + 0.044715 * v * v * v)))


def _rms(v):
    return v * lax.rsqrt(jnp.mean(v * v, axis=-1, keepdims=True) + EPS)


def _mod_kernel(c_ref, w_ref, b_ref, o_ref):
    cv = c_ref[...]
    sv = cv * _sigmoid(cv)
    o_ref[...] = jnp.dot(sv, w_ref[...], preferred_element_type=F32,
                         precision=lax.Precision.HIGHEST) + b_ref[...]


def _mod_call(c, ada_w, ada_b):
    n = ada_w.shape[1]
    return pl.pallas_call(
        _mod_kernel,
        grid=(n // D,),
        in_specs=[pl.BlockSpec((B, D), lambda j: (0, 0)),
                  pl.BlockSpec((D, D), lambda j: (0, j)),
                  pl.BlockSpec((1, D), lambda j: (0, j))],
        out_specs=pl.BlockSpec((B, D), lambda j: (0, j)),
        out_shape=jax.ShapeDtypeStruct((B, n), F32),
        name="adaln_mod",
    )(c, ada_w, ada_b.reshape(1, n))


def _proj_kernel(x_ref, g_ref, shift_ref, scale_ref, w_ref, u_ref, zuv_ref, ga_ref, gb_ref):
    h = _rms(x_ref[0]) * g_ref[...]
    h = h * (1.0 + scale_ref[0]) + shift_ref[0]
    hb = h.astype(BF16)
    u_ref[...] = jnp.dot(hb, w_ref[:, 0:SSM_W], preferred_element_type=F32)
    zuv_ref[0] = jnp.dot(hb, w_ref[:, SSM_W:SSM_W + 2 * GM_W], preferred_element_type=F32)
    ga_ref[0] = jnp.dot(hb, w_ref[:, SSM_W + 2 * GM_W:SSM_W + 2 * GM_W + D], preferred_element_type=F32)
    gb_ref[0] = jnp.dot(hb, w_ref[:, SSM_W + 2 * GM_W + D:], preferred_element_type=F32)


def _proj_call(x, norm_g, mod3, w_in_bf):
    pw = w_in_bf.shape[1]
    tok_spec = pl.BlockSpec((1, TS_PROJ, D), lambda b, s: (b, s, 0))
    return pl.pallas_call(
        _proj_kernel,
        grid=(B, S // TS_PROJ),
        in_specs=[tok_spec,
                  pl.BlockSpec((1, D), lambda b, s: (0, 0)),
                  pl.BlockSpec((1, 1, D), lambda b, s: (b, 0, 0)),
                  pl.BlockSpec((1, 1, D), lambda b, s: (b, 0, 1)),
                  pl.BlockSpec((D, pw), lambda b, s: (0, 0))],
        out_specs=[pl.BlockSpec((TS_PROJ, SSM_W), lambda b, s: (s, b)),
                   tok_spec, tok_spec, tok_spec],
        out_shape=[jax.ShapeDtypeStruct((S, B * SSM_W), F32),
                   jax.ShapeDtypeStruct((B, S, D), F32),
                   jax.ShapeDtypeStruct((B, S, D), F32),
                   jax.ShapeDtypeStruct((B, S, D), F32)],
        compiler_params=pltpu.CompilerParams(vmem_limit_bytes=VMEM_LIMIT),
        name="norm_proj",
    )(x, norm_g.reshape(1, D), mod3, mod3, w_in_bf)


def _s5_kernel(u_ref, bm_ref, cre_ref, cim_ref, are_ref, aim_ref, d_ref, gw_ref, gb_ref, wa_ref,
               o_ref, sre, sim, st_re, st_im):
    @pl.when(pl.program_id(0) == 0)
    def _():
        st_re[...] = jnp.zeros_like(st_re)
        st_im[...] = jnp.zeros_like(st_im)

    u = u_ref[...]
    ub = u.astype(BF16)
    half = PACK_G * SSM_P
    for k in range(N_PACK):
        bu = jnp.dot(ub[:, 128 * k:128 * (k + 1)], bm_ref[k], preferred_element_type=F32)
        sre[k] = bu[:, :half]
        sim[k] = bu[:, half:]

    for k in range(N_PACK):
        ar = are_ref[k]
        ai = aim_ref[k]

        def body(t, carry, k=k, ar=ar, ai=ai):
            r, m = carry
            off = pl.multiple_of(t * B, B)
            nr = ar * r - ai * m + sre[k, pl.ds(off, B), :]
            ni = ar * m + ai * r + sim[k, pl.ds(off, B), :]
            sre[k, pl.ds(off, B), :] = nr
            sim[k, pl.ds(off, B), :] = ni
            return nr, ni

        r, m = lax.fori_loop(0, L_SSM, body, (st_re[k], st_im[k]), unroll=4)
        st_re[k] = r
        st_im[k] = m

    ys = []
    for k in range(N_PACK):
        yk = jnp.dot(sre[k].astype(BF16), cre_ref[k], preferred_element_type=F32)
        yk = yk + jnp.dot(sim[k].astype(BF16), cim_ref[k], preferred_element_type=F32)
        ys.append(yk)
    y = jnp.concatenate(ys, axis=1) + d_ref[...] * u
    z = _gelu(y)
    gl = jnp.dot(z.astype(BF16), gw_ref[...], preferred_element_type=F32) + gb_ref[...]
    out = z * _sigmoid(gl)
    o_ref[...] = jnp.dot(out.astype(BF16), wa_ref[...], preferred_element_type=F32)


def _s5_call(u_tb, bm, cre, cim, are, aim, d_skip, glu_w, glu_b, w_a):
    half = PACK_G * SSM_P
    full = lambda *shape: pl.BlockSpec(shape, lambda i: (0,) * len(shape))
    return pl.pallas_call(
        _s5_kernel,
        grid=(S // L_SSM,),
        in_specs=[pl.BlockSpec((R_SSM, SSM_W), lambda i: (i, 0)),
                  full(N_PACK, 128, 2 * half),
                  full(N_PACK, half, 128),
                  full(N_PACK, half, 128),
                  full(N_PACK, B, half),
                  full(N_PACK, B, half),
                  full(1, SSM_W),
                  full(SSM_W, SSM_W),
                  full(1, SSM_W),
                  full(SSM_W, D)],
        out_specs=pl.BlockSpec((R_SSM, D), lambda i: (i, 0)),
        out_shape=jax.ShapeDtypeStruct((S * B, D), F32),
        scratch_shapes=[pltpu.VMEM((N_PACK, R_SSM, half), F32),
                        pltpu.VMEM((N_PACK, R_SSM, half), F32),
                        pltpu.VMEM((N_PACK, B, half), F32),
                        pltpu.VMEM((N_PACK, B, half), F32)],
        compiler_params=pltpu.CompilerParams(dimension_semantics=("arbitrary",),
                                             vmem_limit_bytes=VMEM_LIMIT),
        name="s5_branch",
    )(u_tb, bm, cre, cim, are, aim, d_skip, glu_w, glu_b, w_a)


def _s5_params(a_re, a_im, log_dt, b_re, b_im, c_re, c_im):
    dt = jnp.exp(log_dt)[:, None]
    mag = jnp.exp(a_re * dt)
    lr = mag * jnp.cos(a_im * dt)
    li = mag * jnp.sin(a_im * dt)
    den = a_re * a_re + a_im * a_im
    cr = ((lr - 1.0) * a_re + li * a_im) / den
    ci = (li * a_re - (lr - 1.0) * a_im) / den
    bbr = cr[..., None] * b_re - ci[..., None] * b_im
    bbi = cr[..., None] * b_im + ci[..., None] * b_re
    eye = jnp.eye(PACK_G, dtype=F32)
    half = PACK_G * SSM_P

    def pack_b(m):
        m4 = m.reshape(N_PACK, PACK_G, SSM_P, SSM_H)
        return jnp.einsum('kgph,gj->kghjp', m4, eye).reshape(N_PACK, PACK_G * SSM_H, half)

    def pack_c(m):
        m4 = m.reshape(N_PACK, PACK_G, SSM_H, SSM_P)
        return jnp.einsum('kghp,gj->kgpjh', m4, eye).reshape(N_PACK, half, PACK_G * SSM_H)

    bm = jnp.concatenate([pack_b(bbr), pack_b(bbi)], axis=-1).astype(BF16)
    cre = pack_c(c_re).astype(BF16)
    cim = (-pack_c(c_im)).astype(BF16)
    are = jnp.broadcast_to(lr.reshape(N_PACK, 1, half), (N_PACK, B, half))
    aim = jnp.broadcast_to(li.reshape(N_PACK, 1, half), (N_PACK, B, half))
    return bm, cre, cim, are, aim


def _mix_kernel(zuv_ref, ga_ref, gb_ref, ya_ref, x_ref, gate1_ref, shift2_ref, scale2_ref,
                lng_ref, lnb_ref, ws_ref, bias_ref, wbb_ref, wo_ref, n2g_ref, rw_ref, rb_ref,
                x1_ref, h2_ref, lg_ref):
    z = _gelu(zuv_ref[0])
    u = z[:, :GM_W]
    v = z[:, GM_W:]
    mu = jnp.mean(v, axis=-1, keepdims=True)
    vc = v - mu
    var = jnp.mean(vc * vc, axis=-1, keepdims=True)
    vn = vc * lax.rsqrt(var + EPS) * lng_ref[...] + lnb_ref[...]

    row = lax.broadcasted_iota(jnp.int32, (CHUNK, 2 * CHUNK), 0)
    col = lax.broadcasted_iota(jnp.int32, (CHUNK, 2 * CHUNK), 1)
    causal = (col % CHUNK) <= row
    lane = lax.broadcasted_iota(jnp.int32, (CHUNK, 2 * GM_HD), 1)
    first = lane < GM_HD
    wpairs = [jnp.where(causal, ws_ref[j], 0.0).astype(BF16) for j in range(GM_HEADS // 2)]
    chunks = []
    for n in range(TS_MIX // CHUNK):
        cols = []
        for j in range(GM_HEADS // 2):
            vp = vn[n * CHUNK:(n + 1) * CHUNK, 2 * GM_HD * j:2 * GM_HD * (j + 1)]
            rhs = jnp.concatenate([jnp.where(first, vp, 0.0), jnp.where(first, 0.0, vp)], axis=0)
            cols.append(jnp.dot(wpairs[j], rhs.astype(BF16), preferred_element_type=F32))
        chunks.append(jnp.concatenate(cols, axis=1) + bias_ref[...])
    mixed = jnp.concatenate(chunks, axis=0)
    gm = u * mixed
    yb = jnp.dot(gm.astype(BF16), wbb_ref[...], preferred_element_type=F32)
    merged = _sigmoid(ga_ref[0]) * ya_ref[...] + _sigmoid(gb_ref[0]) * yb
    o = jnp.dot(merged.astype(BF16), wo_ref[...], preferred_element_type=F32)
    x1 = x_ref[0] + gate1_ref[0] * o
    x1_ref[0] = x1
    h2 = _rms(x1) * n2g_ref[...]
    h2 = h2 * (1.0 + scale2_ref[0]) + shift2_ref[0]
    h2_ref[0] = h2.astype(BF16)
    lg_ref[0] = jnp.dot(h2, rw_ref[...], preferred_element_type=F32,
                        precision=lax.Precision.HIGHEST) + rb_ref[...]


def _mix_call(zuv, ga, gb, ya2d, x, mod3, ln_g, ln_b, ws_pairs, bias_full, wbb, wo, n2g, rw, rb):
    tok_spec = pl.BlockSpec((1, TS_MIX, D), lambda b, s: (b, s, 0))
    full = lambda *shape: pl.BlockSpec(shape, lambda b, s: (0,) * len(shape))
    mod_spec = lambda j: pl.BlockSpec((1, 1, D), lambda b, s: (b, 0, j))
    return pl.pallas_call(
        _mix_kernel,
        grid=(B, S // TS_MIX),
        in_specs=[tok_spec, tok_spec, tok_spec,
                  pl.BlockSpec((TS_MIX, D), lambda b, s: (s, b)),
                  tok_spec,
                  mod_spec(2), mod_spec(3), mod_spec(4),
                  full(1, GM_W), full(1, GM_W),
                  full(GM_HEADS // 2, CHUNK, 2 * CHUNK),
                  full(CHUNK, GM_W),
                  full(GM_W, D), full(D, D), full(1, D),
                  full(D, N_E), full(1, N_E)],
        out_specs=[tok_spec, tok_spec,
                   pl.BlockSpec((1, TS_MIX, N_E), lambda b, s: (b, s, 0))],
        out_shape=[jax.ShapeDtypeStruct((B, S, D), F32),
                   jax.ShapeDtypeStruct((B, S, D), BF16),
                   jax.ShapeDtypeStruct((B, S, N_E), F32)],
        compiler_params=pltpu.CompilerParams(vmem_limit_bytes=VMEM_LIMIT),
        name="gmlp_merge_norm2",
    )(zuv, ga, gb, ya2d, x, mod3, mod3, mod3, ln_g, ln_b, ws_pairs, bias_full, wbb, wo, n2g, rw, rb)


def _route_kernel(lg_ref, pc_ref, pt_ref, cntt_ref, offt_ref, segt_ref, cnt_ref, carry, pstart):
    p = pl.program_id(0)
    i = pl.program_id(1)
    l = lg_ref[...]
    lane = lax.broadcasted_iota(jnp.int32, l.shape, 1).astype(F32)
    sels, vals = [], []
    for _ in range(TOP_K):
        m = jnp.max(l, axis=-1, keepdims=True)
        idx = jnp.min(jnp.where(l == m, lane, float(N_E)), axis=-1, keepdims=True)
        sel = lane == idx
        sels.append(sel)
        vals.append(m)
        l = jnp.where(sel, -jnp.inf, l)
    member = sels[0].astype(F32)
    for k in range(1, TOP_K):
        member = member + sels[k].astype(F32)
    tile_cnt = jnp.sum(member, axis=0, keepdims=True)

    @pl.when((p == 0) & (i == 0))
    def _():
        carry[...] = jnp.zeros_like(carry)

    @pl.when(p == 0)
    def _():
        carry[...] += tile_cnt

    @pl.when((p == 1) & (i == 0))
    def _():
        cnt = carry[...]
        cnt_ref[...] = jnp.broadcast_to(cnt, cnt_ref.shape)
        nblk = jnp.right_shift(cnt.astype(jnp.int32) + (TM - 1), TM.bit_length() - 1)
        nb = jnp.broadcast_to(nblk.astype(F32), (8, N_E)).astype(BF16)
        r = lax.broadcasted_iota(jnp.int32, (N_E, N_E), 0)
        c = lax.broadcasted_iota(jnp.int32, (N_E, N_E), 1)
        tri = (r < c).astype(BF16)
        excl = jnp.dot(nb, tri, preferred_element_type=F32)
        pstart[...] = excl[0:1] * float(TM)
        carry[...] = jnp.zeros_like(carry)

    @pl.when(p == 1)
    def _():
        r = lax.broadcasted_iota(jnp.int32, (N_E, N_E), 0)
        c = lax.broadcasted_iota(jnp.int32, (N_E, N_E), 1)
        tri = (r < c).astype(BF16)
        tcb = jnp.broadcast_to(tile_cnt, (8, N_E)).astype(BF16)
        seg = jnp.dot(tcb, tri, preferred_element_type=F32)[0:1]
        r = lax.broadcasted_iota(jnp.int32, (TT, TT), 0)
        c = lax.broadcasted_iota(jnp.int32, (TT, TT), 1)
        strict = (c < r).astype(BF16)
        rank = jnp.dot(strict, member.astype(BF16), preferred_element_type=F32)
        posb = seg + rank
        denom = jnp.zeros_like(vals[0])
        exps = []
        for k in range(TOP_K):
            e = jnp.exp(vals[k] - vals[0])
            exps.append(e)
            denom = denom + e
        pc = jnp.zeros(l.shape, F32)
        for k in range(TOP_K):
            pk = jnp.sum(jnp.where(sels[k], posb, 0.0), axis=-1, keepdims=True)
            pc = jnp.where(lane == float(k), pk, pc)
            pc = jnp.where(lane == float(TOP_K + k), exps[k] / denom, pc)
        pc_ref[...] = pc
        r8 = lax.broadcasted_iota(jnp.int32, (8, N_E), 0)
        c8 = lax.broadcasted_iota(jnp.int32, (8, N_E), 1)
        eye = (r8 == c8).astype(F32)
        pt_ref[...] = lax.dot_general(eye, pc, (((1,), (1,)), ((), ())),
                                      preferred_element_type=F32, precision=lax.Precision.HIGHEST)
        cntt_ref[0] = jnp.broadcast_to(tile_cnt, (8, N_E)).astype(jnp.int32)
        offt_ref[0] = jnp.broadcast_to(pstart[...] + carry[...], (8, N_E)).astype(jnp.int32)
        segt_ref[0] = jnp.broadcast_to(seg, (8, N_E)).astype(jnp.int32)
        carry[...] += tile_cnt


def _route_call(logits):
    tbl_spec = pl.BlockSpec((1, 8, N_E), lambda p, i: (p * i, 0, 0))
    tbl_shape = jax.ShapeDtypeStruct((NT, 8, N_E), jnp.int32)
    return pl.pallas_call(
        _route_kernel,
        grid=(2, NT),
        in_specs=[pl.BlockSpec((TT, N_E), lambda p, i: (i, 0))],
        out_specs=[pl.BlockSpec((TT, N_E), lambda p, i: (p * i, 0)),
                   pl.BlockSpec((8, TT), lambda p, i: (0, p * i)),
                   tbl_spec, tbl_spec, tbl_spec,
                   pl.BlockSpec((8, N_E), lambda p, i: (0, 0))],
        out_shape=[jax.ShapeDtypeStruct((T, N_E), F32),
                   jax.ShapeDtypeStruct((8, T), F32),
                   tbl_shape, tbl_shape, tbl_shape,
                   jax.ShapeDtypeStruct((8, N_E), F32)],
        scratch_shapes=[pltpu.VMEM((1, N_E), F32), pltpu.VMEM((1, N_E), F32)],
        compiler_params=pltpu.CompilerParams(dimension_semantics=("arbitrary", "arbitrary")),
        name="route",
    )(logits)


def _copy_run(src_ref, dst_ref, src_row, dst_row, n, sem, start):
    def desc(rows, done):
        s = pl.multiple_of((src_row + done) * SUB, SUB)
        d = pl.multiple_of((dst_row + done) * SUB, SUB)
        return pltpu.make_async_copy(src_ref.at[pl.ds(s, rows * SUB)],
                                     dst_ref.at[pl.ds(d, rows * SUB)], sem)

    def go(cp):
        if start:
            cp.start()
        else:
            cp.wait()

    nbulk = jnp.right_shift(n, PIECE.bit_length() - 1)

    def bulk(pi, carry):
        go(desc(PIECE, pi * PIECE))
        return carry

    lax.fori_loop(0, nbulk, bulk, 0)
    done = nbulk * PIECE
    for rows in (4, 2, 1):
        @pl.when((n & rows) != 0)
        def _(rows=rows, done=done):
            go(desc(rows, done))
        done = done + (n & rows)


def _dispatch_kernel(cnt_ref, off_ref, seg_ref, pend_ref, h_ref, pt_ref, xs_ref, sbuf, zbuf, sem_z, sem_r):
    j = pl.program_id(0)

    @pl.when(j == 0)
    def _():
        zbuf[...] = jnp.zeros_like(zbuf)
        for e in range(N_E):
            prev = pend_ref[e - 1] if e > 0 else 0
            end = pend_ref[e]

            @pl.when(end > prev)
            def _():
                first = pl.multiple_of((end - TM) * SUB, TM * SUB)
                cp = pltpu.make_async_copy(zbuf, xs_ref.at[pl.ds(first, TM * SUB)], sem_z)
                cp.start()
                cp.wait()

    rows = lax.broadcasted_iota(jnp.int32, (RS, TT), 0)
    pos = pt_ref[...].astype(jnp.int32)
    hit = rows == pos[0:1, :]
    for k in range(1, TOP_K):
        hit = hit | (rows == pos[k:k + 1, :])
    pm = jnp.where(hit, 1.0, 0.0).astype(BF16)
    srt = jnp.dot(pm, h_ref[...], preferred_element_type=F32)
    for c in range(SUB):
        sbuf[pl.ds(c, RS, stride=SUB), :] = srt[:, c * LANES:(c + 1) * LANES]

    base = j * N_E
    for start in (True, False):
        def per_expert(e, carry, start=start):
            _copy_run(sbuf, xs_ref, seg_ref[base + e], off_ref[base + e], cnt_ref[base + e], sem_r, start)
            return carry

        lax.fori_loop(0, N_E, per_expert, 0)


def _dispatch_call(cnt_t, off_t, seg_t, pad_end, h2, pos_t):
    grid_spec = pltpu.PrefetchScalarGridSpec(
        num_scalar_prefetch=4,
        grid=(NT,),
        in_specs=[pl.BlockSpec((TT, D), lambda j, *_: (j, 0)),
                  pl.BlockSpec((8, TT), lambda j, *_: (0, j))],
        out_specs=pl.BlockSpec(memory_space=pl.ANY),
        scratch_shapes=[pltpu.VMEM((RS * SUB, LANES), F32),
                        pltpu.VMEM((TM * SUB, LANES), F32),
                        pltpu.SemaphoreType.DMA,
                        pltpu.SemaphoreType.DMA],
    )
    return pl.pallas_call(
        _dispatch_kernel,
        grid_spec=grid_spec,
        out_shape=jax.ShapeDtypeStruct((N_ROWS * SUB, LANES), F32),
        compiler_params=pltpu.CompilerParams(dimension_semantics=("arbitrary",),
                                             vmem_limit_bytes=VMEM_LIMIT),
        name="dispatch",
    )(cnt_t, off_t, seg_t, pad_end, h2, pos_t)


def _moe_kernel(be_ref, bf_ref, nv_ref, xs_ref, wi_ref, bi_ref, wo_ref, bo_ref, ys_ref, wi_bf, wo_bf):
    i = pl.program_id(0)

    @pl.when((i < nv_ref[0]) & (bf_ref[i] == 1))
    def _():
        wi_bf[...] = wi_ref[0].astype(BF16)
        wo_bf[...] = wo_ref[0].astype(BF16)

    @pl.when(i < nv_ref[0])
    def _():
        xb = jnp.concatenate([xs_ref[pl.ds(c, TM, stride=SUB), :] for c in range(SUB)],
                             axis=1).astype(BF16)
        gu = jnp.dot(xb, wi_bf[...], preferred_element_type=F32) + bi_ref[0]
        gate = jnp.minimum(gu[:, :D_E], LIMIT)
        up = jnp.clip(gu[:, D_E:], -LIMIT, LIMIT)
        act = (up + 1.0) * (gate * _sigmoid(ALPHA * gate))
        y = jnp.dot(act.astype(BF16), wo_bf[...], preferred_element_type=F32) + bo_ref[0]
        for c in range(SUB):
            ys_ref[pl.ds(c, TM, stride=SUB), :] = y[:, c * LANES:(c + 1) * LANES]


def _moe_call(blk_e, blk_first, n_valid, xs, w_in, b_in, w_out, b_out):
    def row_map(i, be, bf, nv):
        return (jnp.minimum(i, nv[0] - 1), 0)

    def exp_map(i, be, bf, nv):
        return (be[i], 0, 0)

    grid_spec = pltpu.PrefetchScalarGridSpec(
        num_scalar_prefetch=3,
        grid=(N_BLOCKS,),
        in_specs=[pl.BlockSpec((TM * SUB, LANES), row_map),
                  pl.BlockSpec((1, D, 2 * D_E), exp_map),
                  pl.BlockSpec((1, 1, 2 * D_E), exp_map),
                  pl.BlockSpec((1, D_E, D), exp_map),
                  pl.BlockSpec((1, 1, D), exp_map)],
        out_specs=pl.BlockSpec((TM * SUB, LANES), row_map),
        scratch_shapes=[pltpu.VMEM((D, 2 * D_E), BF16),
                        pltpu.VMEM((D_E, D), BF16)],
    )
    return pl.pallas_call(
        _moe_kernel,
        grid_spec=grid_spec,
        out_shape=jax.ShapeDtypeStruct((N_ROWS * SUB, LANES), F32),
        compiler_params=pltpu.CompilerParams(dimension_semantics=("arbitrary",),
                                             vmem_limit_bytes=VMEM_LIMIT),
        name="moe_experts",
    )(blk_e, blk_first, n_valid, xs, w_in, b_in, w_out, b_out)


def _combine_kernel(cnt_ref, off_ref, seg_ref, ys_ref, pc_ref, x1_ref, gate2_ref, fg_ref, o_ref, buf, sem_r):
    base = pl.program_id(0) * N_E
    for start in (True, False):
        def per_expert(e, carry, start=start):
            _copy_run(ys_ref, buf, off_ref[base + e], seg_ref[base + e], cnt_ref[base + e], sem_r, start)
            return carry

        lax.fori_loop(0, N_E, per_expert, 0)

    yt = jnp.concatenate([buf[pl.ds(c, RS, stride=SUB), :] for c in range(SUB)],
                         axis=1).astype(BF16)
    pc = pc_ref[...]
    pos = pc.astype(jnp.int32)
    col = lax.broadcasted_iota(jnp.int32, (TT, RS), 1)
    wm = jnp.zeros((TT, RS), F32)
    for k in range(TOP_K):
        wm = jnp.where(col == pos[:, k:k + 1], pc[:, TOP_K + k:TOP_K + k + 1], wm)
    acc = jnp.dot(wm.astype(BF16), yt, preferred_element_type=F32)
    x2 = x1_ref[...] + gate2_ref[0] * acc
    o_ref[...] = _rms(x2) * fg_ref[...]


def _combine_call(cnt_t, off_t, seg_t, ys, pos_c, x1, mod3, final_g):
    per_b = S // TT
    grid_spec = pltpu.PrefetchScalarGridSpec(
        num_scalar_prefetch=3,
        grid=(NT,),
        in_specs=[pl.BlockSpec(memory_space=pl.ANY),
                  pl.BlockSpec((TT, N_E), lambda j, *_: (j, 0)),
                  pl.BlockSpec((TT, D), lambda j, *_: (j, 0)),
                  pl.BlockSpec((1, 1, D), lambda j, *_: (j // per_b, 0, 5)),
                  pl.BlockSpec((1, D), lambda j, *_: (0, 0))],
        out_specs=pl.BlockSpec((TT, D), lambda j, *_: (j, 0)),
        scratch_shapes=[pltpu.VMEM((RS * SUB, LANES), F32),
                        pltpu.SemaphoreType.DMA],
    )
    return pl.pallas_call(
        _combine_kernel,
        grid_spec=grid_spec,
        out_shape=jax.ShapeDtypeStruct((T, D), F32),
        compiler_params=pltpu.CompilerParams(dimension_semantics=("arbitrary",),
                                             vmem_limit_bytes=VMEM_LIMIT),
        name="combine_norm",
    )(cnt_t, off_t, seg_t, ys, pos_c, x1, mod3, final_g)


def kernel(x, c, ada_w, ada_b, norm1_g, w_in, ssm_a_re, ssm_a_im, ssm_log_dt, ssm_b_re, ssm_b_im, ssm_c_re, ssm_c_im, ssm_d, ssm_glu_w, ssm_glu_b, w_branch_a, gmlp_ln_g, gmlp_ln_b, gmlp_ws, gmlp_bs, w_branch_b, w_out, norm2_g, router_w, router_b, moe_w_in, moe_b_in, moe_w_out, moe_b_out, final_g):
    depth = ada_w.shape[0]
    assert depth == 1, "the final rms_norm is fused into the combine kernel of the only layer"
    for layer in range(depth):
        mod = _mod_call(c, ada_w[layer], ada_b[layer])
        mod3 = mod.reshape(B, 1, 6 * D)

        u_tb, zuv, ga, gb = _proj_call(x, norm1_g[layer], mod3, w_in[layer].astype(BF16))

        bm, cre, cim, are, aim = _s5_params(ssm_a_re[layer], ssm_a_im[layer], ssm_log_dt[layer],
                                            ssm_b_re[layer], ssm_b_im[layer],
                                            ssm_c_re[layer], ssm_c_im[layer])
        ya = _s5_call(u_tb.reshape(S * B, SSM_W), bm, cre, cim, are, aim,
                      ssm_d[layer].reshape(1, SSM_W), ssm_glu_w[layer].astype(BF16),
                      ssm_glu_b[layer].reshape(1, SSM_W), w_branch_a[layer].astype(BF16))

        ws = gmlp_ws[layer]
        ws_pairs = jnp.concatenate([ws[0::2], ws[1::2]], axis=-1)
        bias_full = jnp.repeat(gmlp_bs[layer].T, GM_HD, axis=1)
        x1, h2, logits = _mix_call(
            zuv, ga, gb, ya.reshape(S, B * D), x, mod3,
            gmlp_ln_g[layer].reshape(1, GM_W), gmlp_ln_b[layer].reshape(1, GM_W),
            ws_pairs, bias_full, w_branch_b[layer].astype(BF16), w_out[layer].astype(BF16),
            norm2_g[layer].reshape(1, D), router_w[layer], router_b[layer].reshape(1, N_E))

        pos_c, pos_t, cnt_t, off_t, seg_t, cnt = _route_call(logits.reshape(T, N_E))
        counts = cnt[0].astype(jnp.int32)
        nblk = (counts + TM - 1) // TM
        blk_end = jnp.cumsum(nblk)
        pad_end = (blk_end * TM).astype(jnp.int32)
        blk_ids = jnp.arange(N_BLOCKS, dtype=jnp.int32)
        blk_e = jnp.sum((blk_end[None, :] <= blk_ids[:, None]).astype(jnp.int32), axis=1)
        blk_e = jnp.minimum(blk_e, N_E - 1)
        blk_first = jnp.concatenate([jnp.ones((1,), jnp.int32),
                                     (blk_e[1:] != blk_e[:-1]).astype(jnp.int32)])
        n_valid = blk_end[-1:].astype(jnp.int32)
        cnt_t, off_t, seg_t = (t[:, 0, :].reshape(NT * N_E) for t in (cnt_t, off_t, seg_t))

        xs = _dispatch_call(cnt_t, off_t, seg_t, pad_end, h2.reshape(T, D), pos_t)
        ys = _moe_call(blk_e, blk_first, n_valid, xs, moe_w_in[layer],
                       moe_b_in[layer].reshape(N_E, 1, 2 * D_E), moe_w_out[layer],
                       moe_b_out[layer].reshape(N_E, 1, D))
        x = _combine_call(cnt_t, off_t, seg_t, ys, pos_c, x1.reshape(T, D),
                          mod3, final_g.reshape(1, D)).reshape(B, S, D)
    return x
```

```python
import functools
import math

import jax
import jax.numpy as jnp
from jax import lax
from jax.experimental import pallas as pl
from jax.experimental.pallas import tpu as pltpu

F32 = jnp.float32
BF16 = jnp.bfloat16

D = 1024
B = 8
S = 2048
T = B * S
SSM_W = 512
SSM_G = 32
SSM_H = 16
SSM_P = 64
N_PACK = 4
PACK_G = SSM_G // N_PACK
GM_W = 512
GM_HEADS = 8
GM_HD = 64
CHUNK = 128
N_E = 32
TOP_K = 4
D_E = 1024
LIMIT = 7.0
ALPHA = 1.702
EPS = 1e-6

TS_PROJ = 512
L_SSM = 64
R_SSM = L_SSM * B
TS_MIX = 256
TT = 256
NT = T // TT
RS = TOP_K * TT
TM = 256
N_ROWS = T * TOP_K + N_E * TM
N_BLOCKS = N_ROWS // TM
LANES = 128
SUB = D // LANES
PIECE = 8
VMEM_LIMIT = 56 * 1024 * 1024


def _sigmoid(v):
    return 1.0 / (1.0 + jnp.exp(-v))


def _gelu(v):
    return 0.5 * v * (1.0 + jnp.tanh(math.sqrt(2.0 / math.pi) * (v + 0.044715 * v * v * v)))


def _rms(v):
    return v * lax.rsqrt(jnp.mean(v * v, axis=-1, keepdims=True) + EPS)


def _mod_kernel(c_ref, w_ref, b_ref, o_ref):
    cv = c_ref[...]
    sv = cv * _sigmoid(cv)
    o_ref[...] = jnp.dot(sv, w_ref[...], preferred_element_type=F32,
                         precision=lax.Precision.HIGHEST) + b_ref[...]


def _mod_call(c, ada_w, ada_b):
    n = ada_w.shape[1]
    return pl.pallas_call(
        _mod_kernel,
        grid=(n // D,),
        in_specs=[pl.BlockSpec((B, D), lambda j: (0, 0)),
                  pl.BlockSpec((D, D), lambda j: (0, j)),
                  pl.BlockSpec((1, D), lambda j: (0, j))],
        out_specs=pl.BlockSpec((B, D), lambda j: (0, j)),
        out_shape=jax.ShapeDtypeStruct((B, n), F32),
        name="adaln_mod",
    )(c, ada_w, ada_b.reshape(1, n))


def _proj_kernel(x_ref, g_ref, shift_ref, scale_ref, w_ref, u_ref, zuv_ref, ga_ref, gb_ref):
    h = _rms(x_ref[0]) * g_ref[...]
    h = h * (1.0 + scale_ref[0]) + shift_ref[0]
    hb = h.astype(BF16)
    u_ref[0] = jnp.dot(hb, w_ref[:, 0:SSM_W], preferred_element_type=F32)
    zuv_ref[0] = jnp.dot(hb, w_ref[:, SSM_W:SSM_W + 2 * GM_W], preferred_element_type=F32)
    ga_ref[0] = jnp.dot(hb, w_ref[:, SSM_W + 2 * GM_W:SSM_W + 2 * GM_W + D], preferred_element_type=F32)
    gb_ref[0] = jnp.dot(hb, w_ref[:, SSM_W + 2 * GM_W + D:], preferred_element_type=F32)


def _proj_call(x, norm_g, mod3, w_in_bf):
    pw = w_in_bf.shape[1]
    tok_spec = pl.BlockSpec((1, TS_PROJ, D), lambda b, s: (b, s, 0))
    return pl.pallas_call(
        _proj_kernel,
        grid=(B, S // TS_PROJ),
        in_specs=[tok_spec,
                  pl.BlockSpec((1, D), lambda b, s: (0, 0)),
                  pl.BlockSpec((1, 1, D), lambda b, s: (b, 0, 0)),
                  pl.BlockSpec((1, 1, D), lambda b, s: (b, 0, 1)),
                  pl.BlockSpec((D, pw), lambda b, s: (0, 0))],
        out_specs=[pl.BlockSpec((1, TS_PROJ, SSM_W), lambda b, s: (b, s, 0)),
                   tok_spec, tok_spec, tok_spec],
        out_shape=[jax.ShapeDtypeStruct((B, S, SSM_W), F32),
                   jax.ShapeDtypeStruct((B, S, D), F32),
                   jax.ShapeDtypeStruct((B, S, D), F32),
                   jax.ShapeDtypeStruct((B, S, D), F32)],
        compiler_params=pltpu.CompilerParams(vmem_limit_bytes=VMEM_LIMIT),
        name="norm_proj",
    )(x, norm_g.reshape(1, D), mod3, mod3, w_in_bf)


def _s5_kernel(u_ref, bm_ref, cre_ref, cim_ref, are_ref, aim_ref, d_ref, gw_ref, gb_ref, wa_ref,
               o_ref, usc, ysc, sre, sim, st_re, st_im):
    @pl.when(pl.program_id(0) == 0)
    def _():
        st_re[...] = jnp.zeros_like(st_re)
        st_im[...] = jnp.zeros_like(st_im)

    nslab = SSM_W // LANES
    for b in range(B):
        for c in range(nslab):
            usc[c, pl.ds(b, L_SSM, stride=B), :] = u_ref[b, :, c * LANES:(c + 1) * LANES]
    u = jnp.concatenate([usc[c] for c in range(nslab)], axis=1)
    ub = u.astype(BF16)
    half = PACK_G * SSM_P
    for k in range(N_PACK):
        bu = jnp.dot(ub[:, 128 * k:128 * (k + 1)], bm_ref[k], preferred_element_type=F32)
        sre[k] = bu[:, :half]
        sim[k] = bu[:, half:]

    for k in range(N_PACK):
        ar = are_ref[k]
        ai = aim_ref[k]

        def body(t, carry, k=k, ar=ar, ai=ai):
            r, m = carry
            off = pl.multiple_of(t * B, B)
            nr = ar * r - ai * m + sre[k, pl.ds(off, B), :]
            ni = ar * m + ai * r + sim[k, pl.ds(off, B), :]
            sre[k, pl.ds(off, B), :] = nr
            sim[k, pl.ds(off, B), :] = ni
            return nr, ni

        r, m = lax.fori_loop(0, L_SSM, body, (st_re[k], st_im[k]), unroll=4)
        st_re[k] = r
        st_im[k] = m

    ys = []
    for k in range(N_PACK):
        yk = jnp.dot(sre[k].astype(BF16), cre_ref[k], preferred_element_type=F32)
        yk = yk + jnp.dot(sim[k].astype(BF16), cim_ref[k], preferred_element_type=F32)
        ys.append(yk)
    for c in range(nslab):
        uc = usc[c]
        ysc[c] = ys[c] + d_ref[:, c * LANES:(c + 1) * LANES] * uc
    y = jnp.concatenate(
        [jnp.concatenate([ysc[c, pl.ds(b, L_SSM, stride=B), :] for c in range(nslab)], axis=1)
         for b in range(B)], axis=0)
    z = _gelu(y)
    gl = jnp.dot(z.astype(BF16), gw_ref[...], preferred_element_type=F32) + gb_ref[...]
    out = z * _sigmoid(gl)
    o = jnp.dot(out.astype(BF16), wa_ref[...], preferred_element_type=F32)
    for b in range(B):
        o_ref[b] = o[b * L_SSM:(b + 1) * L_SSM]


def _s5_call(u, bm, cre, cim, are, aim, d_skip, glu_w, glu_b, w_a):
    half = PACK_G * SSM_P
    full = lambda *shape: pl.BlockSpec(shape, lambda i: (0,) * len(shape))
    return pl.pallas_call(
        _s5_kernel,
        grid=(S // L_SSM,),
        in_specs=[pl.BlockSpec((B, L_SSM, SSM_W), lambda i: (0, i, 0)),
                  full(N_PACK, 128, 2 * half),
                  full(N_PACK, half, 128),
                  full(N_PACK, half, 128),
                  full(N_PACK, B, half),
                  full(N_PACK, B, half),
                  full(1, SSM_W),
                  full(SSM_W, SSM_W),
                  full(1, SSM_W),
                  full(SSM_W, D)],
        out_specs=pl.BlockSpec((B, L_SSM, D), lambda i: (0, i, 0)),
        out_shape=jax.ShapeDtypeStruct((B, S, D), F32),
        scratch_shapes=[pltpu.VMEM((SSM_W // LANES, R_SSM, LANES), F32),
                        pltpu.VMEM((SSM_W // LANES, R_SSM, LANES), F32),
                        pltpu.VMEM((N_PACK, R_SSM, half), F32),
                        pltpu.VMEM((N_PACK, R_SSM, half), F32),
                        pltpu.VMEM((N_PACK, B, half), F32),
                        pltpu.VMEM((N_PACK, B, half), F32)],
        compiler_params=pltpu.CompilerParams(dimension_semantics=("arbitrary",),
                                             vmem_limit_bytes=VMEM_LIMIT),
        name="s5_branch",
    )(u, bm, cre, cim, are, aim, d_skip, glu_w, glu_b, w_a)


def _s5_params(a_re, a_im, log_dt, b_re, b_im, c_re, c_im):
    dt = jnp.exp(log_dt)[:, None]
    mag = jnp.exp(a_re * dt)
    lr = mag * jnp.cos(a_im * dt)
    li = mag * jnp.sin(a_im * dt)
    den = a_re * a_re + a_im * a_im
    cr = ((lr - 1.0) * a_re + li * a_im) / den
    ci = (li * a_re - (lr - 1.0) * a_im) / den
    bbr = cr[..., None] * b_re - ci[..., None] * b_im
    bbi = cr[..., None] * b_im + ci[..., None] * b_re
    eye = jnp.eye(PACK_G, dtype=F32)
    half = PACK_G * SSM_P

    def pack_b(m):
        m4 = m.reshape(N_PACK, PACK_G, SSM_P, SSM_H)
        return jnp.einsum('kgph,gj->kghjp', m4, eye).reshape(N_PACK, PACK_G * SSM_H, half)

    def pack_c(m):
        m4 = m.reshape(N_PACK, PACK_G, SSM_H, SSM_P)
        return jnp.einsum('kghp,gj->kgpjh', m4, eye).reshape(N_PACK, half, PACK_G * SSM_H)

    bm = jnp.concatenate([pack_b(bbr), pack_b(bbi)], axis=-1).astype(BF16)
    cre = pack_c(c_re).astype(BF16)
    cim = (-pack_c(c_im)).astype(BF16)
    are = jnp.broadcast_to(lr.reshape(N_PACK, 1, half), (N_PACK, B, half))
    aim = jnp.broadcast_to(li.reshape(N_PACK, 1, half), (N_PACK, B, half))
    return bm, cre, cim, are, aim


def _mix_kernel(zuv_ref, ga_ref, gb_ref, ya_ref, x_ref, gate1_ref, shift2_ref, scale2_ref,
                lng_ref, lnb_ref, ws_ref, bias_ref, wbb_ref, wo_ref, n2g_ref, rw_ref, rb_ref,
                x1_ref, h2_ref, lg_ref):
    z = _gelu(zuv_ref[0])
    u = z[:, :GM_W]
    v = z[:, GM_W:]
    mu = jnp.mean(v, axis=-1, keepdims=True)
    vc = v - mu
    var = jnp.mean(vc * vc, axis=-1, keepdims=True)
    vn = vc * lax.rsqrt(var + EPS) * lng_ref[...] + lnb_ref[...]

    row = lax.broadcasted_iota(jnp.int32, (CHUNK, 2 * CHUNK), 0)
    col = lax.broadcasted_iota(jnp.int32, (CHUNK, 2 * CHUNK), 1)
    causal = (col % CHUNK) <= row
    lane = lax.broadcasted_iota(jnp.int32, (CHUNK, 2 * GM_HD), 1)
    first = lane < GM_HD
    wpairs = [jnp.where(causal, ws_ref[j], 0.0).astype(BF16) for j in range(GM_HEADS // 2)]
    chunks = []
    for n in range(TS_MIX // CHUNK):
        cols = []
        for j in range(GM_HEADS // 2):
            vp = vn[n * CHUNK:(n + 1) * CHUNK, 2 * GM_HD * j:2 * GM_HD * (j + 1)]
            rhs = jnp.concatenate([jnp.where(first, vp, 0.0), jnp.where(first, 0.0, vp)], axis=0)
            cols.append(jnp.dot(wpairs[j], rhs.astype(BF16), preferred_element_type=F32))
        chunks.append(jnp.concatenate(cols, axis=1) + bias_ref[...])
    mixed = jnp.concatenate(chunks, axis=0)
    gm = u * mixed
    yb = jnp.dot(gm.astype(BF16), wbb_ref[...], preferred_element_type=F32)
    merged = _sigmoid(ga_ref[0]) * ya_ref[0] + _sigmoid(gb_ref[0]) * yb
    o = jnp.dot(merged.astype(BF16), wo_ref[...], preferred_element_type=F32)
    x1 = x_ref[0] + gate1_ref[0] * o
    x1_ref[0] = x1
    h2 = _rms(x1) * n2g_ref[...]
    h2 = h2 * (1.0 + scale2_ref[0]) + shift2_ref[0]
    hi = h2.astype(BF16)
    h2_ref[0] = hi
    lo = (h2 - hi.astype(F32)).astype(BF16)
    lg_ref[0] = jnp.dot(h2, rw_ref[...], preferred_element_type=F32,
                        precision=lax.Precision.HIGHEST) + rb_ref[...]


def _mix_call(zuv, ga, gb, ya2d, x, mod3, ln_g, ln_b, ws_pairs, bias_full, wbb, wo, n2g, rw, rb):
    tok_spec = pl.BlockSpec((1, TS_MIX, D), lambda b, s: (b, s, 0))
    full = lambda *shape: pl.BlockSpec(shape, lambda b, s: (0,) * len(shape))
    mod_spec = lambda j: pl.BlockSpec((1, 1, D), lambda b, s: (b, 0, j))
    return pl.pallas_call(
        _mix_kernel,
        grid=(B, S // TS_MIX),
        in_specs=[tok_spec, tok_spec, tok_spec, tok_spec, tok_spec,
                  mod_spec(2), mod_spec(3), mod_spec(4),
                  full(1, GM_W), full(1, GM_W),
                  full(GM_HEADS // 2, CHUNK, 2 * CHUNK),
                  full(CHUNK, GM_W),
                  full(GM_W, D), full(D, D), full(1, D),
                  full(D, N_E), full(1, N_E)],
        out_specs=[tok_spec, tok_spec,
                   pl.BlockSpec((1, TS_MIX, N_E), lambda b, s: (b, s, 0))],
        out_shape=[jax.ShapeDtypeStruct((B, S, D), F32),
                   jax.ShapeDtypeStruct((B, S, D), BF16),
                   jax.ShapeDtypeStruct((B, S, N_E), F32)],
        compiler_params=pltpu.CompilerParams(vmem_limit_bytes=VMEM_LIMIT),
        name="gmlp_merge_norm2",
    )(zuv, ga, gb, ya2d, x, mod3, mod3, mod3, ln_g, ln_b, ws_pairs, bias_full, wbb, wo, n2g, rw, rb)


def _route_kernel(lg_ref, pc_ref, pt_ref, cntt_ref, pret_ref, segt_ref, cnt_ref, carry):
    i = pl.program_id(0)
    l = lg_ref[...]
    lane = lax.broadcasted_iota(jnp.int32, l.shape, 1).astype(F32)
    sels, vals = [], []
    for _ in range(TOP_K):
        m = jnp.max(l, axis=-1, keepdims=True)
        idx = jnp.min(jnp.where(l == m, lane, float(N_E)), axis=-1, keepdims=True)
        sel = lane == idx
        sels.append(sel)
        vals.append(m)
        l = jnp.where(sel, -jnp.inf, l)
    member = sels[0].astype(F32)
    for k in range(1, TOP_K):
        member = member + sels[k].astype(F32)
    tile_cnt = jnp.sum(member, axis=0, keepdims=True)

    @pl.when(i == 0)
    def _():
        carry[...] = jnp.zeros_like(carry)

    r = lax.broadcasted_iota(jnp.int32, (N_E, N_E), 0)
    c = lax.broadcasted_iota(jnp.int32, (N_E, N_E), 1)
    tri = (r < c).astype(BF16)
    tcb = jnp.broadcast_to(tile_cnt, (8, N_E)).astype(BF16)
    seg = jnp.dot(tcb, tri, preferred_element_type=F32)[0:1]
    r = lax.broadcasted_iota(jnp.int32, (TT, TT), 0)
    c = lax.broadcasted_iota(jnp.int32, (TT, TT), 1)
    strict = (c < r).astype(BF16)
    rank = jnp.dot(strict, member.astype(BF16), preferred_element_type=F32)
    posb = seg + rank
    denom = jnp.zeros_like(vals[0])
    exps = []
    for k in range(TOP_K):
        e = jnp.exp(vals[k] - vals[0])
        exps.append(e)
        denom = denom + e
    pc = jnp.zeros(l.shape, F32)
    for k in range(TOP_K):
        pk = jnp.sum(jnp.where(sels[k], posb, 0.0), axis=-1, keepdims=True)
        pc = jnp.where(lane == float(k), pk, pc)
        pc = jnp.where(lane == float(TOP_K + k), exps[k] / denom, pc)
    pc_ref[...] = pc
    r8 = lax.broadcasted_iota(jnp.int32, (8, N_E), 0)
    c8 = lax.broadcasted_iota(jnp.int32, (8, N_E), 1)
    eye = (r8 == c8).astype(F32)
    pt_ref[...] = lax.dot_general(eye, pc, (((1,), (1,)), ((), ())),
                                  preferred_element_type=F32, precision=lax.Precision.HIGHEST)
    cntt_ref[0] = jnp.broadcast_to(tile_cnt, (8, N_E)).astype(jnp.int32)
    pret_ref[0] = jnp.broadcast_to(carry[...], (8, N_E)).astype(jnp.int32)
    segt_ref[0] = jnp.broadcast_to(seg, (8, N_E)).astype(jnp.int32)
    carry[...] += tile_cnt
    cnt_ref[...] = jnp.broadcast_to(carry[...], cnt_ref.shape)


def _route_call(logits):
    tbl_spec = pl.BlockSpec((1, 8, N_E), lambda i: (i, 0, 0))
    tbl_shape = jax.ShapeDtypeStruct((NT, 8, N_E), jnp.int32)
    return pl.pallas_call(
        _route_kernel,
        grid=(NT,),
        in_specs=[pl.BlockSpec((TT, N_E), lambda i: (i, 0))],
        out_specs=[pl.BlockSpec((TT, N_E), lambda i: (i, 0)),
                   pl.BlockSpec((8, TT), lambda i: (0, i)),
                   tbl_spec, tbl_spec, tbl_spec,
                   pl.BlockSpec((8, N_E), lambda i: (0, 0))],
        out_shape=[jax.ShapeDtypeStruct((T, N_E), F32),
                   jax.ShapeDtypeStruct((8, T), F32),
                   tbl_shape, tbl_shape, tbl_shape,
                   jax.ShapeDtypeStruct((8, N_E), F32)],
        scratch_shapes=[pltpu.VMEM((1, N_E), F32)],
        compiler_params=pltpu.CompilerParams(dimension_semantics=("arbitrary",)),
        name="route",
    )(logits)


def _copy_run(src_ref, dst_ref, src_row, dst_row, n, sem, start):
    def desc(rows, done):
        s = pl.multiple_of((src_row + done) * SUB, SUB)
        d = pl.multiple_of((dst_row + done) * SUB, SUB)
        return pltpu.make_async_copy(src_ref.at[pl.ds(s, rows * SUB)],
                                     dst_ref.at[pl.ds(d, rows * SUB)], sem)

    def go(cp):
        if start:
            cp.start()
        else:
            cp.wait()

    nbulk = jnp.right_shift(n, PIECE.bit_length() - 1)

    def bulk(pi, carry):
        go(desc(PIECE, pi * PIECE))
        return carry

    lax.fori_loop(0, nbulk, bulk, 0)
    done = nbulk * PIECE
    for rows in (4, 2, 1):
        @pl.when((n & rows) != 0)
        def _(rows=rows, done=done):
            go(desc(rows, done))
        done = done + (n & rows)


def _dispatch_kernel(cnt_ref, off_ref, seg_ref, pend_ref, h_ref, pt_ref, xs_ref, sbuf, zbuf, sem_z, sem_r):
    j = pl.program_id(0)

    @pl.when(j == 0)
    def _():
        zbuf[...] = jnp.zeros_like(zbuf)
        for e in range(N_E):
            prev = pend_ref[e - 1] if e > 0 else 0
            end = pend_ref[e]

            @pl.when(end > prev)
            def _():
                first = pl.multiple_of((end - TM) * SUB, TM * SUB)
                cp = pltpu.make_async_copy(zbuf, xs_ref.at[pl.ds(first, TM * SUB)], sem_z)
                cp.start()
                cp.wait()

    rows = lax.broadcasted_iota(jnp.int32, (RS, TT), 0)
    pos = pt_ref[...].astype(jnp.int32)
    hit = rows == pos[0:1, :]
    for k in range(1, TOP_K):
        hit = hit | (rows == pos[k:k + 1, :])
    pm = jnp.where(hit, 1.0, 0.0).astype(BF16)
    srt = jnp.dot(pm, h_ref[...], preferred_element_type=F32)
    for c in range(SUB):
        sbuf[pl.ds(c, RS, stride=SUB), :] = srt[:, c * LANES:(c + 1) * LANES]

    base = j * N_E
    for start in (True, False):
        def per_expert(e, carry, start=start):
            _copy_run(sbuf, xs_ref, seg_ref[base + e], off_ref[base + e], cnt_ref[base + e], sem_r, start)
            return carry

        lax.fori_loop(0, N_E, per_expert, 0)


def _dispatch_call(cnt_t, off_t, seg_t, pad_end, h2, pos_t):
    grid_spec = pltpu.PrefetchScalarGridSpec(
        num_scalar_prefetch=4,
        grid=(NT,),
        in_specs=[pl.BlockSpec((TT, D), lambda j, *_: (j, 0)),
                  pl.BlockSpec((8, TT), lambda j, *_: (0, j))],
        out_specs=pl.BlockSpec(memory_space=pl.ANY),
        scratch_shapes=[pltpu.VMEM((RS * SUB, LANES), F32),
                        pltpu.VMEM((TM * SUB, LANES), F32),
                        pltpu.SemaphoreType.DMA,
                        pltpu.SemaphoreType.DMA],
    )
    return pl.pallas_call(
        _dispatch_kernel,
        grid_spec=grid_spec,
        out_shape=jax.ShapeDtypeStruct((N_ROWS * SUB, LANES), F32),
        compiler_params=pltpu.CompilerParams(dimension_semantics=("arbitrary",),
                                             vmem_limit_bytes=VMEM_LIMIT),
        name="dispatch",
    )(cnt_t, off_t, seg_t, pad_end, h2, pos_t)


def _moe_kernel(be_ref, bf_ref, nx_ref, nv_ref, xs_ref, wi_hbm, bi_ref, wo_hbm, bo_ref, ys_ref,
                wi_f32, wo_f32, wi_bf, wo_bf, sem_i, sem_o):
    i = pl.program_id(0)

    def fetch(e):
        return (pltpu.make_async_copy(wi_hbm.at[e], wi_f32, sem_i),
                pltpu.make_async_copy(wo_hbm.at[e], wo_f32, sem_o))

    @pl.when(i == 0)
    def _():
        for cp in fetch(be_ref[0]):
            cp.start()

    @pl.when((i < nv_ref[0]) & (bf_ref[i] == 1))
    def _():
        for cp in fetch(be_ref[i]):
            cp.wait()
        wi_bf[...] = wi_f32[...].astype(BF16)
        wo_bf[...] = wo_f32[...].astype(BF16)

        @pl.when(nx_ref[i] >= 0)
        def _():
            for cp in fetch(nx_ref[i]):
                cp.start()

    @pl.when(i < nv_ref[0])
    def _():
        xb = jnp.concatenate([xs_ref[pl.ds(c, TM, stride=SUB), :] for c in range(SUB)],
                             axis=1).astype(BF16)
        gu = jnp.dot(xb, wi_bf[...], preferred_element_type=F32) + bi_ref[0]
        gate = jnp.minimum(gu[:, :D_E], LIMIT)
        up = jnp.clip(gu[:, D_E:], -LIMIT, LIMIT)
        act = (up + 1.0) * (gate * _sigmoid(ALPHA * gate))
        y = jnp.dot(act.astype(BF16), wo_bf[...], preferred_element_type=F32) + bo_ref[0]
        for c in range(SUB):
            ys_ref[pl.ds(c, TM, stride=SUB), :] = y[:, c * LANES:(c + 1) * LANES]


def _moe_call(blk_e, blk_first, blk_next, n_valid, xs, w_in, b_in, w_out, b_out):
    def row_map(i, be, bf, nx, nv):
        return (jnp.maximum(jnp.minimum(i, nv[0] - 1), 0), 0)

    def exp_map(i, be, bf, nx, nv):
        return (be[i], 0, 0)

    grid_spec = pltpu.PrefetchScalarGridSpec(
        num_scalar_prefetch=4,
        grid=(N_BLOCKS,),
        in_specs=[pl.BlockSpec((TM * SUB, LANES), row_map),
                  pl.BlockSpec(memory_space=pl.ANY),
                  pl.BlockSpec((1, 1, 2 * D_E), exp_map),
                  pl.BlockSpec(memory_space=pl.ANY),
                  pl.BlockSpec((1, 1, D), exp_map)],
        out_specs=pl.BlockSpec((TM * SUB, LANES), row_map),
        scratch_shapes=[pltpu.VMEM((D, 2 * D_E), F32),
                        pltpu.VMEM((D_E, D), F32),
                        pltpu.VMEM((D, 2 * D_E), BF16),
                        pltpu.VMEM((D_E, D), BF16),
                        pltpu.SemaphoreType.DMA,
                        pltpu.SemaphoreType.DMA],
    )
    return pl.pallas_call(
        _moe_kernel,
        grid_spec=grid_spec,
        out_shape=jax.ShapeDtypeStruct((N_ROWS * SUB, LANES), F32),
        compiler_params=pltpu.CompilerParams(dimension_semantics=("arbitrary",),
                                             vmem_limit_bytes=VMEM_LIMIT),
        name="moe_experts",
    )(blk_e, blk_first, blk_next, n_valid, xs, w_in, b_in, w_out, b_out)


def _combine_kernel(cnt_ref, off_ref, seg_ref, ys_ref, pc_ref, x1_ref, gate2_ref, fg_ref, o_ref, buf, sem_r):
    base = pl.program_id(0) * N_E
    for start in (True, False):
        def per_expert(e, carry, start=start):
            _copy_run(ys_ref, buf, off_ref[base + e], seg_ref[base + e], cnt_ref[base + e], sem_r, start)
            return carry

        lax.fori_loop(0, N_E, per_expert, 0)

    yt = jnp.concatenate([buf[pl.ds(c, RS, stride=SUB), :] for c in range(SUB)],
                         axis=1).astype(BF16)
    pc = pc_ref[...]
    pos = pc.astype(jnp.int32)
    col = lax.broadcasted_iota(jnp.int32, (TT, RS), 1)
    wm = jnp.zeros((TT, RS), F32)
    for k in range(TOP_K):
        wm = jnp.where(col == pos[:, k:k + 1], pc[:, TOP_K + k:TOP_K + k + 1], wm)
    acc = jnp.dot(wm.astype(BF16), yt, preferred_element_type=F32)
    x2 = x1_ref[...] + gate2_ref[0] * acc
    o_ref[...] = _rms(x2) * fg_ref[...]


def _combine_call(cnt_t, off_t, seg_t, ys, pos_c, x1, mod3, final_g):
    per_b = S // TT
    grid_spec = pltpu.PrefetchScalarGridSpec(
        num_scalar_prefetch=3,
        grid=(NT,),
        in_specs=[pl.BlockSpec(memory_space=pl.ANY),
                  pl.BlockSpec((TT, N_E), lambda j, *_: (j, 0)),
                  pl.BlockSpec((TT, D), lambda j, *_: (j, 0)),
                  pl.BlockSpec((1, 1, D), lambda j, *_: (j // per_b, 0, 5)),
                  pl.BlockSpec((1, D), lambda j, *_: (0, 0))],
        out_specs=pl.BlockSpec((TT, D), lambda j, *_: (j, 0)),
        scratch_shapes=[pltpu.VMEM((RS * SUB, LANES), F32),
                        pltpu.SemaphoreType.DMA],
    )
    return pl.pallas_call(
        _combine_kernel,
        grid_spec=grid_spec,
        out_shape=jax.ShapeDtypeStruct((T, D), F32),
        compiler_params=pltpu.CompilerParams(dimension_semantics=("arbitrary",),
                                             vmem_limit_bytes=VMEM_LIMIT),
        name="combine_norm",
    )(cnt_t, off_t, seg_t, ys, pos_c, x1, mod3, final_g)


def kernel(x, c, ada_w, ada_b, norm1_g, w_in, ssm_a_re, ssm_a_im, ssm_log_dt, ssm_b_re, ssm_b_im, ssm_c_re, ssm_c_im, ssm_d, ssm_glu_w, ssm_glu_b, w_branch_a, gmlp_ln_g, gmlp_ln_b, gmlp_ws, gmlp_bs, w_branch_b, w_out, norm2_g, router_w, router_b, moe_w_in, moe_b_in, moe_w_out, moe_b_out, final_g):
    depth = ada_w.shape[0]
    assert depth == 1, "the final rms_norm is fused into the combine kernel of the only layer"
    for layer in range(depth):
        mod = _mod_call(c, ada_w[layer], ada_b[layer])
        mod3 = mod.reshape(B, 1, 6 * D)

        u, zuv, ga, gb = _proj_call(x, norm1_g[layer], mod3, w_in[layer].astype(BF16))

        bm, cre, cim, are, aim = _s5_params(ssm_a_re[layer], ssm_a_im[layer], ssm_log_dt[layer],
                                            ssm_b_re[layer], ssm_b_im[layer],
                                            ssm_c_re[layer], ssm_c_im[layer])
        ya = _s5_call(u, bm, cre, cim, are, aim,
                      ssm_d[layer].reshape(1, SSM_W), ssm_glu_w[layer].astype(BF16),
                      ssm_glu_b[layer].reshape(1, SSM_W), w_branch_a[layer].astype(BF16))

        ws = gmlp_ws[layer]
        ws_pairs = jnp.concatenate([ws[0::2], ws[1::2]], axis=-1)
        bias_full = jnp.repeat(gmlp_bs[layer].T, GM_HD, axis=1)
        rw = router_w[layer]
        rw_hi = rw.astype(BF16)
        rw_lo = (rw - rw_hi.astype(F32)).astype(BF16)
        x1, h2, logits = _mix_call(
            zuv, ga, gb, ya, x, mod3,
            gmlp_ln_g[layer].reshape(1, GM_W), gmlp_ln_b[layer].reshape(1, GM_W),
            ws_pairs, bias_full, w_branch_b[layer].astype(BF16), w_out[layer].astype(BF16),
            norm2_g[layer].reshape(1, D), rw,
            router_b[layer].reshape(1, N_E))

        pos_c, pos_t, cnt_t, pre_t, seg_t, cnt = _route_call(logits.reshape(T, N_E))
        counts = cnt[0].astype(jnp.int32)
        nblk = (counts + TM - 1) // TM
        blk_end = jnp.cumsum(nblk)
        pad_end = (blk_end * TM).astype(jnp.int32)
        experts = jnp.arange(N_E, dtype=jnp.int32)
        blk_ids = jnp.arange(N_BLOCKS, dtype=jnp.int32)
        blk_e = jnp.sum((blk_end[None, :] <= blk_ids[:, None]).astype(jnp.int32), axis=1)
        blk_e = jnp.minimum(blk_e, N_E - 1)
        blk_first = jnp.concatenate([jnp.ones((1,), jnp.int32),
                                     (blk_e[1:] != blk_e[:-1]).astype(jnp.int32)])
        later = (experts[None, :] > experts[:, None]) & (nblk[None, :] > 0)
        next_e = jnp.min(jnp.where(later, experts[None, :], N_E), axis=1)
        next_e = jnp.where(next_e == N_E, -1, next_e)
        blk_next = jnp.sum(jnp.where(blk_e[:, None] == experts[None, :], next_e[None, :], 0), axis=1)
        n_valid = blk_end[-1:].astype(jnp.int32)
        pad_start = pad_end - nblk * TM
        off_t = pre_t[:, 0, :] + pad_start[None, :]
        cnt_t, off_t, seg_t = (t.reshape(NT * N_E) for t in (cnt_t[:, 0, :], off_t, seg_t[:, 0, :]))

        xs = _dispatch_call(cnt_t, off_t, seg_t, pad_end, h2.reshape(T, D), pos_t)
        ys = _moe_call(blk_e, blk_first, blk_next.astype(jnp.int32), n_valid, xs, moe_w_in[layer],
                       moe_b_in[layer].reshape(N_E, 1, 2 * D_E), moe_w_out[layer],
                       moe_b_out[layer].reshape(N_E, 1, D))
        x = _combine_call(cnt_t, off_t, seg_t, ys, pos_c, x1.reshape(T, D),
                          mod3, final_g.reshape(1, D)).reshape(B, S, D)
    return x
```

```python
import functools
import math

import jax
import jax.numpy as jnp
from jax import lax
from jax.experimental import pallas as pl
from jax.experimental.pallas import tpu as pltpu

F32 = jnp.float32
BF16 = jnp.bfloat16

D = 1024
B = 8
S = 2048
T = B * S
SSM_W = 512
SSM_G = 32
SSM_H = 16
SSM_P = 64
N_PACK = 4
PACK_G = SSM_G // N_PACK
GM_W = 512
GM_HEADS = 8
GM_HD = 64
CHUNK = 128
N_E = 32
TOP_K = 4
D_E = 1024
LIMIT = 7.0
ALPHA = 1.702
EPS = 1e-6

TS_PROJ = 512
L_SSM = 64
R_SSM = L_SSM * B
TS_MIX = 256
TT = 256
NT = T // TT
RS = TOP_K * TT
TM = 256
N_ROWS = T * TOP_K + N_E * TM
N_BLOCKS = N_ROWS // TM
LANES = 128
SUB = D // LANES
PIECE = 8
VMEM_LIMIT = 56 * 1024 * 1024


def _sigmoid(v):
    return 1.0 / (1.0 + jnp.exp(-v))


def _gelu(v):
    return 0.5 * v * (1.0 + jnp.tanh(math.sqrt(2.0 / math.pi) * (v + 0.044715 * v * v * v)))


def _rms(v):
    return v * lax.rsqrt(jnp.mean(v * v, axis=-1, keepdims=True) + EPS)


def _mod_kernel(c_ref, w_ref, b_ref, o_ref):
    cv = c_ref[...]
    sv = cv * _sigmoid(cv)
    o_ref[...] = jnp.dot(sv, w_ref[...], preferred_element_type=F32,
                         precision=lax.Precision.HIGHEST) + b_ref[...]


def _mod_call(c, ada_w, ada_b):
    n = ada_w.shape[1]
    return pl.pallas_call(
        _mod_kernel,
        grid=(n // D,),
        in_specs=[pl.BlockSpec((B, D), lambda j: (0, 0)),
                  pl.BlockSpec((D, D), lambda j: (0, j)),
                  pl.BlockSpec((1, D), lambda j: (0, j))],
        out_specs=pl.BlockSpec((B, D), lambda j: (0, j)),
        out_shape=jax.ShapeDtypeStruct((B, n), F32),
        name="adaln_mod",
    )(c, ada_w, ada_b.reshape(1, n))


def _proj_kernel(x_ref, g_ref, shift_ref, scale_ref, w_ref, u_ref, zuv_ref, ga_ref, gb_ref):
    h = _rms(x_ref[0]) * g_ref[...]
    h = h * (1.0 + scale_ref[0]) + shift_ref[0]
    hb = h.astype(BF16)
    u_ref[0] = jnp.dot(hb, w_ref[:, 0:SSM_W], preferred_element_type=F32)
    zuv_ref[0] = jnp.dot(hb, w_ref[:, SSM_W:SSM_W + 2 * GM_W], preferred_element_type=F32)
    ga_ref[0] = jnp.dot(hb, w_ref[:, SSM_W + 2 * GM_W:SSM_W + 2 * GM_W + D], preferred_element_type=F32)
    gb_ref[0] = jnp.dot(hb, w_ref[:, SSM_W + 2 * GM_W + D:], preferred_element_type=F32)


def _proj_call(x, norm_g, mod3, w_in_bf):
    pw = w_in_bf.shape[1]
    tok_spec = pl.BlockSpec((1, TS_PROJ, D), lambda b, s: (b, s, 0))
    return pl.pallas_call(
        _proj_kernel,
        grid=(B, S // TS_PROJ),
        in_specs=[tok_spec,
                  pl.BlockSpec((1, D), lambda b, s: (0, 0)),
                  pl.BlockSpec((1, 1, D), lambda b, s: (b, 0, 0)),
                  pl.BlockSpec((1, 1, D), lambda b, s: (b, 0, 1)),
                  pl.BlockSpec((D, pw), lambda b, s: (0, 0))],
        out_specs=[pl.BlockSpec((1, TS_PROJ, SSM_W), lambda b, s: (b, s, 0)),
                   tok_spec, tok_spec, tok_spec],
        out_shape=[jax.ShapeDtypeStruct((B, S, SSM_W), F32),
                   jax.ShapeDtypeStruct((B, S, D), F32),
                   jax.ShapeDtypeStruct((B, S, D), F32),
                   jax.ShapeDtypeStruct((B, S, D), F32)],
        compiler_params=pltpu.CompilerParams(vmem_limit_bytes=VMEM_LIMIT),
        name="norm_proj",
    )(x, norm_g.reshape(1, D), mod3, mod3, w_in_bf)


def _s5_kernel(u_ref, bm_ref, cre_ref, cim_ref, are_ref, aim_ref, d_ref, gw_ref, gb_ref, wa_ref,
               o_ref, usc, ysc, sre, sim, st_re, st_im):
    @pl.when(pl.program_id(0) == 0)
    def _():
        st_re[...] = jnp.zeros_like(st_re)
        st_im[...] = jnp.zeros_like(st_im)

    nslab = SSM_W // LANES
    for b in range(B):
        for c in range(nslab):
            usc[c, pl.ds(b, L_SSM, stride=B), :] = u_ref[b, :, c * LANES:(c + 1) * LANES]
    u = jnp.concatenate([usc[c] for c in range(nslab)], axis=1)
    ub = u.astype(BF16)
    half = PACK_G * SSM_P
    for k in range(N_PACK):
        bu = jnp.dot(ub[:, 128 * k:128 * (k + 1)], bm_ref[k], preferred_element_type=F32)
        sre[k] = bu[:, :half]
        sim[k] = bu[:, half:]

    for k in range(N_PACK):
        ar = are_ref[k]
        ai = aim_ref[k]

        def body(t, carry, k=k, ar=ar, ai=ai):
            r, m = carry
            off = pl.multiple_of(t * B, B)
            nr = ar * r - ai * m + sre[k, pl.ds(off, B), :]
            ni = ar * m + ai * r + sim[k, pl.ds(off, B), :]
            sre[k, pl.ds(off, B), :] = nr
            sim[k, pl.ds(off, B), :] = ni
            return nr, ni

        r, m = lax.fori_loop(0, L_SSM, body, (st_re[k], st_im[k]), unroll=4)
        st_re[k] = r
        st_im[k] = m

    ys = []
    for k in range(N_PACK):
        yk = jnp.dot(sre[k].astype(BF16), cre_ref[k], preferred_element_type=F32)
        yk = yk + jnp.dot(sim[k].astype(BF16), cim_ref[k], preferred_element_type=F32)
        ys.append(yk)
    for c in range(nslab):
        uc = usc[c]
        ysc[c] = ys[c] + d_ref[:, c * LANES:(c + 1) * LANES] * uc
    y = jnp.concatenate(
        [jnp.concatenate([ysc[c, pl.ds(b, L_SSM, stride=B), :] for c in range(nslab)], axis=1)
         for b in range(B)], axis=0)
    z = _gelu(y)
    gl = jnp.dot(z.astype(BF16), gw_ref[...], preferred_element_type=F32) + gb_ref[...]
    out = z * _sigmoid(gl)
    o = jnp.dot(out.astype(BF16), wa_ref[...], preferred_element_type=F32)
    for b in range(B):
        o_ref[b] = o[b * L_SSM:(b + 1) * L_SSM]


def _s5_call(u, bm, cre, cim, are, aim, d_skip, glu_w, glu_b, w_a):
    half = PACK_G * SSM_P
    full = lambda *shape: pl.BlockSpec(shape, lambda i: (0,) * len(shape))
    return pl.pallas_call(
        _s5_kernel,
        grid=(S // L_SSM,),
        in_specs=[pl.BlockSpec((B, L_SSM, SSM_W), lambda i: (0, i, 0)),
                  full(N_PACK, 128, 2 * half),
                  full(N_PACK, half, 128),
                  full(N_PACK, half, 128),
                  full(N_PACK, B, half),
                  full(N_PACK, B, half),
                  full(1, SSM_W),
                  full(SSM_W, SSM_W),
                  full(1, SSM_W),
                  full(SSM_W, D)],
        out_specs=pl.BlockSpec((B, L_SSM, D), lambda i: (0, i, 0)),
        out_shape=jax.ShapeDtypeStruct((B, S, D), F32),
        scratch_shapes=[pltpu.VMEM((SSM_W // LANES, R_SSM, LANES), F32),
                        pltpu.VMEM((SSM_W // LANES, R_SSM, LANES), F32),
                        pltpu.VMEM((N_PACK, R_SSM, half), F32),
                        pltpu.VMEM((N_PACK, R_SSM, half), F32),
                        pltpu.VMEM((N_PACK, B, half), F32),
                        pltpu.VMEM((N_PACK, B, half), F32)],
        compiler_params=pltpu.CompilerParams(dimension_semantics=("arbitrary",),
                                             vmem_limit_bytes=VMEM_LIMIT),
        name="s5_branch",
    )(u, bm, cre, cim, are, aim, d_skip, glu_w, glu_b, w_a)


def _s5_params(a_re, a_im, log_dt, b_re, b_im, c_re, c_im):
    dt = jnp.exp(log_dt)[:, None]
    mag = jnp.exp(a_re * dt)
    lr = mag * jnp.cos(a_im * dt)
    li = mag * jnp.sin(a_im * dt)
    den = a_re * a_re + a_im * a_im
    cr = ((lr - 1.0) * a_re + li * a_im) / den
    ci = (li * a_re - (lr - 1.0) * a_im) / den
    bbr = cr[..., None] * b_re - ci[..., None] * b_im
    bbi = cr[..., None] * b_im + ci[..., None] * b_re
    eye = jnp.eye(PACK_G, dtype=F32)
    half = PACK_G * SSM_P

    def pack_b(m):
        m4 = m.reshape(N_PACK, PACK_G, SSM_P, SSM_H)
        return jnp.einsum('kgph,gj->kghjp', m4, eye).reshape(N_PACK, PACK_G * SSM_H, half)

    def pack_c(m):
        m4 = m.reshape(N_PACK, PACK_G, SSM_H, SSM_P)
        return jnp.einsum('kghp,gj->kgpjh', m4, eye).reshape(N_PACK, half, PACK_G * SSM_H)

    bm = jnp.concatenate([pack_b(bbr), pack_b(bbi)], axis=-1).astype(BF16)
    cre = pack_c(c_re).astype(BF16)
    cim = (-pack_c(c_im)).astype(BF16)
    are = jnp.broadcast_to(lr.reshape(N_PACK, 1, half), (N_PACK, B, half))
    aim = jnp.broadcast_to(li.reshape(N_PACK, 1, half), (N_PACK, B, half))
    return bm, cre, cim, are, aim


def _mix_kernel(zuv_ref, ga_ref, gb_ref, ya_ref, x_ref, gate1_ref, shift2_ref, scale2_ref,
                lng_ref, lnb_ref, ws_ref, bias_ref, wbb_ref, wo_ref, n2g_ref, rw_ref, rb_ref,
                x1_ref, h2_ref, lg_ref):
    z = _gelu(zuv_ref[0])
    u = z[:, :GM_W]
    v = z[:, GM_W:]
    mu = jnp.mean(v, axis=-1, keepdims=True)
    vc = v - mu
    var = jnp.mean(vc * vc, axis=-1, keepdims=True)
    vn = vc * lax.rsqrt(var + EPS) * lng_ref[...] + lnb_ref[...]

    row = lax.broadcasted_iota(jnp.int32, (CHUNK, 2 * CHUNK), 0)
    col = lax.broadcasted_iota(jnp.int32, (CHUNK, 2 * CHUNK), 1)
    causal = (col % CHUNK) <= row
    lane = lax.broadcasted_iota(jnp.int32, (CHUNK, 2 * GM_HD), 1)
    first = lane < GM_HD
    wpairs = [jnp.where(causal, ws_ref[j], 0.0).astype(BF16) for j in range(GM_HEADS // 2)]
    chunks = []
    for n in range(TS_MIX // CHUNK):
        cols = []
        for j in range(GM_HEADS // 2):
            vp = vn[n * CHUNK:(n + 1) * CHUNK, 2 * GM_HD * j:2 * GM_HD * (j + 1)]
            rhs = jnp.concatenate([jnp.where(first, vp, 0.0), jnp.where(first, 0.0, vp)], axis=0)
            cols.append(jnp.dot(wpairs[j], rhs.astype(BF16), preferred_element_type=F32))
        chunks.append(jnp.concatenate(cols, axis=1) + bias_ref[...])
    mixed = jnp.concatenate(chunks, axis=0)
    gm = u * mixed
    yb = jnp.dot(gm.astype(BF16), wbb_ref[...], preferred_element_type=F32)
    merged = _sigmoid(ga_ref[0]) * ya_ref[0] + _sigmoid(gb_ref[0]) * yb
    o = jnp.dot(merged.astype(BF16), wo_ref[...], preferred_element_type=F32)
    x1 = x_ref[0] + gate1_ref[0] * o
    x1_ref[0] = x1
    h2 = _rms(x1) * n2g_ref[...]
    h2 = h2 * (1.0 + scale2_ref[0]) + shift2_ref[0]
    h2_ref[0] = h2.astype(BF16)
    lg_ref[0] = jnp.dot(h2, rw_ref[...], preferred_element_type=F32,
                        precision=lax.Precision.HIGHEST) + rb_ref[...]


def _mix_call(zuv, ga, gb, ya2d, x, mod3, ln_g, ln_b, ws_pairs, bias_full, wbb, wo, n2g, rw, rb):
    tok_spec = pl.BlockSpec((1, TS_MIX, D), lambda b, s: (b, s, 0))
    full = lambda *shape: pl.BlockSpec(shape, lambda b, s: (0,) * len(shape))
    mod_spec = lambda j: pl.BlockSpec((1, 1, D), lambda b, s: (b, 0, j))
    return pl.pallas_call(
        _mix_kernel,
        grid=(B, S // TS_MIX),
        in_specs=[tok_spec, tok_spec, tok_spec, tok_spec, tok_spec,
                  mod_spec(2), mod_spec(3), mod_spec(4),
                  full(1, GM_W), full(1, GM_W),
                  full(GM_HEADS // 2, CHUNK, 2 * CHUNK),
                  full(CHUNK, GM_W),
                  full(GM_W, D), full(D, D), full(1, D),
                  full(D, N_E), full(1, N_E)],
        out_specs=[tok_spec, tok_spec,
                   pl.BlockSpec((1, TS_MIX, N_E), lambda b, s: (b, s, 0))],
        out_shape=[jax.ShapeDtypeStruct((B, S, D), F32),
                   jax.ShapeDtypeStruct((B, S, D), BF16),
                   jax.ShapeDtypeStruct((B, S, N_E), F32)],
        compiler_params=pltpu.CompilerParams(vmem_limit_bytes=VMEM_LIMIT),
        name="gmlp_merge_norm2",
    )(zuv, ga, gb, ya2d, x, mod3, mod3, mod3, ln_g, ln_b, ws_pairs, bias_full, wbb, wo, n2g, rw, rb)


def _route_kernel(lg_ref, pc_ref, pt_ref, cntt_ref, pret_ref, segt_ref, cnt_ref, carry):
    i = pl.program_id(0)
    l = lg_ref[...]
    lane = lax.broadcasted_iota(jnp.int32, l.shape, 1).astype(F32)
    sels, vals = [], []
    for _ in range(TOP_K):
        m = jnp.max(l, axis=-1, keepdims=True)
        idx = jnp.min(jnp.where(l == m, lane, float(N_E)), axis=-1, keepdims=True)
        sel = lane == idx
        sels.append(sel)
        vals.append(m)
        l = jnp.where(sel, -jnp.inf, l)
    member = sels[0].astype(F32)
    for k in range(1, TOP_K):
        member = member + sels[k].astype(F32)
    tile_cnt = jnp.sum(member, axis=0, keepdims=True)

    @pl.when(i == 0)
    def _():
        carry[...] = jnp.zeros_like(carry)

    r = lax.broadcasted_iota(jnp.int32, (N_E, N_E), 0)
    c = lax.broadcasted_iota(jnp.int32, (N_E, N_E), 1)
    tri = (r < c).astype(BF16)
    tcb = jnp.broadcast_to(tile_cnt, (8, N_E)).astype(BF16)
    seg = jnp.dot(tcb, tri, preferred_element_type=F32)[0:1]
    r = lax.broadcasted_iota(jnp.int32, (TT, TT), 0)
    c = lax.broadcasted_iota(jnp.int32, (TT, TT), 1)
    strict = (c < r).astype(BF16)
    rank = jnp.dot(strict, member.astype(BF16), preferred_element_type=F32)
    posb = seg + rank
    denom = jnp.zeros_like(vals[0])
    exps = []
    for k in range(TOP_K):
        e = jnp.exp(vals[k] - vals[0])
        exps.append(e)
        denom = denom + e
    pc = jnp.zeros(l.shape, F32)
    for k in range(TOP_K):
        pk = jnp.sum(jnp.where(sels[k], posb, 0.0), axis=-1, keepdims=True)
        pc = jnp.where(lane == float(k), pk, pc)
        pc = jnp.where(lane == float(TOP_K + k), exps[k] / denom, pc)
    pc_ref[...] = pc
    r8 = lax.broadcasted_iota(jnp.int32, (8, N_E), 0)
    c8 = lax.broadcasted_iota(jnp.int32, (8, N_E), 1)
    eye = (r8 == c8).astype(F32)
    pt_ref[...] = lax.dot_general(eye, pc, (((1,), (1,)), ((), ())),
                                  preferred_element_type=F32, precision=lax.Precision.HIGHEST)
    cntt_ref[0] = jnp.broadcast_to(tile_cnt, (8, N_E)).astype(jnp.int32)
    pret_ref[0] = jnp.broadcast_to(carry[...], (8, N_E)).astype(jnp.int32)
    segt_ref[0] = jnp.broadcast_to(seg, (8, N_E)).astype(jnp.int32)
    carry[...] += tile_cnt
    cnt_ref[...] = jnp.broadcast_to(carry[...], cnt_ref.shape)


def _route_call(logits):
    tbl_spec = pl.BlockSpec((1, 8, N_E), lambda i: (i, 0, 0))
    tbl_shape = jax.ShapeDtypeStruct((NT, 8, N_E), jnp.int32)
    return pl.pallas_call(
        _route_kernel,
        grid=(NT,),
        in_specs=[pl.BlockSpec((TT, N_E), lambda i: (i, 0))],
        out_specs=[pl.BlockSpec((TT, N_E), lambda i: (i, 0)),
                   pl.BlockSpec((8, TT), lambda i: (0, i)),
                   tbl_spec, tbl_spec, tbl_spec,
                   pl.BlockSpec((8, N_E), lambda i: (0, 0))],
        out_shape=[jax.ShapeDtypeStruct((T, N_E), F32),
                   jax.ShapeDtypeStruct((8, T), F32),
                   tbl_shape, tbl_shape, tbl_shape,
                   jax.ShapeDtypeStruct((8, N_E), F32)],
        scratch_shapes=[pltpu.VMEM((1, N_E), F32)],
        compiler_params=pltpu.CompilerParams(dimension_semantics=("arbitrary",)),
        name="route",
    )(logits)


def _start_run(src_ref, dst_ref, src_row, dst_row, n, sem):
    def start(rows, done):
        s = pl.multiple_of((src_row + done) * SUB, SUB)
        d = pl.multiple_of((dst_row + done) * SUB, SUB)
        pltpu.make_async_copy(src_ref.at[pl.ds(s, rows * SUB)],
                              dst_ref.at[pl.ds(d, rows * SUB)], sem).start()

    nbulk = jnp.right_shift(n, PIECE.bit_length() - 1)

    def bulk(pi, carry):
        start(PIECE, pi * PIECE)
        return carry

    lax.fori_loop(0, nbulk, bulk, 0)
    done = nbulk * PIECE
    for rows in (4, 2, 1):
        @pl.when((n & rows) != 0)
        def _(rows=rows, done=done):
            start(rows, done)
        done = done + (n & rows)


def _start_tile_runs(tile, src_ref, dst_ref, src_tbl, dst_tbl, cnt_tbl, sem):
    base = tile * N_E

    def per_expert(e, carry):
        _start_run(src_ref, dst_ref, src_tbl[base + e], dst_tbl[base + e], cnt_tbl[base + e], sem)
        return carry

    lax.fori_loop(0, N_E, per_expert, 0)


def _wait_tile_runs(src_ref, dst_ref, sem):
    pltpu.make_async_copy(src_ref, dst_ref, sem).wait()


def _dispatch_kernel(cnt_ref, off_ref, seg_ref, pend_ref, h_ref, pt_ref, xs_ref,
                     sbuf0, sbuf1, zbuf, sem_z, sem0, sem1):
    j = pl.program_id(0)

    @pl.when(j == 0)
    def _():
        zbuf[...] = jnp.zeros_like(zbuf)
        for e in range(N_E):
            prev = pend_ref[e - 1] if e > 0 else 0
            end = pend_ref[e]

            @pl.when(end > prev)
            def _():
                first = pl.multiple_of((end - TM) * SUB, TM * SUB)
                cp = pltpu.make_async_copy(zbuf, xs_ref.at[pl.ds(first, TM * SUB)], sem_z)
                cp.start()
                cp.wait()

    rows = lax.broadcasted_iota(jnp.int32, (RS, TT), 0)
    pos = pt_ref[...].astype(jnp.int32)
    hit = rows == pos[0:1, :]
    for k in range(1, TOP_K):
        hit = hit | (rows == pos[k:k + 1, :])
    pm = jnp.where(hit, 1.0, 0.0).astype(BF16)
    srt = jnp.dot(pm, h_ref[...], preferred_element_type=F32)
    head = xs_ref.at[pl.ds(0, RS * SUB)]

    def emit(sbuf, sem):
        @pl.when(j >= 2)
        def _():
            _wait_tile_runs(sbuf, head, sem)
        for c in range(SUB):
            sbuf[pl.ds(c, RS, stride=SUB), :] = srt[:, c * LANES:(c + 1) * LANES]
        _start_tile_runs(j, sbuf, xs_ref, seg_ref, off_ref, cnt_ref, sem)

    @pl.when(j % 2 == 0)
    def _():
        emit(sbuf0, sem0)

    @pl.when(j % 2 == 1)
    def _():
        emit(sbuf1, sem1)

    @pl.when(j == NT - 1)
    def _():
        _wait_tile_runs(sbuf0, head, sem0)
        _wait_tile_runs(sbuf1, head, sem1)


def _dispatch_call(cnt_t, off_t, seg_t, pad_end, h2, pos_t):
    assert NT % 2 == 0 and NT >= 2
    grid_spec = pltpu.PrefetchScalarGridSpec(
        num_scalar_prefetch=4,
        grid=(NT,),
        in_specs=[pl.BlockSpec((TT, D), lambda j, *_: (j, 0)),
                  pl.BlockSpec((8, TT), lambda j, *_: (0, j))],
        out_specs=pl.BlockSpec(memory_space=pl.ANY),
        scratch_shapes=[pltpu.VMEM((RS * SUB, LANES), F32),
                        pltpu.VMEM((RS * SUB, LANES), F32),
                        pltpu.VMEM((TM * SUB, LANES), F32),
                        pltpu.SemaphoreType.DMA,
                        pltpu.SemaphoreType.DMA,
                        pltpu.SemaphoreType.DMA],
    )
    return pl.pallas_call(
        _dispatch_kernel,
        grid_spec=grid_spec,
        out_shape=jax.ShapeDtypeStruct((N_ROWS * SUB, LANES), F32),
        compiler_params=pltpu.CompilerParams(dimension_semantics=("arbitrary",),
                                             vmem_limit_bytes=VMEM_LIMIT),
        name="dispatch",
    )(cnt_t, off_t, seg_t, pad_end, h2, pos_t)


def _moe_kernel(be_ref, bf_ref, nx_ref, nv_ref, xs_ref, wi_hbm, bi_ref, wo_hbm, bo_ref, ys_ref,
                wi_f32, wo_f32, wi_bf, wo_bf, sem_i, sem_o):
    i = pl.program_id(0)

    def fetch(e):
        return (pltpu.make_async_copy(wi_hbm.at[e], wi_f32, sem_i),
                pltpu.make_async_copy(wo_hbm.at[e], wo_f32, sem_o))

    @pl.when(i == 0)
    def _():
        for cp in fetch(be_ref[0]):
            cp.start()

    @pl.when((i < nv_ref[0]) & (bf_ref[i] == 1))
    def _():
        for cp in fetch(be_ref[i]):
            cp.wait()
        wi_bf[...] = wi_f32[...].astype(BF16)
        wo_bf[...] = wo_f32[...].astype(BF16)

        @pl.when(nx_ref[i] >= 0)
        def _():
            for cp in fetch(nx_ref[i]):
                cp.start()

    @pl.when(i < nv_ref[0])
    def _():
        xb = jnp.concatenate([xs_ref[pl.ds(c, TM, stride=SUB), :] for c in range(SUB)],
                             axis=1).astype(BF16)
        gu = jnp.dot(xb, wi_bf[...], preferred_element_type=F32) + bi_ref[0]
        gate = jnp.minimum(gu[:, :D_E], LIMIT)
        up = jnp.clip(gu[:, D_E:], -LIMIT, LIMIT)
        act = (up + 1.0) * (gate * _sigmoid(ALPHA * gate))
        y = jnp.dot(act.astype(BF16), wo_bf[...], preferred_element_type=F32) + bo_ref[0]
        for c in range(SUB):
            ys_ref[pl.ds(c, TM, stride=SUB), :] = y[:, c * LANES:(c + 1) * LANES]


def _moe_call(blk_e, blk_first, blk_next, n_valid, xs, w_in, b_in, w_out, b_out):
    def row_map(i, be, bf, nx, nv):
        return (jnp.maximum(jnp.minimum(i, nv[0] - 1), 0), 0)

    def exp_map(i, be, bf, nx, nv):
        return (be[i], 0, 0)

    grid_spec = pltpu.PrefetchScalarGridSpec(
        num_scalar_prefetch=4,
        grid=(N_BLOCKS,),
        in_specs=[pl.BlockSpec((TM * SUB, LANES), row_map),
                  pl.BlockSpec(memory_space=pl.ANY),
                  pl.BlockSpec((1, 1, 2 * D_E), exp_map),
                  pl.BlockSpec(memory_space=pl.ANY),
                  pl.BlockSpec((1, 1, D), exp_map)],
        out_specs=pl.BlockSpec((TM * SUB, LANES), row_map),
        scratch_shapes=[pltpu.VMEM((D, 2 * D_E), F32),
                        pltpu.VMEM((D_E, D), F32),
                        pltpu.VMEM((D, 2 * D_E), BF16),
                        pltpu.VMEM((D_E, D), BF16),
                        pltpu.SemaphoreType.DMA,
                        pltpu.SemaphoreType.DMA],
    )
    return pl.pallas_call(
        _moe_kernel,
        grid_spec=grid_spec,
        out_shape=jax.ShapeDtypeStruct((N_ROWS * SUB, LANES), F32),
        compiler_params=pltpu.CompilerParams(dimension_semantics=("arbitrary",),
                                             vmem_limit_bytes=VMEM_LIMIT),
        name="moe_experts",
    )(blk_e, blk_first, blk_next, n_valid, xs, w_in, b_in, w_out, b_out)


def _combine_kernel(cnt_ref, off_ref, seg_ref, ys_ref, pc_ref, x1_ref, gate2_ref, fg_ref, o_ref,
                    buf0, buf1, sem0, sem1):
    j = pl.program_id(0)
    head = ys_ref.at[pl.ds(0, RS * SUB)]

    def fetch(tile, buf, sem):
        _start_tile_runs(tile, ys_ref, buf, off_ref, seg_ref, cnt_ref, sem)

    @pl.when(j == 0)
    def _():
        fetch(0, buf0, sem0)

    pc = pc_ref[...]
    pos = pc.astype(jnp.int32)
    col = lax.broadcasted_iota(jnp.int32, (TT, RS), 1)
    wm = jnp.zeros((TT, RS), F32)
    for k in range(TOP_K):
        wm = jnp.where(col == pos[:, k:k + 1], pc[:, TOP_K + k:TOP_K + k + 1], wm)
    wm = wm.astype(BF16)

    def tile(buf, sem, next_buf, next_sem):
        @pl.when(j + 1 < NT)
        def _():
            fetch(j + 1, next_buf, next_sem)
        _wait_tile_runs(head, buf, sem)
        yt = jnp.concatenate([buf[pl.ds(c, RS, stride=SUB), :] for c in range(SUB)],
                             axis=1).astype(BF16)
        acc = jnp.dot(wm, yt, preferred_element_type=F32)
        x2 = x1_ref[...] + gate2_ref[0] * acc
        o_ref[...] = _rms(x2) * fg_ref[...]

    @pl.when(j % 2 == 0)
    def _():
        tile(buf0, sem0, buf1, sem1)

    @pl.when(j % 2 == 1)
    def _():
        tile(buf1, sem1, buf0, sem0)


def _combine_call(cnt_t, off_t, seg_t, ys, pos_c, x1, mod3, final_g):
    per_b = S // TT
    grid_spec = pltpu.PrefetchScalarGridSpec(
        num_scalar_prefetch=3,
        grid=(NT,),
        in_specs=[pl.BlockSpec(memory_space=pl.ANY),
                  pl.BlockSpec((TT, N_E), lambda j, *_: (j, 0)),
                  pl.BlockSpec((TT, D), lambda j, *_: (j, 0)),
                  pl.BlockSpec((1, 1, D), lambda j, *_: (j // per_b, 0, 5)),
                  pl.BlockSpec((1, D), lambda j, *_: (0, 0))],
        out_specs=pl.BlockSpec((TT, D), lambda j, *_: (j, 0)),
        scratch_shapes=[pltpu.VMEM((RS * SUB, LANES), F32),
                        pltpu.VMEM((RS * SUB, LANES), F32),
                        pltpu.SemaphoreType.DMA,
                        pltpu.SemaphoreType.DMA],
    )
    return pl.pallas_call(
        _combine_kernel,
        grid_spec=grid_spec,
        out_shape=jax.ShapeDtypeStruct((T, D), F32),
        compiler_params=pltpu.CompilerParams(dimension_semantics=("arbitrary",),
                                             vmem_limit_bytes=VMEM_LIMIT),
        name="combine_norm",
    )(cnt_t, off_t, seg_t, ys, pos_c, x1, mod3, final_g)


def kernel(x, c, ada_w, ada_b, norm1_g, w_in, ssm_a_re, ssm_a_im, ssm_log_dt, ssm_b_re, ssm_b_im, ssm_c_re, ssm_c_im, ssm_d, ssm_glu_w, ssm_glu_b, w_branch_a, gmlp_ln_g, gmlp_ln_b, gmlp_ws, gmlp_bs, w_branch_b, w_out, norm2_g, router_w, router_b, moe_w_in, moe_b_in, moe_w_out, moe_b_out, final_g):
    depth = ada_w.shape[0]
    assert depth == 1, "the final rms_norm is fused into the combine kernel of the only layer"
    for layer in range(depth):
        mod = _mod_call(c, ada_w[layer], ada_b[layer])
        mod3 = mod.reshape(B, 1, 6 * D)

        u, zuv, ga, gb = _proj_call(x, norm1_g[layer], mod3, w_in[layer].astype(BF16))

        bm, cre, cim, are, aim = _s5_params(ssm_a_re[layer], ssm_a_im[layer], ssm_log_dt[layer],
                                            ssm_b_re[layer], ssm_b_im[layer],
                                            ssm_c_re[layer], ssm_c_im[layer])
        ya = _s5_call(u, bm, cre, cim, are, aim,
                      ssm_d[layer].reshape(1, SSM_W), ssm_glu_w[layer].astype(BF16),
                      ssm_glu_b[layer].reshape(1, SSM_W), w_branch_a[layer].astype(BF16))

        ws = gmlp_ws[layer]
        ws_pairs = jnp.concatenate([ws[0::2], ws[1::2]], axis=-1)
        bias_full = jnp.repeat(gmlp_bs[layer].T, GM_HD, axis=1)
        x1, h2, logits = _mix_call(
            zuv, ga, gb, ya, x, mod3,
            gmlp_ln_g[layer].reshape(1, GM_W), gmlp_ln_b[layer].reshape(1, GM_W),
            ws_pairs, bias_full, w_branch_b[layer].astype(BF16), w_out[layer].astype(BF16),
            norm2_g[layer].reshape(1, D), router_w[layer],
            router_b[layer].reshape(1, N_E))

        pos_c, pos_t, cnt_t, pre_t, seg_t, cnt = _route_call(logits.reshape(T, N_E))
        counts = cnt[0].astype(jnp.int32)
        nblk = (counts + TM - 1) // TM
        blk_end = jnp.cumsum(nblk)
        pad_end = (blk_end * TM).astype(jnp.int32)
        experts = jnp.arange(N_E, dtype=jnp.int32)
        blk_ids = jnp.arange(N_BLOCKS, dtype=jnp.int32)
        blk_e = jnp.sum((blk_end[None, :] <= blk_ids[:, None]).astype(jnp.int32), axis=1)
        blk_e = jnp.minimum(blk_e, N_E - 1)
        blk_first = jnp.concatenate([jnp.ones((1,), jnp.int32),
                                     (blk_e[1:] != blk_e[:-1]).astype(jnp.int32)])
        later = (experts[None, :] > experts[:, None]) & (nblk[None, :] > 0)
        next_e = jnp.min(jnp.where(later, experts[None, :], N_E), axis=1)
        next_e = jnp.where(next_e == N_E, -1, next_e)
        blk_next = jnp.sum(jnp.where(blk_e[:, None] == experts[None, :], next_e[None, :], 0), axis=1)
        n_valid = blk_end[-1:].astype(jnp.int32)
        pad_start = pad_end - nblk * TM
        off_t = pre_t[:, 0, :] + pad_start[None, :]
        cnt_t, off_t, seg_t = (t.reshape(NT * N_E) for t in (cnt_t[:, 0, :], off_t, seg_t[:, 0, :]))

        xs = _dispatch_call(cnt_t, off_t, seg_t, pad_end, h2.reshape(T, D), pos_t)
        ys = _moe_call(blk_e, blk_first, blk_next.astype(jnp.int32), n_valid, xs, moe_w_in[layer],
                       moe_b_in[layer].reshape(N_E, 1, 2 * D_E), moe_w_out[layer],
                       moe_b_out[layer].reshape(N_E, 1, D))
        x = _combine_call(cnt_t, off_t, seg_t, ys, pos_c, x1.reshape(T, D),
                          mod3, final_g.reshape(1, D)).reshape(B, S, D)
    return x
```

```python
import functools
import math

import jax
import jax.numpy as jnp
from jax import lax
from jax.experimental import pallas as pl
from jax.experimental.pallas import tpu as pltpu

F32 = jnp.float32
BF16 = jnp.bfloat16

D = 1024
B = 8
S = 2048
T = B * S
SSM_W = 512
SSM_G = 32
SSM_H = 16
SSM_P = 64
N_PACK = 4
PACK_G = SSM_G // N_PACK
GM_W = 512
GM_HEADS = 8
GM_HD = 64
CHUNK = 128
N_E = 32
TOP_K = 4
D_E = 1024
LIMIT = 7.0
ALPHA = 1.702
EPS = 1e-6

TS_PROJ = 512
L_SSM = 64
R_SSM = L_SSM * B
TS_MIX = 256
TT = 256
NT = T // TT
RS = TOP_K * TT
TM = 256
N_ROWS = T * TOP_K + N_E * TM
N_BLOCKS = N_ROWS // TM
LANES = 128
HALF = D // 2
SUB = HALF // LANES
PIECE = 16
VMEM_LIMIT = 56 * 1024 * 1024
U32 = jnp.uint32


def _sigmoid(v):
    return 1.0 / (1.0 + jnp.exp(-v))


def _gelu(v):
    return 0.5 * v * (1.0 + jnp.tanh(math.sqrt(2.0 / math.pi) * (v + 0.044715 * v * v * v)))


def _rms(v):
    return v * lax.rsqrt(jnp.mean(v * v, axis=-1, keepdims=True) + EPS)


def _mod_kernel(c_ref, w_ref, b_ref, o_ref):
    cv = c_ref[...]
    sv = cv * _sigmoid(cv)
    o_ref[...] = jnp.dot(sv, w_ref[...], preferred_element_type=F32,
                         precision=lax.Precision.HIGHEST) + b_ref[...]


def _mod_call(c, ada_w, ada_b):
    n = ada_w.shape[1]
    return pl.pallas_call(
        _mod_kernel,
        grid=(n // D,),
        in_specs=[pl.BlockSpec((B, D), lambda j: (0, 0)),
                  pl.BlockSpec((D, D), lambda j: (0, j)),
                  pl.BlockSpec((1, D), lambda j: (0, j))],
        out_specs=pl.BlockSpec((B, D), lambda j: (0, j)),
        out_shape=jax.ShapeDtypeStruct((B, n), F32),
        name="adaln_mod",
    )(c, ada_w, ada_b.reshape(1, n))


def _proj_kernel(x_ref, g_ref, shift_ref, scale_ref, w_ref, u_ref, zuv_ref, ga_ref, gb_ref):
    h = _rms(x_ref[0]) * g_ref[...]
    h = h * (1.0 + scale_ref[0]) + shift_ref[0]
    hb = h.astype(BF16)
    u_ref[0] = jnp.dot(hb, w_ref[:, 0:SSM_W], preferred_element_type=F32)
    zuv_ref[0] = jnp.dot(hb, w_ref[:, SSM_W:SSM_W + 2 * GM_W], preferred_element_type=F32)
    ga_ref[0] = jnp.dot(hb, w_ref[:, SSM_W + 2 * GM_W:SSM_W + 2 * GM_W + D], preferred_element_type=F32)
    gb_ref[0] = jnp.dot(hb, w_ref[:, SSM_W + 2 * GM_W + D:], preferred_element_type=F32)


def _proj_call(x, norm_g, mod3, w_in_bf):
    pw = w_in_bf.shape[1]
    tok_spec = pl.BlockSpec((1, TS_PROJ, D), lambda b, s: (b, s, 0))
    return pl.pallas_call(
        _proj_kernel,
        grid=(B, S // TS_PROJ),
        in_specs=[tok_spec,
                  pl.BlockSpec((1, D), lambda b, s: (0, 0)),
                  pl.BlockSpec((1, 1, D), lambda b, s: (b, 0, 0)),
                  pl.BlockSpec((1, 1, D), lambda b, s: (b, 0, 1)),
                  pl.BlockSpec((D, pw), lambda b, s: (0, 0))],
        out_specs=[pl.BlockSpec((1, TS_PROJ, SSM_W), lambda b, s: (b, s, 0)),
                   tok_spec, tok_spec, tok_spec],
        out_shape=[jax.ShapeDtypeStruct((B, S, SSM_W), F32),
                   jax.ShapeDtypeStruct((B, S, D), F32),
                   jax.ShapeDtypeStruct((B, S, D), F32),
                   jax.ShapeDtypeStruct((B, S, D), F32)],
        compiler_params=pltpu.CompilerParams(vmem_limit_bytes=VMEM_LIMIT),
        name="norm_proj",
    )(x, norm_g.reshape(1, D), mod3, mod3, w_in_bf)


def _s5_kernel(u_ref, bm_ref, cre_ref, cim_ref, are_ref, aim_ref, d_ref, gw_ref, gb_ref, wa_ref,
               o_ref, usc, ysc, sre, sim, st_re, st_im):
    @pl.when(pl.program_id(0) == 0)
    def _():
        st_re[...] = jnp.zeros_like(st_re)
        st_im[...] = jnp.zeros_like(st_im)

    nslab = SSM_W // LANES
    for b in range(B):
        for c in range(nslab):
            usc[c, pl.ds(b, L_SSM, stride=B), :] = u_ref[b, :, c * LANES:(c + 1) * LANES]
    u = jnp.concatenate([usc[c] for c in range(nslab)], axis=1)
    ub = u.astype(BF16)
    half = PACK_G * SSM_P
    for k in range(N_PACK):
        bu = jnp.dot(ub[:, 128 * k:128 * (k + 1)], bm_ref[k], preferred_element_type=F32)
        sre[k] = bu[:, :half]
        sim[k] = bu[:, half:]

    for k in range(N_PACK):
        ar = are_ref[k]
        ai = aim_ref[k]

        def body(t, carry, k=k, ar=ar, ai=ai):
            r, m = carry
            off = pl.multiple_of(t * B, B)
            nr = ar * r - ai * m + sre[k, pl.ds(off, B), :]
            ni = ar * m + ai * r + sim[k, pl.ds(off, B), :]
            sre[k, pl.ds(off, B), :] = nr
            sim[k, pl.ds(off, B), :] = ni
            return nr, ni

        r, m = lax.fori_loop(0, L_SSM, body, (st_re[k], st_im[k]), unroll=4)
        st_re[k] = r
        st_im[k] = m

    ys = []
    for k in range(N_PACK):
        yk = jnp.dot(sre[k].astype(BF16), cre_ref[k], preferred_element_type=F32)
        yk = yk + jnp.dot(sim[k].astype(BF16), cim_ref[k], preferred_element_type=F32)
        ys.append(yk)
    for c in range(nslab):
        uc = usc[c]
        ysc[c] = ys[c] + d_ref[:, c * LANES:(c + 1) * LANES] * uc
    y = jnp.concatenate(
        [jnp.concatenate([ysc[c, pl.ds(b, L_SSM, stride=B), :] for c in range(nslab)], axis=1)
         for b in range(B)], axis=0)
    z = _gelu(y)
    gl = jnp.dot(z.astype(BF16), gw_ref[...], preferred_element_type=F32) + gb_ref[...]
    out = z * _sigmoid(gl)
    o = jnp.dot(out.astype(BF16), wa_ref[...], preferred_element_type=F32)
    for b in range(B):
        o_ref[b] = o[b * L_SSM:(b + 1) * L_SSM]


def _s5_call(u, bm, cre, cim, are, aim, d_skip, glu_w, glu_b, w_a):
    half = PACK_G * SSM_P
    full = lambda *shape: pl.BlockSpec(shape, lambda i: (0,) * len(shape))
    return pl.pallas_call(
        _s5_kernel,
        grid=(S // L_SSM,),
        in_specs=[pl.BlockSpec((B, L_SSM, SSM_W), lambda i: (0, i, 0)),
                  full(N_PACK, 128, 2 * half),
                  full(N_PACK, half, 128),
                  full(N_PACK, half, 128),
                  full(N_PACK, B, half),
                  full(N_PACK, B, half),
                  full(1, SSM_W),
                  full(SSM_W, SSM_W),
                  full(1, SSM_W),
                  full(SSM_W, D)],
        out_specs=pl.BlockSpec((B, L_SSM, D), lambda i: (0, i, 0)),
        out_shape=jax.ShapeDtypeStruct((B, S, D), F32),
        scratch_shapes=[pltpu.VMEM((SSM_W // LANES, R_SSM, LANES), F32),
                        pltpu.VMEM((SSM_W // LANES, R_SSM, LANES), F32),
                        pltpu.VMEM((N_PACK, R_SSM, half), F32),
                        pltpu.VMEM((N_PACK, R_SSM, half), F32),
                        pltpu.VMEM((N_PACK, B, half), F32),
                        pltpu.VMEM((N_PACK, B, half), F32)],
        compiler_params=pltpu.CompilerParams(dimension_semantics=("arbitrary",),
                                             vmem_limit_bytes=VMEM_LIMIT),
        name="s5_branch",
    )(u, bm, cre, cim, are, aim, d_skip, glu_w, glu_b, w_a)


def _s5_params(a_re, a_im, log_dt, b_re, b_im, c_re, c_im):
    dt = jnp.exp(log_dt)[:, None]
    mag = jnp.exp(a_re * dt)
    lr = mag * jnp.cos(a_im * dt)
    li = mag * jnp.sin(a_im * dt)
    den = a_re * a_re + a_im * a_im
    cr = ((lr - 1.0) * a_re + li * a_im) / den
    ci = (li * a_re - (lr - 1.0) * a_im) / den
    bbr = cr[..., None] * b_re - ci[..., None] * b_im
    bbi = cr[..., None] * b_im + ci[..., None] * b_re
    eye = jnp.eye(PACK_G, dtype=F32)
    half = PACK_G * SSM_P

    def pack_b(m):
        m4 = m.reshape(N_PACK, PACK_G, SSM_P, SSM_H)
        return jnp.einsum('kgph,gj->kghjp', m4, eye).reshape(N_PACK, PACK_G * SSM_H, half)

    def pack_c(m):
        m4 = m.reshape(N_PACK, PACK_G, SSM_H, SSM_P)
        return jnp.einsum('kghp,gj->kgpjh', m4, eye).reshape(N_PACK, half, PACK_G * SSM_H)

    bm = jnp.concatenate([pack_b(bbr), pack_b(bbi)], axis=-1).astype(BF16)
    cre = pack_c(c_re).astype(BF16)
    cim = (-pack_c(c_im)).astype(BF16)
    are = jnp.broadcast_to(lr.reshape(N_PACK, 1, half), (N_PACK, B, half))
    aim = jnp.broadcast_to(li.reshape(N_PACK, 1, half), (N_PACK, B, half))
    return bm, cre, cim, are, aim


def _mix_kernel(zuv_ref, ga_ref, gb_ref, ya_ref, x_ref, gate1_ref, shift2_ref, scale2_ref,
                lng_ref, lnb_ref, ws_ref, bias_ref, wbb_ref, wo_ref, n2g_ref, rw_ref, rb_ref,
                x1_ref, h2_ref, lg_ref):
    z = _gelu(zuv_ref[0])
    u = z[:, :GM_W]
    v = z[:, GM_W:]
    mu = jnp.mean(v, axis=-1, keepdims=True)
    vc = v - mu
    var = jnp.mean(vc * vc, axis=-1, keepdims=True)
    vn = vc * lax.rsqrt(var + EPS) * lng_ref[...] + lnb_ref[...]

    row = lax.broadcasted_iota(jnp.int32, (CHUNK, 2 * CHUNK), 0)
    col = lax.broadcasted_iota(jnp.int32, (CHUNK, 2 * CHUNK), 1)
    causal = (col % CHUNK) <= row
    lane = lax.broadcasted_iota(jnp.int32, (CHUNK, 2 * GM_HD), 1)
    first = lane < GM_HD
    wpairs = [jnp.where(causal, ws_ref[j], 0.0).astype(BF16) for j in range(GM_HEADS // 2)]
    chunks = []
    for n in range(TS_MIX // CHUNK):
        cols = []
        for j in range(GM_HEADS // 2):
            vp = vn[n * CHUNK:(n + 1) * CHUNK, 2 * GM_HD * j:2 * GM_HD * (j + 1)]
            rhs = jnp.concatenate([jnp.where(first, vp, 0.0), jnp.where(first, 0.0, vp)], axis=0)
            cols.append(jnp.dot(wpairs[j], rhs.astype(BF16), preferred_element_type=F32))
        chunks.append(jnp.concatenate(cols, axis=1) + bias_ref[...])
    mixed = jnp.concatenate(chunks, axis=0)
    gm = u * mixed
    yb = jnp.dot(gm.astype(BF16), wbb_ref[...], preferred_element_type=F32)
    merged = _sigmoid(ga_ref[0]) * ya_ref[0] + _sigmoid(gb_ref[0]) * yb
    o = jnp.dot(merged.astype(BF16), wo_ref[...], preferred_element_type=F32)
    x1 = x_ref[0] + gate1_ref[0] * o
    x1_ref[0] = x1
    h2 = _rms(x1) * n2g_ref[...]
    h2 = h2 * (1.0 + scale2_ref[0]) + shift2_ref[0]
    h2_ref[0] = h2.astype(BF16)
    lg_ref[0] = jnp.dot(h2, rw_ref[...], preferred_element_type=F32,
                        precision=lax.Precision.HIGHEST) + rb_ref[...]


def _mix_call(zuv, ga, gb, ya2d, x, mod3, ln_g, ln_b, ws_pairs, bias_full, wbb, wo, n2g, rw, rb):
    tok_spec = pl.BlockSpec((1, TS_MIX, D), lambda b, s: (b, s, 0))
    full = lambda *shape: pl.BlockSpec(shape, lambda b, s: (0,) * len(shape))
    mod_spec = lambda j: pl.BlockSpec((1, 1, D), lambda b, s: (b, 0, j))
    return pl.pallas_call(
        _mix_kernel,
        grid=(B, S // TS_MIX),
        in_specs=[tok_spec, tok_spec, tok_spec, tok_spec, tok_spec,
                  mod_spec(2), mod_spec(3), mod_spec(4),
                  full(1, GM_W), full(1, GM_W),
                  full(GM_HEADS // 2, CHUNK, 2 * CHUNK),
                  full(CHUNK, GM_W),
                  full(GM_W, D), full(D, D), full(1, D),
                  full(D, N_E), full(1, N_E)],
        out_specs=[tok_spec, tok_spec,
                   pl.BlockSpec((1, TS_MIX, N_E), lambda b, s: (b, s, 0))],
        out_shape=[jax.ShapeDtypeStruct((B, S, D), F32),
                   jax.ShapeDtypeStruct((B, S, D), BF16),
                   jax.ShapeDtypeStruct((B, S, N_E), F32)],
        compiler_params=pltpu.CompilerParams(vmem_limit_bytes=VMEM_LIMIT),
        name="gmlp_merge_norm2",
    )(zuv, ga, gb, ya2d, x, mod3, mod3, mod3, ln_g, ln_b, ws_pairs, bias_full, wbb, wo, n2g, rw, rb)


def _route_kernel(lg_ref, pc_ref, pt_ref, cntt_ref, pret_ref, segt_ref, cnt_ref, carry):
    i = pl.program_id(0)
    l = lg_ref[...]
    lane = lax.broadcasted_iota(jnp.int32, l.shape, 1).astype(F32)
    sels, vals = [], []
    for _ in range(TOP_K):
        m = jnp.max(l, axis=-1, keepdims=True)
        idx = jnp.min(jnp.where(l == m, lane, float(N_E)), axis=-1, keepdims=True)
        sel = lane == idx
        sels.append(sel)
        vals.append(m)
        l = jnp.where(sel, -jnp.inf, l)
    member = sels[0].astype(F32)
    for k in range(1, TOP_K):
        member = member + sels[k].astype(F32)
    tile_cnt = jnp.sum(member, axis=0, keepdims=True)

    @pl.when(i == 0)
    def _():
        carry[...] = jnp.zeros_like(carry)

    r = lax.broadcasted_iota(jnp.int32, (N_E, N_E), 0)
    c = lax.broadcasted_iota(jnp.int32, (N_E, N_E), 1)
    tri = (r < c).astype(BF16)
    tcb = jnp.broadcast_to(tile_cnt, (8, N_E)).astype(BF16)
    seg = jnp.dot(tcb, tri, preferred_element_type=F32)[0:1]
    r = lax.broadcasted_iota(jnp.int32, (TT, TT), 0)
    c = lax.broadcasted_iota(jnp.int32, (TT, TT), 1)
    strict = (c < r).astype(BF16)
    rank = jnp.dot(strict, member.astype(BF16), preferred_element_type=F32)
    posb = seg + rank
    denom = jnp.zeros_like(vals[0])
    exps = []
    for k in range(TOP_K):
        e = jnp.exp(vals[k] - vals[0])
        exps.append(e)
        denom = denom + e
    pc = jnp.zeros(l.shape, F32)
    for k in range(TOP_K):
        pk = jnp.sum(jnp.where(sels[k], posb, 0.0), axis=-1, keepdims=True)
        pc = jnp.where(lane == float(k), pk, pc)
        pc = jnp.where(lane == float(TOP_K + k), exps[k] / denom, pc)
    pc_ref[...] = pc
    r8 = lax.broadcasted_iota(jnp.int32, (8, N_E), 0)
    c8 = lax.broadcasted_iota(jnp.int32, (8, N_E), 1)
    eye = (r8 == c8).astype(F32)
    pt_ref[...] = lax.dot_general(eye, pc, (((1,), (1,)), ((), ())),
                                  preferred_element_type=F32, precision=lax.Precision.HIGHEST)
    cntt_ref[0] = jnp.broadcast_to(tile_cnt, (8, N_E)).astype(jnp.int32)
    pret_ref[0] = jnp.broadcast_to(carry[...], (8, N_E)).astype(jnp.int32)
    segt_ref[0] = jnp.broadcast_to(seg, (8, N_E)).astype(jnp.int32)
    carry[...] += tile_cnt
    cnt_ref[...] = jnp.broadcast_to(carry[...], cnt_ref.shape)


def _route_call(logits):
    tbl_spec = pl.BlockSpec((1, 8, N_E), lambda i: (i, 0, 0))
    tbl_shape = jax.ShapeDtypeStruct((NT, 8, N_E), jnp.int32)
    return pl.pallas_call(
        _route_kernel,
        grid=(NT,),
        in_specs=[pl.BlockSpec((TT, N_E), lambda i: (i, 0))],
        out_specs=[pl.BlockSpec((TT, N_E), lambda i: (i, 0)),
                   pl.BlockSpec((8, TT), lambda i: (0, i)),
                   tbl_spec, tbl_spec, tbl_spec,
                   pl.BlockSpec((8, N_E), lambda i: (0, 0))],
        out_shape=[jax.ShapeDtypeStruct((T, N_E), F32),
                   jax.ShapeDtypeStruct((8, T), F32),
                   tbl_shape, tbl_shape, tbl_shape,
                   jax.ShapeDtypeStruct((8, N_E), F32)],
        scratch_shapes=[pltpu.VMEM((1, N_E), F32)],
        compiler_params=pltpu.CompilerParams(dimension_semantics=("arbitrary",)),
        name="route",
    )(logits)


def _pack_rows(v):
    return pltpu.pack_elementwise([v[:, :HALF], v[:, HALF:]], packed_dtype=BF16)


def _unpack_rows(w):
    halves = [pltpu.unpack_elementwise(w, index=i, packed_dtype=BF16, unpacked_dtype=F32)
              for i in range(2)]
    return jnp.concatenate(halves, axis=1)


def _load_grouped(ref, rows):
    return jnp.concatenate([ref[pl.ds(c, rows, stride=SUB), :] for c in range(SUB)], axis=1)


def _store_grouped(ref, w, rows):
    for c in range(SUB):
        ref[pl.ds(c, rows, stride=SUB), :] = w[:, c * LANES:(c + 1) * LANES]


def _start_run(src_ref, dst_ref, src_row, dst_row, n, sem):
    def start(rows, done):
        s = pl.multiple_of((src_row + done) * SUB, SUB)
        d = pl.multiple_of((dst_row + done) * SUB, SUB)
        pltpu.make_async_copy(src_ref.at[pl.ds(s, rows * SUB)],
                              dst_ref.at[pl.ds(d, rows * SUB)], sem).start()

    nbulk = jnp.right_shift(n, PIECE.bit_length() - 1)

    def bulk(pi, carry):
        start(PIECE, pi * PIECE)
        return carry

    lax.fori_loop(0, nbulk, bulk, 0)
    done = nbulk * PIECE
    for rows in [PIECE >> s for s in range(1, PIECE.bit_length())]:
        @pl.when((n & rows) != 0)
        def _(rows=rows, done=done):
            start(rows, done)
        done = done + (n & rows)


def _start_tile_runs(tile, src_ref, dst_ref, src_tbl, dst_tbl, cnt_tbl, sem):
    base = tile * N_E

    def per_expert(e, carry):
        _start_run(src_ref, dst_ref, src_tbl[base + e], dst_tbl[base + e], cnt_tbl[base + e], sem)
        return carry

    lax.fori_loop(0, N_E, per_expert, 0)


def _wait_tile_runs(src_ref, dst_ref, sem):
    pltpu.make_async_copy(src_ref, dst_ref, sem).wait()


def _dispatch_kernel(cnt_ref, off_ref, seg_ref, pend_ref, h_ref, pt_ref, xs_ref,
                     sbuf0, sbuf1, zbuf, sem_z, sem0, sem1):
    j = pl.program_id(0)

    @pl.when(j == 0)
    def _():
        zbuf[...] = jnp.zeros_like(zbuf)
        for e in range(N_E):
            prev = pend_ref[e - 1] if e > 0 else 0
            end = pend_ref[e]

            @pl.when(end > prev)
            def _():
                first = pl.multiple_of((end - TM) * SUB, TM * SUB)
                cp = pltpu.make_async_copy(zbuf, xs_ref.at[pl.ds(first, TM * SUB)], sem_z)
                cp.start()
                cp.wait()

    rows = lax.broadcasted_iota(jnp.int32, (RS, TT), 0)
    pos = pt_ref[...].astype(jnp.int32)
    hit = rows == pos[0:1, :]
    for k in range(1, TOP_K):
        hit = hit | (rows == pos[k:k + 1, :])
    pm = jnp.where(hit, 1.0, 0.0).astype(BF16)
    srt = jnp.dot(pm, h_ref[...], preferred_element_type=F32)
    words = _pack_rows(srt)
    head = xs_ref.at[pl.ds(0, RS * SUB)]

    def emit(sbuf, sem):
        @pl.when(j >= 2)
        def _():
            _wait_tile_runs(sbuf, head, sem)
        _store_grouped(sbuf, words, RS)
        _start_tile_runs(j, sbuf, xs_ref, seg_ref, off_ref, cnt_ref, sem)

    @pl.when(j % 2 == 0)
    def _():
        emit(sbuf0, sem0)

    @pl.when(j % 2 == 1)
    def _():
        emit(sbuf1, sem1)

    @pl.when(j == NT - 1)
    def _():
        _wait_tile_runs(sbuf0, head, sem0)
        _wait_tile_runs(sbuf1, head, sem1)


def _dispatch_call(cnt_t, off_t, seg_t, pad_end, h2, pos_t):
    assert NT % 2 == 0 and NT >= 2
    grid_spec = pltpu.PrefetchScalarGridSpec(
        num_scalar_prefetch=4,
        grid=(NT,),
        in_specs=[pl.BlockSpec((TT, D), lambda j, *_: (j, 0)),
                  pl.BlockSpec((8, TT), lambda j, *_: (0, j))],
        out_specs=pl.BlockSpec(memory_space=pl.ANY),
        scratch_shapes=[pltpu.VMEM((RS * SUB, LANES), U32),
                        pltpu.VMEM((RS * SUB, LANES), U32),
                        pltpu.VMEM((TM * SUB, LANES), U32),
                        pltpu.SemaphoreType.DMA,
                        pltpu.SemaphoreType.DMA,
                        pltpu.SemaphoreType.DMA],
    )
    return pl.pallas_call(
        _dispatch_kernel,
        grid_spec=grid_spec,
        out_shape=jax.ShapeDtypeStruct((N_ROWS * SUB, LANES), U32),
        compiler_params=pltpu.CompilerParams(dimension_semantics=("arbitrary",),
                                             vmem_limit_bytes=VMEM_LIMIT),
        name="dispatch",
    )(cnt_t, off_t, seg_t, pad_end, h2, pos_t)


def _moe_kernel(be_ref, bf_ref, nx_ref, nv_ref, xs_ref, wi_hbm, bi_ref, wo_hbm, bo_ref, ys_ref,
                wi_f32, wo_f32, wi_bf, wo_bf, sem_i, sem_o):
    i = pl.program_id(0)

    def fetch(e):
        return (pltpu.make_async_copy(wi_hbm.at[e], wi_f32, sem_i),
                pltpu.make_async_copy(wo_hbm.at[e], wo_f32, sem_o))

    @pl.when(i == 0)
    def _():
        for cp in fetch(be_ref[0]):
            cp.start()

    @pl.when((i < nv_ref[0]) & (bf_ref[i] == 1))
    def _():
        for cp in fetch(be_ref[i]):
            cp.wait()
        wi_bf[...] = wi_f32[...].astype(BF16)
        wo_bf[...] = wo_f32[...].astype(BF16)

        @pl.when(nx_ref[i] >= 0)
        def _():
            for cp in fetch(nx_ref[i]):
                cp.start()

    @pl.when(i < nv_ref[0])
    def _():
        xb = _unpack_rows(_load_grouped(xs_ref, TM)).astype(BF16)
        gu = jnp.dot(xb, wi_bf[...], preferred_element_type=F32) + bi_ref[0]
        gate = jnp.minimum(gu[:, :D_E], LIMIT)
        up = jnp.clip(gu[:, D_E:], -LIMIT, LIMIT)
        act = (up + 1.0) * (gate * _sigmoid(ALPHA * gate))
        y = jnp.dot(act.astype(BF16), wo_bf[...], preferred_element_type=F32) + bo_ref[0]
        _store_grouped(ys_ref, _pack_rows(y), TM)


def _moe_call(blk_e, blk_first, blk_next, n_valid, xs, w_in, b_in, w_out, b_out):
    def row_map(i, be, bf, nx, nv):
        return (jnp.maximum(jnp.minimum(i, nv[0] - 1), 0), 0)

    def exp_map(i, be, bf, nx, nv):
        return (be[i], 0, 0)

    grid_spec = pltpu.PrefetchScalarGridSpec(
        num_scalar_prefetch=4,
        grid=(N_BLOCKS,),
        in_specs=[pl.BlockSpec((TM * SUB, LANES), row_map),
                  pl.BlockSpec(memory_space=pl.ANY),
                  pl.BlockSpec((1, 1, 2 * D_E), exp_map),
                  pl.BlockSpec(memory_space=pl.ANY),
                  pl.BlockSpec((1, 1, D), exp_map)],
        out_specs=pl.BlockSpec((TM * SUB, LANES), row_map),
        scratch_shapes=[pltpu.VMEM((D, 2 * D_E), F32),
                        pltpu.VMEM((D_E, D), F32),
                        pltpu.VMEM((D, 2 * D_E), BF16),
                        pltpu.VMEM((D_E, D), BF16),
                        pltpu.SemaphoreType.DMA,
                        pltpu.SemaphoreType.DMA],
    )
    return pl.pallas_call(
        _moe_kernel,
        grid_spec=grid_spec,
        out_shape=jax.ShapeDtypeStruct((N_ROWS * SUB, LANES), U32),
        compiler_params=pltpu.CompilerParams(dimension_semantics=("arbitrary",),
                                             vmem_limit_bytes=VMEM_LIMIT),
        name="moe_experts",
    )(blk_e, blk_first, blk_next, n_valid, xs, w_in, b_in, w_out, b_out)


def _combine_kernel(cnt_ref, off_ref, seg_ref, ys_ref, pc_ref, x1_ref, gate2_ref, fg_ref, o_ref,
                    buf0, buf1, sem0, sem1):
    j = pl.program_id(0)
    head = ys_ref.at[pl.ds(0, RS * SUB)]

    def fetch(tile, buf, sem):
        _start_tile_runs(tile, ys_ref, buf, off_ref, seg_ref, cnt_ref, sem)

    @pl.when(j == 0)
    def _():
        fetch(0, buf0, sem0)

    pc = pc_ref[...]
    pos = pc.astype(jnp.int32)
    col = lax.broadcasted_iota(jnp.int32, (TT, RS), 1)
    wm = jnp.zeros((TT, RS), F32)
    for k in range(TOP_K):
        wm = jnp.where(col == pos[:, k:k + 1], pc[:, TOP_K + k:TOP_K + k + 1], wm)
    wm = wm.astype(BF16)

    def tile(buf, sem, next_buf, next_sem):
        @pl.when(j + 1 < NT)
        def _():
            fetch(j + 1, next_buf, next_sem)
        _wait_tile_runs(head, buf, sem)
        yt = _unpack_rows(_load_grouped(buf, RS)).astype(BF16)
        acc = jnp.dot(wm, yt, preferred_element_type=F32)
        x2 = x1_ref[...] + gate2_ref[0] * acc
        o_ref[...] = _rms(x2) * fg_ref[...]

    @pl.when(j % 2 == 0)
    def _():
        tile(buf0, sem0, buf1, sem1)

    @pl.when(j % 2 == 1)
    def _():
        tile(buf1, sem1, buf0, sem0)


def _combine_call(cnt_t, off_t, seg_t, ys, pos_c, x1, mod3, final_g):
    per_b = S // TT
    grid_spec = pltpu.PrefetchScalarGridSpec(
        num_scalar_prefetch=3,
        grid=(NT,),
        in_specs=[pl.BlockSpec(memory_space=pl.ANY),
                  pl.BlockSpec((TT, N_E), lambda j, *_: (j, 0)),
                  pl.BlockSpec((TT, D), lambda j, *_: (j, 0)),
                  pl.BlockSpec((1, 1, D), lambda j, *_: (j // per_b, 0, 5)),
                  pl.BlockSpec((1, D), lambda j, *_: (0, 0))],
        out_specs=pl.BlockSpec((TT, D), lambda j, *_: (j, 0)),
        scratch_shapes=[pltpu.VMEM((RS * SUB, LANES), U32),
                        pltpu.VMEM((RS * SUB, LANES), U32),
                        pltpu.SemaphoreType.DMA,
                        pltpu.SemaphoreType.DMA],
    )
    return pl.pallas_call(
        _combine_kernel,
        grid_spec=grid_spec,
        out_shape=jax.ShapeDtypeStruct((T, D), F32),
        compiler_params=pltpu.CompilerParams(dimension_semantics=("arbitrary",),
                                             vmem_limit_bytes=VMEM_LIMIT),
        name="combine_norm",
    )(cnt_t, off_t, seg_t, ys, pos_c, x1, mod3, final_g)


def kernel(x, c, ada_w, ada_b, norm1_g, w_in, ssm_a_re, ssm_a_im, ssm_log_dt, ssm_b_re, ssm_b_im, ssm_c_re, ssm_c_im, ssm_d, ssm_glu_w, ssm_glu_b, w_branch_a, gmlp_ln_g, gmlp_ln_b, gmlp_ws, gmlp_bs, w_branch_b, w_out, norm2_g, router_w, router_b, moe_w_in, moe_b_in, moe_w_out, moe_b_out, final_g):
    depth = ada_w.shape[0]
    assert depth == 1, "the final rms_norm is fused into the combine kernel of the only layer"
    for layer in range(depth):
        mod = _mod_call(c, ada_w[layer], ada_b[layer])
        mod3 = mod.reshape(B, 1, 6 * D)

        u, zuv, ga, gb = _proj_call(x, norm1_g[layer], mod3, w_in[layer].astype(BF16))

        bm, cre, cim, are, aim = _s5_params(ssm_a_re[layer], ssm_a_im[layer], ssm_log_dt[layer],
                                            ssm_b_re[layer], ssm_b_im[layer],
                                            ssm_c_re[layer], ssm_c_im[layer])
        ya = _s5_call(u, bm, cre, cim, are, aim,
                      ssm_d[layer].reshape(1, SSM_W), ssm_glu_w[layer].astype(BF16),
                      ssm_glu_b[layer].reshape(1, SSM_W), w_branch_a[layer].astype(BF16))

        ws = gmlp_ws[layer]
        ws_pairs = jnp.concatenate([ws[0::2], ws[1::2]], axis=-1)
        bias_full = jnp.repeat(gmlp_bs[layer].T, GM_HD, axis=1)
        x1, h2, logits = _mix_call(
            zuv, ga, gb, ya, x, mod3,
            gmlp_ln_g[layer].reshape(1, GM_W), gmlp_ln_b[layer].reshape(1, GM_W),
            ws_pairs, bias_full, w_branch_b[layer].astype(BF16), w_out[layer].astype(BF16),
            norm2_g[layer].reshape(1, D), router_w[layer],
            router_b[layer].reshape(1, N_E))

        pos_c, pos_t, cnt_t, pre_t, seg_t, cnt = _route_call(logits.reshape(T, N_E))
        counts = cnt[0].astype(jnp.int32)
        nblk = (counts + TM - 1) // TM
        blk_end = jnp.cumsum(nblk)
        pad_end = (blk_end * TM).astype(jnp.int32)
        experts = jnp.arange(N_E, dtype=jnp.int32)
        blk_ids = jnp.arange(N_BLOCKS, dtype=jnp.int32)
        blk_e = jnp.sum((blk_end[None, :] <= blk_ids[:, None]).astype(jnp.int32), axis=1)
        blk_e = jnp.minimum(blk_e, N_E - 1)
        blk_first = jnp.concatenate([jnp.ones((1,), jnp.int32),
                                     (blk_e[1:] != blk_e[:-1]).astype(jnp.int32)])
        later = (experts[None, :] > experts[:, None]) & (nblk[None, :] > 0)
        next_e = jnp.min(jnp.where(later, experts[None, :], N_E), axis=1)
        next_e = jnp.where(next_e == N_E, -1, next_e)
        blk_next = jnp.sum(jnp.where(blk_e[:, None] == experts[None, :], next_e[None, :], 0), axis=1)
        n_valid = blk_end[-1:].astype(jnp.int32)
        pad_start = pad_end - nblk * TM
        off_t = pre_t[:, 0, :] + pad_start[None, :]
        cnt_t, off_t, seg_t = (t.reshape(NT * N_E) for t in (cnt_t[:, 0, :], off_t, seg_t[:, 0, :]))

        xs = _dispatch_call(cnt_t, off_t, seg_t, pad_end, h2.reshape(T, D), pos_t)
        ys = _moe_call(blk_e, blk_first, blk_next.astype(jnp.int32), n_valid, xs, moe_w_in[layer],
                       moe_b_in[layer].reshape(N_E, 1, 2 * D_E), moe_w_out[layer],
                       moe_b_out[layer].reshape(N_E, 1, D))
        x = _combine_call(cnt_t, off_t, seg_t, ys, pos_c, x1.reshape(T, D),
                          mod3, final_g.reshape(1, D)).reshape(B, S, D)
    return x
```

```python
import functools
import math

import jax
import jax.numpy as jnp
from jax import lax
from jax.experimental import pallas as pl
from jax.experimental.pallas import tpu as pltpu

F32 = jnp.float32
BF16 = jnp.bfloat16

D = 1024
B = 8
S = 2048
T = B * S
SSM_W = 512
SSM_G = 32
SSM_H = 16
SSM_P = 64
N_PACK = 4
PACK_G = SSM_G // N_PACK
GM_W = 512
GM_HEADS = 8
GM_HD = 64
CHUNK = 128
N_E = 32
TOP_K = 4
D_E = 1024
LIMIT = 7.0
ALPHA = 1.702
EPS = 1e-6

TS_PROJ = 512
L_SSM = 64
R_SSM = L_SSM * B
TS_MIX = 512
SUB_MIX = 256
TT = 256
NT = T // TT
RS = TOP_K * TT
TM = 256
N_ROWS = T * TOP_K + N_E * TM
N_BLOCKS = N_ROWS // TM
LANES = 128
HALF = D // 2
SUB = HALF // LANES
PIECE = 16
VMEM_LIMIT = 56 * 1024 * 1024
U32 = jnp.uint32


def _sigmoid(v):
    return 1.0 / (1.0 + jnp.exp(-v))


def _gelu(v):
    return 0.5 * v * (1.0 + jnp.tanh(math.sqrt(2.0 / math.pi) * (v + 0.044715 * v * v * v)))


def _rms(v):
    return v * lax.rsqrt(jnp.mean(v * v, axis=-1, keepdims=True) + EPS)


def _mod_kernel(c_ref, w_ref, b_ref, o_ref):
    cv = c_ref[...]
    sv = cv * _sigmoid(cv)
    o_ref[...] = jnp.dot(sv, w_ref[...], preferred_element_type=F32,
                         precision=lax.Precision.HIGHEST) + b_ref[...]


def _mod_call(c, ada_w, ada_b):
    n = ada_w.shape[1]
    return pl.pallas_call(
        _mod_kernel,
        grid=(n // D,),
        in_specs=[pl.BlockSpec((B, D), lambda j: (0, 0)),
                  pl.BlockSpec((D, D), lambda j: (0, j)),
                  pl.BlockSpec((1, D), lambda j: (0, j))],
        out_specs=pl.BlockSpec((B, D), lambda j: (0, j)),
        out_shape=jax.ShapeDtypeStruct((B, n), F32),
        name="adaln_mod",
    )(c, ada_w, ada_b.reshape(1, n))


def _proj_kernel(x_ref, g_ref, shift_ref, scale_ref, w_ref, u_ref, zuv_ref, ga_ref, gb_ref):
    h = _rms(x_ref[0]) * g_ref[...]
    h = h * (1.0 + scale_ref[0]) + shift_ref[0]
    hb = h.astype(BF16)
    u_ref[0] = jnp.dot(hb, w_ref[:, 0:SSM_W], preferred_element_type=F32)
    zuv_ref[0] = jnp.dot(hb, w_ref[:, SSM_W:SSM_W + 2 * GM_W], preferred_element_type=F32)
    ga_ref[0] = jnp.dot(hb, w_ref[:, SSM_W + 2 * GM_W:SSM_W + 2 * GM_W + D], preferred_element_type=F32)
    gb_ref[0] = jnp.dot(hb, w_ref[:, SSM_W + 2 * GM_W + D:], preferred_element_type=F32)


def _proj_call(x, norm_g, mod3, w_in_bf):
    pw = w_in_bf.shape[1]
    tok_spec = pl.BlockSpec((1, TS_PROJ, D), lambda b, s: (b, s, 0))
    return pl.pallas_call(
        _proj_kernel,
        grid=(B, S // TS_PROJ),
        in_specs=[tok_spec,
                  pl.BlockSpec((1, D), lambda b, s: (0, 0)),
                  pl.BlockSpec((1, 1, D), lambda b, s: (b, 0, 0)),
                  pl.BlockSpec((1, 1, D), lambda b, s: (b, 0, 1)),
                  pl.BlockSpec((D, pw), lambda b, s: (0, 0))],
        out_specs=[pl.BlockSpec((1, TS_PROJ, SSM_W), lambda b, s: (b, s, 0)),
                   tok_spec, tok_spec, tok_spec],
        out_shape=[jax.ShapeDtypeStruct((B, S, SSM_W), F32),
                   jax.ShapeDtypeStruct((B, S, D), F32),
                   jax.ShapeDtypeStruct((B, S, D), F32),
                   jax.ShapeDtypeStruct((B, S, D), F32)],
        compiler_params=pltpu.CompilerParams(vmem_limit_bytes=VMEM_LIMIT),
        name="norm_proj",
    )(x, norm_g.reshape(1, D), mod3, mod3, w_in_bf)


def _s5_kernel(u_ref, bm_ref, cre_ref, cim_ref, are_ref, aim_ref, d_ref, gw_ref, gb_ref, wa_ref,
               o_ref, usc, ysc, sre, sim, st_re, st_im):
    @pl.when(pl.program_id(0) == 0)
    def _():
        st_re[...] = jnp.zeros_like(st_re)
        st_im[...] = jnp.zeros_like(st_im)

    nslab = SSM_W // LANES
    for b in range(B):
        for c in range(nslab):
            usc[c, pl.ds(b, L_SSM, stride=B), :] = u_ref[b, :, c * LANES:(c + 1) * LANES]
    u = jnp.concatenate([usc[c] for c in range(nslab)], axis=1)
    ub = u.astype(BF16)
    half = PACK_G * SSM_P
    ys = []
    for k in range(N_PACK):
        bu = jnp.dot(ub[:, 128 * k:128 * (k + 1)], bm_ref[k], preferred_element_type=F32)
        sre[k] = bu[:, :half]
        sim[k] = bu[:, half:]
        ar = are_ref[k]
        ai = aim_ref[k]
        r = st_re[k]
        m = st_im[k]
        for t in range(L_SSM):
            rows = pl.ds(t * B, B)
            nr = ar * r - ai * m + sre[k, rows, :]
            m = ar * m + ai * r + sim[k, rows, :]
            r = nr
            sre[k, rows, :] = r
            sim[k, rows, :] = m
        st_re[k] = r
        st_im[k] = m
        yk = jnp.dot(sre[k].astype(BF16), cre_ref[k], preferred_element_type=F32)
        yk = yk + jnp.dot(sim[k].astype(BF16), cim_ref[k], preferred_element_type=F32)
        ys.append(yk)
    for c in range(nslab):
        uc = usc[c]
        ysc[c] = ys[c] + d_ref[:, c * LANES:(c + 1) * LANES] * uc
    y = jnp.concatenate(
        [jnp.concatenate([ysc[c, pl.ds(b, L_SSM, stride=B), :] for c in range(nslab)], axis=1)
         for b in range(B)], axis=0)
    z = _gelu(y)
    gl = jnp.dot(z.astype(BF16), gw_ref[...], preferred_element_type=F32) + gb_ref[...]
    out = z * _sigmoid(gl)
    o = jnp.dot(out.astype(BF16), wa_ref[...], preferred_element_type=F32)
    for b in range(B):
        o_ref[b] = o[b * L_SSM:(b + 1) * L_SSM]


def _s5_call(u, bm, cre, cim, are, aim, d_skip, glu_w, glu_b, w_a):
    half = PACK_G * SSM_P
    full = lambda *shape: pl.BlockSpec(shape, lambda i: (0,) * len(shape))
    return pl.pallas_call(
        _s5_kernel,
        grid=(S // L_SSM,),
        in_specs=[pl.BlockSpec((B, L_SSM, SSM_W), lambda i: (0, i, 0)),
                  full(N_PACK, 128, 2 * half),
                  full(N_PACK, half, 128),
                  full(N_PACK, half, 128),
                  full(N_PACK, B, half),
                  full(N_PACK, B, half),
                  full(1, SSM_W),
                  full(SSM_W, SSM_W),
                  full(1, SSM_W),
                  full(SSM_W, D)],
        out_specs=pl.BlockSpec((B, L_SSM, D), lambda i: (0, i, 0)),
        out_shape=jax.ShapeDtypeStruct((B, S, D), F32),
        scratch_shapes=[pltpu.VMEM((SSM_W // LANES, R_SSM, LANES), F32),
                        pltpu.VMEM((SSM_W // LANES, R_SSM, LANES), F32),
                        pltpu.VMEM((N_PACK, R_SSM, half), F32),
                        pltpu.VMEM((N_PACK, R_SSM, half), F32),
                        pltpu.VMEM((N_PACK, B, half), F32),
                        pltpu.VMEM((N_PACK, B, half), F32)],
        compiler_params=pltpu.CompilerParams(dimension_semantics=("arbitrary",),
                                             vmem_limit_bytes=VMEM_LIMIT),
        name="s5_branch",
    )(u, bm, cre, cim, are, aim, d_skip, glu_w, glu_b, w_a)


def _s5_params(a_re, a_im, log_dt, b_re, b_im, c_re, c_im):
    dt = jnp.exp(log_dt)[:, None]
    mag = jnp.exp(a_re * dt)
    lr = mag * jnp.cos(a_im * dt)
    li = mag * jnp.sin(a_im * dt)
    den = a_re * a_re + a_im * a_im
    cr = ((lr - 1.0) * a_re + li * a_im) / den
    ci = (li * a_re - (lr - 1.0) * a_im) / den
    bbr = cr[..., None] * b_re - ci[..., None] * b_im
    bbi = cr[..., None] * b_im + ci[..., None] * b_re
    eye = jnp.eye(PACK_G, dtype=F32)
    half = PACK_G * SSM_P

    def pack_b(m):
        m4 = m.reshape(N_PACK, PACK_G, SSM_P, SSM_H)
        return jnp.einsum('kgph,gj->kghjp', m4, eye).reshape(N_PACK, PACK_G * SSM_H, half)

    def pack_c(m):
        m4 = m.reshape(N_PACK, PACK_G, SSM_H, SSM_P)
        return jnp.einsum('kghp,gj->kgpjh', m4, eye).reshape(N_PACK, half, PACK_G * SSM_H)

    bm = jnp.concatenate([pack_b(bbr), pack_b(bbi)], axis=-1).astype(BF16)
    cre = pack_c(c_re).astype(BF16)
    cim = (-pack_c(c_im)).astype(BF16)
    are = jnp.broadcast_to(lr.reshape(N_PACK, 1, half), (N_PACK, B, half))
    aim = jnp.broadcast_to(li.reshape(N_PACK, 1, half), (N_PACK, B, half))
    return bm, cre, cim, are, aim


def _mix_kernel(zuv_ref, ga_ref, gb_ref, ya_ref, x_ref, gate1_ref, shift2_ref, scale2_ref,
                lng_ref, lnb_ref, ws_ref, bias_ref, wbb_ref, wo_ref, n2g_ref, rw_ref, rb_ref,
                x1_ref, h2_ref, lg_ref):
    row = lax.broadcasted_iota(jnp.int32, (CHUNK, 2 * CHUNK), 0)
    col = lax.broadcasted_iota(jnp.int32, (CHUNK, 2 * CHUNK), 1)
    causal = (col % CHUNK) <= row
    lane = lax.broadcasted_iota(jnp.int32, (CHUNK, 2 * GM_HD), 1)
    first = lane < GM_HD
    wpairs = [jnp.where(causal, ws_ref[j], 0.0).astype(BF16) for j in range(GM_HEADS // 2)]

    for g in range(TS_MIX // SUB_MIX):
        rows = pl.ds(g * SUB_MIX, SUB_MIX)
        z = _gelu(zuv_ref[0, rows, :])
        u = z[:, :GM_W]
        v = z[:, GM_W:]
        mu = jnp.mean(v, axis=-1, keepdims=True)
        vc = v - mu
        var = jnp.mean(vc * vc, axis=-1, keepdims=True)
        vn = vc * lax.rsqrt(var + EPS) * lng_ref[...] + lnb_ref[...]
        chunks = []
        for n in range(SUB_MIX // CHUNK):
            cols = []
            for j in range(GM_HEADS // 2):
                vp = vn[n * CHUNK:(n + 1) * CHUNK, 2 * GM_HD * j:2 * GM_HD * (j + 1)]
                rhs = jnp.concatenate([jnp.where(first, vp, 0.0), jnp.where(first, 0.0, vp)], axis=0)
                cols.append(jnp.dot(wpairs[j], rhs.astype(BF16), preferred_element_type=F32))
            chunks.append(jnp.concatenate(cols, axis=1) + bias_ref[...])
        mixed = jnp.concatenate(chunks, axis=0)
        gm = u * mixed
        yb = jnp.dot(gm.astype(BF16), wbb_ref[...], preferred_element_type=F32)
        merged = _sigmoid(ga_ref[0, rows, :]) * ya_ref[0, rows, :] + _sigmoid(gb_ref[0, rows, :]) * yb
        o = jnp.dot(merged.astype(BF16), wo_ref[...], preferred_element_type=F32)
        x1 = x_ref[0, rows, :] + gate1_ref[0] * o
        x1_ref[0, rows, :] = x1
        h2 = _rms(x1) * n2g_ref[...]
        h2 = h2 * (1.0 + scale2_ref[0]) + shift2_ref[0]
        h2_ref[0, rows, :] = h2.astype(BF16)
        lg_ref[0, rows, :] = jnp.dot(h2, rw_ref[...], preferred_element_type=F32,
                                     precision=lax.Precision.HIGHEST) + rb_ref[...]


def _mix_call(zuv, ga, gb, ya2d, x, mod3, ln_g, ln_b, ws_pairs, bias_full, wbb, wo, n2g, rw, rb):
    tok_spec = pl.BlockSpec((1, TS_MIX, D), lambda b, s: (b, s, 0))
    full = lambda *shape: pl.BlockSpec(shape, lambda b, s: (0,) * len(shape))
    mod_spec = lambda j: pl.BlockSpec((1, 1, D), lambda b, s: (b, 0, j))
    return pl.pallas_call(
        _mix_kernel,
        grid=(B, S // TS_MIX),
        in_specs=[tok_spec, tok_spec, tok_spec, tok_spec, tok_spec,
                  mod_spec(2), mod_spec(3), mod_spec(4),
                  full(1, GM_W), full(1, GM_W),
                  full(GM_HEADS // 2, CHUNK, 2 * CHUNK),
                  full(CHUNK, GM_W),
                  full(GM_W, D), full(D, D), full(1, D),
                  full(D, N_E), full(1, N_E)],
        out_specs=[tok_spec, tok_spec,
                   pl.BlockSpec((1, TS_MIX, N_E), lambda b, s: (b, s, 0))],
        out_shape=[jax.ShapeDtypeStruct((B, S, D), F32),
                   jax.ShapeDtypeStruct((B, S, D), BF16),
                   jax.ShapeDtypeStruct((B, S, N_E), F32)],
        compiler_params=pltpu.CompilerParams(vmem_limit_bytes=VMEM_LIMIT),
        name="gmlp_merge_norm2",
    )(zuv, ga, gb, ya2d, x, mod3, mod3, mod3, ln_g, ln_b, ws_pairs, bias_full, wbb, wo, n2g, rw, rb)


def _route_kernel(lg_ref, pc_ref, pt_ref, cntt_ref, pret_ref, segt_ref, cnt_ref, carry):
    i = pl.program_id(0)
    l = lg_ref[...]
    lane = lax.broadcasted_iota(jnp.int32, l.shape, 1).astype(F32)
    sels, vals = [], []
    for _ in range(TOP_K):
        m = jnp.max(l, axis=-1, keepdims=True)
        idx = jnp.min(jnp.where(l == m, lane, float(N_E)), axis=-1, keepdims=True)
        sel = lane == idx
        sels.append(sel)
        vals.append(m)
        l = jnp.where(sel, -jnp.inf, l)
    member = sels[0].astype(F32)
    for k in range(1, TOP_K):
        member = member + sels[k].astype(F32)
    tile_cnt = jnp.sum(member, axis=0, keepdims=True)

    @pl.when(i == 0)
    def _():
        carry[...] = jnp.zeros_like(carry)

    r = lax.broadcasted_iota(jnp.int32, (N_E, N_E), 0)
    c = lax.broadcasted_iota(jnp.int32, (N_E, N_E), 1)
    tri = (r < c).astype(BF16)
    tcb = jnp.broadcast_to(tile_cnt, (8, N_E)).astype(BF16)
    seg = jnp.dot(tcb, tri, preferred_element_type=F32)[0:1]
    r = lax.broadcasted_iota(jnp.int32, (TT, TT), 0)
    c = lax.broadcasted_iota(jnp.int32, (TT, TT), 1)
    strict = (c < r).astype(BF16)
    rank = jnp.dot(strict, member.astype(BF16), preferred_element_type=F32)
    posb = seg + rank
    denom = jnp.zeros_like(vals[0])
    exps = []
    for k in range(TOP_K):
        e = jnp.exp(vals[k] - vals[0])
        exps.append(e)
        denom = denom + e
    pc = jnp.zeros(l.shape, F32)
    for k in range(TOP_K):
        pk = jnp.sum(jnp.where(sels[k], posb, 0.0), axis=-1, keepdims=True)
        pc = jnp.where(lane == float(k), pk, pc)
        pc = jnp.where(lane == float(TOP_K + k), exps[k] / denom, pc)
    pc_ref[...] = pc
    r8 = lax.broadcasted_iota(jnp.int32, (8, N_E), 0)
    c8 = lax.broadcasted_iota(jnp.int32, (8, N_E), 1)
    eye = (r8 == c8).astype(F32)
    pt_ref[...] = lax.dot_general(eye, pc, (((1,), (1,)), ((), ())),
                                  preferred_element_type=F32, precision=lax.Precision.HIGHEST)
    cntt_ref[0] = jnp.broadcast_to(tile_cnt, (8, N_E)).astype(jnp.int32)
    pret_ref[0] = jnp.broadcast_to(carry[...], (8, N_E)).astype(jnp.int32)
    segt_ref[0] = jnp.broadcast_to(seg, (8, N_E)).astype(jnp.int32)
    carry[...] += tile_cnt
    cnt_ref[...] = jnp.broadcast_to(carry[...], cnt_ref.shape)


def _route_call(logits):
    tbl_spec = pl.BlockSpec((1, 8, N_E), lambda i: (i, 0, 0))
    tbl_shape = jax.ShapeDtypeStruct((NT, 8, N_E), jnp.int32)
    return pl.pallas_call(
        _route_kernel,
        grid=(NT,),
        in_specs=[pl.BlockSpec((TT, N_E), lambda i: (i, 0))],
        out_specs=[pl.BlockSpec((TT, N_E), lambda i: (i, 0)),
                   pl.BlockSpec((8, TT), lambda i: (0, i)),
                   tbl_spec, tbl_spec, tbl_spec,
                   pl.BlockSpec((8, N_E), lambda i: (0, 0))],
        out_shape=[jax.ShapeDtypeStruct((T, N_E), F32),
                   jax.ShapeDtypeStruct((8, T), F32),
                   tbl_shape, tbl_shape, tbl_shape,
                   jax.ShapeDtypeStruct((8, N_E), F32)],
        scratch_shapes=[pltpu.VMEM((1, N_E), F32)],
        compiler_params=pltpu.CompilerParams(dimension_semantics=("arbitrary",)),
        name="route",
    )(logits)


def _pack_rows(v):
    return pltpu.pack_elementwise([v[:, :HALF], v[:, HALF:]], packed_dtype=BF16)


def _unpack_rows(w):
    halves = [pltpu.unpack_elementwise(w, index=i, packed_dtype=BF16, unpacked_dtype=F32)
              for i in range(2)]
    return jnp.concatenate(halves, axis=1)


def _load_grouped(ref, rows):
    return jnp.concatenate([ref[pl.ds(c, rows, stride=SUB), :] for c in range(SUB)], axis=1)


def _store_grouped(ref, w, rows):
    for c in range(SUB):
        ref[pl.ds(c, rows, stride=SUB), :] = w[:, c * LANES:(c + 1) * LANES]


def _start_run(src_ref, dst_ref, src_row, dst_row, n, sem):
    def start(rows, done):
        s = pl.multiple_of((src_row + done) * SUB, SUB)
        d = pl.multiple_of((dst_row + done) * SUB, SUB)
        pltpu.make_async_copy(src_ref.at[pl.ds(s, rows * SUB)],
                              dst_ref.at[pl.ds(d, rows * SUB)], sem).start()

    nbulk = jnp.right_shift(n, PIECE.bit_length() - 1)

    def bulk(pi, carry):
        start(PIECE, pi * PIECE)
        return carry

    lax.fori_loop(0, nbulk, bulk, 0)
    done = nbulk * PIECE
    for rows in [PIECE >> s for s in range(1, PIECE.bit_length())]:
        @pl.when((n & rows) != 0)
        def _(rows=rows, done=done):
            start(rows, done)
        done = done + (n & rows)


def _start_tile_runs(tile, src_ref, dst_ref, src_tbl, dst_tbl, cnt_tbl, sem):
    base = tile * N_E

    def per_expert(e, carry):
        _start_run(src_ref, dst_ref, src_tbl[base + e], dst_tbl[base + e], cnt_tbl[base + e], sem)
        return carry

    lax.fori_loop(0, N_E, per_expert, 0)


def _wait_tile_runs(src_ref, dst_ref, sem):
    pltpu.make_async_copy(src_ref, dst_ref, sem).wait()


def _dispatch_kernel(cnt_ref, off_ref, seg_ref, pend_ref, h_ref, pt_ref, xs_ref,
                     sbuf0, sbuf1, zbuf, sem_z, sem0, sem1):
    j = pl.program_id(0)

    @pl.when(j == 0)
    def _():
        zbuf[...] = jnp.zeros_like(zbuf)
        for e in range(N_E):
            prev = pend_ref[e - 1] if e > 0 else 0
            end = pend_ref[e]

            @pl.when(end > prev)
            def _():
                first = pl.multiple_of((end - TM) * SUB, TM * SUB)
                cp = pltpu.make_async_copy(zbuf, xs_ref.at[pl.ds(first, TM * SUB)], sem_z)
                cp.start()
                cp.wait()

    rows = lax.broadcasted_iota(jnp.int32, (RS, TT), 0)
    pos = pt_ref[...].astype(jnp.int32)
    hit = rows == pos[0:1, :]
    for k in range(1, TOP_K):
        hit = hit | (rows == pos[k:k + 1, :])
    pm = jnp.where(hit, 1.0, 0.0).astype(BF16)
    srt = jnp.dot(pm, h_ref[...], preferred_element_type=F32)
    words = _pack_rows(srt)
    head = xs_ref.at[pl.ds(0, RS * SUB)]

    def emit(sbuf, sem):
        @pl.when(j >= 2)
        def _():
            _wait_tile_runs(sbuf, head, sem)
        _store_grouped(sbuf, words, RS)
        _start_tile_runs(j, sbuf, xs_ref, seg_ref, off_ref, cnt_ref, sem)

    @pl.when(j % 2 == 0)
    def _():
        emit(sbuf0, sem0)

    @pl.when(j % 2 == 1)
    def _():
        emit(sbuf1, sem1)

    @pl.when(j == NT - 1)
    def _():
        _wait_tile_runs(sbuf0, head, sem0)
        _wait_tile_runs(sbuf1, head, sem1)


def _dispatch_call(cnt_t, off_t, seg_t, pad_end, h2, pos_t):
    assert NT % 2 == 0 and NT >= 2
    grid_spec = pltpu.PrefetchScalarGridSpec(
        num_scalar_prefetch=4,
        grid=(NT,),
        in_specs=[pl.BlockSpec((TT, D), lambda j, *_: (j, 0)),
                  pl.BlockSpec((8, TT), lambda j, *_: (0, j))],
        out_specs=pl.BlockSpec(memory_space=pl.ANY),
        scratch_shapes=[pltpu.VMEM((RS * SUB, LANES), U32),
                        pltpu.VMEM((RS * SUB, LANES), U32),
                        pltpu.VMEM((TM * SUB, LANES), U32),
                        pltpu.SemaphoreType.DMA,
                        pltpu.SemaphoreType.DMA,
                        pltpu.SemaphoreType.DMA],
    )
    return pl.pallas_call(
        _dispatch_kernel,
        grid_spec=grid_spec,
        out_shape=jax.ShapeDtypeStruct((N_ROWS * SUB, LANES), U32),
        compiler_params=pltpu.CompilerParams(dimension_semantics=("arbitrary",),
                                             vmem_limit_bytes=VMEM_LIMIT),
        name="dispatch",
    )(cnt_t, off_t, seg_t, pad_end, h2, pos_t)


def _moe_kernel(be_ref, bf_ref, nx_ref, nv_ref, xs_ref, wi_hbm, bi_ref, wo_hbm, bo_ref, ys_ref,
                wi_f32, wo_f32, wi_bf, wo_bf, sem_i, sem_o):
    i = pl.program_id(0)

    def fetch(e):
        return (pltpu.make_async_copy(wi_hbm.at[e], wi_f32, sem_i),
                pltpu.make_async_copy(wo_hbm.at[e], wo_f32, sem_o))

    @pl.when(i == 0)
    def _():
        for cp in fetch(be_ref[0]):
            cp.start()

    @pl.when((i < nv_ref[0]) & (bf_ref[i] == 1))
    def _():
        for cp in fetch(be_ref[i]):
            cp.wait()
        wi_bf[...] = wi_f32[...].astype(BF16)
        wo_bf[...] = wo_f32[...].astype(BF16)

        @pl.when(nx_ref[i] >= 0)
        def _():
            for cp in fetch(nx_ref[i]):
                cp.start()

    @pl.when(i < nv_ref[0])
    def _():
        e = be_ref[i]
        xb = _unpack_rows(_load_grouped(xs_ref, TM)).astype(BF16)
        gu = jnp.dot(xb, wi_bf[...], preferred_element_type=F32) + bi_ref[pl.ds(e, 1), :]
        gate = jnp.minimum(gu[:, :D_E], LIMIT)
        up = jnp.clip(gu[:, D_E:], -LIMIT, LIMIT)
        act = (up + 1.0) * (gate * _sigmoid(ALPHA * gate))
        y = jnp.dot(act.astype(BF16), wo_bf[...], preferred_element_type=F32) + bo_ref[pl.ds(e, 1), :]
        _store_grouped(ys_ref, _pack_rows(y), TM)


def _moe_call(blk_e, blk_first, blk_next, n_valid, xs, w_in, b_in, w_out, b_out):
    def row_map(i, be, bf, nx, nv):
        return (jnp.maximum(jnp.minimum(i, nv[0] - 1), 0), 0)

    grid_spec = pltpu.PrefetchScalarGridSpec(
        num_scalar_prefetch=4,
        grid=(N_BLOCKS,),
        in_specs=[pl.BlockSpec((TM * SUB, LANES), row_map),
                  pl.BlockSpec(memory_space=pl.ANY),
                  pl.BlockSpec((N_E, 2 * D_E), lambda i, *_: (0, 0)),
                  pl.BlockSpec(memory_space=pl.ANY),
                  pl.BlockSpec((N_E, D), lambda i, *_: (0, 0))],
        out_specs=pl.BlockSpec((TM * SUB, LANES), row_map),
        scratch_shapes=[pltpu.VMEM((D, 2 * D_E), F32),
                        pltpu.VMEM((D_E, D), F32),
                        pltpu.VMEM((D, 2 * D_E), BF16),
                        pltpu.VMEM((D_E, D), BF16),
                        pltpu.SemaphoreType.DMA,
                        pltpu.SemaphoreType.DMA],
    )
    return pl.pallas_call(
        _moe_kernel,
        grid_spec=grid_spec,
        out_shape=jax.ShapeDtypeStruct((N_ROWS * SUB, LANES), U32),
        compiler_params=pltpu.CompilerParams(dimension_semantics=("arbitrary",),
                                             vmem_limit_bytes=VMEM_LIMIT),
        name="moe_experts",
    )(blk_e, blk_first, blk_next, n_valid, xs, w_in, b_in, w_out, b_out)


def _combine_kernel(cnt_ref, off_ref, seg_ref, ys_ref, pc_ref, x1_ref, gate2_ref, fg_ref, o_ref,
                    buf0, buf1, sem0, sem1):
    j = pl.program_id(0)
    head = ys_ref.at[pl.ds(0, RS * SUB)]

    def fetch(tile, buf, sem):
        _start_tile_runs(tile, ys_ref, buf, off_ref, seg_ref, cnt_ref, sem)

    @pl.when(j == 0)
    def _():
        fetch(0, buf0, sem0)

    pc = pc_ref[...]
    pos = pc.astype(jnp.int32)
    col = lax.broadcasted_iota(jnp.int32, (TT, RS), 1)
    wm = jnp.zeros((TT, RS), F32)
    for k in range(TOP_K):
        wm = jnp.where(col == pos[:, k:k + 1], pc[:, TOP_K + k:TOP_K + k + 1], wm)
    wm = wm.astype(BF16)

    def tile(buf, sem, next_buf, next_sem):
        @pl.when(j + 1 < NT)
        def _():
            fetch(j + 1, next_buf, next_sem)
        _wait_tile_runs(head, buf, sem)
        yt = _unpack_rows(_load_grouped(buf, RS)).astype(BF16)
        acc = jnp.dot(wm, yt, preferred_element_type=F32)
        x2 = x1_ref[...] + gate2_ref[0] * acc
        o_ref[...] = _rms(x2) * fg_ref[...]

    @pl.when(j % 2 == 0)
    def _():
        tile(buf0, sem0, buf1, sem1)

    @pl.when(j % 2 == 1)
    def _():
        tile(buf1, sem1, buf0, sem0)


def _combine_call(cnt_t, off_t, seg_t, ys, pos_c, x1, mod3, final_g):
    per_b = S // TT
    grid_spec = pltpu.PrefetchScalarGridSpec(
        num_scalar_prefetch=3,
        grid=(NT,),
        in_specs=[pl.BlockSpec(memory_space=pl.ANY),
                  pl.BlockSpec((TT, N_E), lambda j, *_: (j, 0)),
                  pl.BlockSpec((TT, D), lambda j, *_: (j, 0)),
                  pl.BlockSpec((1, 1, D), lambda j, *_: (j // per_b, 0, 5)),
                  pl.BlockSpec((1, D), lambda j, *_: (0, 0))],
        out_specs=pl.BlockSpec((TT, D), lambda j, *_: (j, 0)),
        scratch_shapes=[pltpu.VMEM((RS * SUB, LANES), U32),
                        pltpu.VMEM((RS * SUB, LANES), U32),
                        pltpu.SemaphoreType.DMA,
                        pltpu.SemaphoreType.DMA],
    )
    return pl.pallas_call(
        _combine_kernel,
        grid_spec=grid_spec,
        out_shape=jax.ShapeDtypeStruct((T, D), F32),
        compiler_params=pltpu.CompilerParams(dimension_semantics=("arbitrary",),
                                             vmem_limit_bytes=VMEM_LIMIT),
        name="combine_norm",
    )(cnt_t, off_t, seg_t, ys, pos_c, x1, mod3, final_g)


def kernel(x, c, ada_w, ada_b, norm1_g, w_in, ssm_a_re, ssm_a_im, ssm_log_dt, ssm_b_re, ssm_b_im, ssm_c_re, ssm_c_im, ssm_d, ssm_glu_w, ssm_glu_b, w_branch_a, gmlp_ln_g, gmlp_ln_b, gmlp_ws, gmlp_bs, w_branch_b, w_out, norm2_g, router_w, router_b, moe_w_in, moe_b_in, moe_w_out, moe_b_out, final_g):
    depth = ada_w.shape[0]
    assert depth == 1, "the final rms_norm is fused into the combine kernel of the only layer"
    for layer in range(depth):
        mod = _mod_call(c, ada_w[layer], ada_b[layer])
        mod3 = mod.reshape(B, 1, 6 * D)

        u, zuv, ga, gb = _proj_call(x, norm1_g[layer], mod3, w_in[layer].astype(BF16))

        bm, cre, cim, are, aim = _s5_params(ssm_a_re[layer], ssm_a_im[layer], ssm_log_dt[layer],
                                            ssm_b_re[layer], ssm_b_im[layer],
                                            ssm_c_re[layer], ssm_c_im[layer])
        ya = _s5_call(u, bm, cre, cim, are, aim,
                      ssm_d[layer].reshape(1, SSM_W), ssm_glu_w[layer].astype(BF16),
                      ssm_glu_b[layer].reshape(1, SSM_W), w_branch_a[layer].astype(BF16))

        ws = gmlp_ws[layer]
        ws_pairs = jnp.concatenate([ws[0::2], ws[1::2]], axis=-1)
        bias_full = jnp.repeat(gmlp_bs[layer].T, GM_HD, axis=1)
        x1, h2, logits = _mix_call(
            zuv, ga, gb, ya, x, mod3,
            gmlp_ln_g[layer].reshape(1, GM_W), gmlp_ln_b[layer].reshape(1, GM_W),
            ws_pairs, bias_full, w_branch_b[layer].astype(BF16), w_out[layer].astype(BF16),
            norm2_g[layer].reshape(1, D), router_w[layer],
            router_b[layer].reshape(1, N_E))

        pos_c, pos_t, cnt_t, pre_t, seg_t, cnt = _route_call(logits.reshape(T, N_E))
        counts = cnt[0].astype(jnp.int32)
        nblk = (counts + TM - 1) // TM
        blk_end = jnp.cumsum(nblk)
        pad_end = (blk_end * TM).astype(jnp.int32)
        experts = jnp.arange(N_E, dtype=jnp.int32)
        blk_ids = jnp.arange(N_BLOCKS, dtype=jnp.int32)
        blk_e = jnp.sum((blk_end[None, :] <= blk_ids[:, None]).astype(jnp.int32), axis=1)
        blk_e = jnp.minimum(blk_e, N_E - 1)
        blk_first = jnp.concatenate([jnp.ones((1,), jnp.int32),
                                     (blk_e[1:] != blk_e[:-1]).astype(jnp.int32)])
        later = (experts[None, :] > experts[:, None]) & (nblk[None, :] > 0)
        next_e = jnp.min(jnp.where(later, experts[None, :], N_E), axis=1)
        next_e = jnp.where(next_e == N_E, -1, next_e)
        blk_next = jnp.sum(jnp.where(blk_e[:, None] == experts[None, :], next_e[None, :], 0), axis=1)
        n_valid = blk_end[-1:].astype(jnp.int32)
        pad_start = pad_end - nblk * TM
        off_t = pre_t[:, 0, :] + pad_start[None, :]
        cnt_t, off_t, seg_t = (t.reshape(NT * N_E) for t in (cnt_t[:, 0, :], off_t, seg_t[:, 0, :]))

        xs = _dispatch_call(cnt_t, off_t, seg_t, pad_end, h2.reshape(T, D), pos_t)
        ys = _moe_call(blk_e, blk_first, blk_next.astype(jnp.int32), n_valid, xs, moe_w_in[layer],
                       moe_b_in[layer], moe_w_out[layer], moe_b_out[layer])
        x = _combine_call(cnt_t, off_t, seg_t, ys, pos_c, x1.reshape(T, D),
                          mod3, final_g.reshape(1, D)).reshape(B, S, D)
    return x
```

```python
import functools
import math

import jax
import jax.numpy as jnp
from jax import lax
from jax.experimental import pallas as pl
from jax.experimental.pallas import tpu as pltpu

F32 = jnp.float32
BF16 = jnp.bfloat16

D = 1024
B = 8
S = 2048
T = B * S
SSM_W = 512
SSM_G = 32
SSM_H = 16
SSM_P = 64
N_PACK = 4
PACK_G = SSM_G // N_PACK
GM_W = 512
GM_HEADS = 8
GM_HD = 64
CHUNK = 128
N_E = 32
TOP_K = 4
D_E = 1024
LIMIT = 7.0
ALPHA = 1.702
EPS = 1e-6

TS_PROJ = 512
L_SSM = 64
R_SSM = L_SSM * B
TS_MIX = 512
SUB_MIX = 256
TT = 256
NT = T // TT
ROUTE_TILES = 4
RS = TOP_K * TT
TM = 256
BPS = 4
N_ROWS = T * TOP_K + N_E * TM
N_BLOCKS = N_ROWS // TM
LANES = 128
HALF = D // 2
SUB = HALF // LANES
PIECE = 16
VMEM_LIMIT = 56 * 1024 * 1024
U32 = jnp.uint32


def _sigmoid(v):
    return 1.0 / (1.0 + jnp.exp(-v))


def _gelu(v):
    return 0.5 * v * (1.0 + jnp.tanh(math.sqrt(2.0 / math.pi) * (v + 0.044715 * v * v * v)))


def _rms(v):
    return v * lax.rsqrt(jnp.mean(v * v, axis=-1, keepdims=True) + EPS)


def _mod_kernel(c_ref, w_ref, b_ref, o_ref):
    cv = c_ref[...]
    sv = cv * _sigmoid(cv)
    o_ref[...] = jnp.dot(sv, w_ref[...], preferred_element_type=F32,
                         precision=lax.Precision.HIGHEST) + b_ref[...]


def _mod_call(c, ada_w, ada_b):
    n = ada_w.shape[1]
    return pl.pallas_call(
        _mod_kernel,
        grid=(n // D,),
        in_specs=[pl.BlockSpec((B, D), lambda j: (0, 0)),
                  pl.BlockSpec((D, D), lambda j: (0, j)),
                  pl.BlockSpec((1, D), lambda j: (0, j))],
        out_specs=pl.BlockSpec((B, D), lambda j: (0, j)),
        out_shape=jax.ShapeDtypeStruct((B, n), F32),
        name="adaln_mod",
    )(c, ada_w, ada_b.reshape(1, n))


def _proj_kernel(x_ref, g_ref, shift_ref, scale_ref, w_ref, u_ref, zuv_ref, ga_ref, gb_ref):
    h = _rms(x_ref[0]) * g_ref[...]
    h = h * (1.0 + scale_ref[0]) + shift_ref[0]
    hb = h.astype(BF16)
    u_ref[0] = jnp.dot(hb, w_ref[:, 0:SSM_W], preferred_element_type=F32)
    zuv_ref[0] = jnp.dot(hb, w_ref[:, SSM_W:SSM_W + 2 * GM_W], preferred_element_type=F32)
    ga_ref[0] = jnp.dot(hb, w_ref[:, SSM_W + 2 * GM_W:SSM_W + 2 * GM_W + D], preferred_element_type=F32)
    gb_ref[0] = jnp.dot(hb, w_ref[:, SSM_W + 2 * GM_W + D:], preferred_element_type=F32)


def _proj_call(x, norm_g, mod3, w_in_bf):
    pw = w_in_bf.shape[1]
    tok_spec = pl.BlockSpec((1, TS_PROJ, D), lambda b, s: (b, s, 0))
    return pl.pallas_call(
        _proj_kernel,
        grid=(B, S // TS_PROJ),
        in_specs=[tok_spec,
                  pl.BlockSpec((1, D), lambda b, s: (0, 0)),
                  pl.BlockSpec((1, 1, D), lambda b, s: (b, 0, 0)),
                  pl.BlockSpec((1, 1, D), lambda b, s: (b, 0, 1)),
                  pl.BlockSpec((D, pw), lambda b, s: (0, 0))],
        out_specs=[pl.BlockSpec((1, TS_PROJ, SSM_W), lambda b, s: (b, s, 0)),
                   tok_spec, tok_spec, tok_spec],
        out_shape=[jax.ShapeDtypeStruct((B, S, SSM_W), F32),
                   jax.ShapeDtypeStruct((B, S, D), F32),
                   jax.ShapeDtypeStruct((B, S, D), F32),
                   jax.ShapeDtypeStruct((B, S, D), F32)],
        compiler_params=pltpu.CompilerParams(vmem_limit_bytes=VMEM_LIMIT),
        name="norm_proj",
    )(x, norm_g.reshape(1, D), mod3, mod3, w_in_bf)


def _s5_kernel(u_ref, bm_ref, cre_ref, cim_ref, are_ref, aim_ref, d_ref, gw_ref, gb_ref, wa_ref,
               o_ref, usc, ysc, sre, sim, st_re, st_im):
    @pl.when(pl.program_id(0) == 0)
    def _():
        st_re[...] = jnp.zeros_like(st_re)
        st_im[...] = jnp.zeros_like(st_im)

    nslab = SSM_W // LANES
    for b in range(B):
        for c in range(nslab):
            usc[c, pl.ds(b, L_SSM, stride=B), :] = u_ref[b, :, c * LANES:(c + 1) * LANES]
    u = jnp.concatenate([usc[c] for c in range(nslab)], axis=1)
    ub = u.astype(BF16)
    half = PACK_G * SSM_P
    ys = []
    for k in range(N_PACK):
        bu = jnp.dot(ub[:, 128 * k:128 * (k + 1)], bm_ref[k], preferred_element_type=F32)
        sre[k] = bu[:, :half]
        sim[k] = bu[:, half:]
        ar = are_ref[k]
        ai = aim_ref[k]
        r = st_re[k]
        m = st_im[k]
        for t in range(L_SSM):
            rows = pl.ds(t * B, B)
            nr = ar * r - ai * m + sre[k, rows, :]
            m = ar * m + ai * r + sim[k, rows, :]
            r = nr
            sre[k, rows, :] = r
            sim[k, rows, :] = m
        st_re[k] = r
        st_im[k] = m
        yk = jnp.dot(sre[k].astype(BF16), cre_ref[k], preferred_element_type=F32)
        yk = yk + jnp.dot(sim[k].astype(BF16), cim_ref[k], preferred_element_type=F32)
        ys.append(yk)
    for c in range(nslab):
        uc = usc[c]
        ysc[c] = ys[c] + d_ref[:, c * LANES:(c + 1) * LANES] * uc
    y = jnp.concatenate(
        [jnp.concatenate([ysc[c, pl.ds(b, L_SSM, stride=B), :] for c in range(nslab)], axis=1)
         for b in range(B)], axis=0)
    z = _gelu(y)
    gl = jnp.dot(z.astype(BF16), gw_ref[...], preferred_element_type=F32) + gb_ref[...]
    out = z * _sigmoid(gl)
    o = jnp.dot(out.astype(BF16), wa_ref[...], preferred_element_type=F32)
    for b in range(B):
        o_ref[b] = o[b * L_SSM:(b + 1) * L_SSM]


def _s5_call(u, bm, cre, cim, are, aim, d_skip, glu_w, glu_b, w_a):
    half = PACK_G * SSM_P
    full = lambda *shape: pl.BlockSpec(shape, lambda i: (0,) * len(shape))
    return pl.pallas_call(
        _s5_kernel,
        grid=(S // L_SSM,),
        in_specs=[pl.BlockSpec((B, L_SSM, SSM_W), lambda i: (0, i, 0)),
                  full(N_PACK, 128, 2 * half),
                  full(N_PACK, half, 128),
                  full(N_PACK, half, 128),
                  full(N_PACK, B, half),
                  full(N_PACK, B, half),
                  full(1, SSM_W),
                  full(SSM_W, SSM_W),
                  full(1, SSM_W),
                  full(SSM_W, D)],
        out_specs=pl.BlockSpec((B, L_SSM, D), lambda i: (0, i, 0)),
        out_shape=jax.ShapeDtypeStruct((B, S, D), F32),
        scratch_shapes=[pltpu.VMEM((SSM_W // LANES, R_SSM, LANES), F32),
                        pltpu.VMEM((SSM_W // LANES, R_SSM, LANES), F32),
                        pltpu.VMEM((N_PACK, R_SSM, half), F32),
                        pltpu.VMEM((N_PACK, R_SSM, half), F32),
                        pltpu.VMEM((N_PACK, B, half), F32),
                        pltpu.VMEM((N_PACK, B, half), F32)],
        compiler_params=pltpu.CompilerParams(dimension_semantics=("arbitrary",),
                                             vmem_limit_bytes=VMEM_LIMIT),
        name="s5_branch",
    )(u, bm, cre, cim, are, aim, d_skip, glu_w, glu_b, w_a)


def _s5_params(a_re, a_im, log_dt, b_re, b_im, c_re, c_im):
    dt = jnp.exp(log_dt)[:, None]
    mag = jnp.exp(a_re * dt)
    lr = mag * jnp.cos(a_im * dt)
    li = mag * jnp.sin(a_im * dt)
    den = a_re * a_re + a_im * a_im
    cr = ((lr - 1.0) * a_re + li * a_im) / den
    ci = (li * a_re - (lr - 1.0) * a_im) / den
    bbr = cr[..., None] * b_re - ci[..., None] * b_im
    bbi = cr[..., None] * b_im + ci[..., None] * b_re
    eye = jnp.eye(PACK_G, dtype=F32)
    half = PACK_G * SSM_P

    def pack_b(m):
        m4 = m.reshape(N_PACK, PACK_G, SSM_P, SSM_H)
        return jnp.einsum('kgph,gj->kghjp', m4, eye).reshape(N_PACK, PACK_G * SSM_H, half)

    def pack_c(m):
        m4 = m.reshape(N_PACK, PACK_G, SSM_H, SSM_P)
        return jnp.einsum('kghp,gj->kgpjh', m4, eye).reshape(N_PACK, half, PACK_G * SSM_H)

    bm = jnp.concatenate([pack_b(bbr), pack_b(bbi)], axis=-1).astype(BF16)
    cre = pack_c(c_re).astype(BF16)
    cim = (-pack_c(c_im)).astype(BF16)
    are = jnp.broadcast_to(lr.reshape(N_PACK, 1, half), (N_PACK, B, half))
    aim = jnp.broadcast_to(li.reshape(N_PACK, 1, half), (N_PACK, B, half))
    return bm, cre, cim, are, aim


def _route_tile(l, carry):
    lane = lax.broadcasted_iota(jnp.int32, l.shape, 1).astype(F32)
    sels, vals = [], []
    for _ in range(TOP_K):
        m = jnp.max(l, axis=-1, keepdims=True)
        idx = jnp.min(jnp.where(l == m, lane, float(N_E)), axis=-1, keepdims=True)
        sel = lane == idx
        sels.append(sel)
        vals.append(m)
        l = jnp.where(sel, -jnp.inf, l)
    member = sels[0].astype(F32)
    for k in range(1, TOP_K):
        member = member + sels[k].astype(F32)
    tile_cnt = jnp.sum(member, axis=0, keepdims=True)

    r = lax.broadcasted_iota(jnp.int32, (N_E, N_E), 0)
    c = lax.broadcasted_iota(jnp.int32, (N_E, N_E), 1)
    tri = (r < c).astype(BF16)
    tcb = jnp.broadcast_to(tile_cnt, (8, N_E)).astype(BF16)
    seg = jnp.dot(tcb, tri, preferred_element_type=F32)[0:1]
    r = lax.broadcasted_iota(jnp.int32, (TT, TT), 0)
    c = lax.broadcasted_iota(jnp.int32, (TT, TT), 1)
    strict = (c < r).astype(BF16)
    rank = jnp.dot(strict, member.astype(BF16), preferred_element_type=F32)
    posb = seg + rank
    denom = jnp.zeros_like(vals[0])
    exps = []
    for k in range(TOP_K):
        e = jnp.exp(vals[k] - vals[0])
        exps.append(e)
        denom = denom + e
    pc = jnp.zeros(l.shape, F32)
    for k in range(TOP_K):
        pk = jnp.sum(jnp.where(sels[k], posb, 0.0), axis=-1, keepdims=True)
        pc = jnp.where(lane == float(k), pk, pc)
        pc = jnp.where(lane == float(TOP_K + k), exps[k] / denom, pc)
    r8 = lax.broadcasted_iota(jnp.int32, (8, N_E), 0)
    c8 = lax.broadcasted_iota(jnp.int32, (8, N_E), 1)
    eye = (r8 == c8).astype(F32)
    pt = lax.dot_general(eye, pc, (((1,), (1,)), ((), ())),
                         preferred_element_type=F32, precision=lax.Precision.HIGHEST)
    earlier = carry[...]
    carry[...] = earlier + tile_cnt
    return pc, pt, tile_cnt, earlier, seg


def _mix_kernel(zuv_ref, ga_ref, gb_ref, ya_ref, x_ref, gate1_ref, shift2_ref, scale2_ref,
                lng_ref, lnb_ref, ws_ref, bias_ref, wbb_ref, wo_ref, n2g_ref, rw_ref, rb_ref,
                x1_ref, h2_ref, lg_ref):
    row = lax.broadcasted_iota(jnp.int32, (CHUNK, 2 * CHUNK), 0)
    col = lax.broadcasted_iota(jnp.int32, (CHUNK, 2 * CHUNK), 1)
    causal = (col % CHUNK) <= row
    lane = lax.broadcasted_iota(jnp.int32, (CHUNK, 2 * GM_HD), 1)
    first = lane < GM_HD
    wpairs = [jnp.where(causal, ws_ref[j], 0.0).astype(BF16) for j in range(GM_HEADS // 2)]

    for g in range(TS_MIX // SUB_MIX):
        rows = pl.ds(g * SUB_MIX, SUB_MIX)
        z = _gelu(zuv_ref[0, rows, :])
        u = z[:, :GM_W]
        v = z[:, GM_W:]
        mu = jnp.mean(v, axis=-1, keepdims=True)
        vc = v - mu
        var = jnp.mean(vc * vc, axis=-1, keepdims=True)
        vn = vc * lax.rsqrt(var + EPS) * lng_ref[...] + lnb_ref[...]
        chunks = []
        for n in range(SUB_MIX // CHUNK):
            cols = []
            for j in range(GM_HEADS // 2):
                vp = vn[n * CHUNK:(n + 1) * CHUNK, 2 * GM_HD * j:2 * GM_HD * (j + 1)]
                rhs = jnp.concatenate([jnp.where(first, vp, 0.0), jnp.where(first, 0.0, vp)], axis=0)
                cols.append(jnp.dot(wpairs[j], rhs.astype(BF16), preferred_element_type=F32))
            chunks.append(jnp.concatenate(cols, axis=1) + bias_ref[...])
        mixed = jnp.concatenate(chunks, axis=0)
        gm = u * mixed
        yb = jnp.dot(gm.astype(BF16), wbb_ref[...], preferred_element_type=F32)
        merged = _sigmoid(ga_ref[0, rows, :]) * ya_ref[0, rows, :] + _sigmoid(gb_ref[0, rows, :]) * yb
        o = jnp.dot(merged.astype(BF16), wo_ref[...], preferred_element_type=F32)
        x1 = x_ref[0, rows, :] + gate1_ref[0] * o
        x1_ref[0, rows, :] = x1
        h2 = _rms(x1) * n2g_ref[...]
        h2 = h2 * (1.0 + scale2_ref[0]) + shift2_ref[0]
        h2_ref[0, rows, :] = h2.astype(BF16)
        lg_ref[0, rows, :] = jnp.dot(h2, rw_ref[...], preferred_element_type=F32,
                                     precision=lax.Precision.HIGHEST) + rb_ref[...]


def _route_kernel(lg_ref, pc_ref, pt_ref, cntt_ref, pret_ref, segt_ref, cnt_ref, carry):
    @pl.when(pl.program_id(0) == 0)
    def _():
        carry[...] = jnp.zeros_like(carry)

    for t in range(ROUTE_TILES):
        pc, pt, tile_cnt, earlier, seg = _route_tile(lg_ref[t * TT:(t + 1) * TT, :], carry)
        pc_ref[t * TT:(t + 1) * TT, :] = pc
        pt_ref[:, t * TT:(t + 1) * TT] = pt
        cntt_ref[t] = jnp.broadcast_to(tile_cnt, (8, N_E)).astype(jnp.int32)
        pret_ref[t] = jnp.broadcast_to(earlier, (8, N_E)).astype(jnp.int32)
        segt_ref[t] = jnp.broadcast_to(seg, (8, N_E)).astype(jnp.int32)
    cnt_ref[...] = jnp.broadcast_to(carry[...], cnt_ref.shape)


def _route_call(logits):
    assert NT % ROUTE_TILES == 0
    tbl_spec = pl.BlockSpec((ROUTE_TILES, 8, N_E), lambda i: (i, 0, 0))
    tbl_shape = jax.ShapeDtypeStruct((NT, 8, N_E), jnp.int32)
    return pl.pallas_call(
        _route_kernel,
        grid=(NT // ROUTE_TILES,),
        in_specs=[pl.BlockSpec((ROUTE_TILES * TT, N_E), lambda i: (i, 0))],
        out_specs=[pl.BlockSpec((ROUTE_TILES * TT, N_E), lambda i: (i, 0)),
                   pl.BlockSpec((8, ROUTE_TILES * TT), lambda i: (0, i)),
                   tbl_spec, tbl_spec, tbl_spec,
                   pl.BlockSpec((8, N_E), lambda i: (0, 0))],
        out_shape=[jax.ShapeDtypeStruct((T, N_E), F32),
                   jax.ShapeDtypeStruct((8, T), F32),
                   tbl_shape, tbl_shape, tbl_shape,
                   jax.ShapeDtypeStruct((8, N_E), F32)],
        scratch_shapes=[pltpu.VMEM((1, N_E), F32)],
        compiler_params=pltpu.CompilerParams(dimension_semantics=("arbitrary",)),
        name="route",
    )(logits)


def _mix_call(zuv, ga, gb, ya2d, x, mod3, ln_g, ln_b, ws_pairs, bias_full, wbb, wo, n2g, rw, rb):
    tok_spec = pl.BlockSpec((1, TS_MIX, D), lambda b, s: (b, s, 0))
    full = lambda *shape: pl.BlockSpec(shape, lambda b, s: (0,) * len(shape))
    mod_spec = lambda j: pl.BlockSpec((1, 1, D), lambda b, s: (b, 0, j))
    return pl.pallas_call(
        _mix_kernel,
        grid=(B, S // TS_MIX),
        in_specs=[tok_spec, tok_spec, tok_spec, tok_spec, tok_spec,
                  mod_spec(2), mod_spec(3), mod_spec(4),
                  full(1, GM_W), full(1, GM_W),
                  full(GM_HEADS // 2, CHUNK, 2 * CHUNK),
                  full(CHUNK, GM_W),
                  full(GM_W, D), full(D, D), full(1, D),
                  full(D, N_E), full(1, N_E)],
        out_specs=[tok_spec, tok_spec,
                   pl.BlockSpec((1, TS_MIX, N_E), lambda b, s: (b, s, 0))],
        out_shape=[jax.ShapeDtypeStruct((B, S, D), F32),
                   jax.ShapeDtypeStruct((B, S, D), BF16),
                   jax.ShapeDtypeStruct((B, S, N_E), F32)],
        compiler_params=pltpu.CompilerParams(vmem_limit_bytes=VMEM_LIMIT),
        name="gmlp_merge_norm2",
    )(zuv, ga, gb, ya2d, x, mod3, mod3, mod3, ln_g, ln_b, ws_pairs, bias_full, wbb, wo, n2g, rw, rb)


def _pack_rows(v):
    return pltpu.pack_elementwise([v[:, :HALF], v[:, HALF:]], packed_dtype=BF16)


def _unpack_rows(w):
    halves = [pltpu.unpack_elementwise(w, index=i, packed_dtype=BF16, unpacked_dtype=F32)
              for i in range(2)]
    return jnp.concatenate(halves, axis=1)


def _load_grouped(ref, rows, first=0):
    return jnp.concatenate([ref[pl.ds(first * SUB + c, rows, stride=SUB), :] for c in range(SUB)], axis=1)


def _store_grouped(ref, w, rows, first=0):
    for c in range(SUB):
        ref[pl.ds(first * SUB + c, rows, stride=SUB), :] = w[:, c * LANES:(c + 1) * LANES]


def _start_run(src_ref, dst_ref, src_row, dst_row, n, sem):
    def start(rows, done):
        s = pl.multiple_of((src_row + done) * SUB, SUB)
        d = pl.multiple_of((dst_row + done) * SUB, SUB)
        pltpu.make_async_copy(src_ref.at[pl.ds(s, rows * SUB)],
                              dst_ref.at[pl.ds(d, rows * SUB)], sem).start()

    nbulk = jnp.right_shift(n, PIECE.bit_length() - 1)

    def bulk(pi, carry):
        start(PIECE, pi * PIECE)
        return carry

    lax.fori_loop(0, nbulk, bulk, 0)
    done = nbulk * PIECE
    for rows in [PIECE >> s for s in range(1, PIECE.bit_length())]:
        @pl.when((n & rows) != 0)
        def _(rows=rows, done=done):
            start(rows, done)
        done = done + (n & rows)


def _start_tile_runs(tile, src_ref, dst_ref, src_tbl, dst_tbl, cnt_tbl, sem):
    base = tile * N_E

    def per_expert(e, carry):
        _start_run(src_ref, dst_ref, src_tbl[base + e], dst_tbl[base + e], cnt_tbl[base + e], sem)
        return carry

    lax.fori_loop(0, N_E, per_expert, 0)


def _wait_tile_runs(src_ref, dst_ref, sem):
    pltpu.make_async_copy(src_ref, dst_ref, sem).wait()


def _dispatch_kernel(cnt_ref, off_ref, seg_ref, pend_ref, h_ref, pt_ref, xs_ref,
                     sbuf0, sbuf1, zbuf, sem_z, sem0, sem1):
    j = pl.program_id(0)

    @pl.when(j == 0)
    def _():
        zbuf[...] = jnp.zeros_like(zbuf)
        for e in range(N_E):
            prev = pend_ref[e - 1] if e > 0 else 0
            end = pend_ref[e]

            @pl.when(end > prev)
            def _():
                first = pl.multiple_of((end - TM) * SUB, TM * SUB)
                cp = pltpu.make_async_copy(zbuf, xs_ref.at[pl.ds(first, TM * SUB)], sem_z)
                cp.start()
                cp.wait()

    rows = lax.broadcasted_iota(jnp.int32, (RS, TT), 0)
    pos = pt_ref[...].astype(jnp.int32)
    hit = rows == pos[0:1, :]
    for k in range(1, TOP_K):
        hit = hit | (rows == pos[k:k + 1, :])
    pm = jnp.where(hit, 1.0, 0.0).astype(BF16)
    srt = jnp.dot(pm, h_ref[...], preferred_element_type=F32)
    words = _pack_rows(srt)
    head = xs_ref.at[pl.ds(0, RS * SUB)]

    def emit(sbuf, sem):
        @pl.when(j >= 2)
        def _():
            _wait_tile_runs(sbuf, head, sem)
        _store_grouped(sbuf, words, RS)
        _start_tile_runs(j, sbuf, xs_ref, seg_ref, off_ref, cnt_ref, sem)

    @pl.when(j % 2 == 0)
    def _():
        emit(sbuf0, sem0)

    @pl.when(j % 2 == 1)
    def _():
        emit(sbuf1, sem1)

    @pl.when(j == NT - 1)
    def _():
        _wait_tile_runs(sbuf0, head, sem0)
        _wait_tile_runs(sbuf1, head, sem1)


def _dispatch_call(cnt_t, off_t, seg_t, pad_end, h2, pos_t):
    assert NT % 2 == 0 and NT >= 2
    grid_spec = pltpu.PrefetchScalarGridSpec(
        num_scalar_prefetch=4,
        grid=(NT,),
        in_specs=[pl.BlockSpec((TT, D), lambda j, *_: (j, 0)),
                  pl.BlockSpec((8, TT), lambda j, *_: (0, j))],
        out_specs=pl.BlockSpec(memory_space=pl.ANY),
        scratch_shapes=[pltpu.VMEM((RS * SUB, LANES), U32),
                        pltpu.VMEM((RS * SUB, LANES), U32),
                        pltpu.VMEM((TM * SUB, LANES), U32),
                        pltpu.SemaphoreType.DMA,
                        pltpu.SemaphoreType.DMA,
                        pltpu.SemaphoreType.DMA],
    )
    return pl.pallas_call(
        _dispatch_kernel,
        grid_spec=grid_spec,
        out_shape=jax.ShapeDtypeStruct((N_ROWS * SUB, LANES), U32),
        compiler_params=pltpu.CompilerParams(dimension_semantics=("arbitrary",),
                                             vmem_limit_bytes=VMEM_LIMIT),
        name="dispatch",
    )(cnt_t, off_t, seg_t, pad_end, h2, pos_t)


def _moe_kernel(be_ref, bf_ref, nx_ref, nv_ref, xs_ref, wi_hbm, bi_ref, wo_hbm, bo_ref, ys_ref,
                wi_f32, wo_f32, wi_bf, wo_bf, sem_i, sem_o):
    step = pl.program_id(0)

    def fetch(e):
        return (pltpu.make_async_copy(wi_hbm.at[e], wi_f32, sem_i),
                pltpu.make_async_copy(wo_hbm.at[e], wo_f32, sem_o))

    @pl.when(step == 0)
    def _():
        for cp in fetch(be_ref[0]):
            cp.start()

    for sub in range(BPS):
        i = step * BPS + sub

        @pl.when((i < nv_ref[0]) & (bf_ref[i] == 1))
        def _(i=i):
            for cp in fetch(be_ref[i]):
                cp.wait()
            wi_bf[...] = wi_f32[...].astype(BF16)
            wo_bf[...] = wo_f32[...].astype(BF16)

            @pl.when(nx_ref[i] >= 0)
            def _():
                for cp in fetch(nx_ref[i]):
                    cp.start()

        @pl.when(i < nv_ref[0])
        def _(i=i, sub=sub):
            e = be_ref[i]
            xb = _unpack_rows(_load_grouped(xs_ref, TM, sub * TM)).astype(BF16)
            gu = jnp.dot(xb, wi_bf[...], preferred_element_type=F32) + bi_ref[pl.ds(e, 1), :]
            gate = jnp.minimum(gu[:, :D_E], LIMIT)
            up = jnp.clip(gu[:, D_E:], -LIMIT, LIMIT)
            act = (up + 1.0) * (gate * _sigmoid(ALPHA * gate))
            y = jnp.dot(act.astype(BF16), wo_bf[...], preferred_element_type=F32) + bo_ref[pl.ds(e, 1), :]
            _store_grouped(ys_ref, _pack_rows(y), TM, sub * TM)


def _moe_call(blk_e, blk_first, blk_next, n_valid, xs, w_in, b_in, w_out, b_out):
    assert N_BLOCKS % BPS == 0

    def row_map(s, be, bf, nx, nv):
        last = (nv[0] + BPS - 1) // BPS - 1
        return (jnp.maximum(jnp.minimum(s, last), 0), 0)

    grid_spec = pltpu.PrefetchScalarGridSpec(
        num_scalar_prefetch=4,
        grid=(N_BLOCKS // BPS,),
        in_specs=[pl.BlockSpec((BPS * TM * SUB, LANES), row_map),
                  pl.BlockSpec(memory_space=pl.ANY),
                  pl.BlockSpec((N_E, 2 * D_E), lambda s, *_: (0, 0)),
                  pl.BlockSpec(memory_space=pl.ANY),
                  pl.BlockSpec((N_E, D), lambda s, *_: (0, 0))],
        out_specs=pl.BlockSpec((BPS * TM * SUB, LANES), row_map),
        scratch_shapes=[pltpu.VMEM((D, 2 * D_E), F32),
                        pltpu.VMEM((D_E, D), F32),
                        pltpu.VMEM((D, 2 * D_E), BF16),
                        pltpu.VMEM((D_E, D), BF16),
                        pltpu.SemaphoreType.DMA,
                        pltpu.SemaphoreType.DMA],
    )
    return pl.pallas_call(
        _moe_kernel,
        grid_spec=grid_spec,
        out_shape=jax.ShapeDtypeStruct((N_ROWS * SUB, LANES), U32),
        compiler_params=pltpu.CompilerParams(dimension_semantics=("arbitrary",),
                                             vmem_limit_bytes=VMEM_LIMIT),
        name="moe_experts",
    )(blk_e, blk_first, blk_next, n_valid, xs, w_in, b_in, w_out, b_out)


def _combine_kernel(cnt_ref, off_ref, seg_ref, ys_ref, pc_ref, x1_ref, gate2_ref, fg_ref, o_ref,
                    buf0, buf1, sem0, sem1):
    j = pl.program_id(0)
    head = ys_ref.at[pl.ds(0, RS * SUB)]

    def fetch(tile, buf, sem):
        _start_tile_runs(tile, ys_ref, buf, off_ref, seg_ref, cnt_ref, sem)

    @pl.when(j == 0)
    def _():
        fetch(0, buf0, sem0)

    pc = pc_ref[...]
    pos = pc.astype(jnp.int32)
    col = lax.broadcasted_iota(jnp.int32, (TT, RS), 1)
    wm = jnp.zeros((TT, RS), F32)
    for k in range(TOP_K):
        wm = jnp.where(col == pos[:, k:k + 1], pc[:, TOP_K + k:TOP_K + k + 1], wm)
    wm = wm.astype(BF16)

    def tile(buf, sem, next_buf, next_sem):
        @pl.when(j + 1 < NT)
        def _():
            fetch(j + 1, next_buf, next_sem)
        _wait_tile_runs(head, buf, sem)
        yt = _unpack_rows(_load_grouped(buf, RS)).astype(BF16)
        acc = jnp.dot(wm, yt, preferred_element_type=F32)
        x2 = x1_ref[...] + gate2_ref[0] * acc
        o_ref[...] = _rms(x2) * fg_ref[...]

    @pl.when(j % 2 == 0)
    def _():
        tile(buf0, sem0, buf1, sem1)

    @pl.when(j % 2 == 1)
    def _():
        tile(buf1, sem1, buf0, sem0)


def _combine_call(cnt_t, off_t, seg_t, ys, pos_c, x1, mod3, final_g):
    per_b = S // TT
    grid_spec = pltpu.PrefetchScalarGridSpec(
        num_scalar_prefetch=3,
        grid=(NT,),
        in_specs=[pl.BlockSpec(memory_space=pl.ANY),
                  pl.BlockSpec((TT, N_E), lambda j, *_: (j, 0)),
                  pl.BlockSpec((TT, D), lambda j, *_: (j, 0)),
                  pl.BlockSpec((1, 1, D), lambda j, *_: (j // per_b, 0, 5)),
                  pl.BlockSpec((1, D), lambda j, *_: (0, 0))],
        out_specs=pl.BlockSpec((TT, D), lambda j, *_: (j, 0)),
        scratch_shapes=[pltpu.VMEM((RS * SUB, LANES), U32),
                        pltpu.VMEM((RS * SUB, LANES), U32),
                        pltpu.SemaphoreType.DMA,
                        pltpu.SemaphoreType.DMA],
    )
    return pl.pallas_call(
        _combine_kernel,
        grid_spec=grid_spec,
        out_shape=jax.ShapeDtypeStruct((T, D), F32),
        compiler_params=pltpu.CompilerParams(dimension_semantics=("arbitrary",),
                                             vmem_limit_bytes=VMEM_LIMIT),
        name="combine_norm",
    )(cnt_t, off_t, seg_t, ys, pos_c, x1, mod3, final_g)


def kernel(x, c, ada_w, ada_b, norm1_g, w_in, ssm_a_re, ssm_a_im, ssm_log_dt, ssm_b_re, ssm_b_im, ssm_c_re, ssm_c_im, ssm_d, ssm_glu_w, ssm_glu_b, w_branch_a, gmlp_ln_g, gmlp_ln_b, gmlp_ws, gmlp_bs, w_branch_b, w_out, norm2_g, router_w, router_b, moe_w_in, moe_b_in, moe_w_out, moe_b_out, final_g):
    depth = ada_w.shape[0]
    assert depth == 1, "the final rms_norm is fused into the combine kernel of the only layer"
    for layer in range(depth):
        mod = _mod_call(c, ada_w[layer], ada_b[layer])
        mod3 = mod.reshape(B, 1, 6 * D)

        u, zuv, ga, gb = _proj_call(x, norm1_g[layer], mod3, w_in[layer].astype(BF16))

        bm, cre, cim, are, aim = _s5_params(ssm_a_re[layer], ssm_a_im[layer], ssm_log_dt[layer],
                                            ssm_b_re[layer], ssm_b_im[layer],
                                            ssm_c_re[layer], ssm_c_im[layer])
        ya = _s5_call(u, bm, cre, cim, are, aim,
                      ssm_d[layer].reshape(1, SSM_W), ssm_glu_w[layer].astype(BF16),
                      ssm_glu_b[layer].reshape(1, SSM_W), w_branch_a[layer].astype(BF16))

        ws = gmlp_ws[layer]
        ws_pairs = jnp.concatenate([ws[0::2], ws[1::2]], axis=-1)
        bias_full = jnp.repeat(gmlp_bs[layer].T, GM_HD, axis=1)
        x1, h2, logits = _mix_call(
            zuv, ga, gb, ya, x, mod3,
            gmlp_ln_g[layer].reshape(1, GM_W), gmlp_ln_b[layer].reshape(1, GM_W),
            ws_pairs, bias_full, w_branch_b[layer].astype(BF16), w_out[layer].astype(BF16),
            norm2_g[layer].reshape(1, D), router_w[layer],
            router_b[layer].reshape(1, N_E))

        pos_c, pos_t, cnt_t, pre_t, seg_t, cnt = _route_call(logits.reshape(T, N_E))
        counts = cnt[0].astype(jnp.int32)
        nblk = (counts + TM - 1) // TM
        blk_end = jnp.cumsum(nblk)
        pad_end = (blk_end * TM).astype(jnp.int32)
        experts = jnp.arange(N_E, dtype=jnp.int32)
        blk_ids = jnp.arange(N_BLOCKS, dtype=jnp.int32)
        blk_e = jnp.sum((blk_end[None, :] <= blk_ids[:, None]).astype(jnp.int32), axis=1)
        blk_e = jnp.minimum(blk_e, N_E - 1)
        blk_first = jnp.concatenate([jnp.ones((1,), jnp.int32),
                                     (blk_e[1:] != blk_e[:-1]).astype(jnp.int32)])
        later = (experts[None, :] > experts[:, None]) & (nblk[None, :] > 0)
        next_e = jnp.min(jnp.where(later, experts[None, :], N_E), axis=1)
        next_e = jnp.where(next_e == N_E, -1, next_e)
        blk_next = jnp.sum(jnp.where(blk_e[:, None] == experts[None, :], next_e[None, :], 0), axis=1)
        n_valid = blk_end[-1:].astype(jnp.int32)
        pad_start = pad_end - nblk * TM
        off_t = pre_t[:, 0, :] + pad_start[None, :]
        cnt_t, off_t, seg_t = (t.reshape(NT * N_E) for t in (cnt_t[:, 0, :], off_t, seg_t[:, 0, :]))

        xs = _dispatch_call(cnt_t, off_t, seg_t, pad_end, h2.reshape(T, D), pos_t)
        ys = _moe_call(blk_e, blk_first, blk_next.astype(jnp.int32), n_valid, xs, moe_w_in[layer],
                       moe_b_in[layer], moe_w_out[layer], moe_b_out[layer])
        x = _combine_call(cnt_t, off_t, seg_t, ys, pos_c, x1.reshape(T, D),
                          mod3, final_g.reshape(1, D)).reshape(B, S, D)
    return x
```

```python
import functools
import math

import jax
import jax.numpy as jnp
from jax import lax
from jax.experimental import pallas as pl
from jax.experimental.pallas import tpu as pltpu

F32 = jnp.float32
BF16 = jnp.bfloat16

D = 1024
B = 8
S = 2048
T = B * S
SSM_W = 512
SSM_G = 32
SSM_H = 16
SSM_P = 64
N_PACK = 4
PACK_G = SSM_G // N_PACK
GM_W = 512
GM_HEADS = 8
GM_HD = 64
CHUNK = 128
N_E = 32
TOP_K = 4
D_E = 1024
LIMIT = 7.0
ALPHA = 1.702
EPS = 1e-6

TS_PROJ = 512
L_SSM = 128
R_SSM = L_SSM * B
TS_MIX = 512
SUB_MIX = 256
TT = 256
NT = T // TT
ROUTE_TILES = 4
RS = TOP_K * TT
TM = 256
BPS = 4
N_ROWS = T * TOP_K + N_E * TM
N_BLOCKS = N_ROWS // TM
LANES = 128
HALF = D // 2
SUB = HALF // LANES
PIECE = 16
VMEM_LIMIT = 56 * 1024 * 1024
U32 = jnp.uint32


def _sigmoid(v):
    return 1.0 / (1.0 + jnp.exp(-v))


def _gelu(v):
    return 0.5 * v * (1.0 + jnp.tanh(math.sqrt(2.0 / math.pi) * (v + 0.044715 * v * v * v)))


def _rms(v):
    return v * lax.rsqrt(jnp.mean(v * v, axis=-1, keepdims=True) + EPS)


def _mod_kernel(c_ref, w_ref, b_ref, o_ref):
    cv = c_ref[...]
    sv = cv * _sigmoid(cv)
    o_ref[...] = jnp.dot(sv, w_ref[...], preferred_element_type=F32,
                         precision=lax.Precision.HIGHEST) + b_ref[...]


def _mod_call(c, ada_w, ada_b):
    n = ada_w.shape[1]
    return pl.pallas_call(
        _mod_kernel,
        grid=(n // D,),
        in_specs=[pl.BlockSpec((B, D), lambda j: (0, 0)),
                  pl.BlockSpec((D, D), lambda j: (0, j)),
                  pl.BlockSpec((1, D), lambda j: (0, j))],
        out_specs=pl.BlockSpec((B, D), lambda j: (0, j)),
        out_shape=jax.ShapeDtypeStruct((B, n), F32),
        name="adaln_mod",
    )(c, ada_w, ada_b.reshape(1, n))


def _proj_kernel(x_ref, g_ref, shift_ref, scale_ref, w_ref, u_ref, zuv_ref, ga_ref, gb_ref):
    h = _rms(x_ref[0]) * g_ref[...]
    h = h * (1.0 + scale_ref[0]) + shift_ref[0]
    hb = h.astype(BF16)
    u_ref[0] = jnp.dot(hb, w_ref[:, 0:SSM_W], preferred_element_type=F32)
    zuv_ref[0] = jnp.dot(hb, w_ref[:, SSM_W:SSM_W + 2 * GM_W], preferred_element_type=F32)
    ga_ref[0] = jnp.dot(hb, w_ref[:, SSM_W + 2 * GM_W:SSM_W + 2 * GM_W + D], preferred_element_type=F32)
    gb_ref[0] = jnp.dot(hb, w_ref[:, SSM_W + 2 * GM_W + D:], preferred_element_type=F32)


def _proj_call(x, norm_g, mod3, w_in_bf):
    pw = w_in_bf.shape[1]
    tok_spec = pl.BlockSpec((1, TS_PROJ, D), lambda b, s: (b, s, 0))
    return pl.pallas_call(
        _proj_kernel,
        grid=(B, S // TS_PROJ),
        in_specs=[tok_spec,
                  pl.BlockSpec((1, D), lambda b, s: (0, 0)),
                  pl.BlockSpec((1, 1, D), lambda b, s: (b, 0, 0)),
                  pl.BlockSpec((1, 1, D), lambda b, s: (b, 0, 1)),
                  pl.BlockSpec((D, pw), lambda b, s: (0, 0))],
        out_specs=[pl.BlockSpec((1, TS_PROJ, SSM_W), lambda b, s: (b, s, 0)),
                   tok_spec, tok_spec, tok_spec],
        out_shape=[jax.ShapeDtypeStruct((B, S, SSM_W), F32),
                   jax.ShapeDtypeStruct((B, S, D), F32),
                   jax.ShapeDtypeStruct((B, S, D), F32),
                   jax.ShapeDtypeStruct((B, S, D), F32)],
        compiler_params=pltpu.CompilerParams(vmem_limit_bytes=VMEM_LIMIT),
        name="norm_proj",
    )(x, norm_g.reshape(1, D), mod3, mod3, w_in_bf)


def _s5_kernel(u_ref, bm_ref, cre_ref, cim_ref, are_ref, aim_ref, d_ref, gw_ref, gb_ref, wa_ref,
               o_ref, usc, ysc, sre, sim, st_re, st_im):
    @pl.when(pl.program_id(0) == 0)
    def _():
        st_re[...] = jnp.zeros_like(st_re)
        st_im[...] = jnp.zeros_like(st_im)

    nslab = SSM_W // LANES
    for b in range(B):
        for c in range(nslab):
            usc[c, pl.ds(b, L_SSM, stride=B), :] = u_ref[b, :, c * LANES:(c + 1) * LANES]
    u = jnp.concatenate([usc[c] for c in range(nslab)], axis=1)
    ub = u.astype(BF16)
    half = PACK_G * SSM_P
    ys = []
    for k in range(N_PACK):
        bu = jnp.dot(ub[:, 128 * k:128 * (k + 1)], bm_ref[k], preferred_element_type=F32)
        sre[k] = bu[:, :half]
        sim[k] = bu[:, half:]
        ar = are_ref[k]
        ai = aim_ref[k]
        r = st_re[k]
        m = st_im[k]
        for t in range(L_SSM):
            rows = pl.ds(t * B, B)
            nr = ar * r - ai * m + sre[k, rows, :]
            m = ar * m + ai * r + sim[k, rows, :]
            r = nr
            sre[k, rows, :] = r
            sim[k, rows, :] = m
        st_re[k] = r
        st_im[k] = m
        yk = jnp.dot(sre[k].astype(BF16), cre_ref[k], preferred_element_type=F32)
        yk = yk + jnp.dot(sim[k].astype(BF16), cim_ref[k], preferred_element_type=F32)
        ys.append(yk)
    for c in range(nslab):
        uc = usc[c]
        ysc[c] = ys[c] + d_ref[:, c * LANES:(c + 1) * LANES] * uc
    y = jnp.concatenate(
        [jnp.concatenate([ysc[c, pl.ds(b, L_SSM, stride=B), :] for c in range(nslab)], axis=1)
         for b in range(B)], axis=0)
    z = _gelu(y)
    gl = jnp.dot(z.astype(BF16), gw_ref[...], preferred_element_type=F32) + gb_ref[...]
    out = z * _sigmoid(gl)
    o = jnp.dot(out.astype(BF16), wa_ref[...], preferred_element_type=F32)
    for b in range(B):
        o_ref[b] = o[b * L_SSM:(b + 1) * L_SSM]


def _s5_call(u, bm, cre, cim, are, aim, d_skip, glu_w, glu_b, w_a):
    half = PACK_G * SSM_P
    full = lambda *shape: pl.BlockSpec(shape, lambda i: (0,) * len(shape))
    return pl.pallas_call(
        _s5_kernel,
        grid=(S // L_SSM,),
        in_specs=[pl.BlockSpec((B, L_SSM, SSM_W), lambda i: (0, i, 0)),
                  full(N_PACK, 128, 2 * half),
                  full(N_PACK, half, 128),
                  full(N_PACK, half, 128),
                  full(N_PACK, B, half),
                  full(N_PACK, B, half),
                  full(1, SSM_W),
                  full(SSM_W, SSM_W),
                  full(1, SSM_W),
                  full(SSM_W, D)],
        out_specs=pl.BlockSpec((B, L_SSM, D), lambda i: (0, i, 0)),
        out_shape=jax.ShapeDtypeStruct((B, S, D), F32),
        scratch_shapes=[pltpu.VMEM((SSM_W // LANES, R_SSM, LANES), F32),
                        pltpu.VMEM((SSM_W // LANES, R_SSM, LANES), F32),
                        pltpu.VMEM((N_PACK, R_SSM, half), F32),
                        pltpu.VMEM((N_PACK, R_SSM, half), F32),
                        pltpu.VMEM((N_PACK, B, half), F32),
                        pltpu.VMEM((N_PACK, B, half), F32)],
        compiler_params=pltpu.CompilerParams(dimension_semantics=("arbitrary",),
                                             vmem_limit_bytes=VMEM_LIMIT),
        name="s5_branch",
    )(u, bm, cre, cim, are, aim, d_skip, glu_w, glu_b, w_a)


def _s5_params(a_re, a_im, log_dt, b_re, b_im, c_re, c_im):
    dt = jnp.exp(log_dt)[:, None]
    mag = jnp.exp(a_re * dt)
    lr = mag * jnp.cos(a_im * dt)
    li = mag * jnp.sin(a_im * dt)
    den = a_re * a_re + a_im * a_im
    cr = ((lr - 1.0) * a_re + li * a_im) / den
    ci = (li * a_re - (lr - 1.0) * a_im) / den
    bbr = cr[..., None] * b_re - ci[..., None] * b_im
    bbi = cr[..., None] * b_im + ci[..., None] * b_re
    eye = jnp.eye(PACK_G, dtype=F32)
    half = PACK_G * SSM_P

    def pack_b(m):
        m4 = m.reshape(N_PACK, PACK_G, SSM_P, SSM_H)
        return jnp.einsum('kgph,gj->kghjp', m4, eye).reshape(N_PACK, PACK_G * SSM_H, half)

    def pack_c(m):
        m4 = m.reshape(N_PACK, PACK_G, SSM_H, SSM_P)
        return jnp.einsum('kghp,gj->kgpjh', m4, eye).reshape(N_PACK, half, PACK_G * SSM_H)

    bm = jnp.concatenate([pack_b(bbr), pack_b(bbi)], axis=-1).astype(BF16)
    cre = pack_c(c_re).astype(BF16)
    cim = (-pack_c(c_im)).astype(BF16)
    are = jnp.broadcast_to(lr.reshape(N_PACK, 1, half), (N_PACK, B, half))
    aim = jnp.broadcast_to(li.reshape(N_PACK, 1, half), (N_PACK, B, half))
    return bm, cre, cim, are, aim


def _route_tile(l, carry):
    lane = lax.broadcasted_iota(jnp.int32, l.shape, 1).astype(F32)
    sels, vals = [], []
    for _ in range(TOP_K):
        m = jnp.max(l, axis=-1, keepdims=True)
        idx = jnp.min(jnp.where(l == m, lane, float(N_E)), axis=-1, keepdims=True)
        sel = lane == idx
        sels.append(sel)
        vals.append(m)
        l = jnp.where(sel, -jnp.inf, l)
    member = sels[0].astype(F32)
    for k in range(1, TOP_K):
        member = member + sels[k].astype(F32)
    tile_cnt = jnp.sum(member, axis=0, keepdims=True)

    r = lax.broadcasted_iota(jnp.int32, (N_E, N_E), 0)
    c = lax.broadcasted_iota(jnp.int32, (N_E, N_E), 1)
    tri = (r < c).astype(BF16)
    tcb = jnp.broadcast_to(tile_cnt, (8, N_E)).astype(BF16)
    seg = jnp.dot(tcb, tri, preferred_element_type=F32)[0:1]
    r = lax.broadcasted_iota(jnp.int32, (TT, TT), 0)
    c = lax.broadcasted_iota(jnp.int32, (TT, TT), 1)
    strict = (c < r).astype(BF16)
    rank = jnp.dot(strict, member.astype(BF16), preferred_element_type=F32)
    posb = seg + rank
    denom = jnp.zeros_like(vals[0])
    exps = []
    for k in range(TOP_K):
        e = jnp.exp(vals[k] - vals[0])
        exps.append(e)
        denom = denom + e
    pc = jnp.zeros(l.shape, F32)
    for k in range(TOP_K):
        pk = jnp.sum(jnp.where(sels[k], posb, 0.0), axis=-1, keepdims=True)
        pc = jnp.where(lane == float(k), pk, pc)
        pc = jnp.where(lane == float(TOP_K + k), exps[k] / denom, pc)
    r8 = lax.broadcasted_iota(jnp.int32, (8, N_E), 0)
    c8 = lax.broadcasted_iota(jnp.int32, (8, N_E), 1)
    eye = (r8 == c8).astype(F32)
    pt = lax.dot_general(eye, pc, (((1,), (1,)), ((), ())),
                         preferred_element_type=F32, precision=lax.Precision.HIGHEST)
    earlier = carry[...]
    carry[...] = earlier + tile_cnt
    return pc, pt, tile_cnt, earlier, seg


def _mix_kernel(zuv_ref, ga_ref, gb_ref, ya_ref, x_ref, gate1_ref, shift2_ref, scale2_ref,
                lng_ref, lnb_ref, ws_ref, bias_ref, wbb_ref, wo_ref, n2g_ref, rw_ref, rb_ref,
                x1_ref, h2_ref, lg_ref):
    row = lax.broadcasted_iota(jnp.int32, (CHUNK, 2 * CHUNK), 0)
    col = lax.broadcasted_iota(jnp.int32, (CHUNK, 2 * CHUNK), 1)
    causal = (col % CHUNK) <= row
    lane = lax.broadcasted_iota(jnp.int32, (CHUNK, 2 * GM_HD), 1)
    first = lane < GM_HD
    wpairs = [jnp.where(causal, ws_ref[j], 0.0).astype(BF16) for j in range(GM_HEADS // 2)]

    for g in range(TS_MIX // SUB_MIX):
        rows = pl.ds(g * SUB_MIX, SUB_MIX)
        z = _gelu(zuv_ref[0, rows, :])
        u = z[:, :GM_W]
        v = z[:, GM_W:]
        mu = jnp.mean(v, axis=-1, keepdims=True)
        vc = v - mu
        var = jnp.mean(vc * vc, axis=-1, keepdims=True)
        vn = vc * lax.rsqrt(var + EPS) * lng_ref[...] + lnb_ref[...]
        chunks = []
        for n in range(SUB_MIX // CHUNK):
            cols = []
            for j in range(GM_HEADS // 2):
                vp = vn[n * CHUNK:(n + 1) * CHUNK, 2 * GM_HD * j:2 * GM_HD * (j + 1)]
                rhs = jnp.concatenate([jnp.where(first, vp, 0.0), jnp.where(first, 0.0, vp)], axis=0)
                cols.append(jnp.dot(wpairs[j], rhs.astype(BF16), preferred_element_type=F32))
            chunks.append(jnp.concatenate(cols, axis=1) + bias_ref[...])
        mixed = jnp.concatenate(chunks, axis=0)
        gm = u * mixed
        yb = jnp.dot(gm.astype(BF16), wbb_ref[...], preferred_element_type=F32)
        merged = _sigmoid(ga_ref[0, rows, :]) * ya_ref[0, rows, :] + _sigmoid(gb_ref[0, rows, :]) * yb
        o = jnp.dot(merged.astype(BF16), wo_ref[...], preferred_element_type=F32)
        x1 = x_ref[0, rows, :] + gate1_ref[0] * o
        x1_ref[0, rows, :] = x1
        h2 = _rms(x1) * n2g_ref[...]
        h2 = h2 * (1.0 + scale2_ref[0]) + shift2_ref[0]
        h2_ref[0, rows, :] = h2.astype(BF16)
        lg_ref[0, rows, :] = jnp.dot(h2, rw_ref[...], preferred_element_type=F32,
                                     precision=lax.Precision.HIGHEST) + rb_ref[...]


def _route_kernel(lg_ref, pc_ref, pt_ref, cntt_ref, pret_ref, segt_ref, cnt_ref, carry):
    @pl.when(pl.program_id(0) == 0)
    def _():
        carry[...] = jnp.zeros_like(carry)

    for t in range(ROUTE_TILES):
        pc, pt, tile_cnt, earlier, seg = _route_tile(lg_ref[t * TT:(t + 1) * TT, :], carry)
        pc_ref[t * TT:(t + 1) * TT, :] = pc
        pt_ref[:, t * TT:(t + 1) * TT] = pt
        cntt_ref[t] = jnp.broadcast_to(tile_cnt, (8, N_E)).astype(jnp.int32)
        pret_ref[t] = jnp.broadcast_to(earlier, (8, N_E)).astype(jnp.int32)
        segt_ref[t] = jnp.broadcast_to(seg, (8, N_E)).astype(jnp.int32)
    cnt_ref[...] = jnp.broadcast_to(carry[...], cnt_ref.shape)


def _route_call(logits):
    assert NT % ROUTE_TILES == 0
    tbl_spec = pl.BlockSpec((ROUTE_TILES, 8, N_E), lambda i: (i, 0, 0))
    tbl_shape = jax.ShapeDtypeStruct((NT, 8, N_E), jnp.int32)
    return pl.pallas_call(
        _route_kernel,
        grid=(NT // ROUTE_TILES,),
        in_specs=[pl.BlockSpec((ROUTE_TILES * TT, N_E), lambda i: (i, 0))],
        out_specs=[pl.BlockSpec((ROUTE_TILES * TT, N_E), lambda i: (i, 0)),
                   pl.BlockSpec((8, ROUTE_TILES * TT), lambda i: (0, i)),
                   tbl_spec, tbl_spec, tbl_spec,
                   pl.BlockSpec((8, N_E), lambda i: (0, 0))],
        out_shape=[jax.ShapeDtypeStruct((T, N_E), F32),
                   jax.ShapeDtypeStruct((8, T), F32),
                   tbl_shape, tbl_shape, tbl_shape,
                   jax.ShapeDtypeStruct((8, N_E), F32)],
        scratch_shapes=[pltpu.VMEM((1, N_E), F32)],
        compiler_params=pltpu.CompilerParams(dimension_semantics=("arbitrary",)),
        name="route",
    )(logits)


def _mix_call(zuv, ga, gb, ya2d, x, mod3, ln_g, ln_b, ws_pairs, bias_full, wbb, wo, n2g, rw, rb):
    tok_spec = pl.BlockSpec((1, TS_MIX, D), lambda b, s: (b, s, 0))
    full = lambda *shape: pl.BlockSpec(shape, lambda b, s: (0,) * len(shape))
    mod_spec = lambda j: pl.BlockSpec((1, 1, D), lambda b, s: (b, 0, j))
    return pl.pallas_call(
        _mix_kernel,
        grid=(B, S // TS_MIX),
        in_specs=[tok_spec, tok_spec, tok_spec, tok_spec, tok_spec,
                  mod_spec(2), mod_spec(3), mod_spec(4),
                  full(1, GM_W), full(1, GM_W),
                  full(GM_HEADS // 2, CHUNK, 2 * CHUNK),
                  full(CHUNK, GM_W),
                  full(GM_W, D), full(D, D), full(1, D),
                  full(D, N_E), full(1, N_E)],
        out_specs=[tok_spec, tok_spec,
                   pl.BlockSpec((1, TS_MIX, N_E), lambda b, s: (b, s, 0))],
        out_shape=[jax.ShapeDtypeStruct((B, S, D), F32),
                   jax.ShapeDtypeStruct((B, S, D), BF16),
                   jax.ShapeDtypeStruct((B, S, N_E), F32)],
        compiler_params=pltpu.CompilerParams(vmem_limit_bytes=VMEM_LIMIT),
        name="gmlp_merge_norm2",
    )(zuv, ga, gb, ya2d, x, mod3, mod3, mod3, ln_g, ln_b, ws_pairs, bias_full, wbb, wo, n2g, rw, rb)


def _pack_rows(v):
    return pltpu.pack_elementwise([v[:, :HALF], v[:, HALF:]], packed_dtype=BF16)


def _unpack_rows(w):
    halves = [pltpu.unpack_elementwise(w, index=i, packed_dtype=BF16, unpacked_dtype=F32)
              for i in range(2)]
    return jnp.concatenate(halves, axis=1)


def _load_grouped(ref, rows, first=0):
    return jnp.concatenate([ref[pl.ds(first * SUB + c, rows, stride=SUB), :] for c in range(SUB)], axis=1)


def _store_grouped(ref, w, rows, first=0):
    for c in range(SUB):
        ref[pl.ds(first * SUB + c, rows, stride=SUB), :] = w[:, c * LANES:(c + 1) * LANES]


def _start_run(src_ref, dst_ref, src_row, dst_row, n, sem):
    def start(rows, done):
        s = pl.multiple_of((src_row + done) * SUB, SUB)
        d = pl.multiple_of((dst_row + done) * SUB, SUB)
        pltpu.make_async_copy(src_ref.at[pl.ds(s, rows * SUB)],
                              dst_ref.at[pl.ds(d, rows * SUB)], sem).start()

    nbulk = jnp.right_shift(n, PIECE.bit_length() - 1)

    def bulk(pi, carry):
        start(PIECE, pi * PIECE)
        return carry

    lax.fori_loop(0, nbulk, bulk, 0)
    done = nbulk * PIECE
    for rows in [PIECE >> s for s in range(1, PIECE.bit_length())]:
        @pl.when((n & rows) != 0)
        def _(rows=rows, done=done):
            start(rows, done)
        done = done + (n & rows)


def _start_tile_runs(tile, src_ref, dst_ref, src_tbl, dst_tbl, cnt_tbl, sem):
    base = tile * N_E

    def per_expert(e, carry):
        _start_run(src_ref, dst_ref, src_tbl[base + e], dst_tbl[base + e], cnt_tbl[base + e], sem)
        return carry

    lax.fori_loop(0, N_E, per_expert, 0)


def _wait_tile_runs(src_ref, dst_ref, sem):
    pltpu.make_async_copy(src_ref, dst_ref, sem).wait()


def _dispatch_kernel(cnt_ref, off_ref, seg_ref, pend_ref, h_ref, pt_ref, xs_ref,
                     sbuf0, sbuf1, zbuf, sem_z, sem0, sem1):
    j = pl.program_id(0)

    @pl.when(j == 0)
    def _():
        zbuf[...] = jnp.zeros_like(zbuf)
        for e in range(N_E):
            prev = pend_ref[e - 1] if e > 0 else 0
            end = pend_ref[e]

            @pl.when(end > prev)
            def _():
                first = pl.multiple_of((end - TM) * SUB, TM * SUB)
                cp = pltpu.make_async_copy(zbuf, xs_ref.at[pl.ds(first, TM * SUB)], sem_z)
                cp.start()
                cp.wait()

    rows = lax.broadcasted_iota(jnp.int32, (RS, TT), 0)
    words = []
    for t in range(2):
        pos = pt_ref[:, t * TT:(t + 1) * TT].astype(jnp.int32)
        hit = rows == pos[0:1, :]
        for k in range(1, TOP_K):
            hit = hit | (rows == pos[k:k + 1, :])
        pm = jnp.where(hit, 1.0, 0.0).astype(BF16)
        srt = jnp.dot(pm, h_ref[t * TT:(t + 1) * TT, :], preferred_element_type=F32)
        words.append(_pack_rows(srt))
    head = xs_ref.at[pl.ds(0, RS * SUB)]

    for t, (sbuf, sem) in enumerate(((sbuf0, sem0), (sbuf1, sem1))):
        @pl.when(j >= 1)
        def _(sbuf=sbuf, sem=sem):
            _wait_tile_runs(sbuf, head, sem)
        _store_grouped(sbuf, words[t], RS)
        _start_tile_runs(2 * j + t, sbuf, xs_ref, seg_ref, off_ref, cnt_ref, sem)

    @pl.when(j == NT // 2 - 1)
    def _():
        _wait_tile_runs(sbuf0, head, sem0)
        _wait_tile_runs(sbuf1, head, sem1)


def _dispatch_call(cnt_t, off_t, seg_t, pad_end, h2, pos_t):
    assert NT % 2 == 0
    grid_spec = pltpu.PrefetchScalarGridSpec(
        num_scalar_prefetch=4,
        grid=(NT // 2,),
        in_specs=[pl.BlockSpec((2 * TT, D), lambda j, *_: (j, 0)),
                  pl.BlockSpec((8, 2 * TT), lambda j, *_: (0, j))],
        out_specs=pl.BlockSpec(memory_space=pl.ANY),
        scratch_shapes=[pltpu.VMEM((RS * SUB, LANES), U32),
                        pltpu.VMEM((RS * SUB, LANES), U32),
                        pltpu.VMEM((TM * SUB, LANES), U32),
                        pltpu.SemaphoreType.DMA,
                        pltpu.SemaphoreType.DMA,
                        pltpu.SemaphoreType.DMA],
    )
    return pl.pallas_call(
        _dispatch_kernel,
        grid_spec=grid_spec,
        out_shape=jax.ShapeDtypeStruct((N_ROWS * SUB, LANES), U32),
        compiler_params=pltpu.CompilerParams(dimension_semantics=("arbitrary",),
                                             vmem_limit_bytes=VMEM_LIMIT),
        name="dispatch",
    )(cnt_t, off_t, seg_t, pad_end, h2, pos_t)


def _moe_kernel(be_ref, bf_ref, nx_ref, nv_ref, xs_ref, wi_hbm, bi_ref, wo_hbm, bo_ref, ys_ref,
                wi_f32, wo_f32, wi_bf, wo_bf, sem_i, sem_o):
    step = pl.program_id(0)

    def fetch(e):
        return (pltpu.make_async_copy(wi_hbm.at[e], wi_f32, sem_i),
                pltpu.make_async_copy(wo_hbm.at[e], wo_f32, sem_o))

    @pl.when(step == 0)
    def _():
        for cp in fetch(be_ref[0]):
            cp.start()

    for sub in range(BPS):
        i = step * BPS + sub

        @pl.when((i < nv_ref[0]) & (bf_ref[i] == 1))
        def _(i=i):
            for cp in fetch(be_ref[i]):
                cp.wait()
            wi_bf[...] = wi_f32[...].astype(BF16)
            wo_bf[...] = wo_f32[...].astype(BF16)

            @pl.when(nx_ref[i] >= 0)
            def _():
                for cp in fetch(nx_ref[i]):
                    cp.start()

        @pl.when(i < nv_ref[0])
        def _(i=i, sub=sub):
            e = be_ref[i]
            xb = _unpack_rows(_load_grouped(xs_ref, TM, sub * TM)).astype(BF16)
            gu = jnp.dot(xb, wi_bf[...], preferred_element_type=F32) + bi_ref[pl.ds(e, 1), :]
            gate = jnp.minimum(gu[:, :D_E], LIMIT)
            up = jnp.clip(gu[:, D_E:], -LIMIT, LIMIT)
            act = (up + 1.0) * (gate * _sigmoid(ALPHA * gate))
            y = jnp.dot(act.astype(BF16), wo_bf[...], preferred_element_type=F32) + bo_ref[pl.ds(e, 1), :]
            _store_grouped(ys_ref, _pack_rows(y), TM, sub * TM)


def _moe_call(blk_e, blk_first, blk_next, n_valid, xs, w_in, b_in, w_out, b_out):
    assert N_BLOCKS % BPS == 0

    def row_map(s, be, bf, nx, nv):
        last = (nv[0] + BPS - 1) // BPS - 1
        return (jnp.maximum(jnp.minimum(s, last), 0), 0)

    grid_spec = pltpu.PrefetchScalarGridSpec(
        num_scalar_prefetch=4,
        grid=(N_BLOCKS // BPS,),
        in_specs=[pl.BlockSpec((BPS * TM * SUB, LANES), row_map),
                  pl.BlockSpec(memory_space=pl.ANY),
                  pl.BlockSpec((N_E, 2 * D_E), lambda s, *_: (0, 0)),
                  pl.BlockSpec(memory_space=pl.ANY),
                  pl.BlockSpec((N_E, D), lambda s, *_: (0, 0))],
        out_specs=pl.BlockSpec((BPS * TM * SUB, LANES), row_map),
        scratch_shapes=[pltpu.VMEM((D, 2 * D_E), F32),
                        pltpu.VMEM((D_E, D), F32),
                        pltpu.VMEM((D, 2 * D_E), BF16),
                        pltpu.VMEM((D_E, D), BF16),
                        pltpu.SemaphoreType.DMA,
                        pltpu.SemaphoreType.DMA],
    )
    return pl.pallas_call(
        _moe_kernel,
        grid_spec=grid_spec,
        out_shape=jax.ShapeDtypeStruct((N_ROWS * SUB, LANES), U32),
        compiler_params=pltpu.CompilerParams(dimension_semantics=("arbitrary",),
                                             vmem_limit_bytes=VMEM_LIMIT),
        name="moe_experts",
    )(blk_e, blk_first, blk_next, n_valid, xs, w_in, b_in, w_out, b_out)


def _combine_kernel(cnt_ref, off_ref, seg_ref, ys_ref, pc_ref, x1_ref, gate2_ref, fg_ref, o_ref,
                    buf0, buf1, buf2, buf3, sem0, sem1, sem2, sem3):
    j = pl.program_id(0)
    head = ys_ref.at[pl.ds(0, RS * SUB)]
    even = ((buf0, sem0), (buf1, sem1))
    odd = ((buf2, sem2), (buf3, sem3))

    def fetch(step, slots):
        for t, (buf, sem) in enumerate(slots):
            _start_tile_runs(2 * step + t, ys_ref, buf, off_ref, seg_ref, cnt_ref, sem)

    @pl.when(j == 0)
    def _():
        fetch(0, even)

    col = lax.broadcasted_iota(jnp.int32, (TT, RS), 1)
    wms = []
    for t in range(2):
        pc = pc_ref[t * TT:(t + 1) * TT, :]
        pos = pc.astype(jnp.int32)
        wm = jnp.zeros((TT, RS), F32)
        for k in range(TOP_K):
            wm = jnp.where(col == pos[:, k:k + 1], pc[:, TOP_K + k:TOP_K + k + 1], wm)
        wms.append(wm.astype(BF16))

    def step(cur, nxt):
        @pl.when(j + 1 < NT // 2)
        def _():
            fetch(j + 1, nxt)
        for t, (buf, sem) in enumerate(cur):
            _wait_tile_runs(head, buf, sem)
            yt = _unpack_rows(_load_grouped(buf, RS)).astype(BF16)
            acc = jnp.dot(wms[t], yt, preferred_element_type=F32)
            x2 = x1_ref[t * TT:(t + 1) * TT, :] + gate2_ref[0] * acc
            o_ref[t * TT:(t + 1) * TT, :] = _rms(x2) * fg_ref[...]

    @pl.when(j % 2 == 0)
    def _():
        step(even, odd)

    @pl.when(j % 2 == 1)
    def _():
        step(odd, even)


def _combine_call(cnt_t, off_t, seg_t, ys, pos_c, x1, mod3, final_g):
    per_b = S // (2 * TT)
    grid_spec = pltpu.PrefetchScalarGridSpec(
        num_scalar_prefetch=3,
        grid=(NT // 2,),
        in_specs=[pl.BlockSpec(memory_space=pl.ANY),
                  pl.BlockSpec((2 * TT, N_E), lambda j, *_: (j, 0)),
                  pl.BlockSpec((2 * TT, D), lambda j, *_: (j, 0)),
                  pl.BlockSpec((1, 1, D), lambda j, *_: (j // per_b, 0, 5)),
                  pl.BlockSpec((1, D), lambda j, *_: (0, 0))],
        out_specs=pl.BlockSpec((2 * TT, D), lambda j, *_: (j, 0)),
        scratch_shapes=[pltpu.VMEM((RS * SUB, LANES), U32)] * 4 + [pltpu.SemaphoreType.DMA] * 4,
    )
    return pl.pallas_call(
        _combine_kernel,
        grid_spec=grid_spec,
        out_shape=jax.ShapeDtypeStruct((T, D), F32),
        compiler_params=pltpu.CompilerParams(dimension_semantics=("arbitrary",),
                                             vmem_limit_bytes=VMEM_LIMIT),
        name="combine_norm",
    )(cnt_t, off_t, seg_t, ys, pos_c, x1, mod3, final_g)


def kernel(x, c, ada_w, ada_b, norm1_g, w_in, ssm_a_re, ssm_a_im, ssm_log_dt, ssm_b_re, ssm_b_im, ssm_c_re, ssm_c_im, ssm_d, ssm_glu_w, ssm_glu_b, w_branch_a, gmlp_ln_g, gmlp_ln_b, gmlp_ws, gmlp_bs, w_branch_b, w_out, norm2_g, router_w, router_b, moe_w_in, moe_b_in, moe_w_out, moe_b_out, final_g):
    depth = ada_w.shape[0]
    assert depth == 1, "the final rms_norm is fused into the combine kernel of the only layer"
    for layer in range(depth):
        mod = _mod_call(c, ada_w[layer], ada_b[layer])
        mod3 = mod.reshape(B, 1, 6 * D)

        u, zuv, ga, gb = _proj_call(x, norm1_g[layer], mod3, w_in[layer].astype(BF16))

        bm, cre, cim, are, aim = _s5_params(ssm_a_re[layer], ssm_a_im[layer], ssm_log_dt[layer],
                                            ssm_b_re[layer], ssm_b_im[layer],
                                            ssm_c_re[layer], ssm_c_im[layer])
        ya = _s5_call(u, bm, cre, cim, are, aim,
                      ssm_d[layer].reshape(1, SSM_W), ssm_glu_w[layer].astype(BF16),
                      ssm_glu_b[layer].reshape(1, SSM_W), w_branch_a[layer].astype(BF16))

        ws = gmlp_ws[layer]
        ws_pairs = jnp.concatenate([ws[0::2], ws[1::2]], axis=-1)
        bias_full = jnp.repeat(gmlp_bs[layer].T, GM_HD, axis=1)
        x1, h2, logits = _mix_call(
            zuv, ga, gb, ya, x, mod3,
            gmlp_ln_g[layer].reshape(1, GM_W), gmlp_ln_b[layer].reshape(1, GM_W),
            ws_pairs, bias_full, w_branch_b[layer].astype(BF16), w_out[layer].astype(BF16),
            norm2_g[layer].reshape(1, D), router_w[layer],
            router_b[layer].reshape(1, N_E))

        pos_c, pos_t, cnt_t, pre_t, seg_t, cnt = _route_call(logits.reshape(T, N_E))
        counts = cnt[0].astype(jnp.int32)
        nblk = (counts + TM - 1) // TM
        blk_end = jnp.cumsum(nblk)
        pad_end = (blk_end * TM).astype(jnp.int32)
        experts = jnp.arange(N_E, dtype=jnp.int32)
        blk_ids = jnp.arange(N_BLOCKS, dtype=jnp.int32)
        blk_e = jnp.sum((blk_end[None, :] <= blk_ids[:, None]).astype(jnp.int32), axis=1)
        blk_e = jnp.minimum(blk_e, N_E - 1)
        blk_first = jnp.concatenate([jnp.ones((1,), jnp.int32),
                                     (blk_e[1:] != blk_e[:-1]).astype(jnp.int32)])
        later = (experts[None, :] > experts[:, None]) & (nblk[None, :] > 0)
        next_e = jnp.min(jnp.where(later, experts[None, :], N_E), axis=1)
        next_e = jnp.where(next_e == N_E, -1, next_e)
        blk_next = jnp.sum(jnp.where(blk_e[:, None] == experts[None, :], next_e[None, :], 0), axis=1)
        n_valid = blk_end[-1:].astype(jnp.int32)
        pad_start = pad_end - nblk * TM
        off_t = pre_t[:, 0, :] + pad_start[None, :]
        cnt_t, off_t, seg_t = (t.reshape(NT * N_E) for t in (cnt_t[:, 0, :], off_t, seg_t[:, 0, :]))

        xs = _dispatch_call(cnt_t, off_t, seg_t, pad_end, h2.reshape(T, D), pos_t)
        ys = _moe_call(blk_e, blk_first, blk_next.astype(jnp.int32), n_valid, xs, moe_w_in[layer],
                       moe_b_in[layer], moe_w_out[layer], moe_b_out[layer])
        x = _combine_call(cnt_t, off_t, seg_t, ys, pos_c, x1.reshape(T, D),
                          mod3, final_g.reshape(1, D)).reshape(B, S, D)
    return x
```

```python
import functools
import math

import jax
import jax.numpy as jnp
from jax import lax
from jax.experimental import pallas as pl
from jax.experimental.pallas import tpu as pltpu

F32 = jnp.float32
BF16 = jnp.bfloat16

D = 1024
B = 8
S = 2048
T = B * S
SSM_W = 512
SSM_G = 32
SSM_H = 16
SSM_P = 64
N_PACK = 4
PACK_G = SSM_G // N_PACK
GM_W = 512
GM_HEADS = 8
GM_HD = 64
CHUNK = 128
N_E = 32
TOP_K = 4
D_E = 1024
LIMIT = 7.0
ALPHA = 1.702
EPS = 1e-6

TS_PROJ = 512
L_SSM = 128
R_SSM = L_SSM * B
TS_MIX = 512
SUB_MIX = 256
TT = 256
NT = T // TT
ROUTE_TILES = 4
RS = TOP_K * TT
TM = 256
BPS = 4
N_ROWS = T * TOP_K + N_E * TM
N_BLOCKS = N_ROWS // TM
LANES = 128
HALF = D // 2
SUB = HALF // LANES
PIECE = 16
VMEM_LIMIT = 56 * 1024 * 1024
U32 = jnp.uint32


def _sigmoid(v):
    return 1.0 / (1.0 + jnp.exp(-v))


def _gelu(v):
    return 0.5 * v * (1.0 + jnp.tanh(math.sqrt(2.0 / math.pi) * (v + 0.044715 * v * v * v)))


def _rms(v):
    return v * lax.rsqrt(jnp.mean(v * v, axis=-1, keepdims=True) + EPS)


def _mod_kernel(c_ref, w_ref, b_ref, o_ref):
    cv = c_ref[...]
    sv = cv * _sigmoid(cv)
    o_ref[...] = jnp.dot(sv, w_ref[...], preferred_element_type=F32,
                         precision=lax.Precision.HIGHEST) + b_ref[...]


def _mod_call(c, ada_w, ada_b):
    n = ada_w.shape[1]
    return pl.pallas_call(
        _mod_kernel,
        grid=(n // D,),
        in_specs=[pl.BlockSpec((B, D), lambda j: (0, 0)),
                  pl.BlockSpec((D, D), lambda j: (0, j)),
                  pl.BlockSpec((1, D), lambda j: (0, j))],
        out_specs=pl.BlockSpec((B, D), lambda j: (0, j)),
        out_shape=jax.ShapeDtypeStruct((B, n), F32),
        name="adaln_mod",
    )(c, ada_w, ada_b.reshape(1, n))


def _proj_kernel(x_ref, g_ref, shift_ref, scale_ref, w_ref, u_ref, zuv_ref, ga_ref, gb_ref):
    h = _rms(x_ref[0]) * g_ref[...]
    h = h * (1.0 + scale_ref[0]) + shift_ref[0]
    hb = h.astype(BF16)
    u_ref[0] = jnp.dot(hb, w_ref[:, 0:SSM_W], preferred_element_type=F32)
    zuv_ref[0] = jnp.dot(hb, w_ref[:, SSM_W:SSM_W + 2 * GM_W], preferred_element_type=F32)
    ga_ref[0] = jnp.dot(hb, w_ref[:, SSM_W + 2 * GM_W:SSM_W + 2 * GM_W + D], preferred_element_type=F32)
    gb_ref[0] = jnp.dot(hb, w_ref[:, SSM_W + 2 * GM_W + D:], preferred_element_type=F32)


def _proj_call(x, norm_g, mod3, w_in_bf):
    pw = w_in_bf.shape[1]
    tok_spec = pl.BlockSpec((1, TS_PROJ, D), lambda b, s: (b, s, 0))
    return pl.pallas_call(
        _proj_kernel,
        grid=(B, S // TS_PROJ),
        in_specs=[tok_spec,
                  pl.BlockSpec((1, D), lambda b, s: (0, 0)),
                  pl.BlockSpec((1, 1, D), lambda b, s: (b, 0, 0)),
                  pl.BlockSpec((1, 1, D), lambda b, s: (b, 0, 1)),
                  pl.BlockSpec((D, pw), lambda b, s: (0, 0))],
        out_specs=[pl.BlockSpec((1, TS_PROJ, SSM_W), lambda b, s: (b, s, 0)),
                   tok_spec, tok_spec, tok_spec],
        out_shape=[jax.ShapeDtypeStruct((B, S, SSM_W), F32),
                   jax.ShapeDtypeStruct((B, S, D), F32),
                   jax.ShapeDtypeStruct((B, S, D), F32),
                   jax.ShapeDtypeStruct((B, S, D), F32)],
        compiler_params=pltpu.CompilerParams(vmem_limit_bytes=VMEM_LIMIT),
        name="norm_proj",
    )(x, norm_g.reshape(1, D), mod3, mod3, w_in_bf)


def _s5_kernel(u_ref, bm_ref, cre_ref, cim_ref, are_ref, aim_ref, d_ref, gw_ref, gb_ref, wa_ref,
               o_ref, usc, ysc, sre, sim, st_re, st_im):
    @pl.when(pl.program_id(0) == 0)
    def _():
        st_re[...] = jnp.zeros_like(st_re)
        st_im[...] = jnp.zeros_like(st_im)

    nslab = SSM_W // LANES
    for b in range(B):
        for c in range(nslab):
            usc[c, pl.ds(b, L_SSM, stride=B), :] = u_ref[b, :, c * LANES:(c + 1) * LANES]
    u = jnp.concatenate([usc[c] for c in range(nslab)], axis=1)
    ub = u.astype(BF16)
    half = PACK_G * SSM_P
    ys = []
    for k in range(N_PACK):
        bu = jnp.dot(ub[:, 128 * k:128 * (k + 1)], bm_ref[k], preferred_element_type=F32)
        sre[k] = bu[:, :half]
        sim[k] = bu[:, half:]
        ar = are_ref[k]
        ai = aim_ref[k]
        r = st_re[k]
        m = st_im[k]
        for t in range(L_SSM):
            rows = pl.ds(t * B, B)
            nr = ar * r - ai * m + sre[k, rows, :]
            m = ar * m + ai * r + sim[k, rows, :]
            r = nr
            sre[k, rows, :] = r
            sim[k, rows, :] = m
        st_re[k] = r
        st_im[k] = m
        yk = jnp.dot(sre[k].astype(BF16), cre_ref[k], preferred_element_type=F32)
        yk = yk + jnp.dot(sim[k].astype(BF16), cim_ref[k], preferred_element_type=F32)
        ys.append(yk)
    for c in range(nslab):
        uc = usc[c]
        ysc[c] = ys[c] + d_ref[:, c * LANES:(c + 1) * LANES] * uc
    y = jnp.concatenate(
        [jnp.concatenate([ysc[c, pl.ds(b, L_SSM, stride=B), :] for c in range(nslab)], axis=1)
         for b in range(B)], axis=0)
    z = _gelu(y)
    gl = jnp.dot(z.astype(BF16), gw_ref[...], preferred_element_type=F32) + gb_ref[...]
    out = z * _sigmoid(gl)
    o = jnp.dot(out.astype(BF16), wa_ref[...], preferred_element_type=F32)
    for b in range(B):
        o_ref[b] = o[b * L_SSM:(b + 1) * L_SSM]


def _s5_call(u, bm, cre, cim, are, aim, d_skip, glu_w, glu_b, w_a):
    half = PACK_G * SSM_P
    full = lambda *shape: pl.BlockSpec(shape, lambda i: (0,) * len(shape))
    return pl.pallas_call(
        _s5_kernel,
        grid=(S // L_SSM,),
        in_specs=[pl.BlockSpec((B, L_SSM, SSM_W), lambda i: (0, i, 0)),
                  full(N_PACK, 128, 2 * half),
                  full(N_PACK, half, 128),
                  full(N_PACK, half, 128),
                  full(N_PACK, B, half),
                  full(N_PACK, B, half),
                  full(1, SSM_W),
                  full(SSM_W, SSM_W),
                  full(1, SSM_W),
                  full(SSM_W, D)],
        out_specs=pl.BlockSpec((B, L_SSM, D), lambda i: (0, i, 0)),
        out_shape=jax.ShapeDtypeStruct((B, S, D), F32),
        scratch_shapes=[pltpu.VMEM((SSM_W // LANES, R_SSM, LANES), F32),
                        pltpu.VMEM((SSM_W // LANES, R_SSM, LANES), F32),
                        pltpu.VMEM((N_PACK, R_SSM, half), F32),
                        pltpu.VMEM((N_PACK, R_SSM, half), F32),
                        pltpu.VMEM((N_PACK, B, half), F32),
                        pltpu.VMEM((N_PACK, B, half), F32)],
        compiler_params=pltpu.CompilerParams(dimension_semantics=("arbitrary",),
                                             vmem_limit_bytes=VMEM_LIMIT),
        name="s5_branch",
    )(u, bm, cre, cim, are, aim, d_skip, glu_w, glu_b, w_a)


def _s5_params(a_re, a_im, log_dt, b_re, b_im, c_re, c_im):
    dt = jnp.exp(log_dt)[:, None]
    mag = jnp.exp(a_re * dt)
    lr = mag * jnp.cos(a_im * dt)
    li = mag * jnp.sin(a_im * dt)
    den = a_re * a_re + a_im * a_im
    cr = ((lr - 1.0) * a_re + li * a_im) / den
    ci = (li * a_re - (lr - 1.0) * a_im) / den
    bbr = cr[..., None] * b_re - ci[..., None] * b_im
    bbi = cr[..., None] * b_im + ci[..., None] * b_re
    eye = jnp.eye(PACK_G, dtype=F32)
    half = PACK_G * SSM_P

    def pack_b(m):
        m4 = m.reshape(N_PACK, PACK_G, SSM_P, SSM_H)
        return jnp.einsum('kgph,gj->kghjp', m4, eye).reshape(N_PACK, PACK_G * SSM_H, half)

    def pack_c(m):
        m4 = m.reshape(N_PACK, PACK_G, SSM_H, SSM_P)
        return jnp.einsum('kghp,gj->kgpjh', m4, eye).reshape(N_PACK, half, PACK_G * SSM_H)

    bm = jnp.concatenate([pack_b(bbr), pack_b(bbi)], axis=-1).astype(BF16)
    cre = pack_c(c_re).astype(BF16)
    cim = (-pack_c(c_im)).astype(BF16)
    are = jnp.broadcast_to(lr.reshape(N_PACK, 1, half), (N_PACK, B, half))
    aim = jnp.broadcast_to(li.reshape(N_PACK, 1, half), (N_PACK, B, half))
    return bm, cre, cim, are, aim


def _route_tile(l, carry):
    lane = lax.broadcasted_iota(jnp.int32, l.shape, 1).astype(F32)
    sels, vals = [], []
    for _ in range(TOP_K):
        m = jnp.max(l, axis=-1, keepdims=True)
        idx = jnp.min(jnp.where(l == m, lane, float(N_E)), axis=-1, keepdims=True)
        sel = lane == idx
        sels.append(sel)
        vals.append(m)
        l = jnp.where(sel, -jnp.inf, l)
    member = sels[0].astype(F32)
    for k in range(1, TOP_K):
        member = member + sels[k].astype(F32)
    tile_cnt = jnp.sum(member, axis=0, keepdims=True)

    r = lax.broadcasted_iota(jnp.int32, (N_E, N_E), 0)
    c = lax.broadcasted_iota(jnp.int32, (N_E, N_E), 1)
    tri = (r < c).astype(BF16)
    tcb = jnp.broadcast_to(tile_cnt, (8, N_E)).astype(BF16)
    seg = jnp.dot(tcb, tri, preferred_element_type=F32)[0:1]
    r = lax.broadcasted_iota(jnp.int32, (TT, TT), 0)
    c = lax.broadcasted_iota(jnp.int32, (TT, TT), 1)
    strict = (c < r).astype(BF16)
    rank = jnp.dot(strict, member.astype(BF16), preferred_element_type=F32)
    posb = seg + rank
    denom = jnp.zeros_like(vals[0])
    exps = []
    for k in range(TOP_K):
        e = jnp.exp(vals[k] - vals[0])
        exps.append(e)
        denom = denom + e
    pc = jnp.zeros(l.shape, F32)
    for k in range(TOP_K):
        pk = jnp.sum(jnp.where(sels[k], posb, 0.0), axis=-1, keepdims=True)
        pc = jnp.where(lane == float(k), pk, pc)
        pc = jnp.where(lane == float(TOP_K + k), exps[k] / denom, pc)
    r8 = lax.broadcasted_iota(jnp.int32, (8, N_E), 0)
    c8 = lax.broadcasted_iota(jnp.int32, (8, N_E), 1)
    eye = (r8 == c8).astype(F32)
    pt = lax.dot_general(eye, pc, (((1,), (1,)), ((), ())),
                         preferred_element_type=F32, precision=lax.Precision.HIGHEST)
    earlier = carry[...]
    carry[...] = earlier + tile_cnt
    return pc, pt, tile_cnt, earlier, seg


def _mix_kernel(zuv_ref, ga_ref, gb_ref, ya_ref, x_ref, gate1_ref, shift2_ref, scale2_ref,
                lng_ref, lnb_ref, ws_ref, bias_ref, wbb_ref, wo_ref, n2g_ref, rw_ref, rb_ref,
                x1_ref, h2_ref, lg_ref):
    row = lax.broadcasted_iota(jnp.int32, (CHUNK, 2 * CHUNK), 0)
    col = lax.broadcasted_iota(jnp.int32, (CHUNK, 2 * CHUNK), 1)
    causal = (col % CHUNK) <= row
    lane = lax.broadcasted_iota(jnp.int32, (CHUNK, 2 * GM_HD), 1)
    first = lane < GM_HD
    wpairs = [jnp.where(causal, ws_ref[j], 0.0).astype(BF16) for j in range(GM_HEADS // 2)]

    for g in range(TS_MIX // SUB_MIX):
        rows = pl.ds(g * SUB_MIX, SUB_MIX)
        z = _gelu(zuv_ref[0, rows, :])
        u = z[:, :GM_W]
        v = z[:, GM_W:]
        mu = jnp.mean(v, axis=-1, keepdims=True)
        vc = v - mu
        var = jnp.mean(vc * vc, axis=-1, keepdims=True)
        vn = vc * lax.rsqrt(var + EPS) * lng_ref[...] + lnb_ref[...]
        chunks = []
        for n in range(SUB_MIX // CHUNK):
            cols = []
            for j in range(GM_HEADS // 2):
                vp = vn[n * CHUNK:(n + 1) * CHUNK, 2 * GM_HD * j:2 * GM_HD * (j + 1)]
                rhs = jnp.concatenate([jnp.where(first, vp, 0.0), jnp.where(first, 0.0, vp)], axis=0)
                cols.append(jnp.dot(wpairs[j], rhs.astype(BF16), preferred_element_type=F32))
            chunks.append(jnp.concatenate(cols, axis=1) + bias_ref[...])
        mixed = jnp.concatenate(chunks, axis=0)
        gm = u * mixed
        yb = jnp.dot(gm.astype(BF16), wbb_ref[...], preferred_element_type=F32)
        merged = _sigmoid(ga_ref[0, rows, :]) * ya_ref[0, rows, :] + _sigmoid(gb_ref[0, rows, :]) * yb
        o = jnp.dot(merged.astype(BF16), wo_ref[...], preferred_element_type=F32)
        x1 = x_ref[0, rows, :] + gate1_ref[0] * o
        x1_ref[0, rows, :] = x1
        h2 = _rms(x1) * n2g_ref[...]
        h2 = h2 * (1.0 + scale2_ref[0]) + shift2_ref[0]
        h2_ref[0, rows, :] = h2.astype(BF16)
        lg_ref[0, rows, :] = jnp.dot(h2, rw_ref[...], preferred_element_type=F32,
                                     precision=lax.Precision.HIGHEST) + rb_ref[...]


def _route_kernel(lg_ref, pc_ref, pt_ref, cntt_ref, pret_ref, segt_ref, cnt_ref, carry):
    @pl.when(pl.program_id(0) == 0)
    def _():
        carry[...] = jnp.zeros_like(carry)

    for t in range(ROUTE_TILES):
        pc, pt, tile_cnt, earlier, seg = _route_tile(lg_ref[t * TT:(t + 1) * TT, :], carry)
        pc_ref[t * TT:(t + 1) * TT, :] = pc
        pt_ref[:, t * TT:(t + 1) * TT] = pt
        cntt_ref[t] = jnp.broadcast_to(tile_cnt, (8, N_E)).astype(jnp.int32)
        pret_ref[t] = jnp.broadcast_to(earlier, (8, N_E)).astype(jnp.int32)
        segt_ref[t] = jnp.broadcast_to(seg, (8, N_E)).astype(jnp.int32)
    cnt_ref[...] = jnp.broadcast_to(carry[...], cnt_ref.shape)


def _route_call(logits):
    assert NT % ROUTE_TILES == 0
    tbl_spec = pl.BlockSpec((ROUTE_TILES, 8, N_E), lambda i: (i, 0, 0))
    tbl_shape = jax.ShapeDtypeStruct((NT, 8, N_E), jnp.int32)
    return pl.pallas_call(
        _route_kernel,
        grid=(NT // ROUTE_TILES,),
        in_specs=[pl.BlockSpec((ROUTE_TILES * TT, N_E), lambda i: (i, 0))],
        out_specs=[pl.BlockSpec((ROUTE_TILES * TT, N_E), lambda i: (i, 0)),
                   pl.BlockSpec((8, ROUTE_TILES * TT), lambda i: (0, i)),
                   tbl_spec, tbl_spec, tbl_spec,
                   pl.BlockSpec((8, N_E), lambda i: (0, 0))],
        out_shape=[jax.ShapeDtypeStruct((T, N_E), F32),
                   jax.ShapeDtypeStruct((8, T), F32),
                   tbl_shape, tbl_shape, tbl_shape,
                   jax.ShapeDtypeStruct((8, N_E), F32)],
        scratch_shapes=[pltpu.VMEM((1, N_E), F32)],
        compiler_params=pltpu.CompilerParams(dimension_semantics=("arbitrary",)),
        name="route",
    )(logits)


def _mix_call(zuv, ga, gb, ya2d, x, mod3, ln_g, ln_b, ws_pairs, bias_full, wbb, wo, n2g, rw, rb):
    tok_spec = pl.BlockSpec((1, TS_MIX, D), lambda b, s: (b, s, 0))
    full = lambda *shape: pl.BlockSpec(shape, lambda b, s: (0,) * len(shape))
    mod_spec = lambda j: pl.BlockSpec((1, 1, D), lambda b, s: (b, 0, j))
    return pl.pallas_call(
        _mix_kernel,
        grid=(B, S // TS_MIX),
        in_specs=[tok_spec, tok_spec, tok_spec, tok_spec, tok_spec,
                  mod_spec(2), mod_spec(3), mod_spec(4),
                  full(1, GM_W), full(1, GM_W),
                  full(GM_HEADS // 2, CHUNK, 2 * CHUNK),
                  full(CHUNK, GM_W),
                  full(GM_W, D), full(D, D), full(1, D),
                  full(D, N_E), full(1, N_E)],
        out_specs=[tok_spec, tok_spec,
                   pl.BlockSpec((1, TS_MIX, N_E), lambda b, s: (b, s, 0))],
        out_shape=[jax.ShapeDtypeStruct((B, S, D), F32),
                   jax.ShapeDtypeStruct((B, S, D), BF16),
                   jax.ShapeDtypeStruct((B, S, N_E), F32)],
        compiler_params=pltpu.CompilerParams(vmem_limit_bytes=VMEM_LIMIT),
        name="gmlp_merge_norm2",
    )(zuv, ga, gb, ya2d, x, mod3, mod3, mod3, ln_g, ln_b, ws_pairs, bias_full, wbb, wo, n2g, rw, rb)


def _pack_rows(v):
    return pltpu.pack_elementwise([v[:, :HALF], v[:, HALF:]], packed_dtype=BF16)


def _unpack_rows(w):
    halves = [pltpu.unpack_elementwise(w, index=i, packed_dtype=BF16, unpacked_dtype=F32)
              for i in range(2)]
    return jnp.concatenate(halves, axis=1)


def _load_grouped(ref, rows, first=0):
    return jnp.concatenate([ref[pl.ds(first * SUB + c, rows, stride=SUB), :] for c in range(SUB)], axis=1)


def _store_grouped(ref, w, rows, first=0):
    for c in range(SUB):
        ref[pl.ds(first * SUB + c, rows, stride=SUB), :] = w[:, c * LANES:(c + 1) * LANES]


def _start_run(src_ref, dst_ref, src_row, dst_row, n, sem):
    def start(rows, done):
        s = pl.multiple_of((src_row + done) * SUB, SUB)
        d = pl.multiple_of((dst_row + done) * SUB, SUB)
        pltpu.make_async_copy(src_ref.at[pl.ds(s, rows * SUB)],
                              dst_ref.at[pl.ds(d, rows * SUB)], sem).start()

    nbulk = jnp.right_shift(n, PIECE.bit_length() - 1)

    def bulk(pi, carry):
        start(PIECE, pi * PIECE)
        return carry

    lax.fori_loop(0, nbulk, bulk, 0)
    done = nbulk * PIECE
    for rows in [PIECE >> s for s in range(1, PIECE.bit_length())]:
        @pl.when((n & rows) != 0)
        def _(rows=rows, done=done):
            start(rows, done)
        done = done + (n & rows)


def _start_tile_runs(tile, src_ref, dst_ref, src_tbl, dst_tbl, cnt_tbl, sem):
    base = tile * N_E

    def per_expert(e, carry):
        _start_run(src_ref, dst_ref, src_tbl[base + e], dst_tbl[base + e], cnt_tbl[base + e], sem)
        return carry

    lax.fori_loop(0, N_E, per_expert, 0)


def _wait_tile_runs(src_ref, dst_ref, sem):
    pltpu.make_async_copy(src_ref, dst_ref, sem).wait()


def _dispatch_kernel(cnt_ref, off_ref, seg_ref, pend_ref, h_ref, pt_ref, xs_ref,
                     sbuf0, sbuf1, zbuf, sem_z, sem0, sem1):
    j = pl.program_id(0)

    @pl.when(j == 0)
    def _():
        zbuf[...] = jnp.zeros_like(zbuf)
        for e in range(N_E):
            prev = pend_ref[e - 1] if e > 0 else 0
            end = pend_ref[e]

            @pl.when(end > prev)
            def _():
                first = pl.multiple_of((end - TM) * SUB, TM * SUB)
                cp = pltpu.make_async_copy(zbuf, xs_ref.at[pl.ds(first, TM * SUB)], sem_z)
                cp.start()
                cp.wait()

    rows = lax.broadcasted_iota(jnp.int32, (RS, TT), 0)
    words = []
    for t in range(2):
        pos = pt_ref[:, t * TT:(t + 1) * TT].astype(jnp.int32)
        hit = rows == pos[0:1, :]
        for k in range(1, TOP_K):
            hit = hit | (rows == pos[k:k + 1, :])
        pm = jnp.where(hit, 1.0, 0.0).astype(BF16)
        srt = jnp.dot(pm, h_ref[t * TT:(t + 1) * TT, :], preferred_element_type=F32)
        words.append(_pack_rows(srt))
    head = xs_ref.at[pl.ds(0, RS * SUB)]

    for t, (sbuf, sem) in enumerate(((sbuf0, sem0), (sbuf1, sem1))):
        @pl.when(j >= 1)
        def _(sbuf=sbuf, sem=sem):
            _wait_tile_runs(sbuf, head, sem)
        _store_grouped(sbuf, words[t], RS)
        _start_tile_runs(2 * j + t, sbuf, xs_ref, seg_ref, off_ref, cnt_ref, sem)

    @pl.when(j == NT // 2 - 1)
    def _():
        _wait_tile_runs(sbuf0, head, sem0)
        _wait_tile_runs(sbuf1, head, sem1)


def _dispatch_call(cnt_t, off_t, seg_t, pad_end, h2, pos_t):
    assert NT % 2 == 0
    grid_spec = pltpu.PrefetchScalarGridSpec(
        num_scalar_prefetch=4,
        grid=(NT // 2,),
        in_specs=[pl.BlockSpec((2 * TT, D), lambda j, *_: (j, 0)),
                  pl.BlockSpec((8, 2 * TT), lambda j, *_: (0, j))],
        out_specs=pl.BlockSpec(memory_space=pl.ANY),
        scratch_shapes=[pltpu.VMEM((RS * SUB, LANES), U32),
                        pltpu.VMEM((RS * SUB, LANES), U32),
                        pltpu.VMEM((TM * SUB, LANES), U32),
                        pltpu.SemaphoreType.DMA,
                        pltpu.SemaphoreType.DMA,
                        pltpu.SemaphoreType.DMA],
    )
    return pl.pallas_call(
        _dispatch_kernel,
        grid_spec=grid_spec,
        out_shape=jax.ShapeDtypeStruct((N_ROWS * SUB, LANES), U32),
        compiler_params=pltpu.CompilerParams(dimension_semantics=("arbitrary",),
                                             vmem_limit_bytes=VMEM_LIMIT),
        name="dispatch",
    )(cnt_t, off_t, seg_t, pad_end, h2, pos_t)


def _moe_kernel(be_ref, bf_ref, nx_ref, nv_ref, xs_ref, wi_hbm, bi_ref, wo_hbm, bo_ref, ys_ref,
                wi_f32, wo_f32, wi_bf, wo_bf, sem_i, sem_o):
    step = pl.program_id(0)

    def fetch(e):
        return (pltpu.make_async_copy(wi_hbm.at[e], wi_f32, sem_i),
                pltpu.make_async_copy(wo_hbm.at[e], wo_f32, sem_o))

    @pl.when(step == 0)
    def _():
        for cp in fetch(be_ref[0]):
            cp.start()

    def load_weights(i):
        @pl.when(bf_ref[i] == 1)
        def _():
            for cp in fetch(be_ref[i]):
                cp.wait()
            wi_bf[...] = wi_f32[...].astype(BF16)
            wo_bf[...] = wo_f32[...].astype(BF16)

            @pl.when(nx_ref[i] >= 0)
            def _():
                for cp in fetch(nx_ref[i]):
                    cp.start()

    def ffn(i, first, rows):
        e = be_ref[i]
        xb = _unpack_rows(_load_grouped(xs_ref, rows, first)).astype(BF16)
        gu = jnp.dot(xb, wi_bf[...], preferred_element_type=F32) + bi_ref[pl.ds(e, 1), :]
        gate = jnp.minimum(gu[:, :D_E], LIMIT)
        up = jnp.clip(gu[:, D_E:], -LIMIT, LIMIT)
        act = (up + 1.0) * (gate * _sigmoid(ALPHA * gate))
        y = jnp.dot(act.astype(BF16), wo_bf[...], preferred_element_type=F32) + bo_ref[pl.ds(e, 1), :]
        _store_grouped(ys_ref, _pack_rows(y), rows, first)

    i0 = step * BPS
    last = i0 + BPS - 1
    uniform = (last < nv_ref[0]) & (be_ref[i0] == be_ref[last])

    @pl.when(uniform)
    def _():
        load_weights(i0)
        ffn(i0, 0, BPS * TM)

    @pl.when(jnp.logical_not(uniform))
    def _():
        for sub in range(BPS):
            i = i0 + sub

            @pl.when(i < nv_ref[0])
            def _(i=i, sub=sub):
                load_weights(i)
                ffn(i, sub * TM, TM)


def _moe_call(blk_e, blk_first, blk_next, n_valid, xs, w_in, b_in, w_out, b_out):
    assert N_BLOCKS % BPS == 0

    def row_map(s, be, bf, nx, nv):
        last = (nv[0] + BPS - 1) // BPS - 1
        return (jnp.maximum(jnp.minimum(s, last), 0), 0)

    grid_spec = pltpu.PrefetchScalarGridSpec(
        num_scalar_prefetch=4,
        grid=(N_BLOCKS // BPS,),
        in_specs=[pl.BlockSpec((BPS * TM * SUB, LANES), row_map),
                  pl.BlockSpec(memory_space=pl.ANY),
                  pl.BlockSpec((N_E, 2 * D_E), lambda s, *_: (0, 0)),
                  pl.BlockSpec(memory_space=pl.ANY),
                  pl.BlockSpec((N_E, D), lambda s, *_: (0, 0))],
        out_specs=pl.BlockSpec((BPS * TM * SUB, LANES), row_map),
        scratch_shapes=[pltpu.VMEM((D, 2 * D_E), F32),
                        pltpu.VMEM((D_E, D), F32),
                        pltpu.VMEM((D, 2 * D_E), BF16),
                        pltpu.VMEM((D_E, D), BF16),
                        pltpu.SemaphoreType.DMA,
                        pltpu.SemaphoreType.DMA],
    )
    return pl.pallas_call(
        _moe_kernel,
        grid_spec=grid_spec,
        out_shape=jax.ShapeDtypeStruct((N_ROWS * SUB, LANES), U32),
        compiler_params=pltpu.CompilerParams(dimension_semantics=("arbitrary",),
                                             vmem_limit_bytes=VMEM_LIMIT),
        name="moe_experts",
    )(blk_e, blk_first, blk_next, n_valid, xs, w_in, b_in, w_out, b_out)


def _combine_kernel(cnt_ref, off_ref, seg_ref, ys_ref, pc_ref, x1_ref, gate2_ref, fg_ref, o_ref,
                    buf0, buf1, buf2, buf3, sem0, sem1, sem2, sem3):
    j = pl.program_id(0)
    head = ys_ref.at[pl.ds(0, RS * SUB)]
    even = ((buf0, sem0), (buf1, sem1))
    odd = ((buf2, sem2), (buf3, sem3))

    def fetch(step, slots):
        for t, (buf, sem) in enumerate(slots):
            _start_tile_runs(2 * step + t, ys_ref, buf, off_ref, seg_ref, cnt_ref, sem)

    @pl.when(j == 0)
    def _():
        fetch(0, even)

    col = lax.broadcasted_iota(jnp.int32, (TT, RS), 1)
    wms = []
    for t in range(2):
        pc = pc_ref[t * TT:(t + 1) * TT, :]
        pos = pc.astype(jnp.int32)
        wm = jnp.zeros((TT, RS), F32)
        for k in range(TOP_K):
            wm = jnp.where(col == pos[:, k:k + 1], pc[:, TOP_K + k:TOP_K + k + 1], wm)
        wms.append(wm.astype(BF16))

    def step(cur, nxt):
        @pl.when(j + 1 < NT // 2)
        def _():
            fetch(j + 1, nxt)
        for t, (buf, sem) in enumerate(cur):
            _wait_tile_runs(head, buf, sem)
            yt = _unpack_rows(_load_grouped(buf, RS)).astype(BF16)
            acc = jnp.dot(wms[t], yt, preferred_element_type=F32)
            x2 = x1_ref[t * TT:(t + 1) * TT, :] + gate2_ref[0] * acc
            o_ref[t * TT:(t + 1) * TT, :] = _rms(x2) * fg_ref[...]

    @pl.when(j % 2 == 0)
    def _():
        step(even, odd)

    @pl.when(j % 2 == 1)
    def _():
        step(odd, even)


def _combine_call(cnt_t, off_t, seg_t, ys, pos_c, x1, mod3, final_g):
    per_b = S // (2 * TT)
    grid_spec = pltpu.PrefetchScalarGridSpec(
        num_scalar_prefetch=3,
        grid=(NT // 2,),
        in_specs=[pl.BlockSpec(memory_space=pl.ANY),
                  pl.BlockSpec((2 * TT, N_E), lambda j, *_: (j, 0)),
                  pl.BlockSpec((2 * TT, D), lambda j, *_: (j, 0)),
                  pl.BlockSpec((1, 1, D), lambda j, *_: (j // per_b, 0, 5)),
                  pl.BlockSpec((1, D), lambda j, *_: (0, 0))],
        out_specs=pl.BlockSpec((2 * TT, D), lambda j, *_: (j, 0)),
        scratch_shapes=[pltpu.VMEM((RS * SUB, LANES), U32)] * 4 + [pltpu.SemaphoreType.DMA] * 4,
    )
    return pl.pallas_call(
        _combine_kernel,
        grid_spec=grid_spec,
        out_shape=jax.ShapeDtypeStruct((T, D), F32),
        compiler_params=pltpu.CompilerParams(dimension_semantics=("arbitrary",),
                                             vmem_limit_bytes=VMEM_LIMIT),
        name="combine_norm",
    )(cnt_t, off_t, seg_t, ys, pos_c, x1, mod3, final_g)


def kernel(x, c, ada_w, ada_b, norm1_g, w_in, ssm_a_re, ssm_a_im, ssm_log_dt, ssm_b_re, ssm_b_im, ssm_c_re, ssm_c_im, ssm_d, ssm_glu_w, ssm_glu_b, w_branch_a, gmlp_ln_g, gmlp_ln_b, gmlp_ws, gmlp_bs, w_branch_b, w_out, norm2_g, router_w, router_b, moe_w_in, moe_b_in, moe_w_out, moe_b_out, final_g):
    depth = ada_w.shape[0]
    assert depth == 1, "the final rms_norm is fused into the combine kernel of the only layer"
    for layer in range(depth):
        mod = _mod_call(c, ada_w[layer], ada_b[layer])
        mod3 = mod.reshape(B, 1, 6 * D)

        u, zuv, ga, gb = _proj_call(x, norm1_g[layer], mod3, w_in[layer].astype(BF16))

        bm, cre, cim, are, aim = _s5_params(ssm_a_re[layer], ssm_a_im[layer], ssm_log_dt[layer],
                                            ssm_b_re[layer], ssm_b_im[layer],
                                            ssm_c_re[layer], ssm_c_im[layer])
        ya = _s5_call(u, bm, cre, cim, are, aim,
                      ssm_d[layer].reshape(1, SSM_W), ssm_glu_w[layer].astype(BF16),
                      ssm_glu_b[layer].reshape(1, SSM_W), w_branch_a[layer].astype(BF16))

        ws = gmlp_ws[layer]
        ws_pairs = jnp.concatenate([ws[0::2], ws[1::2]], axis=-1)
        bias_full = jnp.repeat(gmlp_bs[layer].T, GM_HD, axis=1)
        x1, h2, logits = _mix_call(
            zuv, ga, gb, ya, x, mod3,
            gmlp_ln_g[layer].reshape(1, GM_W), gmlp_ln_b[layer].reshape(1, GM_W),
            ws_pairs, bias_full, w_branch_b[layer].astype(BF16), w_out[layer].astype(BF16),
            norm2_g[layer].reshape(1, D), router_w[layer],
            router_b[layer].reshape(1, N_E))

        pos_c, pos_t, cnt_t, pre_t, seg_t, cnt = _route_call(logits.reshape(T, N_E))
        counts = cnt[0].astype(jnp.int32)
        nblk = (counts + TM - 1) // TM
        blk_end = jnp.cumsum(nblk)
        pad_end = (blk_end * TM).astype(jnp.int32)
        experts = jnp.arange(N_E, dtype=jnp.int32)
        blk_ids = jnp.arange(N_BLOCKS, dtype=jnp.int32)
        blk_e = jnp.sum((blk_end[None, :] <= blk_ids[:, None]).astype(jnp.int32), axis=1)
        blk_e = jnp.minimum(blk_e, N_E - 1)
        blk_first = jnp.concatenate([jnp.ones((1,), jnp.int32),
                                     (blk_e[1:] != blk_e[:-1]).astype(jnp.int32)])
        later = (experts[None, :] > experts[:, None]) & (nblk[None, :] > 0)
        next_e = jnp.min(jnp.where(later, experts[None, :], N_E), axis=1)
        next_e = jnp.where(next_e == N_E, -1, next_e)
        blk_next = jnp.sum(jnp.where(blk_e[:, None] == experts[None, :], next_e[None, :], 0), axis=1)
        n_valid = blk_end[-1:].astype(jnp.int32)
        pad_start = pad_end - nblk * TM
        off_t = pre_t[:, 0, :] + pad_start[None, :]
        cnt_t, off_t, seg_t = (t.reshape(NT * N_E) for t in (cnt_t[:, 0, :], off_t, seg_t[:, 0, :]))

        xs = _dispatch_call(cnt_t, off_t, seg_t, pad_end, h2.reshape(T, D), pos_t)
        ys = _moe_call(blk_e, blk_first, blk_next.astype(jnp.int32), n_valid, xs, moe_w_in[layer],
                       moe_b_in[layer], moe_w_out[layer], moe_b_out[layer])
        x = _combine_call(cnt_t, off_t, seg_t, ys, pos_c, x1.reshape(T, D),
                          mod3, final_g.reshape(1, D)).reshape(B, S, D)
    return x
```

```python
import functools
import math

import jax
import jax.numpy as jnp
from jax import lax
from jax.experimental import pallas as pl
from jax.experimental.pallas import tpu as pltpu

F32 = jnp.float32
BF16 = jnp.bfloat16

D = 1024
B = 8
S = 2048
T = B * S
SSM_W = 512
SSM_G = 32
SSM_H = 16
SSM_P = 64
N_PACK = 4
PACK_G = SSM_G // N_PACK
GM_W = 512
GM_HEADS = 8
GM_HD = 64
CHUNK = 128
N_E = 32
TOP_K = 4
D_E = 1024
LIMIT = 7.0
ALPHA = 1.702
EPS = 1e-6

TS_PROJ = 512
L_SSM = 128
R_SSM = L_SSM * B
TS_MIX = 512
SUB_MIX = 256
TT = 256
NT = T // TT
ROUTE_TILES = 4
RS = TOP_K * TT
TM = 256
BPS = 4
N_ROWS = T * TOP_K + N_E * TM
N_BLOCKS = N_ROWS // TM
LANES = 128
HALF = D // 2
SUB = HALF // LANES
PIECE = 16
COPY_CLASSES = ((PIECE, 0),) + tuple((PIECE >> s, RS // PIECE + N_E * (s - 1))
                                     for s in range(1, PIECE.bit_length()))
LIST_W = 256
assert COPY_CLASSES[-1][1] + N_E <= LIST_W and len(COPY_CLASSES) <= 8
VMEM_LIMIT = 56 * 1024 * 1024
U32 = jnp.uint32


def _sigmoid(v):
    return 1.0 / (1.0 + jnp.exp(-v))


def _gelu(v):
    return 0.5 * v * (1.0 + jnp.tanh(math.sqrt(2.0 / math.pi) * (v + 0.044715 * v * v * v)))


def _rms(v):
    return v * lax.rsqrt(jnp.mean(v * v, axis=-1, keepdims=True) + EPS)


def _mod_kernel(c_ref, w_ref, b_ref, o_ref):
    cv = c_ref[...]
    sv = cv * _sigmoid(cv)
    o_ref[...] = jnp.dot(sv, w_ref[...], preferred_element_type=F32,
                         precision=lax.Precision.HIGHEST) + b_ref[...]


def _mod_call(c, ada_w, ada_b):
    n = ada_w.shape[1]
    return pl.pallas_call(
        _mod_kernel,
        grid=(n // D,),
        in_specs=[pl.BlockSpec((B, D), lambda j: (0, 0)),
                  pl.BlockSpec((D, D), lambda j: (0, j)),
                  pl.BlockSpec((1, D), lambda j: (0, j))],
        out_specs=pl.BlockSpec((B, D), lambda j: (0, j)),
        out_shape=jax.ShapeDtypeStruct((B, n), F32),
        name="adaln_mod",
    )(c, ada_w, ada_b.reshape(1, n))


def _proj_kernel(x_ref, g_ref, shift_ref, scale_ref, w_ref, u_ref, zuv_ref, ga_ref, gb_ref):
    h = _rms(x_ref[0]) * g_ref[...]
    h = h * (1.0 + scale_ref[0]) + shift_ref[0]
    hb = h.astype(BF16)
    u_ref[0] = jnp.dot(hb, w_ref[:, 0:SSM_W], preferred_element_type=F32)
    zuv_ref[0] = jnp.dot(hb, w_ref[:, SSM_W:SSM_W + 2 * GM_W], preferred_element_type=F32)
    ga_ref[0] = jnp.dot(hb, w_ref[:, SSM_W + 2 * GM_W:SSM_W + 2 * GM_W + D], preferred_element_type=F32)
    gb_ref[0] = jnp.dot(hb, w_ref[:, SSM_W + 2 * GM_W + D:], preferred_element_type=F32)


def _proj_call(x, norm_g, mod3, w_in_bf):
    pw = w_in_bf.shape[1]
    tok_spec = pl.BlockSpec((1, TS_PROJ, D), lambda b, s: (b, s, 0))
    return pl.pallas_call(
        _proj_kernel,
        grid=(B, S // TS_PROJ),
        in_specs=[tok_spec,
                  pl.BlockSpec((1, D), lambda b, s: (0, 0)),
                  pl.BlockSpec((1, 1, D), lambda b, s: (b, 0, 0)),
                  pl.BlockSpec((1, 1, D), lambda b, s: (b, 0, 1)),
                  pl.BlockSpec((D, pw), lambda b, s: (0, 0))],
        out_specs=[pl.BlockSpec((1, TS_PROJ, SSM_W), lambda b, s: (b, s, 0)),
                   tok_spec, tok_spec, tok_spec],
        out_shape=[jax.ShapeDtypeStruct((B, S, SSM_W), F32),
                   jax.ShapeDtypeStruct((B, S, D), F32),
                   jax.ShapeDtypeStruct((B, S, D), F32),
                   jax.ShapeDtypeStruct((B, S, D), F32)],
        compiler_params=pltpu.CompilerParams(vmem_limit_bytes=VMEM_LIMIT),
        name="norm_proj",
    )(x, norm_g.reshape(1, D), mod3, mod3, w_in_bf)


def _s5_kernel(u_ref, bm_ref, cre_ref, cim_ref, are_ref, aim_ref, d_ref, gw_ref, gb_ref, wa_ref,
               o_ref, usc, ysc, sre, sim, st_re, st_im):
    @pl.when(pl.program_id(0) == 0)
    def _():
        st_re[...] = jnp.zeros_like(st_re)
        st_im[...] = jnp.zeros_like(st_im)

    nslab = SSM_W // LANES
    for b in range(B):
        for c in range(nslab):
            usc[c, pl.ds(b, L_SSM, stride=B), :] = u_ref[b, :, c * LANES:(c + 1) * LANES]
    u = jnp.concatenate([usc[c] for c in range(nslab)], axis=1)
    ub = u.astype(BF16)
    half = PACK_G * SSM_P
    ys = []
    for k in range(N_PACK):
        bu = jnp.dot(ub[:, 128 * k:128 * (k + 1)], bm_ref[k], preferred_element_type=F32)
        sre[k] = bu[:, :half]
        sim[k] = bu[:, half:]
        ar = are_ref[k]
        ai = aim_ref[k]
        r = st_re[k]
        m = st_im[k]
        for t in range(L_SSM):
            rows = pl.ds(t * B, B)
            nr = ar * r - ai * m + sre[k, rows, :]
            m = ar * m + ai * r + sim[k, rows, :]
            r = nr
            sre[k, rows, :] = r
            sim[k, rows, :] = m
        st_re[k] = r
        st_im[k] = m
        yk = jnp.dot(sre[k].astype(BF16), cre_ref[k], preferred_element_type=F32)
        yk = yk + jnp.dot(sim[k].astype(BF16), cim_ref[k], preferred_element_type=F32)
        ys.append(yk)
    for c in range(nslab):
        uc = usc[c]
        ysc[c] = ys[c] + d_ref[:, c * LANES:(c + 1) * LANES] * uc
    y = jnp.concatenate(
        [jnp.concatenate([ysc[c, pl.ds(b, L_SSM, stride=B), :] for c in range(nslab)], axis=1)
         for b in range(B)], axis=0)
    z = _gelu(y)
    gl = jnp.dot(z.astype(BF16), gw_ref[...], preferred_element_type=F32) + gb_ref[...]
    out = z * _sigmoid(gl)
    o = jnp.dot(out.astype(BF16), wa_ref[...], preferred_element_type=F32)
    for b in range(B):
        o_ref[b] = o[b * L_SSM:(b + 1) * L_SSM]


def _s5_call(u, bm, cre, cim, are, aim, d_skip, glu_w, glu_b, w_a):
    half = PACK_G * SSM_P
    full = lambda *shape: pl.BlockSpec(shape, lambda i: (0,) * len(shape))
    return pl.pallas_call(
        _s5_kernel,
        grid=(S // L_SSM,),
        in_specs=[pl.BlockSpec((B, L_SSM, SSM_W), lambda i: (0, i, 0)),
                  full(N_PACK, 128, 2 * half),
                  full(N_PACK, half, 128),
                  full(N_PACK, half, 128),
                  full(N_PACK, B, half),
                  full(N_PACK, B, half),
                  full(1, SSM_W),
                  full(SSM_W, SSM_W),
                  full(1, SSM_W),
                  full(SSM_W, D)],
        out_specs=pl.BlockSpec((B, L_SSM, D), lambda i: (0, i, 0)),
        out_shape=jax.ShapeDtypeStruct((B, S, D), F32),
        scratch_shapes=[pltpu.VMEM((SSM_W // LANES, R_SSM, LANES), F32),
                        pltpu.VMEM((SSM_W // LANES, R_SSM, LANES), F32),
                        pltpu.VMEM((N_PACK, R_SSM, half), F32),
                        pltpu.VMEM((N_PACK, R_SSM, half), F32),
                        pltpu.VMEM((N_PACK, B, half), F32),
                        pltpu.VMEM((N_PACK, B, half), F32)],
        compiler_params=pltpu.CompilerParams(dimension_semantics=("arbitrary",),
                                             vmem_limit_bytes=VMEM_LIMIT),
        name="s5_branch",
    )(u, bm, cre, cim, are, aim, d_skip, glu_w, glu_b, w_a)


def _s5_params(a_re, a_im, log_dt, b_re, b_im, c_re, c_im):
    dt = jnp.exp(log_dt)[:, None]
    mag = jnp.exp(a_re * dt)
    lr = mag * jnp.cos(a_im * dt)
    li = mag * jnp.sin(a_im * dt)
    den = a_re * a_re + a_im * a_im
    cr = ((lr - 1.0) * a_re + li * a_im) / den
    ci = (li * a_re - (lr - 1.0) * a_im) / den
    bbr = cr[..., None] * b_re - ci[..., None] * b_im
    bbi = cr[..., None] * b_im + ci[..., None] * b_re
    eye = jnp.eye(PACK_G, dtype=F32)
    half = PACK_G * SSM_P

    def pack_b(m):
        m4 = m.reshape(N_PACK, PACK_G, SSM_P, SSM_H)
        return jnp.einsum('kgph,gj->kghjp', m4, eye).reshape(N_PACK, PACK_G * SSM_H, half)

    def pack_c(m):
        m4 = m.reshape(N_PACK, PACK_G, SSM_H, SSM_P)
        return jnp.einsum('kghp,gj->kgpjh', m4, eye).reshape(N_PACK, half, PACK_G * SSM_H)

    bm = jnp.concatenate([pack_b(bbr), pack_b(bbi)], axis=-1).astype(BF16)
    cre = pack_c(c_re).astype(BF16)
    cim = (-pack_c(c_im)).astype(BF16)
    are = jnp.broadcast_to(lr.reshape(N_PACK, 1, half), (N_PACK, B, half))
    aim = jnp.broadcast_to(li.reshape(N_PACK, 1, half), (N_PACK, B, half))
    return bm, cre, cim, are, aim


def _route_tile(l, carry):
    lane = lax.broadcasted_iota(jnp.int32, l.shape, 1).astype(F32)
    sels, vals = [], []
    for _ in range(TOP_K):
        m = jnp.max(l, axis=-1, keepdims=True)
        idx = jnp.min(jnp.where(l == m, lane, float(N_E)), axis=-1, keepdims=True)
        sel = lane == idx
        sels.append(sel)
        vals.append(m)
        l = jnp.where(sel, -jnp.inf, l)
    member = sels[0].astype(F32)
    for k in range(1, TOP_K):
        member = member + sels[k].astype(F32)
    tile_cnt = jnp.sum(member, axis=0, keepdims=True)

    r = lax.broadcasted_iota(jnp.int32, (N_E, N_E), 0)
    c = lax.broadcasted_iota(jnp.int32, (N_E, N_E), 1)
    tri = (r < c).astype(BF16)
    tcb = jnp.broadcast_to(tile_cnt, (8, N_E)).astype(BF16)
    seg = jnp.dot(tcb, tri, preferred_element_type=F32)[0:1]
    r = lax.broadcasted_iota(jnp.int32, (TT, TT), 0)
    c = lax.broadcasted_iota(jnp.int32, (TT, TT), 1)
    strict = (c < r).astype(BF16)
    rank = jnp.dot(strict, member.astype(BF16), preferred_element_type=F32)
    posb = seg + rank
    denom = jnp.zeros_like(vals[0])
    exps = []
    for k in range(TOP_K):
        e = jnp.exp(vals[k] - vals[0])
        exps.append(e)
        denom = denom + e
    pc = jnp.zeros(l.shape, F32)
    for k in range(TOP_K):
        pk = jnp.sum(jnp.where(sels[k], posb, 0.0), axis=-1, keepdims=True)
        pc = jnp.where(lane == float(k), pk, pc)
        pc = jnp.where(lane == float(TOP_K + k), exps[k] / denom, pc)
    r8 = lax.broadcasted_iota(jnp.int32, (8, N_E), 0)
    c8 = lax.broadcasted_iota(jnp.int32, (8, N_E), 1)
    eye = (r8 == c8).astype(F32)
    pt = lax.dot_general(eye, pc, (((1,), (1,)), ((), ())),
                         preferred_element_type=F32, precision=lax.Precision.HIGHEST)
    earlier = carry[...]
    carry[...] = earlier + tile_cnt
    lists, totals = _copy_lists(member, tile_cnt, seg, earlier)
    return pc, pt, lists, totals


def _copy_lists(member, tile_cnt, seg, earlier):
    shift = PIECE.bit_length() - 1

    def pieces(n, ci):
        if ci == 0:
            return jnp.right_shift(n, shift)
        return jnp.bitwise_and(jnp.right_shift(n, shift - ci), 1)

    def done(n, ci):
        if ci == 0:
            return jnp.zeros_like(n)
        return n - jnp.bitwise_and(n, (PIECE >> (ci - 1)) - 1)

    re = lax.broadcasted_iota(jnp.int32, (N_E, N_E), 0)
    ce = lax.broadcasted_iota(jnp.int32, (N_E, N_E), 1)
    member_t = lax.dot_general((re == ce).astype(BF16), member.astype(BF16), (((1,), (1,)), ((), ())),
                               preferred_element_type=F32)
    n_col = jnp.sum(member_t, axis=1, keepdims=True).astype(jnp.int32)
    n_row = tile_cnt.astype(jnp.int32)
    lane8 = lax.broadcasted_iota(jnp.int32, (N_E, 8), 1)
    x = jnp.zeros((N_E, 8), F32)
    for ci in range(len(COPY_CLASSES)):
        x = jnp.where(lane8 == ci, pieces(n_col, ci).astype(F32), x)
    xb = x.astype(BF16)
    before = jnp.dot((ce < re).astype(BF16), xb, preferred_element_type=F32)
    totals = jnp.dot(jnp.ones((8, N_E), BF16), xb, preferred_element_type=F32)
    bulk_row = jnp.broadcast_to(pieces(n_row, 0).astype(F32), (8, N_E)).astype(BF16)
    before_bulk_row = jnp.dot(bulk_row, (re < ce).astype(BF16), preferred_element_type=F32)[0:1]

    lanes = lax.broadcasted_iota(jnp.int32, (N_E, LIST_W), 1)
    sub = lax.broadcasted_iota(jnp.int32, (8, N_E), 0)
    e_row = lax.broadcasted_iota(jnp.int32, (8, N_E), 1).astype(F32)
    lists = jnp.zeros((8, LIST_W), F32)
    for ci, (_, lane0) in enumerate(COPY_CLASSES):
        first = before[:, ci:ci + 1].astype(jnp.int32) + lane0
        sel = (lanes >= first) & (lanes < first + pieces(n_col, ci))
        d_row = done(n_row, ci).astype(F32)
        if ci == 0:
            d_row = d_row - PIECE * before_bulk_row
        v = jnp.where(sub == 0, seg + d_row, jnp.where(sub == 1, earlier + d_row,
                                                       jnp.where(sub == 2, e_row, 0.0)))
        lists = lists + jnp.dot(v, sel.astype(F32), preferred_element_type=F32,
                                precision=lax.Precision.HIGHEST)
    q = lax.broadcasted_iota(jnp.int32, (8, LIST_W), 1)
    s8 = lax.broadcasted_iota(jnp.int32, (8, LIST_W), 0)
    lists = lists + jnp.where((s8 < 2) & (q < RS // PIECE), (PIECE * q).astype(F32), 0.0)
    return lists, totals


def _mix_kernel(zuv_ref, ga_ref, gb_ref, ya_ref, x_ref, gate1_ref, shift2_ref, scale2_ref,
                lng_ref, lnb_ref, ws_ref, bias_ref, wbb_ref, wo_ref, n2g_ref, rw_ref, rb_ref,
                x1_ref, h2_ref, lg_ref):
    row = lax.broadcasted_iota(jnp.int32, (CHUNK, 2 * CHUNK), 0)
    col = lax.broadcasted_iota(jnp.int32, (CHUNK, 2 * CHUNK), 1)
    causal = (col % CHUNK) <= row
    lane = lax.broadcasted_iota(jnp.int32, (CHUNK, 2 * GM_HD), 1)
    first = lane < GM_HD
    wpairs = [jnp.where(causal, ws_ref[j], 0.0).astype(BF16) for j in range(GM_HEADS // 2)]

    for g in range(TS_MIX // SUB_MIX):
        rows = pl.ds(g * SUB_MIX, SUB_MIX)
        z = _gelu(zuv_ref[0, rows, :])
        u = z[:, :GM_W]
        v = z[:, GM_W:]
        mu = jnp.mean(v, axis=-1, keepdims=True)
        vc = v - mu
        var = jnp.mean(vc * vc, axis=-1, keepdims=True)
        vn = vc * lax.rsqrt(var + EPS) * lng_ref[...] + lnb_ref[...]
        chunks = []
        for n in range(SUB_MIX // CHUNK):
            cols = []
            for j in range(GM_HEADS // 2):
                vp = vn[n * CHUNK:(n + 1) * CHUNK, 2 * GM_HD * j:2 * GM_HD * (j + 1)]
                rhs = jnp.concatenate([jnp.where(first, vp, 0.0), jnp.where(first, 0.0, vp)], axis=0)
                cols.append(jnp.dot(wpairs[j], rhs.astype(BF16), preferred_element_type=F32))
            chunks.append(jnp.concatenate(cols, axis=1) + bias_ref[...])
        mixed = jnp.concatenate(chunks, axis=0)
        gm = u * mixed
        yb = jnp.dot(gm.astype(BF16), wbb_ref[...], preferred_element_type=F32)
        merged = _sigmoid(ga_ref[0, rows, :]) * ya_ref[0, rows, :] + _sigmoid(gb_ref[0, rows, :]) * yb
        o = jnp.dot(merged.astype(BF16), wo_ref[...], preferred_element_type=F32)
        x1 = x_ref[0, rows, :] + gate1_ref[0] * o
        x1_ref[0, rows, :] = x1
        h2 = _rms(x1) * n2g_ref[...]
        h2 = h2 * (1.0 + scale2_ref[0]) + shift2_ref[0]
        h2_ref[0, rows, :] = h2.astype(BF16)
        lg_ref[0, rows, :] = jnp.dot(h2, rw_ref[...], preferred_element_type=F32,
                                     precision=lax.Precision.HIGHEST) + rb_ref[...]


def _route_kernel(lg_ref, pc_ref, pt_ref, lists_ref, totals_ref, cnt_ref, carry):
    @pl.when(pl.program_id(0) == 0)
    def _():
        carry[...] = jnp.zeros_like(carry)

    for t in range(ROUTE_TILES):
        pc, pt, lists, totals = _route_tile(lg_ref[t * TT:(t + 1) * TT, :], carry)
        pc_ref[t * TT:(t + 1) * TT, :] = pc
        pt_ref[:, t * TT:(t + 1) * TT] = pt
        lists_ref[t] = lists.astype(jnp.int32)
        totals_ref[t] = totals.astype(jnp.int32)
    cnt_ref[...] = jnp.broadcast_to(carry[...], cnt_ref.shape)


def _route_call(logits):
    assert NT % ROUTE_TILES == 0
    return pl.pallas_call(
        _route_kernel,
        grid=(NT // ROUTE_TILES,),
        in_specs=[pl.BlockSpec((ROUTE_TILES * TT, N_E), lambda i: (i, 0))],
        out_specs=[pl.BlockSpec((ROUTE_TILES * TT, N_E), lambda i: (i, 0)),
                   pl.BlockSpec((8, ROUTE_TILES * TT), lambda i: (0, i)),
                   pl.BlockSpec((ROUTE_TILES, 8, LIST_W), lambda i: (i, 0, 0)),
                   pl.BlockSpec((ROUTE_TILES, 8, 8), lambda i: (i, 0, 0)),
                   pl.BlockSpec((8, N_E), lambda i: (0, 0))],
        out_shape=[jax.ShapeDtypeStruct((T, N_E), F32),
                   jax.ShapeDtypeStruct((8, T), F32),
                   jax.ShapeDtypeStruct((NT, 8, LIST_W), jnp.int32),
                   jax.ShapeDtypeStruct((NT, 8, 8), jnp.int32),
                   jax.ShapeDtypeStruct((8, N_E), F32)],
        scratch_shapes=[pltpu.VMEM((1, N_E), F32)],
        compiler_params=pltpu.CompilerParams(dimension_semantics=("arbitrary",)),
        name="route",
    )(logits)


def _mix_call(zuv, ga, gb, ya2d, x, mod3, ln_g, ln_b, ws_pairs, bias_full, wbb, wo, n2g, rw, rb):
    tok_spec = pl.BlockSpec((1, TS_MIX, D), lambda b, s: (b, s, 0))
    full = lambda *shape: pl.BlockSpec(shape, lambda b, s: (0,) * len(shape))
    mod_spec = lambda j: pl.BlockSpec((1, 1, D), lambda b, s: (b, 0, j))
    return pl.pallas_call(
        _mix_kernel,
        grid=(B, S // TS_MIX),
        in_specs=[tok_spec, tok_spec, tok_spec, tok_spec, tok_spec,
                  mod_spec(2), mod_spec(3), mod_spec(4),
                  full(1, GM_W), full(1, GM_W),
                  full(GM_HEADS // 2, CHUNK, 2 * CHUNK),
                  full(CHUNK, GM_W),
                  full(GM_W, D), full(D, D), full(1, D),
                  full(D, N_E), full(1, N_E)],
        out_specs=[tok_spec, tok_spec,
                   pl.BlockSpec((1, TS_MIX, N_E), lambda b, s: (b, s, 0))],
        out_shape=[jax.ShapeDtypeStruct((B, S, D), F32),
                   jax.ShapeDtypeStruct((B, S, D), BF16),
                   jax.ShapeDtypeStruct((B, S, N_E), F32)],
        compiler_params=pltpu.CompilerParams(vmem_limit_bytes=VMEM_LIMIT),
        name="gmlp_merge_norm2",
    )(zuv, ga, gb, ya2d, x, mod3, mod3, mod3, ln_g, ln_b, ws_pairs, bias_full, wbb, wo, n2g, rw, rb)


def _pack_rows(v):
    return pltpu.pack_elementwise([v[:, :HALF], v[:, HALF:]], packed_dtype=BF16)


def _unpack_rows(w):
    halves = [pltpu.unpack_elementwise(w, index=i, packed_dtype=BF16, unpacked_dtype=F32)
              for i in range(2)]
    return jnp.concatenate(halves, axis=1)


def _load_grouped(ref, rows, first=0):
    return jnp.concatenate([ref[pl.ds(first * SUB + c, rows, stride=SUB), :] for c in range(SUB)], axis=1)


def _store_grouped(ref, w, rows, first=0):
    for c in range(SUB):
        ref[pl.ds(first * SUB + c, rows, stride=SUB), :] = w[:, c * LANES:(c + 1) * LANES]


def _start_tile_runs(tile, src_ref, dst_ref, src_tbl, dst_tbl, totals_tbl, sem):
    for ci, (rows, lane0) in enumerate(COPY_CLASSES):
        base = tile * LIST_W + lane0

        def start(i, carry, rows=rows, base=base):
            s = pl.multiple_of(src_tbl[base + i] * SUB, SUB)
            d = pl.multiple_of(dst_tbl[base + i] * SUB, SUB)
            pltpu.make_async_copy(src_ref.at[pl.ds(s, rows * SUB)],
                                  dst_ref.at[pl.ds(d, rows * SUB)], sem).start()
            return carry

        lax.fori_loop(0, totals_tbl[tile * 8 + ci], start, 0)


def _wait_tile_runs(src_ref, dst_ref, sem):
    pltpu.make_async_copy(src_ref, dst_ref, sem).wait()


def _dispatch_kernel(loc_ref, glob_ref, tot_ref, pend_ref, h_ref, pt_ref, xs_ref,
                     sbuf0, sbuf1, zbuf, sem_z, sem0, sem1):
    j = pl.program_id(0)

    @pl.when(j == 0)
    def _():
        zbuf[...] = jnp.zeros_like(zbuf)
        for e in range(N_E):
            prev = pend_ref[e - 1] if e > 0 else 0
            end = pend_ref[e]

            @pl.when(end > prev)
            def _():
                first = pl.multiple_of((end - TM) * SUB, TM * SUB)
                cp = pltpu.make_async_copy(zbuf, xs_ref.at[pl.ds(first, TM * SUB)], sem_z)
                cp.start()
                cp.wait()

    rows = lax.broadcasted_iota(jnp.int32, (RS, TT), 0)
    words = []
    for t in range(2):
        pos = pt_ref[:, t * TT:(t + 1) * TT].astype(jnp.int32)
        hit = rows == pos[0:1, :]
        for k in range(1, TOP_K):
            hit = hit | (rows == pos[k:k + 1, :])
        pm = jnp.where(hit, 1.0, 0.0).astype(BF16)
        srt = jnp.dot(pm, h_ref[t * TT:(t + 1) * TT, :], preferred_element_type=F32)
        words.append(_pack_rows(srt))
    head = xs_ref.at[pl.ds(0, RS * SUB)]

    for t, (sbuf, sem) in enumerate(((sbuf0, sem0), (sbuf1, sem1))):
        @pl.when(j >= 1)
        def _(sbuf=sbuf, sem=sem):
            _wait_tile_runs(sbuf, head, sem)
        _store_grouped(sbuf, words[t], RS)
        _start_tile_runs(2 * j + t, sbuf, xs_ref, loc_ref, glob_ref, tot_ref, sem)

    @pl.when(j == NT // 2 - 1)
    def _():
        _wait_tile_runs(sbuf0, head, sem0)
        _wait_tile_runs(sbuf1, head, sem1)


def _dispatch_call(loc_t, glob_t, tot_t, pad_end, h2, pos_t):
    assert NT % 2 == 0
    grid_spec = pltpu.PrefetchScalarGridSpec(
        num_scalar_prefetch=4,
        grid=(NT // 2,),
        in_specs=[pl.BlockSpec((2 * TT, D), lambda j, *_: (j, 0)),
                  pl.BlockSpec((8, 2 * TT), lambda j, *_: (0, j))],
        out_specs=pl.BlockSpec(memory_space=pl.ANY),
        scratch_shapes=[pltpu.VMEM((RS * SUB, LANES), U32),
                        pltpu.VMEM((RS * SUB, LANES), U32),
                        pltpu.VMEM((TM * SUB, LANES), U32),
                        pltpu.SemaphoreType.DMA,
                        pltpu.SemaphoreType.DMA,
                        pltpu.SemaphoreType.DMA],
    )
    return pl.pallas_call(
        _dispatch_kernel,
        grid_spec=grid_spec,
        out_shape=jax.ShapeDtypeStruct((N_ROWS * SUB, LANES), U32),
        compiler_params=pltpu.CompilerParams(dimension_semantics=("arbitrary",),
                                             vmem_limit_bytes=VMEM_LIMIT),
        name="dispatch",
    )(loc_t, glob_t, tot_t, pad_end, h2, pos_t)


def _moe_kernel(be_ref, bf_ref, nx_ref, nv_ref, xs_ref, wi_hbm, bi_ref, wo_hbm, bo_ref, ys_ref,
                wi_f32, wo_f32, wi_bf, wo_bf, sem_i, sem_o):
    step = pl.program_id(0)

    def fetch(e):
        return (pltpu.make_async_copy(wi_hbm.at[e], wi_f32, sem_i),
                pltpu.make_async_copy(wo_hbm.at[e], wo_f32, sem_o))

    @pl.when(step == 0)
    def _():
        for cp in fetch(be_ref[0]):
            cp.start()

    def load_weights(i):
        @pl.when(bf_ref[i] == 1)
        def _():
            for cp in fetch(be_ref[i]):
                cp.wait()
            wi_bf[...] = wi_f32[...].astype(BF16)
            wo_bf[...] = wo_f32[...].astype(BF16)

            @pl.when(nx_ref[i] >= 0)
            def _():
                for cp in fetch(nx_ref[i]):
                    cp.start()

    def ffn(i, first, rows):
        e = be_ref[i]
        xb = _unpack_rows(_load_grouped(xs_ref, rows, first)).astype(BF16)
        gu = jnp.dot(xb, wi_bf[...], preferred_element_type=F32) + bi_ref[pl.ds(e, 1), :]
        gate = jnp.minimum(gu[:, :D_E], LIMIT)
        up = jnp.clip(gu[:, D_E:], -LIMIT, LIMIT)
        act = (up + 1.0) * (gate * _sigmoid(ALPHA * gate))
        y = jnp.dot(act.astype(BF16), wo_bf[...], preferred_element_type=F32) + bo_ref[pl.ds(e, 1), :]
        _store_grouped(ys_ref, _pack_rows(y), rows, first)

    i0 = step * BPS
    last = i0 + BPS - 1
    uniform = (last < nv_ref[0]) & (be_ref[i0] == be_ref[last])

    @pl.when(uniform)
    def _():
        load_weights(i0)
        ffn(i0, 0, BPS * TM)

    @pl.when(jnp.logical_not(uniform))
    def _():
        for sub in range(BPS):
            i = i0 + sub

            @pl.when(i < nv_ref[0])
            def _(i=i, sub=sub):
                load_weights(i)
                ffn(i, sub * TM, TM)


def _moe_call(blk_e, blk_first, blk_next, n_valid, xs, w_in, b_in, w_out, b_out):
    assert N_BLOCKS % BPS == 0

    def row_map(s, be, bf, nx, nv):
        last = (nv[0] + BPS - 1) // BPS - 1
        return (jnp.maximum(jnp.minimum(s, last), 0), 0)

    grid_spec = pltpu.PrefetchScalarGridSpec(
        num_scalar_prefetch=4,
        grid=(N_BLOCKS // BPS,),
        in_specs=[pl.BlockSpec((BPS * TM * SUB, LANES), row_map),
                  pl.BlockSpec(memory_space=pl.ANY),
                  pl.BlockSpec((N_E, 2 * D_E), lambda s, *_: (0, 0)),
                  pl.BlockSpec(memory_space=pl.ANY),
                  pl.BlockSpec((N_E, D), lambda s, *_: (0, 0))],
        out_specs=pl.BlockSpec((BPS * TM * SUB, LANES), row_map),
        scratch_shapes=[pltpu.VMEM((D, 2 * D_E), F32),
                        pltpu.VMEM((D_E, D), F32),
                        pltpu.VMEM((D, 2 * D_E), BF16),
                        pltpu.VMEM((D_E, D), BF16),
                        pltpu.SemaphoreType.DMA,
                        pltpu.SemaphoreType.DMA],
    )
    return pl.pallas_call(
        _moe_kernel,
        grid_spec=grid_spec,
        out_shape=jax.ShapeDtypeStruct((N_ROWS * SUB, LANES), U32),
        compiler_params=pltpu.CompilerParams(dimension_semantics=("arbitrary",),
                                             vmem_limit_bytes=VMEM_LIMIT),
        name="moe_experts",
    )(blk_e, blk_first, blk_next, n_valid, xs, w_in, b_in, w_out, b_out)


def _combine_kernel(loc_ref, glob_ref, tot_ref, ys_ref, pc_ref, x1_ref, gate2_ref, fg_ref, o_ref,
                    buf0, buf1, buf2, buf3, sem0, sem1, sem2, sem3):
    j = pl.program_id(0)
    head = ys_ref.at[pl.ds(0, RS * SUB)]
    even = ((buf0, sem0), (buf1, sem1))
    odd = ((buf2, sem2), (buf3, sem3))

    def fetch(step, slots):
        for t, (buf, sem) in enumerate(slots):
            _start_tile_runs(2 * step + t, ys_ref, buf, glob_ref, loc_ref, tot_ref, sem)

    @pl.when(j == 0)
    def _():
        fetch(0, even)

    col = lax.broadcasted_iota(jnp.int32, (TT, RS), 1)
    wms = []
    for t in range(2):
        pc = pc_ref[t * TT:(t + 1) * TT, :]
        pos = pc.astype(jnp.int32)
        wm = jnp.zeros((TT, RS), F32)
        for k in range(TOP_K):
            wm = jnp.where(col == pos[:, k:k + 1], pc[:, TOP_K + k:TOP_K + k + 1], wm)
        wms.append(wm.astype(BF16))

    def step(cur, nxt):
        @pl.when(j + 1 < NT // 2)
        def _():
            fetch(j + 1, nxt)
        for t, (buf, sem) in enumerate(cur):
            _wait_tile_runs(head, buf, sem)
            yt = _unpack_rows(_load_grouped(buf, RS)).astype(BF16)
            acc = jnp.dot(wms[t], yt, preferred_element_type=F32)
            x2 = x1_ref[t * TT:(t + 1) * TT, :] + gate2_ref[0] * acc
            o_ref[t * TT:(t + 1) * TT, :] = _rms(x2) * fg_ref[...]

    @pl.when(j % 2 == 0)
    def _():
        step(even, odd)

    @pl.when(j % 2 == 1)
    def _():
        step(odd, even)


def _combine_call(loc_t, glob_t, tot_t, ys, pos_c, x1, mod3, final_g):
    per_b = S // (2 * TT)
    grid_spec = pltpu.PrefetchScalarGridSpec(
        num_scalar_prefetch=3,
        grid=(NT // 2,),
        in_specs=[pl.BlockSpec(memory_space=pl.ANY),
                  pl.BlockSpec((2 * TT, N_E), lambda j, *_: (j, 0)),
                  pl.BlockSpec((2 * TT, D), lambda j, *_: (j, 0)),
                  pl.BlockSpec((1, 1, D), lambda j, *_: (j // per_b, 0, 5)),
                  pl.BlockSpec((1, D), lambda j, *_: (0, 0))],
        out_specs=pl.BlockSpec((2 * TT, D), lambda j, *_: (j, 0)),
        scratch_shapes=[pltpu.VMEM((RS * SUB, LANES), U32)] * 4 + [pltpu.SemaphoreType.DMA] * 4,
    )
    return pl.pallas_call(
        _combine_kernel,
        grid_spec=grid_spec,
        out_shape=jax.ShapeDtypeStruct((T, D), F32),
        compiler_params=pltpu.CompilerParams(dimension_semantics=("arbitrary",),
                                             vmem_limit_bytes=VMEM_LIMIT),
        name="combine_norm",
    )(loc_t, glob_t, tot_t, ys, pos_c, x1, mod3, final_g)


def kernel(x, c, ada_w, ada_b, norm1_g, w_in, ssm_a_re, ssm_a_im, ssm_log_dt, ssm_b_re, ssm_b_im, ssm_c_re, ssm_c_im, ssm_d, ssm_glu_w, ssm_glu_b, w_branch_a, gmlp_ln_g, gmlp_ln_b, gmlp_ws, gmlp_bs, w_branch_b, w_out, norm2_g, router_w, router_b, moe_w_in, moe_b_in, moe_w_out, moe_b_out, final_g):
    depth = ada_w.shape[0]
    assert depth == 1, "the final rms_norm is fused into the combine kernel of the only layer"
    for layer in range(depth):
        mod = _mod_call(c, ada_w[layer], ada_b[layer])
        mod3 = mod.reshape(B, 1, 6 * D)

        u, zuv, ga, gb = _proj_call(x, norm1_g[layer], mod3, w_in[layer].astype(BF16))

        bm, cre, cim, are, aim = _s5_params(ssm_a_re[layer], ssm_a_im[layer], ssm_log_dt[layer],
                                            ssm_b_re[layer], ssm_b_im[layer],
                                            ssm_c_re[layer], ssm_c_im[layer])
        ya = _s5_call(u, bm, cre, cim, are, aim,
                      ssm_d[layer].reshape(1, SSM_W), ssm_glu_w[layer].astype(BF16),
                      ssm_glu_b[layer].reshape(1, SSM_W), w_branch_a[layer].astype(BF16))

        ws = gmlp_ws[layer]
        ws_pairs = jnp.concatenate([ws[0::2], ws[1::2]], axis=-1)
        bias_full = jnp.repeat(gmlp_bs[layer].T, GM_HD, axis=1)
        x1, h2, logits = _mix_call(
            zuv, ga, gb, ya, x, mod3,
            gmlp_ln_g[layer].reshape(1, GM_W), gmlp_ln_b[layer].reshape(1, GM_W),
            ws_pairs, bias_full, w_branch_b[layer].astype(BF16), w_out[layer].astype(BF16),
            norm2_g[layer].reshape(1, D), router_w[layer],
            router_b[layer].reshape(1, N_E))

        pos_c, pos_t, lists, totals, cnt = _route_call(logits.reshape(T, N_E))
        counts = cnt[0].astype(jnp.int32)
        nblk = (counts + TM - 1) // TM
        blk_end = jnp.cumsum(nblk)
        pad_end = (blk_end * TM).astype(jnp.int32)
        experts = jnp.arange(N_E, dtype=jnp.int32)
        blk_ids = jnp.arange(N_BLOCKS, dtype=jnp.int32)
        blk_e = jnp.sum((blk_end[None, :] <= blk_ids[:, None]).astype(jnp.int32), axis=1)
        blk_e = jnp.minimum(blk_e, N_E - 1)
        blk_first = jnp.concatenate([jnp.ones((1,), jnp.int32),
                                     (blk_e[1:] != blk_e[:-1]).astype(jnp.int32)])
        later = (experts[None, :] > experts[:, None]) & (nblk[None, :] > 0)
        next_e = jnp.min(jnp.where(later, experts[None, :], N_E), axis=1)
        next_e = jnp.where(next_e == N_E, -1, next_e)
        blk_next = jnp.sum(jnp.where(blk_e[:, None] == experts[None, :], next_e[None, :], 0), axis=1)
        n_valid = blk_end[-1:].astype(jnp.int32)
        pad_start = pad_end - nblk * TM
        loc_t = lists[:, 0, :].reshape(NT * LIST_W)
        owner = lists[:, 2, :, None] == experts[None, None, :]
        glob_t = (lists[:, 1, :] + jnp.sum(jnp.where(owner, pad_start, 0), axis=-1)).reshape(NT * LIST_W)
        tot_t = totals[:, 0, :].reshape(NT * 8)

        xs = _dispatch_call(loc_t, glob_t, tot_t, pad_end, h2.reshape(T, D), pos_t)
        ys = _moe_call(blk_e, blk_first, blk_next.astype(jnp.int32), n_valid, xs, moe_w_in[layer],
                       moe_b_in[layer], moe_w_out[layer], moe_b_out[layer])
        x = _combine_call(loc_t, glob_t, tot_t, ys, pos_c, x1.reshape(T, D),
                          mod3, final_g.reshape(1, D)).reshape(B, S, D)
    return x
```

```python
import functools
import math

import jax
import jax.numpy as jnp
from jax import lax
from jax.experimental import pallas as pl
from jax.experimental.pallas import tpu as pltpu

F32 = jnp.float32
BF16 = jnp.bfloat16

D = 1024
B = 8
S = 2048
T = B * S
SSM_W = 512
SSM_G = 32
SSM_H = 16
SSM_P = 64
N_PACK = 4
PACK_G = SSM_G // N_PACK
GM_W = 512
GM_HEADS = 8
GM_HD = 64
CHUNK = 128
N_E = 32
TOP_K = 4
D_E = 1024
LIMIT = 7.0
ALPHA = 1.702
EPS = 1e-6

TS_PROJ = 512
L_SSM = 128
R_SSM = L_SSM * B
TS_MIX = 512
SUB_MIX = 256
TT = 256
NT = T // TT
ROUTE_TILES = 4
RS = TOP_K * TT
TM = 256
BPS = 4
N_ROWS = T * TOP_K + N_E * TM
N_BLOCKS = N_ROWS // TM
LANES = 128
HALF = D // 2
SUB = HALF // LANES
PIECE = 16
COPY_CLASSES = ((PIECE, 0),) + tuple((PIECE >> s, RS // PIECE + N_E * (s - 1))
                                     for s in range(1, PIECE.bit_length()))
LIST_W = 256
assert COPY_CLASSES[-1][1] + N_E <= LIST_W and len(COPY_CLASSES) <= 8
VMEM_LIMIT = 56 * 1024 * 1024
U32 = jnp.uint32


def _sigmoid(v):
    return 1.0 / (1.0 + jnp.exp(-v))


def _gelu(v):
    return 0.5 * v * (1.0 + jnp.tanh(math.sqrt(2.0 / math.pi) * (v + 0.044715 * v * v * v)))


def _rms(v):
    return v * lax.rsqrt(jnp.mean(v * v, axis=-1, keepdims=True) + EPS)


def _mod_kernel(c_ref, w_ref, b_ref, o_ref):
    cv = c_ref[...]
    sv = cv * _sigmoid(cv)
    o_ref[...] = jnp.dot(sv, w_ref[...], preferred_element_type=F32,
                         precision=lax.Precision.HIGHEST) + b_ref[...]


def _mod_call(c, ada_w, ada_b):
    n = ada_w.shape[1]
    return pl.pallas_call(
        _mod_kernel,
        grid=(n // D,),
        in_specs=[pl.BlockSpec((B, D), lambda j: (0, 0)),
                  pl.BlockSpec((D, D), lambda j: (0, j)),
                  pl.BlockSpec((1, D), lambda j: (0, j))],
        out_specs=pl.BlockSpec((B, D), lambda j: (0, j)),
        out_shape=jax.ShapeDtypeStruct((B, n), F32),
        name="adaln_mod",
    )(c, ada_w, ada_b.reshape(1, n))


def _proj_kernel(x_ref, g_ref, shift_ref, scale_ref, w_ref, u_ref, zuv_ref, ga_ref, gb_ref):
    h = _rms(x_ref[0]) * g_ref[...]
    h = h * (1.0 + scale_ref[0]) + shift_ref[0]
    hb = h.astype(BF16)
    u_ref[0] = jnp.dot(hb, w_ref[:, 0:SSM_W], preferred_element_type=F32)
    zuv_ref[0] = jnp.dot(hb, w_ref[:, SSM_W:SSM_W + 2 * GM_W], preferred_element_type=F32)
    ga_ref[0] = jnp.dot(hb, w_ref[:, SSM_W + 2 * GM_W:SSM_W + 2 * GM_W + D], preferred_element_type=F32)
    gb_ref[0] = jnp.dot(hb, w_ref[:, SSM_W + 2 * GM_W + D:], preferred_element_type=F32)


def _proj_call(x, norm_g, mod3, w_in_bf):
    pw = w_in_bf.shape[1]
    tok_spec = pl.BlockSpec((1, TS_PROJ, D), lambda b, s: (b, s, 0))
    return pl.pallas_call(
        _proj_kernel,
        grid=(B, S // TS_PROJ),
        in_specs=[tok_spec,
                  pl.BlockSpec((1, D), lambda b, s: (0, 0)),
                  pl.BlockSpec((1, 1, D), lambda b, s: (b, 0, 0)),
                  pl.BlockSpec((1, 1, D), lambda b, s: (b, 0, 1)),
                  pl.BlockSpec((D, pw), lambda b, s: (0, 0))],
        out_specs=[pl.BlockSpec((1, TS_PROJ, SSM_W), lambda b, s: (b, s, 0)),
                   tok_spec, tok_spec, tok_spec],
        out_shape=[jax.ShapeDtypeStruct((B, S, SSM_W), F32),
                   jax.ShapeDtypeStruct((B, S, D), F32),
                   jax.ShapeDtypeStruct((B, S, D), F32),
                   jax.ShapeDtypeStruct((B, S, D), F32)],
        compiler_params=pltpu.CompilerParams(vmem_limit_bytes=VMEM_LIMIT),
        name="norm_proj",
    )(x, norm_g.reshape(1, D), mod3, mod3, w_in_bf)


def _s5_kernel(u_ref, bm_ref, cre_ref, cim_ref, are_ref, aim_ref, d_ref, gw_ref, gb_ref, wa_ref,
               o_ref, usc, ysc, sre, sim, st_re, st_im):
    @pl.when(pl.program_id(0) == 0)
    def _():
        st_re[...] = jnp.zeros_like(st_re)
        st_im[...] = jnp.zeros_like(st_im)

    nslab = SSM_W // LANES
    for b in range(B):
        for c in range(nslab):
            usc[c, pl.ds(b, L_SSM, stride=B), :] = u_ref[b, :, c * LANES:(c + 1) * LANES]
    u = jnp.concatenate([usc[c] for c in range(nslab)], axis=1)
    ub = u.astype(BF16)
    half = PACK_G * SSM_P
    ys = []
    for k in range(N_PACK):
        bu = jnp.dot(ub[:, 128 * k:128 * (k + 1)], bm_ref[k], preferred_element_type=F32)
        sre[k] = bu[:, :half]
        sim[k] = bu[:, half:]
        ar = are_ref[k]
        ai = aim_ref[k]
        r = st_re[k]
        m = st_im[k]
        for t in range(L_SSM):
            rows = pl.ds(t * B, B)
            nr = ar * r - ai * m + sre[k, rows, :]
            m = ar * m + ai * r + sim[k, rows, :]
            r = nr
            sre[k, rows, :] = r
            sim[k, rows, :] = m
        st_re[k] = r
        st_im[k] = m
        yk = jnp.dot(sre[k].astype(BF16), cre_ref[k], preferred_element_type=F32)
        yk = yk + jnp.dot(sim[k].astype(BF16), cim_ref[k], preferred_element_type=F32)
        ys.append(yk)
    for c in range(nslab):
        uc = usc[c]
        ysc[c] = ys[c] + d_ref[:, c * LANES:(c + 1) * LANES] * uc
    y = jnp.concatenate(
        [jnp.concatenate([ysc[c, pl.ds(b, L_SSM, stride=B), :] for c in range(nslab)], axis=1)
         for b in range(B)], axis=0)
    z = _gelu(y)
    gl = jnp.dot(z.astype(BF16), gw_ref[...], preferred_element_type=F32) + gb_ref[...]
    out = z * _sigmoid(gl)
    o = jnp.dot(out.astype(BF16), wa_ref[...], preferred_element_type=F32)
    for b in range(B):
        o_ref[b] = o[b * L_SSM:(b + 1) * L_SSM]


def _s5_call(u, bm, cre, cim, are, aim, d_skip, glu_w, glu_b, w_a):
    half = PACK_G * SSM_P
    full = lambda *shape: pl.BlockSpec(shape, lambda i: (0,) * len(shape))
    return pl.pallas_call(
        _s5_kernel,
        grid=(S // L_SSM,),
        in_specs=[pl.BlockSpec((B, L_SSM, SSM_W), lambda i: (0, i, 0)),
                  full(N_PACK, 128, 2 * half),
                  full(N_PACK, half, 128),
                  full(N_PACK, half, 128),
                  full(N_PACK, B, half),
                  full(N_PACK, B, half),
                  full(1, SSM_W),
                  full(SSM_W, SSM_W),
                  full(1, SSM_W),
                  full(SSM_W, D)],
        out_specs=pl.BlockSpec((B, L_SSM, D), lambda i: (0, i, 0)),
        out_shape=jax.ShapeDtypeStruct((B, S, D), F32),
        scratch_shapes=[pltpu.VMEM((SSM_W // LANES, R_SSM, LANES), F32),
                        pltpu.VMEM((SSM_W // LANES, R_SSM, LANES), F32),
                        pltpu.VMEM((N_PACK, R_SSM, half), F32),
                        pltpu.VMEM((N_PACK, R_SSM, half), F32),
                        pltpu.VMEM((N_PACK, B, half), F32),
                        pltpu.VMEM((N_PACK, B, half), F32)],
        compiler_params=pltpu.CompilerParams(dimension_semantics=("arbitrary",),
                                             vmem_limit_bytes=VMEM_LIMIT),
        name="s5_branch",
    )(u, bm, cre, cim, are, aim, d_skip, glu_w, glu_b, w_a)


def _s5_params(a_re, a_im, log_dt, b_re, b_im, c_re, c_im):
    dt = jnp.exp(log_dt)[:, None]
    mag = jnp.exp(a_re * dt)
    lr = mag * jnp.cos(a_im * dt)
    li = mag * jnp.sin(a_im * dt)
    den = a_re * a_re + a_im * a_im
    cr = ((lr - 1.0) * a_re + li * a_im) / den
    ci = (li * a_re - (lr - 1.0) * a_im) / den
    bbr = cr[..., None] * b_re - ci[..., None] * b_im
    bbi = cr[..., None] * b_im + ci[..., None] * b_re
    eye = jnp.eye(PACK_G, dtype=F32)
    half = PACK_G * SSM_P

    def pack_b(m):
        m4 = m.reshape(N_PACK, PACK_G, SSM_P, SSM_H)
        return jnp.einsum('kgph,gj->kghjp', m4, eye).reshape(N_PACK, PACK_G * SSM_H, half)

    def pack_c(m):
        m4 = m.reshape(N_PACK, PACK_G, SSM_H, SSM_P)
        return jnp.einsum('kghp,gj->kgpjh', m4, eye).reshape(N_PACK, half, PACK_G * SSM_H)

    bm = jnp.concatenate([pack_b(bbr), pack_b(bbi)], axis=-1).astype(BF16)
    cre = pack_c(c_re).astype(BF16)
    cim = (-pack_c(c_im)).astype(BF16)
    are = jnp.broadcast_to(lr.reshape(N_PACK, 1, half), (N_PACK, B, half))
    aim = jnp.broadcast_to(li.reshape(N_PACK, 1, half), (N_PACK, B, half))
    return bm, cre, cim, are, aim


def _route_tile(l, carry):
    lane = lax.broadcasted_iota(jnp.int32, l.shape, 1).astype(F32)
    sels, vals = [], []
    for _ in range(TOP_K):
        m = jnp.max(l, axis=-1, keepdims=True)
        idx = jnp.min(jnp.where(l == m, lane, float(N_E)), axis=-1, keepdims=True)
        sel = lane == idx
        sels.append(sel)
        vals.append(m)
        l = jnp.where(sel, -jnp.inf, l)
    member = sels[0].astype(F32)
    for k in range(1, TOP_K):
        member = member + sels[k].astype(F32)
    tile_cnt = jnp.sum(member, axis=0, keepdims=True)

    r = lax.broadcasted_iota(jnp.int32, (N_E, N_E), 0)
    c = lax.broadcasted_iota(jnp.int32, (N_E, N_E), 1)
    tri = (r < c).astype(BF16)
    tcb = jnp.broadcast_to(tile_cnt, (8, N_E)).astype(BF16)
    seg = jnp.dot(tcb, tri, preferred_element_type=F32)[0:1]
    r = lax.broadcasted_iota(jnp.int32, (TT, TT), 0)
    c = lax.broadcasted_iota(jnp.int32, (TT, TT), 1)
    strict = (c < r).astype(BF16)
    rank = jnp.dot(strict, member.astype(BF16), preferred_element_type=F32)
    posb = seg + rank
    denom = jnp.zeros_like(vals[0])
    exps = []
    for k in range(TOP_K):
        e = jnp.exp(vals[k] - vals[0])
        exps.append(e)
        denom = denom + e
    pc = jnp.zeros(l.shape, F32)
    for k in range(TOP_K):
        pk = jnp.sum(jnp.where(sels[k], posb, 0.0), axis=-1, keepdims=True)
        pc = jnp.where(lane == float(k), pk, pc)
        pc = jnp.where(lane == float(TOP_K + k), exps[k] / denom, pc)
    r8 = lax.broadcasted_iota(jnp.int32, (8, N_E), 0)
    c8 = lax.broadcasted_iota(jnp.int32, (8, N_E), 1)
    eye = (r8 == c8).astype(F32)
    pt = lax.dot_general(eye, pc, (((1,), (1,)), ((), ())),
                         preferred_element_type=F32, precision=lax.Precision.HIGHEST)
    earlier = carry[...]
    carry[...] = earlier + tile_cnt
    lists, totals = _copy_lists(member, tile_cnt, seg, earlier)
    return pc, pt, lists, totals


def _copy_lists(member, tile_cnt, seg, earlier):
    shift = PIECE.bit_length() - 1

    def pieces(n, ci):
        if ci == 0:
            return jnp.right_shift(n, shift)
        return jnp.bitwise_and(jnp.right_shift(n, shift - ci), 1)

    def done(n, ci):
        if ci == 0:
            return jnp.zeros_like(n)
        return n - jnp.bitwise_and(n, (PIECE >> (ci - 1)) - 1)

    re = lax.broadcasted_iota(jnp.int32, (N_E, N_E), 0)
    ce = lax.broadcasted_iota(jnp.int32, (N_E, N_E), 1)
    member_t = lax.dot_general((re == ce).astype(BF16), member.astype(BF16), (((1,), (1,)), ((), ())),
                               preferred_element_type=F32)
    n_col = jnp.sum(member_t, axis=1, keepdims=True).astype(jnp.int32)
    n_row = tile_cnt.astype(jnp.int32)
    lane8 = lax.broadcasted_iota(jnp.int32, (N_E, 8), 1)
    x = jnp.zeros((N_E, 8), F32)
    for ci in range(len(COPY_CLASSES)):
        x = jnp.where(lane8 == ci, pieces(n_col, ci).astype(F32), x)
    xb = x.astype(BF16)
    before = jnp.dot((ce < re).astype(BF16), xb, preferred_element_type=F32)
    totals = jnp.dot(jnp.ones((8, N_E), BF16), xb, preferred_element_type=F32)
    bulk_row = jnp.broadcast_to(pieces(n_row, 0).astype(F32), (8, N_E)).astype(BF16)
    before_bulk_row = jnp.dot(bulk_row, (re < ce).astype(BF16), preferred_element_type=F32)[0:1]

    lanes = lax.broadcasted_iota(jnp.int32, (N_E, LIST_W), 1)
    sub = lax.broadcasted_iota(jnp.int32, (8, N_E), 0)
    e_row = lax.broadcasted_iota(jnp.int32, (8, N_E), 1).astype(F32)
    lists = jnp.zeros((8, LIST_W), F32)
    for ci, (_, lane0) in enumerate(COPY_CLASSES):
        first = before[:, ci:ci + 1].astype(jnp.int32) + lane0
        sel = (lanes >= first) & (lanes < first + pieces(n_col, ci))
        d_row = done(n_row, ci).astype(F32)
        if ci == 0:
            d_row = d_row - PIECE * before_bulk_row
        v = jnp.where(sub == 0, seg + d_row, jnp.where(sub == 1, earlier + d_row,
                                                       jnp.where(sub == 2, e_row, 0.0)))
        lists = lists + jnp.dot(v, sel.astype(F32), preferred_element_type=F32,
                                precision=lax.Precision.HIGHEST)
    q = lax.broadcasted_iota(jnp.int32, (8, LIST_W), 1)
    s8 = lax.broadcasted_iota(jnp.int32, (8, LIST_W), 0)
    lists = lists + jnp.where((s8 < 2) & (q < RS // PIECE), (PIECE * q).astype(F32), 0.0)
    return lists, totals


def _mix_kernel(zuv_ref, ga_ref, gb_ref, ya_ref, x_ref, gate1_ref, shift2_ref, scale2_ref,
                lng_ref, lnb_ref, ws_ref, bias_ref, wbb_ref, wo_ref, n2g_ref, rw_ref, rb_ref,
                x1_ref, h2_ref, lg_ref):
    row = lax.broadcasted_iota(jnp.int32, (CHUNK, 2 * CHUNK), 0)
    col = lax.broadcasted_iota(jnp.int32, (CHUNK, 2 * CHUNK), 1)
    causal = (col % CHUNK) <= row
    lane = lax.broadcasted_iota(jnp.int32, (CHUNK, 2 * GM_HD), 1)
    first = lane < GM_HD
    wpairs = [jnp.where(causal, ws_ref[j], 0.0).astype(BF16) for j in range(GM_HEADS // 2)]

    for g in range(TS_MIX // SUB_MIX):
        rows = pl.ds(g * SUB_MIX, SUB_MIX)
        z = _gelu(zuv_ref[0, rows, :])
        u = z[:, :GM_W]
        v = z[:, GM_W:]
        mu = jnp.mean(v, axis=-1, keepdims=True)
        vc = v - mu
        var = jnp.mean(vc * vc, axis=-1, keepdims=True)
        vn = vc * lax.rsqrt(var + EPS) * lng_ref[...] + lnb_ref[...]
        chunks = []
        for n in range(SUB_MIX // CHUNK):
            cols = []
            for j in range(GM_HEADS // 2):
                vp = vn[n * CHUNK:(n + 1) * CHUNK, 2 * GM_HD * j:2 * GM_HD * (j + 1)]
                rhs = jnp.concatenate([jnp.where(first, vp, 0.0), jnp.where(first, 0.0, vp)], axis=0)
                cols.append(jnp.dot(wpairs[j], rhs.astype(BF16), preferred_element_type=F32))
            chunks.append(jnp.concatenate(cols, axis=1) + bias_ref[...])
        mixed = jnp.concatenate(chunks, axis=0)
        gm = u * mixed
        yb = jnp.dot(gm.astype(BF16), wbb_ref[...], preferred_element_type=F32)
        merged = _sigmoid(ga_ref[0, rows, :]) * ya_ref[0, rows, :] + _sigmoid(gb_ref[0, rows, :]) * yb
        o = jnp.dot(merged.astype(BF16), wo_ref[...], preferred_element_type=F32)
        x1 = x_ref[0, rows, :] + gate1_ref[0] * o
        x1_ref[0, rows, :] = x1
        h2 = _rms(x1) * n2g_ref[...]
        h2 = h2 * (1.0 + scale2_ref[0]) + shift2_ref[0]
        hb = h2.astype(BF16)
        h2_ref[0, rows, :] = hb
        lg_ref[0, rows, :] = jnp.dot(hb, rw_ref[...], preferred_element_type=F32) + rb_ref[...]


def _route_kernel(lg_ref, pc_ref, pt_ref, lists_ref, totals_ref, cnt_ref, carry):
    @pl.when(pl.program_id(0) == 0)
    def _():
        carry[...] = jnp.zeros_like(carry)

    for t in range(ROUTE_TILES):
        pc, pt, lists, totals = _route_tile(lg_ref[t * TT:(t + 1) * TT, :], carry)
        pc_ref[t * TT:(t + 1) * TT, :] = pc
        pt_ref[:, t * TT:(t + 1) * TT] = pt
        lists_ref[t] = lists.astype(jnp.int32)
        totals_ref[t] = totals.astype(jnp.int32)
    cnt_ref[...] = jnp.broadcast_to(carry[...], cnt_ref.shape)


def _route_call(logits):
    assert NT % ROUTE_TILES == 0
    return pl.pallas_call(
        _route_kernel,
        grid=(NT // ROUTE_TILES,),
        in_specs=[pl.BlockSpec((ROUTE_TILES * TT, N_E), lambda i: (i, 0))],
        out_specs=[pl.BlockSpec((ROUTE_TILES * TT, N_E), lambda i: (i, 0)),
                   pl.BlockSpec((8, ROUTE_TILES * TT), lambda i: (0, i)),
                   pl.BlockSpec((ROUTE_TILES, 8, LIST_W), lambda i: (i, 0, 0)),
                   pl.BlockSpec((ROUTE_TILES, 8, 8), lambda i: (i, 0, 0)),
                   pl.BlockSpec((8, N_E), lambda i: (0, 0))],
        out_shape=[jax.ShapeDtypeStruct((T, N_E), F32),
                   jax.ShapeDtypeStruct((8, T), F32),
                   jax.ShapeDtypeStruct((NT, 8, LIST_W), jnp.int32),
                   jax.ShapeDtypeStruct((NT, 8, 8), jnp.int32),
                   jax.ShapeDtypeStruct((8, N_E), F32)],
        scratch_shapes=[pltpu.VMEM((1, N_E), F32)],
        compiler_params=pltpu.CompilerParams(dimension_semantics=("arbitrary",)),
        name="route",
    )(logits)


def _mix_call(zuv, ga, gb, ya2d, x, mod3, ln_g, ln_b, ws_pairs, bias_full, wbb, wo, n2g, rw, rb):
    tok_spec = pl.BlockSpec((1, TS_MIX, D), lambda b, s: (b, s, 0))
    full = lambda *shape: pl.BlockSpec(shape, lambda b, s: (0,) * len(shape))
    mod_spec = lambda j: pl.BlockSpec((1, 1, D), lambda b, s: (b, 0, j))
    return pl.pallas_call(
        _mix_kernel,
        grid=(B, S // TS_MIX),
        in_specs=[tok_spec, tok_spec, tok_spec, tok_spec, tok_spec,
                  mod_spec(2), mod_spec(3), mod_spec(4),
                  full(1, GM_W), full(1, GM_W),
                  full(GM_HEADS // 2, CHUNK, 2 * CHUNK),
                  full(CHUNK, GM_W),
                  full(GM_W, D), full(D, D), full(1, D),
                  full(D, N_E), full(1, N_E)],
        out_specs=[tok_spec, tok_spec,
                   pl.BlockSpec((1, TS_MIX, N_E), lambda b, s: (b, s, 0))],
        out_shape=[jax.ShapeDtypeStruct((B, S, D), F32),
                   jax.ShapeDtypeStruct((B, S, D), BF16),
                   jax.ShapeDtypeStruct((B, S, N_E), F32)],
        compiler_params=pltpu.CompilerParams(vmem_limit_bytes=VMEM_LIMIT),
        name="gmlp_merge_norm2",
    )(zuv, ga, gb, ya2d, x, mod3, mod3, mod3, ln_g, ln_b, ws_pairs, bias_full, wbb, wo, n2g, rw, rb)


def _pack_rows(v):
    return pltpu.pack_elementwise([v[:, :HALF], v[:, HALF:]], packed_dtype=BF16)


def _unpack_rows(w):
    halves = [pltpu.unpack_elementwise(w, index=i, packed_dtype=BF16, unpacked_dtype=F32)
              for i in range(2)]
    return jnp.concatenate(halves, axis=1)


def _load_grouped(ref, rows, first=0):
    return jnp.concatenate([ref[pl.ds(first * SUB + c, rows, stride=SUB), :] for c in range(SUB)], axis=1)


def _store_grouped(ref, w, rows, first=0):
    for c in range(SUB):
        ref[pl.ds(first * SUB + c, rows, stride=SUB), :] = w[:, c * LANES:(c + 1) * LANES]


def _start_tile_runs(tile, src_ref, dst_ref, src_tbl, dst_tbl, totals_tbl, sem):
    for ci, (rows, lane0) in enumerate(COPY_CLASSES):
        base = tile * LIST_W + lane0

        def start(i, carry, rows=rows, base=base):
            s = pl.multiple_of(src_tbl[base + i] * SUB, SUB)
            d = pl.multiple_of(dst_tbl[base + i] * SUB, SUB)
            pltpu.make_async_copy(src_ref.at[pl.ds(s, rows * SUB)],
                                  dst_ref.at[pl.ds(d, rows * SUB)], sem).start()
            return carry

        lax.fori_loop(0, totals_tbl[tile * 8 + ci], start, 0)


def _wait_tile_runs(src_ref, dst_ref, sem):
    pltpu.make_async_copy(src_ref, dst_ref, sem).wait()


def _dispatch_kernel(loc_ref, glob_ref, tot_ref, pend_ref, h_ref, pt_ref, xs_ref,
                     sbuf0, sbuf1, zbuf, sem_z, sem0, sem1):
    j = pl.program_id(0)

    @pl.when(j == 0)
    def _():
        zbuf[...] = jnp.zeros_like(zbuf)
        for e in range(N_E):
            prev = pend_ref[e - 1] if e > 0 else 0
            end = pend_ref[e]

            @pl.when(end > prev)
            def _():
                first = pl.multiple_of((end - TM) * SUB, TM * SUB)
                cp = pltpu.make_async_copy(zbuf, xs_ref.at[pl.ds(first, TM * SUB)], sem_z)
                cp.start()
                cp.wait()

    rows = lax.broadcasted_iota(jnp.int32, (RS, TT), 0)
    words = []
    for t in range(2):
        pos = pt_ref[:, t * TT:(t + 1) * TT].astype(jnp.int32)
        hit = rows == pos[0:1, :]
        for k in range(1, TOP_K):
            hit = hit | (rows == pos[k:k + 1, :])
        pm = jnp.where(hit, 1.0, 0.0).astype(BF16)
        srt = jnp.dot(pm, h_ref[t * TT:(t + 1) * TT, :], preferred_element_type=F32)
        words.append(_pack_rows(srt))
    head = xs_ref.at[pl.ds(0, RS * SUB)]

    for t, (sbuf, sem) in enumerate(((sbuf0, sem0), (sbuf1, sem1))):
        @pl.when(j >= 1)
        def _(sbuf=sbuf, sem=sem):
            _wait_tile_runs(sbuf, head, sem)
        _store_grouped(sbuf, words[t], RS)
        _start_tile_runs(2 * j + t, sbuf, xs_ref, loc_ref, glob_ref, tot_ref, sem)

    @pl.when(j == NT // 2 - 1)
    def _():
        _wait_tile_runs(sbuf0, head, sem0)
        _wait_tile_runs(sbuf1, head, sem1)


def _dispatch_call(loc_t, glob_t, tot_t, pad_end, h2, pos_t):
    assert NT % 2 == 0
    grid_spec = pltpu.PrefetchScalarGridSpec(
        num_scalar_prefetch=4,
        grid=(NT // 2,),
        in_specs=[pl.BlockSpec((2 * TT, D), lambda j, *_: (j, 0)),
                  pl.BlockSpec((8, 2 * TT), lambda j, *_: (0, j))],
        out_specs=pl.BlockSpec(memory_space=pl.ANY),
        scratch_shapes=[pltpu.VMEM((RS * SUB, LANES), U32),
                        pltpu.VMEM((RS * SUB, LANES), U32),
                        pltpu.VMEM((TM * SUB, LANES), U32),
                        pltpu.SemaphoreType.DMA,
                        pltpu.SemaphoreType.DMA,
                        pltpu.SemaphoreType.DMA],
    )
    return pl.pallas_call(
        _dispatch_kernel,
        grid_spec=grid_spec,
        out_shape=jax.ShapeDtypeStruct((N_ROWS * SUB, LANES), U32),
        compiler_params=pltpu.CompilerParams(dimension_semantics=("arbitrary",),
                                             vmem_limit_bytes=VMEM_LIMIT),
        name="dispatch",
    )(loc_t, glob_t, tot_t, pad_end, h2, pos_t)


def _moe_kernel(be_ref, bf_ref, nx_ref, nv_ref, xs_ref, wi_hbm, bi_ref, wo_hbm, bo_ref, ys_ref,
                wi_f32, wo_f32, wi_bf, wo_bf, sem_i, sem_o):
    step = pl.program_id(0)

    def fetch(e):
        return (pltpu.make_async_copy(wi_hbm.at[e], wi_f32, sem_i),
                pltpu.make_async_copy(wo_hbm.at[e], wo_f32, sem_o))

    @pl.when(step == 0)
    def _():
        for cp in fetch(be_ref[0]):
            cp.start()

    def load_weights(i):
        @pl.when(bf_ref[i] == 1)
        def _():
            for cp in fetch(be_ref[i]):
                cp.wait()
            wi_bf[...] = wi_f32[...].astype(BF16)
            wo_bf[...] = wo_f32[...].astype(BF16)

            @pl.when(nx_ref[i] >= 0)
            def _():
                for cp in fetch(nx_ref[i]):
                    cp.start()

    def ffn(i, first, rows):
        e = be_ref[i]
        xb = _unpack_rows(_load_grouped(xs_ref, rows, first)).astype(BF16)
        gu = jnp.dot(xb, wi_bf[...], preferred_element_type=F32) + bi_ref[pl.ds(e, 1), :]
        gate = jnp.minimum(gu[:, :D_E], LIMIT)
        up = jnp.clip(gu[:, D_E:], -LIMIT, LIMIT)
        act = (up + 1.0) * (gate * _sigmoid(ALPHA * gate))
        y = jnp.dot(act.astype(BF16), wo_bf[...], preferred_element_type=F32) + bo_ref[pl.ds(e, 1), :]
        _store_grouped(ys_ref, _pack_rows(y), rows, first)

    i0 = step * BPS
    last = i0 + BPS - 1
    uniform = (last < nv_ref[0]) & (be_ref[i0] == be_ref[last])

    @pl.when(uniform)
    def _():
        load_weights(i0)
        ffn(i0, 0, BPS * TM)

    @pl.when(jnp.logical_not(uniform))
    def _():
        for sub in range(BPS):
            i = i0 + sub

            @pl.when(i < nv_ref[0])
            def _(i=i, sub=sub):
                load_weights(i)
                ffn(i, sub * TM, TM)


def _moe_call(blk_e, blk_first, blk_next, n_valid, xs, w_in, b_in, w_out, b_out):
    assert N_BLOCKS % BPS == 0

    def row_map(s, be, bf, nx, nv):
        last = (nv[0] + BPS - 1) // BPS - 1
        return (jnp.maximum(jnp.minimum(s, last), 0), 0)

    grid_spec = pltpu.PrefetchScalarGridSpec(
        num_scalar_prefetch=4,
        grid=(N_BLOCKS // BPS,),
        in_specs=[pl.BlockSpec((BPS * TM * SUB, LANES), row_map),
                  pl.BlockSpec(memory_space=pl.ANY),
                  pl.BlockSpec((N_E, 2 * D_E), lambda s, *_: (0, 0)),
                  pl.BlockSpec(memory_space=pl.ANY),
                  pl.BlockSpec((N_E, D), lambda s, *_: (0, 0))],
        out_specs=pl.BlockSpec((BPS * TM * SUB, LANES), row_map),
        scratch_shapes=[pltpu.VMEM((D, 2 * D_E), F32),
                        pltpu.VMEM((D_E, D), F32),
                        pltpu.VMEM((D, 2 * D_E), BF16),
                        pltpu.VMEM((D_E, D), BF16),
                        pltpu.SemaphoreType.DMA,
                        pltpu.SemaphoreType.DMA],
    )
    return pl.pallas_call(
        _moe_kernel,
        grid_spec=grid_spec,
        out_shape=jax.ShapeDtypeStruct((N_ROWS * SUB, LANES), U32),
        compiler_params=pltpu.CompilerParams(dimension_semantics=("arbitrary",),
                                             vmem_limit_bytes=VMEM_LIMIT),
        name="moe_experts",
    )(blk_e, blk_first, blk_next, n_valid, xs, w_in, b_in, w_out, b_out)


def _combine_kernel(loc_ref, glob_ref, tot_ref, ys_ref, pc_ref, x1_ref, gate2_ref, fg_ref, o_ref,
                    buf0, buf1, buf2, buf3, sem0, sem1, sem2, sem3):
    j = pl.program_id(0)
    head = ys_ref.at[pl.ds(0, RS * SUB)]
    even = ((buf0, sem0), (buf1, sem1))
    odd = ((buf2, sem2), (buf3, sem3))

    def fetch(step, slots):
        for t, (buf, sem) in enumerate(slots):
            _start_tile_runs(2 * step + t, ys_ref, buf, glob_ref, loc_ref, tot_ref, sem)

    @pl.when(j == 0)
    def _():
        fetch(0, even)

    col = lax.broadcasted_iota(jnp.int32, (TT, RS), 1)
    wms = []
    for t in range(2):
        pc = pc_ref[t * TT:(t + 1) * TT, :]
        pos = pc.astype(jnp.int32)
        wm = jnp.zeros((TT, RS), F32)
        for k in range(TOP_K):
            wm = jnp.where(col == pos[:, k:k + 1], pc[:, TOP_K + k:TOP_K + k + 1], wm)
        wms.append(wm.astype(BF16))

    def step(cur, nxt):
        @pl.when(j + 1 < NT // 2)
        def _():
            fetch(j + 1, nxt)
        for t, (buf, sem) in enumerate(cur):
            _wait_tile_runs(head, buf, sem)
            yt = _unpack_rows(_load_grouped(buf, RS)).astype(BF16)
            acc = jnp.dot(wms[t], yt, preferred_element_type=F32)
            x2 = x1_ref[t * TT:(t + 1) * TT, :] + gate2_ref[0] * acc
            o_ref[t * TT:(t + 1) * TT, :] = _rms(x2) * fg_ref[...]

    @pl.when(j % 2 == 0)
    def _():
        step(even, odd)

    @pl.when(j % 2 == 1)
    def _():
        step(odd, even)


def _combine_call(loc_t, glob_t, tot_t, ys, pos_c, x1, mod3, final_g):
    per_b = S // (2 * TT)
    grid_spec = pltpu.PrefetchScalarGridSpec(
        num_scalar_prefetch=3,
        grid=(NT // 2,),
        in_specs=[pl.BlockSpec(memory_space=pl.ANY),
                  pl.BlockSpec((2 * TT, N_E), lambda j, *_: (j, 0)),
                  pl.BlockSpec((2 * TT, D), lambda j, *_: (j, 0)),
                  pl.BlockSpec((1, 1, D), lambda j, *_: (j // per_b, 0, 5)),
                  pl.BlockSpec((1, D), lambda j, *_: (0, 0))],
        out_specs=pl.BlockSpec((2 * TT, D), lambda j, *_: (j, 0)),
        scratch_shapes=[pltpu.VMEM((RS * SUB, LANES), U32)] * 4 + [pltpu.SemaphoreType.DMA] * 4,
    )
    return pl.pallas_call(
        _combine_kernel,
        grid_spec=grid_spec,
        out_shape=jax.ShapeDtypeStruct((T, D), F32),
        compiler_params=pltpu.CompilerParams(dimension_semantics=("arbitrary",),
                                             vmem_limit_bytes=VMEM_LIMIT),
        name="combine_norm",
    )(loc_t, glob_t, tot_t, ys, pos_c, x1, mod3, final_g)


def kernel(x, c, ada_w, ada_b, norm1_g, w_in, ssm_a_re, ssm_a_im, ssm_log_dt, ssm_b_re, ssm_b_im, ssm_c_re, ssm_c_im, ssm_d, ssm_glu_w, ssm_glu_b, w_branch_a, gmlp_ln_g, gmlp_ln_b, gmlp_ws, gmlp_bs, w_branch_b, w_out, norm2_g, router_w, router_b, moe_w_in, moe_b_in, moe_w_out, moe_b_out, final_g):
    depth = ada_w.shape[0]
    assert depth == 1, "the final rms_norm is fused into the combine kernel of the only layer"
    for layer in range(depth):
        mod = _mod_call(c, ada_w[layer], ada_b[layer])
        mod3 = mod.reshape(B, 1, 6 * D)

        u, zuv, ga, gb = _proj_call(x, norm1_g[layer], mod3, w_in[layer].astype(BF16))

        bm, cre, cim, are, aim = _s5_params(ssm_a_re[layer], ssm_a_im[layer], ssm_log_dt[layer],
                                            ssm_b_re[layer], ssm_b_im[layer],
                                            ssm_c_re[layer], ssm_c_im[layer])
        ya = _s5_call(u, bm, cre, cim, are, aim,
                      ssm_d[layer].reshape(1, SSM_W), ssm_glu_w[layer].astype(BF16),
                      ssm_glu_b[layer].reshape(1, SSM_W), w_branch_a[layer].astype(BF16))

        ws = gmlp_ws[layer]
        ws_pairs = jnp.concatenate([ws[0::2], ws[1::2]], axis=-1)
        bias_full = jnp.repeat(gmlp_bs[layer].T, GM_HD, axis=1)
        x1, h2, logits = _mix_call(
            zuv, ga, gb, ya, x, mod3,
            gmlp_ln_g[layer].reshape(1, GM_W), gmlp_ln_b[layer].reshape(1, GM_W),
            ws_pairs, bias_full, w_branch_b[layer].astype(BF16), w_out[layer].astype(BF16),
            norm2_g[layer].reshape(1, D), router_w[layer].astype(BF16),
            router_b[layer].reshape(1, N_E))

        pos_c, pos_t, lists, totals, cnt = _route_call(logits.reshape(T, N_E))
        counts = cnt[0].astype(jnp.int32)
        nblk = (counts + TM - 1) // TM
        blk_end = jnp.cumsum(nblk)
        pad_end = (blk_end * TM).astype(jnp.int32)
        experts = jnp.arange(N_E, dtype=jnp.int32)
        blk_ids = jnp.arange(N_BLOCKS, dtype=jnp.int32)
        blk_e = jnp.sum((blk_end[None, :] <= blk_ids[:, None]).astype(jnp.int32), axis=1)
        blk_e = jnp.minimum(blk_e, N_E - 1)
        blk_first = jnp.concatenate([jnp.ones((1,), jnp.int32),
                                     (blk_e[1:] != blk_e[:-1]).astype(jnp.int32)])
        later = (experts[None, :] > experts[:, None]) & (nblk[None, :] > 0)
        next_e = jnp.min(jnp.where(later, experts[None, :], N_E), axis=1)
        next_e = jnp.where(next_e == N_E, -1, next_e)
        blk_next = jnp.sum(jnp.where(blk_e[:, None] == experts[None, :], next_e[None, :], 0), axis=1)
        n_valid = blk_end[-1:].astype(jnp.int32)
        pad_start = pad_end - nblk * TM
        loc_t = lists[:, 0, :].reshape(NT * LIST_W)
        owner = lists[:, 2, :, None] == experts[None, None, :]
        glob_t = (lists[:, 1, :] + jnp.sum(jnp.where(owner, pad_start, 0), axis=-1)).reshape(NT * LIST_W)
        tot_t = totals[:, 0, :].reshape(NT * 8)

        xs = _dispatch_call(loc_t, glob_t, tot_t, pad_end, h2.reshape(T, D), pos_t)
        ys = _moe_call(blk_e, blk_first, blk_next.astype(jnp.int32), n_valid, xs, moe_w_in[layer],
                       moe_b_in[layer], moe_w_out[layer], moe_b_out[layer])
        x = _combine_call(loc_t, glob_t, tot_t, ys, pos_c, x1.reshape(T, D),
                          mod3, final_g.reshape(1, D)).reshape(B, S, D)
    return x
```

```python
import functools
import math

import jax
import jax.numpy as jnp
from jax import lax
from jax.experimental import pallas as pl
from jax.experimental.pallas import tpu as pltpu

F32 = jnp.float32
BF16 = jnp.bfloat16

D = 1024
B = 8
S = 2048
T = B * S
SSM_W = 512
SSM_G = 32
SSM_H = 16
SSM_P = 64
N_PACK = 4
PACK_G = SSM_G // N_PACK
GM_W = 512
GM_HEADS = 8
GM_HD = 64
CHUNK = 128
N_E = 32
TOP_K = 4
D_E = 1024
LIMIT = 7.0
ALPHA = 1.702
EPS = 1e-6

TS_PROJ = 512
L_SSM = 128
R_SSM = L_SSM * B
TS_MIX = 512
SUB_MIX = 256
TT = 256
NT = T // TT
ROUTE_TILES = 4
RS = TOP_K * TT
TM = 256
BPS = 4
N_ROWS = T * TOP_K + N_E * TM
N_BLOCKS = N_ROWS // TM
LANES = 128
HALF = D // 2
SUB = HALF // LANES
PIECE = 16
COPY_CLASSES = ((PIECE, 0),) + tuple((PIECE >> s, RS // PIECE + N_E * (s - 1))
                                     for s in range(1, PIECE.bit_length()))
LIST_W = 256
assert COPY_CLASSES[-1][1] + N_E <= LIST_W and len(COPY_CLASSES) <= 8
VMEM_LIMIT = 56 * 1024 * 1024
U32 = jnp.uint32
ACT = jnp.bfloat16


def _sigmoid(v):
    return 1.0 / (1.0 + jnp.exp(-v))


def _gelu(v):
    return 0.5 * v * (1.0 + jnp.tanh(math.sqrt(2.0 / math.pi) * (v + 0.044715 * v * v * v)))


def _rms(v):
    return v * lax.rsqrt(jnp.mean(v * v, axis=-1, keepdims=True) + EPS)


def _mod_kernel(c_ref, w_ref, b_ref, o_ref):
    cv = c_ref[...]
    sv = cv * _sigmoid(cv)
    o_ref[...] = jnp.dot(sv, w_ref[...], preferred_element_type=F32,
                         precision=lax.Precision.HIGHEST) + b_ref[...]


def _mod_call(c, ada_w, ada_b):
    n = ada_w.shape[1]
    return pl.pallas_call(
        _mod_kernel,
        grid=(n // D,),
        in_specs=[pl.BlockSpec((B, D), lambda j: (0, 0)),
                  pl.BlockSpec((D, D), lambda j: (0, j)),
                  pl.BlockSpec((1, D), lambda j: (0, j))],
        out_specs=pl.BlockSpec((B, D), lambda j: (0, j)),
        out_shape=jax.ShapeDtypeStruct((B, n), F32),
        name="adaln_mod",
    )(c, ada_w, ada_b.reshape(1, n))


def _proj_kernel(x_ref, g_ref, shift_ref, scale_ref, w_ref, u_ref, zuv_ref, ga_ref, gb_ref):
    h = _rms(x_ref[0]) * g_ref[...]
    h = h * (1.0 + scale_ref[0]) + shift_ref[0]
    hb = h.astype(BF16)
    u_ref[0] = jnp.dot(hb, w_ref[:, 0:SSM_W], preferred_element_type=F32)
    zuv_ref[0] = jnp.dot(hb, w_ref[:, SSM_W:SSM_W + 2 * GM_W],
                         preferred_element_type=F32).astype(zuv_ref.dtype)
    ga_ref[0] = jnp.dot(hb, w_ref[:, SSM_W + 2 * GM_W:SSM_W + 2 * GM_W + D],
                        preferred_element_type=F32).astype(ga_ref.dtype)
    gb_ref[0] = jnp.dot(hb, w_ref[:, SSM_W + 2 * GM_W + D:],
                        preferred_element_type=F32).astype(gb_ref.dtype)


def _proj_call(x, norm_g, mod3, w_in_bf):
    pw = w_in_bf.shape[1]
    tok_spec = pl.BlockSpec((1, TS_PROJ, D), lambda b, s: (b, s, 0))
    return pl.pallas_call(
        _proj_kernel,
        grid=(B, S // TS_PROJ),
        in_specs=[tok_spec,
                  pl.BlockSpec((1, D), lambda b, s: (0, 0)),
                  pl.BlockSpec((1, 1, D), lambda b, s: (b, 0, 0)),
                  pl.BlockSpec((1, 1, D), lambda b, s: (b, 0, 1)),
                  pl.BlockSpec((D, pw), lambda b, s: (0, 0))],
        out_specs=[pl.BlockSpec((1, TS_PROJ, SSM_W), lambda b, s: (b, s, 0)),
                   tok_spec, tok_spec, tok_spec],
        out_shape=[jax.ShapeDtypeStruct((B, S, SSM_W), F32),
                   jax.ShapeDtypeStruct((B, S, D), ACT),
                   jax.ShapeDtypeStruct((B, S, D), ACT),
                   jax.ShapeDtypeStruct((B, S, D), ACT)],
        compiler_params=pltpu.CompilerParams(vmem_limit_bytes=VMEM_LIMIT),
        name="norm_proj",
    )(x, norm_g.reshape(1, D), mod3, mod3, w_in_bf)


def _s5_kernel(u_ref, bm_ref, cre_ref, cim_ref, are_ref, aim_ref, d_ref, gw_ref, gb_ref, wa_ref,
               o_ref, usc, ysc, sre, sim, st_re, st_im):
    @pl.when(pl.program_id(0) == 0)
    def _():
        st_re[...] = jnp.zeros_like(st_re)
        st_im[...] = jnp.zeros_like(st_im)

    nslab = SSM_W // LANES
    for b in range(B):
        for c in range(nslab):
            usc[c, pl.ds(b, L_SSM, stride=B), :] = u_ref[b, :, c * LANES:(c + 1) * LANES]
    u = jnp.concatenate([usc[c] for c in range(nslab)], axis=1)
    ub = u.astype(BF16)
    half = PACK_G * SSM_P
    ys = []
    for k in range(N_PACK):
        bu = jnp.dot(ub[:, 128 * k:128 * (k + 1)], bm_ref[k], preferred_element_type=F32)
        sre[k] = bu[:, :half]
        sim[k] = bu[:, half:]
        ar = are_ref[k]
        ai = aim_ref[k]
        r = st_re[k]
        m = st_im[k]
        for t in range(L_SSM):
            rows = pl.ds(t * B, B)
            nr = ar * r - ai * m + sre[k, rows, :]
            m = ar * m + ai * r + sim[k, rows, :]
            r = nr
            sre[k, rows, :] = r
            sim[k, rows, :] = m
        st_re[k] = r
        st_im[k] = m
        yk = jnp.dot(sre[k].astype(BF16), cre_ref[k], preferred_element_type=F32)
        yk = yk + jnp.dot(sim[k].astype(BF16), cim_ref[k], preferred_element_type=F32)
        ys.append(yk)
    for c in range(nslab):
        uc = usc[c]
        ysc[c] = ys[c] + d_ref[:, c * LANES:(c + 1) * LANES] * uc
    y = jnp.concatenate(
        [jnp.concatenate([ysc[c, pl.ds(b, L_SSM, stride=B), :] for c in range(nslab)], axis=1)
         for b in range(B)], axis=0)
    z = _gelu(y)
    gl = jnp.dot(z.astype(BF16), gw_ref[...], preferred_element_type=F32) + gb_ref[...]
    out = z * _sigmoid(gl)
    o = jnp.dot(out.astype(BF16), wa_ref[...], preferred_element_type=F32)
    for b in range(B):
        o_ref[b] = o[b * L_SSM:(b + 1) * L_SSM].astype(o_ref.dtype)


def _s5_call(u, bm, cre, cim, are, aim, d_skip, glu_w, glu_b, w_a):
    half = PACK_G * SSM_P
    full = lambda *shape: pl.BlockSpec(shape, lambda i: (0,) * len(shape))
    return pl.pallas_call(
        _s5_kernel,
        grid=(S // L_SSM,),
        in_specs=[pl.BlockSpec((B, L_SSM, SSM_W), lambda i: (0, i, 0)),
                  full(N_PACK, 128, 2 * half),
                  full(N_PACK, half, 128),
                  full(N_PACK, half, 128),
                  full(N_PACK, B, half),
                  full(N_PACK, B, half),
                  full(1, SSM_W),
                  full(SSM_W, SSM_W),
                  full(1, SSM_W),
                  full(SSM_W, D)],
        out_specs=pl.BlockSpec((B, L_SSM, D), lambda i: (0, i, 0)),
        out_shape=jax.ShapeDtypeStruct((B, S, D), ACT),
        scratch_shapes=[pltpu.VMEM((SSM_W // LANES, R_SSM, LANES), F32),
                        pltpu.VMEM((SSM_W // LANES, R_SSM, LANES), F32),
                        pltpu.VMEM((N_PACK, R_SSM, half), F32),
                        pltpu.VMEM((N_PACK, R_SSM, half), F32),
                        pltpu.VMEM((N_PACK, B, half), F32),
                        pltpu.VMEM((N_PACK, B, half), F32)],
        compiler_params=pltpu.CompilerParams(dimension_semantics=("arbitrary",),
                                             vmem_limit_bytes=VMEM_LIMIT),
        name="s5_branch",
    )(u, bm, cre, cim, are, aim, d_skip, glu_w, glu_b, w_a)


def _s5_params(a_re, a_im, log_dt, b_re, b_im, c_re, c_im):
    dt = jnp.exp(log_dt)[:, None]
    mag = jnp.exp(a_re * dt)
    lr = mag * jnp.cos(a_im * dt)
    li = mag * jnp.sin(a_im * dt)
    den = a_re * a_re + a_im * a_im
    cr = ((lr - 1.0) * a_re + li * a_im) / den
    ci = (li * a_re - (lr - 1.0) * a_im) / den
    bbr = cr[..., None] * b_re - ci[..., None] * b_im
    bbi = cr[..., None] * b_im + ci[..., None] * b_re
    eye = jnp.eye(PACK_G, dtype=F32)
    half = PACK_G * SSM_P

    def pack_b(m):
        m4 = m.reshape(N_PACK, PACK_G, SSM_P, SSM_H)
        return jnp.einsum('kgph,gj->kghjp', m4, eye).reshape(N_PACK, PACK_G * SSM_H, half)

    def pack_c(m):
        m4 = m.reshape(N_PACK, PACK_G, SSM_H, SSM_P)
        return jnp.einsum('kghp,gj->kgpjh', m4, eye).reshape(N_PACK, half, PACK_G * SSM_H)

    bm = jnp.concatenate([pack_b(bbr), pack_b(bbi)], axis=-1).astype(BF16)
    cre = pack_c(c_re).astype(BF16)
    cim = (-pack_c(c_im)).astype(BF16)
    are = jnp.broadcast_to(lr.reshape(N_PACK, 1, half), (N_PACK, B, half))
    aim = jnp.broadcast_to(li.reshape(N_PACK, 1, half), (N_PACK, B, half))
    return bm, cre, cim, are, aim


def _route_tile(l, carry):
    lane = lax.broadcasted_iota(jnp.int32, l.shape, 1).astype(F32)
    sels, vals = [], []
    for _ in range(TOP_K):
        m = jnp.max(l, axis=-1, keepdims=True)
        idx = jnp.min(jnp.where(l == m, lane, float(N_E)), axis=-1, keepdims=True)
        sel = lane == idx
        sels.append(sel)
        vals.append(m)
        l = jnp.where(sel, -jnp.inf, l)
    member = sels[0].astype(F32)
    for k in range(1, TOP_K):
        member = member + sels[k].astype(F32)
    tile_cnt = jnp.sum(member, axis=0, keepdims=True)

    r = lax.broadcasted_iota(jnp.int32, (N_E, N_E), 0)
    c = lax.broadcasted_iota(jnp.int32, (N_E, N_E), 1)
    tri = (r < c).astype(BF16)
    tcb = jnp.broadcast_to(tile_cnt, (8, N_E)).astype(BF16)
    seg = jnp.dot(tcb, tri, preferred_element_type=F32)[0:1]
    r = lax.broadcasted_iota(jnp.int32, (TT, TT), 0)
    c = lax.broadcasted_iota(jnp.int32, (TT, TT), 1)
    strict = (c < r).astype(BF16)
    rank = jnp.dot(strict, member.astype(BF16), preferred_element_type=F32)
    posb = seg + rank
    denom = jnp.zeros_like(vals[0])
    exps = []
    for k in range(TOP_K):
        e = jnp.exp(vals[k] - vals[0])
        exps.append(e)
        denom = denom + e
    pc = jnp.zeros(l.shape, F32)
    for k in range(TOP_K):
        pk = jnp.sum(jnp.where(sels[k], posb, 0.0), axis=-1, keepdims=True)
        pc = jnp.where(lane == float(k), pk, pc)
        pc = jnp.where(lane == float(TOP_K + k), exps[k] / denom, pc)
    r8 = lax.broadcasted_iota(jnp.int32, (8, N_E), 0)
    c8 = lax.broadcasted_iota(jnp.int32, (8, N_E), 1)
    eye = (r8 == c8).astype(F32)
    pt = lax.dot_general(eye, pc, (((1,), (1,)), ((), ())),
                         preferred_element_type=F32, precision=lax.Precision.HIGHEST)
    earlier = carry[...]
    carry[...] = earlier + tile_cnt
    lists, totals = _copy_lists(member, tile_cnt, seg, earlier)
    return pc, pt, lists, totals


def _copy_lists(member, tile_cnt, seg, earlier):
    shift = PIECE.bit_length() - 1

    def pieces(n, ci):
        if ci == 0:
            return jnp.right_shift(n, shift)
        return jnp.bitwise_and(jnp.right_shift(n, shift - ci), 1)

    def done(n, ci):
        if ci == 0:
            return jnp.zeros_like(n)
        return n - jnp.bitwise_and(n, (PIECE >> (ci - 1)) - 1)

    re = lax.broadcasted_iota(jnp.int32, (N_E, N_E), 0)
    ce = lax.broadcasted_iota(jnp.int32, (N_E, N_E), 1)
    member_t = lax.dot_general((re == ce).astype(BF16), member.astype(BF16), (((1,), (1,)), ((), ())),
                               preferred_element_type=F32)
    n_col = jnp.sum(member_t, axis=1, keepdims=True).astype(jnp.int32)
    n_row = tile_cnt.astype(jnp.int32)
    lane8 = lax.broadcasted_iota(jnp.int32, (N_E, 8), 1)
    x = jnp.zeros((N_E, 8), F32)
    for ci in range(len(COPY_CLASSES)):
        x = jnp.where(lane8 == ci, pieces(n_col, ci).astype(F32), x)
    xb = x.astype(BF16)
    before = jnp.dot((ce < re).astype(BF16), xb, preferred_element_type=F32)
    totals = jnp.dot(jnp.ones((8, N_E), BF16), xb, preferred_element_type=F32)
    bulk_row = jnp.broadcast_to(pieces(n_row, 0).astype(F32), (8, N_E)).astype(BF16)
    before_bulk_row = jnp.dot(bulk_row, (re < ce).astype(BF16), preferred_element_type=F32)[0:1]

    lanes = lax.broadcasted_iota(jnp.int32, (N_E, LIST_W), 1)
    sub = lax.broadcasted_iota(jnp.int32, (8, N_E), 0)
    e_row = lax.broadcasted_iota(jnp.int32, (8, N_E), 1).astype(F32)
    lists = jnp.zeros((8, LIST_W), F32)
    for ci, (_, lane0) in enumerate(COPY_CLASSES):
        first = before[:, ci:ci + 1].astype(jnp.int32) + lane0
        sel = (lanes >= first) & (lanes < first + pieces(n_col, ci))
        d_row = done(n_row, ci).astype(F32)
        if ci == 0:
            d_row = d_row - PIECE * before_bulk_row
        v = jnp.where(sub == 0, seg + d_row, jnp.where(sub == 1, earlier + d_row,
                                                       jnp.where(sub == 2, e_row, 0.0)))
        lists = lists + jnp.dot(v, sel.astype(F32), preferred_element_type=F32,
                                precision=lax.Precision.HIGHEST)
    q = lax.broadcasted_iota(jnp.int32, (8, LIST_W), 1)
    s8 = lax.broadcasted_iota(jnp.int32, (8, LIST_W), 0)
    lists = lists + jnp.where((s8 < 2) & (q < RS // PIECE), (PIECE * q).astype(F32), 0.0)
    return lists, totals


def _mix_kernel(zuv_ref, ga_ref, gb_ref, ya_ref, x_ref, gate1_ref, shift2_ref, scale2_ref,
                lng_ref, lnb_ref, ws_ref, bias_ref, wbb_ref, wo_ref, n2g_ref, rw_ref, rb_ref,
                x1_ref, h2_ref, lg_ref):
    row = lax.broadcasted_iota(jnp.int32, (CHUNK, 2 * CHUNK), 0)
    col = lax.broadcasted_iota(jnp.int32, (CHUNK, 2 * CHUNK), 1)
    causal = (col % CHUNK) <= row
    lane = lax.broadcasted_iota(jnp.int32, (CHUNK, 2 * GM_HD), 1)
    first = lane < GM_HD
    wpairs = [jnp.where(causal, ws_ref[j], 0.0).astype(BF16) for j in range(GM_HEADS // 2)]

    for g in range(TS_MIX // SUB_MIX):
        rows = pl.ds(g * SUB_MIX, SUB_MIX)
        z = _gelu(zuv_ref[0, rows, :].astype(F32))
        u = z[:, :GM_W]
        v = z[:, GM_W:]
        mu = jnp.mean(v, axis=-1, keepdims=True)
        vc = v - mu
        var = jnp.mean(vc * vc, axis=-1, keepdims=True)
        vn = vc * lax.rsqrt(var + EPS) * lng_ref[...] + lnb_ref[...]
        chunks = []
        for n in range(SUB_MIX // CHUNK):
            cols = []
            for j in range(GM_HEADS // 2):
                vp = vn[n * CHUNK:(n + 1) * CHUNK, 2 * GM_HD * j:2 * GM_HD * (j + 1)]
                rhs = jnp.concatenate([jnp.where(first, vp, 0.0), jnp.where(first, 0.0, vp)], axis=0)
                cols.append(jnp.dot(wpairs[j], rhs.astype(BF16), preferred_element_type=F32))
            chunks.append(jnp.concatenate(cols, axis=1) + bias_ref[...])
        mixed = jnp.concatenate(chunks, axis=0)
        gm = u * mixed
        yb = jnp.dot(gm.astype(BF16), wbb_ref[...], preferred_element_type=F32)
        merged = (_sigmoid(ga_ref[0, rows, :].astype(F32)) * ya_ref[0, rows, :].astype(F32)
                  + _sigmoid(gb_ref[0, rows, :].astype(F32)) * yb)
        o = jnp.dot(merged.astype(BF16), wo_ref[...], preferred_element_type=F32)
        x1 = x_ref[0, rows, :] + gate1_ref[0] * o
        x1_ref[0, rows, :] = x1
        h2 = _rms(x1) * n2g_ref[...]
        h2 = h2 * (1.0 + scale2_ref[0]) + shift2_ref[0]
        hb = h2.astype(BF16)
        h2_ref[0, rows, :] = hb
        lg_ref[0, rows, :] = jnp.dot(hb, rw_ref[...], preferred_element_type=F32) + rb_ref[...]


def _route_kernel(lg_ref, pc_ref, pt_ref, lists_ref, totals_ref, cnt_ref, carry):
    @pl.when(pl.program_id(0) == 0)
    def _():
        carry[...] = jnp.zeros_like(carry)

    for t in range(ROUTE_TILES):
        pc, pt, lists, totals = _route_tile(lg_ref[t * TT:(t + 1) * TT, :], carry)
        pc_ref[t * TT:(t + 1) * TT, :] = pc
        pt_ref[:, t * TT:(t + 1) * TT] = pt
        lists_ref[t] = lists.astype(jnp.int32)
        totals_ref[t] = totals.astype(jnp.int32)
    cnt_ref[...] = jnp.broadcast_to(carry[...], cnt_ref.shape)


def _route_call(logits):
    assert NT % ROUTE_TILES == 0
    return pl.pallas_call(
        _route_kernel,
        grid=(NT // ROUTE_TILES,),
        in_specs=[pl.BlockSpec((ROUTE_TILES * TT, N_E), lambda i: (i, 0))],
        out_specs=[pl.BlockSpec((ROUTE_TILES * TT, N_E), lambda i: (i, 0)),
                   pl.BlockSpec((8, ROUTE_TILES * TT), lambda i: (0, i)),
                   pl.BlockSpec((ROUTE_TILES, 8, LIST_W), lambda i: (i, 0, 0)),
                   pl.BlockSpec((ROUTE_TILES, 8, 8), lambda i: (i, 0, 0)),
                   pl.BlockSpec((8, N_E), lambda i: (0, 0))],
        out_shape=[jax.ShapeDtypeStruct((T, N_E), F32),
                   jax.ShapeDtypeStruct((8, T), F32),
                   jax.ShapeDtypeStruct((NT, 8, LIST_W), jnp.int32),
                   jax.ShapeDtypeStruct((NT, 8, 8), jnp.int32),
                   jax.ShapeDtypeStruct((8, N_E), F32)],
        scratch_shapes=[pltpu.VMEM((1, N_E), F32)],
        compiler_params=pltpu.CompilerParams(dimension_semantics=("arbitrary",)),
        name="route",
    )(logits)


def _mix_call(zuv, ga, gb, ya2d, x, mod3, ln_g, ln_b, ws_pairs, bias_full, wbb, wo, n2g, rw, rb):
    tok_spec = pl.BlockSpec((1, TS_MIX, D), lambda b, s: (b, s, 0))
    full = lambda *shape: pl.BlockSpec(shape, lambda b, s: (0,) * len(shape))
    mod_spec = lambda j: pl.BlockSpec((1, 1, D), lambda b, s: (b, 0, j))
    return pl.pallas_call(
        _mix_kernel,
        grid=(B, S // TS_MIX),
        in_specs=[tok_spec, tok_spec, tok_spec, tok_spec, tok_spec,
                  mod_spec(2), mod_spec(3), mod_spec(4),
                  full(1, GM_W), full(1, GM_W),
                  full(GM_HEADS // 2, CHUNK, 2 * CHUNK),
                  full(CHUNK, GM_W),
                  full(GM_W, D), full(D, D), full(1, D),
                  full(D, N_E), full(1, N_E)],
        out_specs=[tok_spec, tok_spec,
                   pl.BlockSpec((1, TS_MIX, N_E), lambda b, s: (b, s, 0))],
        out_shape=[jax.ShapeDtypeStruct((B, S, D), F32),
                   jax.ShapeDtypeStruct((B, S, D), BF16),
                   jax.ShapeDtypeStruct((B, S, N_E), F32)],
        compiler_params=pltpu.CompilerParams(vmem_limit_bytes=VMEM_LIMIT),
        name="gmlp_merge_norm2",
    )(zuv, ga, gb, ya2d, x, mod3, mod3, mod3, ln_g, ln_b, ws_pairs, bias_full, wbb, wo, n2g, rw, rb)


def _pack_rows(v):
    return pltpu.pack_elementwise([v[:, :HALF], v[:, HALF:]], packed_dtype=BF16)


def _unpack_rows(w):
    halves = [pltpu.unpack_elementwise(w, index=i, packed_dtype=BF16, unpacked_dtype=F32)
              for i in range(2)]
    return jnp.concatenate(halves, axis=1)


def _load_grouped(ref, rows, first=0):
    return jnp.concatenate([ref[pl.ds(first * SUB + c, rows, stride=SUB), :] for c in range(SUB)], axis=1)


def _store_grouped(ref, w, rows, first=0):
    for c in range(SUB):
        ref[pl.ds(first * SUB + c, rows, stride=SUB), :] = w[:, c * LANES:(c + 1) * LANES]


def _start_tile_runs(tile, src_ref, dst_ref, src_tbl, dst_tbl, totals_tbl, sem):
    for ci, (rows, lane0) in enumerate(COPY_CLASSES):
        base = tile * LIST_W + lane0

        def start(i, carry, rows=rows, base=base):
            s = pl.multiple_of(src_tbl[base + i] * SUB, SUB)
            d = pl.multiple_of(dst_tbl[base + i] * SUB, SUB)
            pltpu.make_async_copy(src_ref.at[pl.ds(s, rows * SUB)],
                                  dst_ref.at[pl.ds(d, rows * SUB)], sem).start()
            return carry

        lax.fori_loop(0, totals_tbl[tile * 8 + ci], start, 0)


def _wait_tile_runs(src_ref, dst_ref, sem):
    pltpu.make_async_copy(src_ref, dst_ref, sem).wait()


def _dispatch_kernel(loc_ref, glob_ref, tot_ref, pend_ref, h_ref, pt_ref, xs_ref,
                     sbuf0, sbuf1, zbuf, sem_z, sem0, sem1):
    j = pl.program_id(0)

    @pl.when(j == 0)
    def _():
        zbuf[...] = jnp.zeros_like(zbuf)
        for e in range(N_E):
            prev = pend_ref[e - 1] if e > 0 else 0
            end = pend_ref[e]

            @pl.when(end > prev)
            def _():
                first = pl.multiple_of((end - TM) * SUB, TM * SUB)
                cp = pltpu.make_async_copy(zbuf, xs_ref.at[pl.ds(first, TM * SUB)], sem_z)
                cp.start()
                cp.wait()

    rows = lax.broadcasted_iota(jnp.int32, (RS, TT), 0)
    words = []
    for t in range(2):
        pos = pt_ref[:, t * TT:(t + 1) * TT].astype(jnp.int32)
        hit = rows == pos[0:1, :]
        for k in range(1, TOP_K):
            hit = hit | (rows == pos[k:k + 1, :])
        pm = jnp.where(hit, 1.0, 0.0).astype(BF16)
        srt = jnp.dot(pm, h_ref[t * TT:(t + 1) * TT, :], preferred_element_type=F32)
        words.append(_pack_rows(srt))
    head = xs_ref.at[pl.ds(0, RS * SUB)]

    for t, (sbuf, sem) in enumerate(((sbuf0, sem0), (sbuf1, sem1))):
        @pl.when(j >= 1)
        def _(sbuf=sbuf, sem=sem):
            _wait_tile_runs(sbuf, head, sem)
        _store_grouped(sbuf, words[t], RS)
        _start_tile_runs(2 * j + t, sbuf, xs_ref, loc_ref, glob_ref, tot_ref, sem)

    @pl.when(j == NT // 2 - 1)
    def _():
        _wait_tile_runs(sbuf0, head, sem0)
        _wait_tile_runs(sbuf1, head, sem1)


def _dispatch_call(loc_t, glob_t, tot_t, pad_end, h2, pos_t):
    assert NT % 2 == 0
    grid_spec = pltpu.PrefetchScalarGridSpec(
        num_scalar_prefetch=4,
        grid=(NT // 2,),
        in_specs=[pl.BlockSpec((2 * TT, D), lambda j, *_: (j, 0)),
                  pl.BlockSpec((8, 2 * TT), lambda j, *_: (0, j))],
        out_specs=pl.BlockSpec(memory_space=pl.ANY),
        scratch_shapes=[pltpu.VMEM((RS * SUB, LANES), U32),
                        pltpu.VMEM((RS * SUB, LANES), U32),
                        pltpu.VMEM((TM * SUB, LANES), U32),
                        pltpu.SemaphoreType.DMA,
                        pltpu.SemaphoreType.DMA,
                        pltpu.SemaphoreType.DMA],
    )
    return pl.pallas_call(
        _dispatch_kernel,
        grid_spec=grid_spec,
        out_shape=jax.ShapeDtypeStruct((N_ROWS * SUB, LANES), U32),
        compiler_params=pltpu.CompilerParams(dimension_semantics=("arbitrary",),
                                             vmem_limit_bytes=VMEM_LIMIT),
        name="dispatch",
    )(loc_t, glob_t, tot_t, pad_end, h2, pos_t)


def _moe_kernel(be_ref, bf_ref, nx_ref, nv_ref, xs_ref, wi_hbm, bi_ref, wo_hbm, bo_ref, ys_ref,
                wi_f32, wo_f32, wi_bf, wo_bf, sem_i, sem_o):
    step = pl.program_id(0)

    def fetch(e):
        return (pltpu.make_async_copy(wi_hbm.at[e], wi_f32, sem_i),
                pltpu.make_async_copy(wo_hbm.at[e], wo_f32, sem_o))

    @pl.when(step == 0)
    def _():
        for cp in fetch(be_ref[0]):
            cp.start()

    def load_weights(i):
        @pl.when(bf_ref[i] == 1)
        def _():
            for cp in fetch(be_ref[i]):
                cp.wait()
            wi_bf[...] = wi_f32[...].astype(BF16)
            wo_bf[...] = wo_f32[...].astype(BF16)

            @pl.when(nx_ref[i] >= 0)
            def _():
                for cp in fetch(nx_ref[i]):
                    cp.start()

    def ffn(i, first, rows):
        e = be_ref[i]
        xb = _unpack_rows(_load_grouped(xs_ref, rows, first)).astype(BF16)
        gu = jnp.dot(xb, wi_bf[...], preferred_element_type=F32) + bi_ref[pl.ds(e, 1), :]
        gate = jnp.minimum(gu[:, :D_E], LIMIT)
        up = jnp.clip(gu[:, D_E:], -LIMIT, LIMIT)
        act = (up + 1.0) * (gate * _sigmoid(ALPHA * gate))
        y = jnp.dot(act.astype(BF16), wo_bf[...], preferred_element_type=F32) + bo_ref[pl.ds(e, 1), :]
        _store_grouped(ys_ref, _pack_rows(y), rows, first)

    i0 = step * BPS
    last = i0 + BPS - 1
    uniform = (last < nv_ref[0]) & (be_ref[i0] == be_ref[last])

    @pl.when(uniform)
    def _():
        load_weights(i0)
        ffn(i0, 0, BPS * TM)

    @pl.when(jnp.logical_not(uniform))
    def _():
        for sub in range(BPS):
            i = i0 + sub

            @pl.when(i < nv_ref[0])
            def _(i=i, sub=sub):
                load_weights(i)
                ffn(i, sub * TM, TM)


def _moe_call(blk_e, blk_first, blk_next, n_valid, xs, w_in, b_in, w_out, b_out):
    assert N_BLOCKS % BPS == 0

    def row_map(s, be, bf, nx, nv):
        last = (nv[0] + BPS - 1) // BPS - 1
        return (jnp.maximum(jnp.minimum(s, last), 0), 0)

    grid_spec = pltpu.PrefetchScalarGridSpec(
        num_scalar_prefetch=4,
        grid=(N_BLOCKS // BPS,),
        in_specs=[pl.BlockSpec((BPS * TM * SUB, LANES), row_map),
                  pl.BlockSpec(memory_space=pl.ANY),
                  pl.BlockSpec((N_E, 2 * D_E), lambda s, *_: (0, 0)),
                  pl.BlockSpec(memory_space=pl.ANY),
                  pl.BlockSpec((N_E, D), lambda s, *_: (0, 0))],
        out_specs=pl.BlockSpec((BPS * TM * SUB, LANES), row_map),
        scratch_shapes=[pltpu.VMEM((D, 2 * D_E), F32),
                        pltpu.VMEM((D_E, D), F32),
                        pltpu.VMEM((D, 2 * D_E), BF16),
                        pltpu.VMEM((D_E, D), BF16),
                        pltpu.SemaphoreType.DMA,
                        pltpu.SemaphoreType.DMA],
    )
    return pl.pallas_call(
        _moe_kernel,
        grid_spec=grid_spec,
        out_shape=jax.ShapeDtypeStruct((N_ROWS * SUB, LANES), U32),
        compiler_params=pltpu.CompilerParams(dimension_semantics=("arbitrary",),
                                             vmem_limit_bytes=VMEM_LIMIT),
        name="moe_experts",
    )(blk_e, blk_first, blk_next, n_valid, xs, w_in, b_in, w_out, b_out)


def _combine_kernel(loc_ref, glob_ref, tot_ref, ys_ref, pc_ref, x1_ref, gate2_ref, fg_ref, o_ref,
                    buf0, buf1, buf2, buf3, sem0, sem1, sem2, sem3):
    j = pl.program_id(0)
    head = ys_ref.at[pl.ds(0, RS * SUB)]
    even = ((buf0, sem0), (buf1, sem1))
    odd = ((buf2, sem2), (buf3, sem3))

    def fetch(step, slots):
        for t, (buf, sem) in enumerate(slots):
            _start_tile_runs(2 * step + t, ys_ref, buf, glob_ref, loc_ref, tot_ref, sem)

    @pl.when(j == 0)
    def _():
        fetch(0, even)

    col = lax.broadcasted_iota(jnp.int32, (TT, RS), 1)
    wms = []
    for t in range(2):
        pc = pc_ref[t * TT:(t + 1) * TT, :]
        pos = pc.astype(jnp.int32)
        wm = jnp.zeros((TT, RS), F32)
        for k in range(TOP_K):
            wm = jnp.where(col == pos[:, k:k + 1], pc[:, TOP_K + k:TOP_K + k + 1], wm)
        wms.append(wm.astype(BF16))

    def step(cur, nxt):
        @pl.when(j + 1 < NT // 2)
        def _():
            fetch(j + 1, nxt)
        for t, (buf, sem) in enumerate(cur):
            _wait_tile_runs(head, buf, sem)
            yt = _unpack_rows(_load_grouped(buf, RS)).astype(BF16)
            acc = jnp.dot(wms[t], yt, preferred_element_type=F32)
            x2 = x1_ref[t * TT:(t + 1) * TT, :] + gate2_ref[0] * acc
            o_ref[t * TT:(t + 1) * TT, :] = _rms(x2) * fg_ref[...]

    @pl.when(j % 2 == 0)
    def _():
        step(even, odd)

    @pl.when(j % 2 == 1)
    def _():
        step(odd, even)


def _combine_call(loc_t, glob_t, tot_t, ys, pos_c, x1, mod3, final_g):
    per_b = S // (2 * TT)
    grid_spec = pltpu.PrefetchScalarGridSpec(
        num_scalar_prefetch=3,
        grid=(NT // 2,),
        in_specs=[pl.BlockSpec(memory_space=pl.ANY),
                  pl.BlockSpec((2 * TT, N_E), lambda j, *_: (j, 0)),
                  pl.BlockSpec((2 * TT, D), lambda j, *_: (j, 0)),
                  pl.BlockSpec((1, 1, D), lambda j, *_: (j // per_b, 0, 5)),
                  pl.BlockSpec((1, D), lambda j, *_: (0, 0))],
        out_specs=pl.BlockSpec((2 * TT, D), lambda j, *_: (j, 0)),
        scratch_shapes=[pltpu.VMEM((RS * SUB, LANES), U32)] * 4 + [pltpu.SemaphoreType.DMA] * 4,
    )
    return pl.pallas_call(
        _combine_kernel,
        grid_spec=grid_spec,
        out_shape=jax.ShapeDtypeStruct((T, D), F32),
        compiler_params=pltpu.CompilerParams(dimension_semantics=("arbitrary",),
                                             vmem_limit_bytes=VMEM_LIMIT),
        name="combine_norm",
    )(loc_t, glob_t, tot_t, ys, pos_c, x1, mod3, final_g)


def kernel(x, c, ada_w, ada_b, norm1_g, w_in, ssm_a_re, ssm_a_im, ssm_log_dt, ssm_b_re, ssm_b_im, ssm_c_re, ssm_c_im, ssm_d, ssm_glu_w, ssm_glu_b, w_branch_a, gmlp_ln_g, gmlp_ln_b, gmlp_ws, gmlp_bs, w_branch_b, w_out, norm2_g, router_w, router_b, moe_w_in, moe_b_in, moe_w_out, moe_b_out, final_g):
    depth = ada_w.shape[0]
    assert depth == 1, "the final rms_norm is fused into the combine kernel of the only layer"
    for layer in range(depth):
        mod = _mod_call(c, ada_w[layer], ada_b[layer])
        mod3 = mod.reshape(B, 1, 6 * D)

        u, zuv, ga, gb = _proj_call(x, norm1_g[layer], mod3, w_in[layer].astype(BF16))

        bm, cre, cim, are, aim = _s5_params(ssm_a_re[layer], ssm_a_im[layer], ssm_log_dt[layer],
                                            ssm_b_re[layer], ssm_b_im[layer],
                                            ssm_c_re[layer], ssm_c_im[layer])
        ya = _s5_call(u, bm, cre, cim, are, aim,
                      ssm_d[layer].reshape(1, SSM_W), ssm_glu_w[layer].astype(BF16),
                      ssm_glu_b[layer].reshape(1, SSM_W), w_branch_a[layer].astype(BF16))

        ws = gmlp_ws[layer]
        ws_pairs = jnp.concatenate([ws[0::2], ws[1::2]], axis=-1)
        bias_full = jnp.repeat(gmlp_bs[layer].T, GM_HD, axis=1)
        x1, h2, logits = _mix_call(
            zuv, ga, gb, ya, x, mod3,
            gmlp_ln_g[layer].reshape(1, GM_W), gmlp_ln_b[layer].reshape(1, GM_W),
            ws_pairs, bias_full, w_branch_b[layer].astype(BF16), w_out[layer].astype(BF16),
            norm2_g[layer].reshape(1, D), router_w[layer].astype(BF16),
            router_b[layer].reshape(1, N_E))

        pos_c, pos_t, lists, totals, cnt = _route_call(logits.reshape(T, N_E))
        counts = cnt[0].astype(jnp.int32)
        nblk = (counts + TM - 1) // TM
        blk_end = jnp.cumsum(nblk)
        pad_end = (blk_end * TM).astype(jnp.int32)
        experts = jnp.arange(N_E, dtype=jnp.int32)
        blk_ids = jnp.arange(N_BLOCKS, dtype=jnp.int32)
        blk_e = jnp.sum((blk_end[None, :] <= blk_ids[:, None]).astype(jnp.int32), axis=1)
        blk_e = jnp.minimum(blk_e, N_E - 1)
        blk_first = jnp.concatenate([jnp.ones((1,), jnp.int32),
                                     (blk_e[1:] != blk_e[:-1]).astype(jnp.int32)])
        later = (experts[None, :] > experts[:, None]) & (nblk[None, :] > 0)
        next_e = jnp.min(jnp.where(later, experts[None, :], N_E), axis=1)
        next_e = jnp.where(next_e == N_E, -1, next_e)
        blk_next = jnp.sum(jnp.where(blk_e[:, None] == experts[None, :], next_e[None, :], 0), axis=1)
        n_valid = blk_end[-1:].astype(jnp.int32)
        pad_start = pad_end - nblk * TM
        loc_t = lists[:, 0, :].reshape(NT * LIST_W)
        owner = lists[:, 2, :, None] == experts[None, None, :]
        glob_t = (lists[:, 1, :] + jnp.sum(jnp.where(owner, pad_start, 0), axis=-1)).reshape(NT * LIST_W)
        tot_t = totals[:, 0, :].reshape(NT * 8)

        xs = _dispatch_call(loc_t, glob_t, tot_t, pad_end, h2.reshape(T, D), pos_t)
        ys = _moe_call(blk_e, blk_first, blk_next.astype(jnp.int32), n_valid, xs, moe_w_in[layer],
                       moe_b_in[layer], moe_w_out[layer], moe_b_out[layer])
        x = _combine_call(loc_t, glob_t, tot_t, ys, pos_c, x1.reshape(T, D),
                          mod3, final_g.reshape(1, D)).reshape(B, S, D)
    return x
```

```python
import functools
import math

import jax
import jax.numpy as jnp
from jax import lax
from jax.experimental import pallas as pl
from jax.experimental.pallas import tpu as pltpu

F32 = jnp.float32
BF16 = jnp.bfloat16

D = 1024
B = 8
S = 2048
T = B * S
SSM_W = 512
SSM_G = 32
SSM_H = 16
SSM_P = 64
N_PACK = 4
PACK_G = SSM_G // N_PACK
GM_W = 512
GM_HEADS = 8
GM_HD = 64
CHUNK = 128
N_E = 32
TOP_K = 4
D_E = 1024
LIMIT = 7.0
ALPHA = 1.702
EPS = 1e-6

TS_PROJ = 512
L_SSM = 128
R_SSM = L_SSM * B
TS_MIX = 512
SUB_MIX = 256
TT = 256
NT = T // TT
ROUTE_TILES = 4
RS = TOP_K * TT
TM = 256
BPS = 4
N_ROWS = T * TOP_K + N_E * TM
N_BLOCKS = N_ROWS // TM
LANES = 128
HALF = D // 2
SUB = HALF // LANES
PIECE = 32
COPY_CLASSES = ((PIECE, 0),) + tuple((PIECE >> s, RS // PIECE + N_E * (s - 1))
                                     for s in range(1, PIECE.bit_length()))
LIST_W = 256
assert COPY_CLASSES[-1][1] + N_E <= LIST_W and len(COPY_CLASSES) <= 8
VMEM_LIMIT = 56 * 1024 * 1024
U32 = jnp.uint32


def _sigmoid(v):
    return 0.5 * jnp.tanh(0.5 * v) + 0.5


def _gelu(v):
    c = math.sqrt(2.0 / math.pi)
    inner = v * (c + (c * 0.044715) * (v * v))
    return v * (0.5 + 0.5 * jnp.tanh(inner))


def _rms(v):
    return v * lax.rsqrt(jnp.mean(v * v, axis=-1, keepdims=True) + EPS)


def _mod_kernel(c_ref, w_ref, b_ref, o_ref):
    cv = c_ref[...]
    sv = cv * _sigmoid(cv)
    o_ref[...] = jnp.dot(sv, w_ref[...], preferred_element_type=F32,
                         precision=lax.Precision.HIGHEST) + b_ref[...]


def _mod_call(c, ada_w, ada_b):
    n = ada_w.shape[1]
    return pl.pallas_call(
        _mod_kernel,
        grid=(n // D,),
        in_specs=[pl.BlockSpec((B, D), lambda j: (0, 0)),
                  pl.BlockSpec((D, D), lambda j: (0, j)),
                  pl.BlockSpec((1, D), lambda j: (0, j))],
        out_specs=pl.BlockSpec((B, D), lambda j: (0, j)),
        out_shape=jax.ShapeDtypeStruct((B, n), F32),
        name="adaln_mod",
    )(c, ada_w, ada_b.reshape(1, n))


def _proj_kernel(x_ref, g_ref, shift_ref, scale_ref, w_ref, u_ref, zuv_ref, ga_ref, gb_ref):
    h = _rms(x_ref[0]) * (g_ref[...] * (1.0 + scale_ref[0])) + shift_ref[0]
    hb = h.astype(BF16)
    u_ref[0] = jnp.dot(hb, w_ref[:, 0:SSM_W], preferred_element_type=F32)
    zuv_ref[0] = jnp.dot(hb, w_ref[:, SSM_W:SSM_W + 2 * GM_W], preferred_element_type=F32)
    ga_ref[0] = jnp.dot(hb, w_ref[:, SSM_W + 2 * GM_W:SSM_W + 2 * GM_W + D], preferred_element_type=F32)
    gb_ref[0] = jnp.dot(hb, w_ref[:, SSM_W + 2 * GM_W + D:], preferred_element_type=F32)


def _proj_call(x, norm_g, mod3, w_in_bf):
    pw = w_in_bf.shape[1]
    tok_spec = pl.BlockSpec((1, TS_PROJ, D), lambda b, s: (b, s, 0))
    return pl.pallas_call(
        _proj_kernel,
        grid=(B, S // TS_PROJ),
        in_specs=[tok_spec,
                  pl.BlockSpec((1, D), lambda b, s: (0, 0)),
                  pl.BlockSpec((1, 1, D), lambda b, s: (b, 0, 0)),
                  pl.BlockSpec((1, 1, D), lambda b, s: (b, 0, 1)),
                  pl.BlockSpec((D, pw), lambda b, s: (0, 0))],
        out_specs=[pl.BlockSpec((1, TS_PROJ, SSM_W), lambda b, s: (b, s, 0)),
                   tok_spec, tok_spec, tok_spec],
        out_shape=[jax.ShapeDtypeStruct((B, S, SSM_W), F32),
                   jax.ShapeDtypeStruct((B, S, D), F32),
                   jax.ShapeDtypeStruct((B, S, D), F32),
                   jax.ShapeDtypeStruct((B, S, D), F32)],
        compiler_params=pltpu.CompilerParams(vmem_limit_bytes=VMEM_LIMIT),
        name="norm_proj",
    )(x, norm_g.reshape(1, D), mod3, mod3, w_in_bf)


def _s5_kernel(u_ref, bm_ref, cre_ref, cim_ref, are_ref, aim_ref, d_ref, gw_ref, gb_ref, wa_ref,
               o_ref, usc, ysc, sre, sim, st_re, st_im):
    @pl.when(pl.program_id(0) == 0)
    def _():
        st_re[...] = jnp.zeros_like(st_re)
        st_im[...] = jnp.zeros_like(st_im)

    nslab = SSM_W // LANES
    for b in range(B):
        for c in range(nslab):
            usc[c, pl.ds(b, L_SSM, stride=B), :] = u_ref[b, :, c * LANES:(c + 1) * LANES]
    u = jnp.concatenate([usc[c] for c in range(nslab)], axis=1)
    ub = u.astype(BF16)
    half = PACK_G * SSM_P
    ys = []
    for k in range(N_PACK):
        bu = jnp.dot(ub[:, 128 * k:128 * (k + 1)], bm_ref[k], preferred_element_type=F32)
        sre[k] = bu[:, :half]
        sim[k] = bu[:, half:]
        ar = are_ref[k]
        ai = aim_ref[k]
        r = st_re[k]
        m = st_im[k]
        for t in range(L_SSM):
            rows = pl.ds(t * B, B)
            nr = ar * r - ai * m + sre[k, rows, :]
            m = ar * m + ai * r + sim[k, rows, :]
            r = nr
            sre[k, rows, :] = r
            sim[k, rows, :] = m
        st_re[k] = r
        st_im[k] = m
        yk = jnp.dot(sre[k].astype(BF16), cre_ref[k], preferred_element_type=F32)
        yk = yk + jnp.dot(sim[k].astype(BF16), cim_ref[k], preferred_element_type=F32)
        ys.append(yk)
    for c in range(nslab):
        uc = usc[c]
        ysc[c] = ys[c] + d_ref[:, c * LANES:(c + 1) * LANES] * uc
    y = jnp.concatenate(
        [jnp.concatenate([ysc[c, pl.ds(b, L_SSM, stride=B), :] for c in range(nslab)], axis=1)
         for b in range(B)], axis=0)
    z = _gelu(y)
    gl = jnp.dot(z.astype(BF16), gw_ref[...], preferred_element_type=F32) + gb_ref[...]
    out = z * _sigmoid(gl)
    o = jnp.dot(out.astype(BF16), wa_ref[...], preferred_element_type=F32)
    for b in range(B):
        o_ref[b] = o[b * L_SSM:(b + 1) * L_SSM]


def _s5_call(u, bm, cre, cim, are, aim, d_skip, glu_w, glu_b, w_a):
    half = PACK_G * SSM_P
    full = lambda *shape: pl.BlockSpec(shape, lambda i: (0,) * len(shape))
    return pl.pallas_call(
        _s5_kernel,
        grid=(S // L_SSM,),
        in_specs=[pl.BlockSpec((B, L_SSM, SSM_W), lambda i: (0, i, 0)),
                  full(N_PACK, 128, 2 * half),
                  full(N_PACK, half, 128),
                  full(N_PACK, half, 128),
                  full(N_PACK, B, half),
                  full(N_PACK, B, half),
                  full(1, SSM_W),
                  full(SSM_W, SSM_W),
                  full(1, SSM_W),
                  full(SSM_W, D)],
        out_specs=pl.BlockSpec((B, L_SSM, D), lambda i: (0, i, 0)),
        out_shape=jax.ShapeDtypeStruct((B, S, D), F32),
        scratch_shapes=[pltpu.VMEM((SSM_W // LANES, R_SSM, LANES), F32),
                        pltpu.VMEM((SSM_W // LANES, R_SSM, LANES), F32),
                        pltpu.VMEM((N_PACK, R_SSM, half), F32),
                        pltpu.VMEM((N_PACK, R_SSM, half), F32),
                        pltpu.VMEM((N_PACK, B, half), F32),
                        pltpu.VMEM((N_PACK, B, half), F32)],
        compiler_params=pltpu.CompilerParams(dimension_semantics=("arbitrary",),
                                             vmem_limit_bytes=VMEM_LIMIT),
        name="s5_branch",
    )(u, bm, cre, cim, are, aim, d_skip, glu_w, glu_b, w_a)


def _s5_params(a_re, a_im, log_dt, b_re, b_im, c_re, c_im):
    dt = jnp.exp(log_dt)[:, None]
    mag = jnp.exp(a_re * dt)
    lr = mag * jnp.cos(a_im * dt)
    li = mag * jnp.sin(a_im * dt)
    den = a_re * a_re + a_im * a_im
    cr = ((lr - 1.0) * a_re + li * a_im) / den
    ci = (li * a_re - (lr - 1.0) * a_im) / den
    bbr = cr[..., None] * b_re - ci[..., None] * b_im
    bbi = cr[..., None] * b_im + ci[..., None] * b_re
    eye = jnp.eye(PACK_G, dtype=F32)
    half = PACK_G * SSM_P

    def pack_b(m):
        m4 = m.reshape(N_PACK, PACK_G, SSM_P, SSM_H)
        return jnp.einsum('kgph,gj->kghjp', m4, eye).reshape(N_PACK, PACK_G * SSM_H, half)

    def pack_c(m):
        m4 = m.reshape(N_PACK, PACK_G, SSM_H, SSM_P)
        return jnp.einsum('kghp,gj->kgpjh', m4, eye).reshape(N_PACK, half, PACK_G * SSM_H)

    bm = jnp.concatenate([pack_b(bbr), pack_b(bbi)], axis=-1).astype(BF16)
    cre = pack_c(c_re).astype(BF16)
    cim = (-pack_c(c_im)).astype(BF16)
    are = jnp.broadcast_to(lr.reshape(N_PACK, 1, half), (N_PACK, B, half))
    aim = jnp.broadcast_to(li.reshape(N_PACK, 1, half), (N_PACK, B, half))
    return bm, cre, cim, are, aim


def _route_tile(l, carry):
    lane = lax.broadcasted_iota(jnp.int32, l.shape, 1).astype(F32)
    sels, vals = [], []
    for _ in range(TOP_K):
        m = jnp.max(l, axis=-1, keepdims=True)
        idx = jnp.min(jnp.where(l == m, lane, float(N_E)), axis=-1, keepdims=True)
        sel = lane == idx
        sels.append(sel)
        vals.append(m)
        l = jnp.where(sel, -jnp.inf, l)
    member = sels[0].astype(F32)
    for k in range(1, TOP_K):
        member = member + sels[k].astype(F32)
    tile_cnt = jnp.sum(member, axis=0, keepdims=True)

    r = lax.broadcasted_iota(jnp.int32, (N_E, N_E), 0)
    c = lax.broadcasted_iota(jnp.int32, (N_E, N_E), 1)
    tri = (r < c).astype(BF16)
    tcb = jnp.broadcast_to(tile_cnt, (8, N_E)).astype(BF16)
    seg = jnp.dot(tcb, tri, preferred_element_type=F32)[0:1]
    r = lax.broadcasted_iota(jnp.int32, (TT, TT), 0)
    c = lax.broadcasted_iota(jnp.int32, (TT, TT), 1)
    strict = (c < r).astype(BF16)
    rank = jnp.dot(strict, member.astype(BF16), preferred_element_type=F32)
    posb = seg + rank
    denom = jnp.zeros_like(vals[0])
    exps = []
    for k in range(TOP_K):
        e = jnp.exp(vals[k] - vals[0])
        exps.append(e)
        denom = denom + e
    pc = jnp.zeros(l.shape, F32)
    for k in range(TOP_K):
        pk = jnp.sum(jnp.where(sels[k], posb, 0.0), axis=-1, keepdims=True)
        pc = jnp.where(lane == float(k), pk, pc)
        pc = jnp.where(lane == float(TOP_K + k), exps[k] / denom, pc)
    r8 = lax.broadcasted_iota(jnp.int32, (8, N_E), 0)
    c8 = lax.broadcasted_iota(jnp.int32, (8, N_E), 1)
    eye = (r8 == c8).astype(F32)
    pt = lax.dot_general(eye, pc, (((1,), (1,)), ((), ())),
                         preferred_element_type=F32, precision=lax.Precision.HIGHEST)
    earlier = carry[...]
    carry[...] = earlier + tile_cnt
    lists, totals = _copy_lists(member, tile_cnt, seg, earlier)
    return pc, pt, lists, totals


def _copy_lists(member, tile_cnt, seg, earlier):
    shift = PIECE.bit_length() - 1

    def pieces(n, ci):
        if ci == 0:
            return jnp.right_shift(n, shift)
        return jnp.bitwise_and(jnp.right_shift(n, shift - ci), 1)

    def done(n, ci):
        if ci == 0:
            return jnp.zeros_like(n)
        return n - jnp.bitwise_and(n, (PIECE >> (ci - 1)) - 1)

    re = lax.broadcasted_iota(jnp.int32, (N_E, N_E), 0)
    ce = lax.broadcasted_iota(jnp.int32, (N_E, N_E), 1)
    member_t = lax.dot_general((re == ce).astype(BF16), member.astype(BF16), (((1,), (1,)), ((), ())),
                               preferred_element_type=F32)
    n_col = jnp.sum(member_t, axis=1, keepdims=True).astype(jnp.int32)
    n_row = tile_cnt.astype(jnp.int32)
    lane8 = lax.broadcasted_iota(jnp.int32, (N_E, 8), 1)
    x = jnp.zeros((N_E, 8), F32)
    for ci in range(len(COPY_CLASSES)):
        x = jnp.where(lane8 == ci, pieces(n_col, ci).astype(F32), x)
    xb = x.astype(BF16)
    before = jnp.dot((ce < re).astype(BF16), xb, preferred_element_type=F32)
    totals = jnp.dot(jnp.ones((8, N_E), BF16), xb, preferred_element_type=F32)
    bulk_row = jnp.broadcast_to(pieces(n_row, 0).astype(F32), (8, N_E)).astype(BF16)
    before_bulk_row = jnp.dot(bulk_row, (re < ce).astype(BF16), preferred_element_type=F32)[0:1]

    lanes = lax.broadcasted_iota(jnp.int32, (N_E, LIST_W), 1)
    sub = lax.broadcasted_iota(jnp.int32, (8, N_E), 0)
    e_row = lax.broadcasted_iota(jnp.int32, (8, N_E), 1).astype(F32)
    lists = jnp.zeros((8, LIST_W), F32)
    for ci, (_, lane0) in enumerate(COPY_CLASSES):
        first = before[:, ci:ci + 1].astype(jnp.int32) + lane0
        sel = (lanes >= first) & (lanes < first + pieces(n_col, ci))
        d_row = done(n_row, ci).astype(F32)
        if ci == 0:
            d_row = d_row - PIECE * before_bulk_row
        v = jnp.where(sub == 0, seg + d_row, jnp.where(sub == 1, earlier + d_row,
                                                       jnp.where(sub == 2, e_row, 0.0)))
        lists = lists + jnp.dot(v, sel.astype(F32), preferred_element_type=F32,
                                precision=lax.Precision.HIGHEST)
    q = lax.broadcasted_iota(jnp.int32, (8, LIST_W), 1)
    s8 = lax.broadcasted_iota(jnp.int32, (8, LIST_W), 0)
    lists = lists + jnp.where((s8 < 2) & (q < RS // PIECE), (PIECE * q).astype(F32), 0.0)
    return lists, totals


def _mix_kernel(zuv_ref, ga_ref, gb_ref, ya_ref, x_ref, gate1_ref, shift2_ref, scale2_ref,
                lng_ref, lnb_ref, ws_ref, bias_ref, wbb_ref, wo_ref, n2g_ref, rw_ref, rb_ref,
                x1_ref, h2_ref, lg_ref):
    row = lax.broadcasted_iota(jnp.int32, (CHUNK, 2 * CHUNK), 0)
    col = lax.broadcasted_iota(jnp.int32, (CHUNK, 2 * CHUNK), 1)
    causal = (col % CHUNK) <= row
    lane = lax.broadcasted_iota(jnp.int32, (CHUNK, 2 * GM_HD), 1)
    first = lane < GM_HD
    wpairs = [jnp.where(causal, ws_ref[j], 0.0).astype(BF16) for j in range(GM_HEADS // 2)]

    for g in range(TS_MIX // SUB_MIX):
        rows = pl.ds(g * SUB_MIX, SUB_MIX)
        z = _gelu(zuv_ref[0, rows, :])
        u = z[:, :GM_W]
        v = z[:, GM_W:]
        mu = jnp.mean(v, axis=-1, keepdims=True)
        vc = v - mu
        var = jnp.mean(vc * vc, axis=-1, keepdims=True)
        vn = vc * lax.rsqrt(var + EPS) * lng_ref[...] + lnb_ref[...]
        chunks = []
        for n in range(SUB_MIX // CHUNK):
            cols = []
            for j in range(GM_HEADS // 2):
                vp = vn[n * CHUNK:(n + 1) * CHUNK, 2 * GM_HD * j:2 * GM_HD * (j + 1)]
                rhs = jnp.concatenate([jnp.where(first, vp, 0.0), jnp.where(first, 0.0, vp)], axis=0)
                cols.append(jnp.dot(wpairs[j], rhs.astype(BF16), preferred_element_type=F32))
            chunks.append(jnp.concatenate(cols, axis=1) + bias_ref[...])
        mixed = jnp.concatenate(chunks, axis=0)
        gm = u * mixed
        yb = jnp.dot(gm.astype(BF16), wbb_ref[...], preferred_element_type=F32)
        merged = _sigmoid(ga_ref[0, rows, :]) * ya_ref[0, rows, :] + _sigmoid(gb_ref[0, rows, :]) * yb
        o = jnp.dot(merged.astype(BF16), wo_ref[...], preferred_element_type=F32)
        x1 = x_ref[0, rows, :] + gate1_ref[0] * o
        x1_ref[0, rows, :] = x1
        h2 = _rms(x1) * (n2g_ref[...] * (1.0 + scale2_ref[0])) + shift2_ref[0]
        hb = h2.astype(BF16)
        h2_ref[0, rows, :] = hb
        lg_ref[0, rows, :] = jnp.dot(hb, rw_ref[...], preferred_element_type=F32) + rb_ref[...]


def _route_kernel(lg_ref, pc_ref, pt_ref, lists_ref, totals_ref, cnt_ref, carry):
    @pl.when(pl.program_id(0) == 0)
    def _():
        carry[...] = jnp.zeros_like(carry)

    for t in range(ROUTE_TILES):
        pc, pt, lists, totals = _route_tile(lg_ref[t * TT:(t + 1) * TT, :], carry)
        pc_ref[t * TT:(t + 1) * TT, :] = pc
        pt_ref[:, t * TT:(t + 1) * TT] = pt
        lists_ref[t] = lists.astype(jnp.int32)
        totals_ref[t] = totals.astype(jnp.int32)
    cnt_ref[...] = jnp.broadcast_to(carry[...], cnt_ref.shape)


def _route_call(logits):
    assert NT % ROUTE_TILES == 0
    return pl.pallas_call(
        _route_kernel,
        grid=(NT // ROUTE_TILES,),
        in_specs=[pl.BlockSpec((ROUTE_TILES * TT, N_E), lambda i: (i, 0))],
        out_specs=[pl.BlockSpec((ROUTE_TILES * TT, N_E), lambda i: (i, 0)),
                   pl.BlockSpec((8, ROUTE_TILES * TT), lambda i: (0, i)),
                   pl.BlockSpec((ROUTE_TILES, 8, LIST_W), lambda i: (i, 0, 0)),
                   pl.BlockSpec((ROUTE_TILES, 8, 8), lambda i: (i, 0, 0)),
                   pl.BlockSpec((8, N_E), lambda i: (0, 0))],
        out_shape=[jax.ShapeDtypeStruct((T, N_E), F32),
                   jax.ShapeDtypeStruct((8, T), F32),
                   jax.ShapeDtypeStruct((NT, 8, LIST_W), jnp.int32),
                   jax.ShapeDtypeStruct((NT, 8, 8), jnp.int32),
                   jax.ShapeDtypeStruct((8, N_E), F32)],
        scratch_shapes=[pltpu.VMEM((1, N_E), F32)],
        compiler_params=pltpu.CompilerParams(dimension_semantics=("arbitrary",)),
        name="route",
    )(logits)


def _mix_call(zuv, ga, gb, ya2d, x, mod3, ln_g, ln_b, ws_pairs, bias_full, wbb, wo, n2g, rw, rb):
    tok_spec = pl.BlockSpec((1, TS_MIX, D), lambda b, s: (b, s, 0))
    full = lambda *shape: pl.BlockSpec(shape, lambda b, s: (0,) * len(shape))
    mod_spec = lambda j: pl.BlockSpec((1, 1, D), lambda b, s: (b, 0, j))
    return pl.pallas_call(
        _mix_kernel,
        grid=(B, S // TS_MIX),
        in_specs=[tok_spec, tok_spec, tok_spec, tok_spec, tok_spec,
                  mod_spec(2), mod_spec(3), mod_spec(4),
                  full(1, GM_W), full(1, GM_W),
                  full(GM_HEADS // 2, CHUNK, 2 * CHUNK),
                  full(CHUNK, GM_W),
                  full(GM_W, D), full(D, D), full(1, D),
                  full(D, N_E), full(1, N_E)],
        out_specs=[tok_spec, tok_spec,
                   pl.BlockSpec((1, TS_MIX, N_E), lambda b, s: (b, s, 0))],
        out_shape=[jax.ShapeDtypeStruct((B, S, D), F32),
                   jax.ShapeDtypeStruct((B, S, D), BF16),
                   jax.ShapeDtypeStruct((B, S, N_E), F32)],
        compiler_params=pltpu.CompilerParams(vmem_limit_bytes=VMEM_LIMIT),
        name="gmlp_merge_norm2",
    )(zuv, ga, gb, ya2d, x, mod3, mod3, mod3, ln_g, ln_b, ws_pairs, bias_full, wbb, wo, n2g, rw, rb)


def _pack_rows(v):
    return pltpu.pack_elementwise([v[:, :HALF], v[:, HALF:]], packed_dtype=BF16)


def _unpack_rows(w):
    halves = [pltpu.unpack_elementwise(w, index=i, packed_dtype=BF16, unpacked_dtype=F32)
              for i in range(2)]
    return jnp.concatenate(halves, axis=1)


def _load_grouped(ref, rows, first=0):
    return jnp.concatenate([ref[pl.ds(first * SUB + c, rows, stride=SUB), :] for c in range(SUB)], axis=1)


def _store_grouped(ref, w, rows, first=0):
    for c in range(SUB):
        ref[pl.ds(first * SUB + c, rows, stride=SUB), :] = w[:, c * LANES:(c + 1) * LANES]


def _start_tile_runs(tile, src_ref, dst_ref, src_tbl, dst_tbl, totals_tbl, sem):
    for ci, (rows, lane0) in enumerate(COPY_CLASSES):
        base = tile * LIST_W + lane0

        def start(i, carry, rows=rows, base=base):
            s = pl.multiple_of(src_tbl[base + i] * SUB, SUB)
            d = pl.multiple_of(dst_tbl[base + i] * SUB, SUB)
            pltpu.make_async_copy(src_ref.at[pl.ds(s, rows * SUB)],
                                  dst_ref.at[pl.ds(d, rows * SUB)], sem).start()
            return carry

        lax.fori_loop(0, totals_tbl[tile * 8 + ci], start, 0)


def _wait_tile_runs(src_ref, dst_ref, sem):
    pltpu.make_async_copy(src_ref, dst_ref, sem).wait()


def _dispatch_kernel(loc_ref, glob_ref, tot_ref, pend_ref, h_ref, pt_ref, xs_ref,
                     sbuf0, sbuf1, zbuf, sem_z, sem0, sem1):
    j = pl.program_id(0)

    @pl.when(j == 0)
    def _():
        zbuf[...] = jnp.zeros_like(zbuf)
        for e in range(N_E):
            prev = pend_ref[e - 1] if e > 0 else 0
            end = pend_ref[e]

            @pl.when(end > prev)
            def _():
                first = pl.multiple_of((end - TM) * SUB, TM * SUB)
                cp = pltpu.make_async_copy(zbuf, xs_ref.at[pl.ds(first, TM * SUB)], sem_z)
                cp.start()
                cp.wait()

    rows = lax.broadcasted_iota(jnp.int32, (RS, TT), 0)
    words = []
    for t in range(2):
        pos = pt_ref[:, t * TT:(t + 1) * TT].astype(jnp.int32)
        hit = rows == pos[0:1, :]
        for k in range(1, TOP_K):
            hit = hit | (rows == pos[k:k + 1, :])
        pm = jnp.where(hit, 1.0, 0.0).astype(BF16)
        srt = jnp.dot(pm, h_ref[t * TT:(t + 1) * TT, :], preferred_element_type=F32)
        words.append(_pack_rows(srt))
    head = xs_ref.at[pl.ds(0, RS * SUB)]

    for t, (sbuf, sem) in enumerate(((sbuf0, sem0), (sbuf1, sem1))):
        @pl.when(j >= 1)
        def _(sbuf=sbuf, sem=sem):
            _wait_tile_runs(sbuf, head, sem)
        _store_grouped(sbuf, words[t], RS)
        _start_tile_runs(2 * j + t, sbuf, xs_ref, loc_ref, glob_ref, tot_ref, sem)

    @pl.when(j == NT // 2 - 1)
    def _():
        _wait_tile_runs(sbuf0, head, sem0)
        _wait_tile_runs(sbuf1, head, sem1)


def _dispatch_call(loc_t, glob_t, tot_t, pad_end, h2, pos_t):
    assert NT % 2 == 0
    grid_spec = pltpu.PrefetchScalarGridSpec(
        num_scalar_prefetch=4,
        grid=(NT // 2,),
        in_specs=[pl.BlockSpec((2 * TT, D), lambda j, *_: (j, 0)),
                  pl.BlockSpec((8, 2 * TT), lambda j, *_: (0, j))],
        out_specs=pl.BlockSpec(memory_space=pl.ANY),
        scratch_shapes=[pltpu.VMEM((RS * SUB, LANES), U32),
                        pltpu.VMEM((RS * SUB, LANES), U32),
                        pltpu.VMEM((TM * SUB, LANES), U32),
                        pltpu.SemaphoreType.DMA,
                        pltpu.SemaphoreType.DMA,
                        pltpu.SemaphoreType.DMA],
    )
    return pl.pallas_call(
        _dispatch_kernel,
        grid_spec=grid_spec,
        out_shape=jax.ShapeDtypeStruct((N_ROWS * SUB, LANES), U32),
        compiler_params=pltpu.CompilerParams(dimension_semantics=("arbitrary",),
                                             vmem_limit_bytes=VMEM_LIMIT),
        name="dispatch",
    )(loc_t, glob_t, tot_t, pad_end, h2, pos_t)


def _moe_kernel(be_ref, bf_ref, nx_ref, nv_ref, xs_ref, wi_hbm, bi_ref, wo_hbm, bo_ref, ys_ref,
                wi_f32, wo_f32, wi_bf, wo_bf, sem_i, sem_o):
    step = pl.program_id(0)

    def fetch(e):
        return (pltpu.make_async_copy(wi_hbm.at[e], wi_f32, sem_i),
                pltpu.make_async_copy(wo_hbm.at[e], wo_f32, sem_o))

    @pl.when(step == 0)
    def _():
        for cp in fetch(be_ref[0]):
            cp.start()

    def load_weights(i):
        @pl.when(bf_ref[i] == 1)
        def _():
            for cp in fetch(be_ref[i]):
                cp.wait()
            wi_bf[...] = wi_f32[...].astype(BF16)
            wo_bf[...] = wo_f32[...].astype(BF16)

            @pl.when(nx_ref[i] >= 0)
            def _():
                for cp in fetch(nx_ref[i]):
                    cp.start()

    def ffn(i, first, rows):
        e = be_ref[i]
        xb = _unpack_rows(_load_grouped(xs_ref, rows, first)).astype(BF16)
        gu = jnp.dot(xb, wi_bf[...], preferred_element_type=F32) + bi_ref[pl.ds(e, 1), :]
        gate = jnp.minimum(gu[:, :D_E], LIMIT)
        up = jnp.clip(gu[:, D_E:], -LIMIT, LIMIT)
        act = (up + 1.0) * (gate * _sigmoid(ALPHA * gate))
        y = jnp.dot(act.astype(BF16), wo_bf[...], preferred_element_type=F32) + bo_ref[pl.ds(e, 1), :]
        _store_grouped(ys_ref, _pack_rows(y), rows, first)

    i0 = step * BPS
    last = i0 + BPS - 1
    uniform = (last < nv_ref[0]) & (be_ref[i0] == be_ref[last])

    @pl.when(uniform)
    def _():
        load_weights(i0)
        ffn(i0, 0, BPS * TM)

    @pl.when(jnp.logical_not(uniform))
    def _():
        for sub in range(BPS):
            i = i0 + sub

            @pl.when(i < nv_ref[0])
            def _(i=i, sub=sub):
                load_weights(i)
                ffn(i, sub * TM, TM)


def _moe_call(blk_e, blk_first, blk_next, n_valid, xs, w_in, b_in, w_out, b_out):
    assert N_BLOCKS % BPS == 0

    def row_map(s, be, bf, nx, nv):
        last = (nv[0] + BPS - 1) // BPS - 1
        return (jnp.maximum(jnp.minimum(s, last), 0), 0)

    grid_spec = pltpu.PrefetchScalarGridSpec(
        num_scalar_prefetch=4,
        grid=(N_BLOCKS // BPS,),
        in_specs=[pl.BlockSpec((BPS * TM * SUB, LANES), row_map),
                  pl.BlockSpec(memory_space=pl.ANY),
                  pl.BlockSpec((N_E, 2 * D_E), lambda s, *_: (0, 0)),
                  pl.BlockSpec(memory_space=pl.ANY),
                  pl.BlockSpec((N_E, D), lambda s, *_: (0, 0))],
        out_specs=pl.BlockSpec((BPS * TM * SUB, LANES), row_map),
        scratch_shapes=[pltpu.VMEM((D, 2 * D_E), F32),
                        pltpu.VMEM((D_E, D), F32),
                        pltpu.VMEM((D, 2 * D_E), BF16),
                        pltpu.VMEM((D_E, D), BF16),
                        pltpu.SemaphoreType.DMA,
                        pltpu.SemaphoreType.DMA],
    )
    return pl.pallas_call(
        _moe_kernel,
        grid_spec=grid_spec,
        out_shape=jax.ShapeDtypeStruct((N_ROWS * SUB, LANES), U32),
        compiler_params=pltpu.CompilerParams(dimension_semantics=("arbitrary",),
                                             vmem_limit_bytes=VMEM_LIMIT),
        name="moe_experts",
    )(blk_e, blk_first, blk_next, n_valid, xs, w_in, b_in, w_out, b_out)


def _combine_kernel(loc_ref, glob_ref, tot_ref, ys_ref, pc_ref, x1_ref, gate2_ref, fg_ref, o_ref,
                    buf0, buf1, buf2, buf3, sem0, sem1, sem2, sem3):
    j = pl.program_id(0)
    head = ys_ref.at[pl.ds(0, RS * SUB)]
    even = ((buf0, sem0), (buf1, sem1))
    odd = ((buf2, sem2), (buf3, sem3))

    def fetch(step, slots):
        for t, (buf, sem) in enumerate(slots):
            _start_tile_runs(2 * step + t, ys_ref, buf, glob_ref, loc_ref, tot_ref, sem)

    @pl.when(j == 0)
    def _():
        fetch(0, even)

    col = lax.broadcasted_iota(jnp.int32, (TT, RS), 1)
    wms = []
    for t in range(2):
        pc = pc_ref[t * TT:(t + 1) * TT, :]
        pos = pc.astype(jnp.int32)
        wm = jnp.zeros((TT, RS), F32)
        for k in range(TOP_K):
            wm = jnp.where(col == pos[:, k:k + 1], pc[:, TOP_K + k:TOP_K + k + 1], wm)
        wms.append(wm.astype(BF16))

    def step(cur, nxt):
        @pl.when(j + 1 < NT // 2)
        def _():
            fetch(j + 1, nxt)
        for t, (buf, sem) in enumerate(cur):
            _wait_tile_runs(head, buf, sem)
            yt = _unpack_rows(_load_grouped(buf, RS)).astype(BF16)
            acc = jnp.dot(wms[t], yt, preferred_element_type=F32)
            x2 = x1_ref[t * TT:(t + 1) * TT, :] + gate2_ref[0] * acc
            o_ref[t * TT:(t + 1) * TT, :] = _rms(x2) * fg_ref[...]

    @pl.when(j % 2 == 0)
    def _():
        step(even, odd)

    @pl.when(j % 2 == 1)
    def _():
        step(odd, even)


def _combine_call(loc_t, glob_t, tot_t, ys, pos_c, x1, mod3, final_g):
    per_b = S // (2 * TT)
    grid_spec = pltpu.PrefetchScalarGridSpec(
        num_scalar_prefetch=3,
        grid=(NT // 2,),
        in_specs=[pl.BlockSpec(memory_space=pl.ANY),
                  pl.BlockSpec((2 * TT, N_E), lambda j, *_: (j, 0)),
                  pl.BlockSpec((2 * TT, D), lambda j, *_: (j, 0)),
                  pl.BlockSpec((1, 1, D), lambda j, *_: (j // per_b, 0, 5)),
                  pl.BlockSpec((1, D), lambda j, *_: (0, 0))],
        out_specs=pl.BlockSpec((2 * TT, D), lambda j, *_: (j, 0)),
        scratch_shapes=[pltpu.VMEM((RS * SUB, LANES), U32)] * 4 + [pltpu.SemaphoreType.DMA] * 4,
    )
    return pl.pallas_call(
        _combine_kernel,
        grid_spec=grid_spec,
        out_shape=jax.ShapeDtypeStruct((T, D), F32),
        compiler_params=pltpu.CompilerParams(dimension_semantics=("arbitrary",),
                                             vmem_limit_bytes=VMEM_LIMIT),
        name="combine_norm",
    )(loc_t, glob_t, tot_t, ys, pos_c, x1, mod3, final_g)


def kernel(x, c, ada_w, ada_b, norm1_g, w_in, ssm_a_re, ssm_a_im, ssm_log_dt, ssm_b_re, ssm_b_im, ssm_c_re, ssm_c_im, ssm_d, ssm_glu_w, ssm_glu_b, w_branch_a, gmlp_ln_g, gmlp_ln_b, gmlp_ws, gmlp_bs, w_branch_b, w_out, norm2_g, router_w, router_b, moe_w_in, moe_b_in, moe_w_out, moe_b_out, final_g):
    depth = ada_w.shape[0]
    assert depth == 1, "the final rms_norm is fused into the combine kernel of the only layer"
    for layer in range(depth):
        mod = _mod_call(c, ada_w[layer], ada_b[layer])
        mod3 = mod.reshape(B, 1, 6 * D)

        u, zuv, ga, gb = _proj_call(x, norm1_g[layer], mod3, w_in[layer].astype(BF16))

        bm, cre, cim, are, aim = _s5_params(ssm_a_re[layer], ssm_a_im[layer], ssm_log_dt[layer],
                                            ssm_b_re[layer], ssm_b_im[layer],
                                            ssm_c_re[layer], ssm_c_im[layer])
        ya = _s5_call(u, bm, cre, cim, are, aim,
                      ssm_d[layer].reshape(1, SSM_W), ssm_glu_w[layer].astype(BF16),
                      ssm_glu_b[layer].reshape(1, SSM_W), w_branch_a[layer].astype(BF16))

        ws = gmlp_ws[layer]
        ws_pairs = jnp.concatenate([ws[0::2], ws[1::2]], axis=-1)
        bias_full = jnp.repeat(gmlp_bs[layer].T, GM_HD, axis=1)
        x1, h2, logits = _mix_call(
            zuv, ga, gb, ya, x, mod3,
            gmlp_ln_g[layer].reshape(1, GM_W), gmlp_ln_b[layer].reshape(1, GM_W),
            ws_pairs, bias_full, w_branch_b[layer].astype(BF16), w_out[layer].astype(BF16),
            norm2_g[layer].reshape(1, D), router_w[layer].astype(BF16),
            router_b[layer].reshape(1, N_E))

        pos_c, pos_t, lists, totals, cnt = _route_call(logits.reshape(T, N_E))
        counts = cnt[0].astype(jnp.int32)
        nblk = (counts + TM - 1) // TM
        blk_end = jnp.cumsum(nblk)
        pad_end = (blk_end * TM).astype(jnp.int32)
        experts = jnp.arange(N_E, dtype=jnp.int32)
        blk_ids = jnp.arange(N_BLOCKS, dtype=jnp.int32)
        blk_e = jnp.sum((blk_end[None, :] <= blk_ids[:, None]).astype(jnp.int32), axis=1)
        blk_e = jnp.minimum(blk_e, N_E - 1)
        blk_first = jnp.concatenate([jnp.ones((1,), jnp.int32),
                                     (blk_e[1:] != blk_e[:-1]).astype(jnp.int32)])
        later = (experts[None, :] > experts[:, None]) & (nblk[None, :] > 0)
        next_e = jnp.min(jnp.where(later, experts[None, :], N_E), axis=1)
        next_e = jnp.where(next_e == N_E, -1, next_e)
        blk_next = jnp.sum(jnp.where(blk_e[:, None] == experts[None, :], next_e[None, :], 0), axis=1)
        n_valid = blk_end[-1:].astype(jnp.int32)
        pad_start = pad_end - nblk * TM
        loc_t = lists[:, 0, :].reshape(NT * LIST_W)
        owner = lists[:, 2, :, None] == experts[None, None, :]
        glob_t = (lists[:, 1, :] + jnp.sum(jnp.where(owner, pad_start, 0), axis=-1)).reshape(NT * LIST_W)
        tot_t = totals[:, 0, :].reshape(NT * 8)

        xs = _dispatch_call(loc_t, glob_t, tot_t, pad_end, h2.reshape(T, D), pos_t)
        ys = _moe_call(blk_e, blk_first, blk_next.astype(jnp.int32), n_valid, xs, moe_w_in[layer],
                       moe_b_in[layer], moe_w_out[layer], moe_b_out[layer])
        x = _combine_call(loc_t, glob_t, tot_t, ys, pos_c, x1.reshape(T, D),
                          mod3, final_g.reshape(1, D)).reshape(B, S, D)
    return x
```

```python
import functools
import math

import jax
import jax.numpy as jnp
from jax import lax
from jax.experimental import pallas as pl
from jax.experimental.pallas import tpu as pltpu

F32 = jnp.float32
BF16 = jnp.bfloat16

D = 1024
B = 8
S = 2048
T = B * S
SSM_W = 512
SSM_G = 32
SSM_H = 16
SSM_P = 64
N_PACK = 4
PACK_G = SSM_G // N_PACK
GM_W = 512
GM_HEADS = 8
GM_HD = 64
CHUNK = 128
N_E = 32
TOP_K = 4
D_E = 1024
LIMIT = 7.0
ALPHA = 1.702
EPS = 1e-6

TS_PROJ = 512
L_SSM = 128
R_SSM = L_SSM * B
TS_MIX = 512
SUB_MIX = 256
TT = 256
NT = T // TT
ROUTE_TILES = 4
RS = TOP_K * TT
TM = 256
BPS = 4
N_ROWS = T * TOP_K + N_E * TM
N_BLOCKS = N_ROWS // TM
LANES = 128
HALF = D // 2
SUB = HALF // LANES
PIECE = 32
COPY_CLASSES = ((PIECE, 0),) + tuple((PIECE >> s, RS // PIECE + N_E * (s - 1))
                                     for s in range(1, PIECE.bit_length()))
LIST_W = 256
assert COPY_CLASSES[-1][1] + N_E <= LIST_W and len(COPY_CLASSES) <= 8
VMEM_LIMIT = 56 * 1024 * 1024
U32 = jnp.uint32


def _sigmoid(v):
    return 0.5 * jnp.tanh(0.5 * v) + 0.5


def _gelu(v):
    c = math.sqrt(2.0 / math.pi)
    inner = v * (c + (c * 0.044715) * (v * v))
    return v * (0.5 + 0.5 * jnp.tanh(inner))


def _rms(v):
    return v * lax.rsqrt(jnp.mean(v * v, axis=-1, keepdims=True) + EPS)


def _mod_kernel(c_ref, w_ref, b_ref, o_ref):
    cv = c_ref[...]
    sv = cv * _sigmoid(cv)
    o_ref[...] = jnp.dot(sv, w_ref[...], preferred_element_type=F32,
                         precision=lax.Precision.HIGHEST) + b_ref[...]


def _mod_call(c, ada_w, ada_b):
    n = ada_w.shape[1]
    return pl.pallas_call(
        _mod_kernel,
        grid=(n // D,),
        in_specs=[pl.BlockSpec((B, D), lambda j: (0, 0)),
                  pl.BlockSpec((D, D), lambda j: (0, j)),
                  pl.BlockSpec((1, D), lambda j: (0, j))],
        out_specs=pl.BlockSpec((B, D), lambda j: (0, j)),
        out_shape=jax.ShapeDtypeStruct((B, n), F32),
        name="adaln_mod",
    )(c, ada_w, ada_b.reshape(1, n))


def _proj_kernel(x_ref, g_ref, shift_ref, scale_ref, w_ref, u_ref, zuv_ref, ga_ref, gb_ref):
    h = _rms(x_ref[0]) * (g_ref[...] * (1.0 + scale_ref[0])) + shift_ref[0]
    hb = h.astype(BF16)
    u_ref[0] = jnp.dot(hb, w_ref[:, 0:SSM_W], preferred_element_type=F32)
    zuv_ref[0] = jnp.dot(hb, w_ref[:, SSM_W:SSM_W + 2 * GM_W], preferred_element_type=F32)
    ga_ref[0] = jnp.dot(hb, w_ref[:, SSM_W + 2 * GM_W:SSM_W + 2 * GM_W + D], preferred_element_type=F32)
    gb_ref[0] = jnp.dot(hb, w_ref[:, SSM_W + 2 * GM_W + D:], preferred_element_type=F32)


def _proj_call(x, norm_g, mod3, w_in_bf):
    pw = w_in_bf.shape[1]
    tok_spec = pl.BlockSpec((1, TS_PROJ, D), lambda b, s: (b, s, 0))
    return pl.pallas_call(
        _proj_kernel,
        grid=(B, S // TS_PROJ),
        in_specs=[tok_spec,
                  pl.BlockSpec((1, D), lambda b, s: (0, 0)),
                  pl.BlockSpec((1, 1, D), lambda b, s: (b, 0, 0)),
                  pl.BlockSpec((1, 1, D), lambda b, s: (b, 0, 1)),
                  pl.BlockSpec((D, pw), lambda b, s: (0, 0))],
        out_specs=[pl.BlockSpec((1, TS_PROJ, SSM_W), lambda b, s: (b, s, 0)),
                   tok_spec, tok_spec, tok_spec],
        out_shape=[jax.ShapeDtypeStruct((B, S, SSM_W), F32),
                   jax.ShapeDtypeStruct((B, S, D), F32),
                   jax.ShapeDtypeStruct((B, S, D), F32),
                   jax.ShapeDtypeStruct((B, S, D), F32)],
        compiler_params=pltpu.CompilerParams(vmem_limit_bytes=VMEM_LIMIT),
        name="norm_proj",
    )(x, norm_g.reshape(1, D), mod3, mod3, w_in_bf)


def _s5_kernel(u_ref, bm_ref, cre_ref, cim_ref, are_ref, aim_ref, d_ref, gw_ref, gb_ref, wa_ref,
               o_ref, usc, ysc, sre, sim, st_re, st_im):
    @pl.when(pl.program_id(0) == 0)
    def _():
        st_re[...] = jnp.zeros_like(st_re)
        st_im[...] = jnp.zeros_like(st_im)

    nslab = SSM_W // LANES
    for b in range(B):
        for c in range(nslab):
            usc[c, pl.ds(b, L_SSM, stride=B), :] = u_ref[b, :, c * LANES:(c + 1) * LANES]
    u = jnp.concatenate([usc[c] for c in range(nslab)], axis=1)
    ub = u.astype(BF16)
    half = PACK_G * SSM_P
    ys = []
    for k in range(N_PACK):
        bu = jnp.dot(ub[:, 128 * k:128 * (k + 1)], bm_ref[k], preferred_element_type=F32)
        sre[k] = bu[:, :half]
        sim[k] = bu[:, half:]
        ar = are_ref[k]
        ai = aim_ref[k]
        r = st_re[k]
        m = st_im[k]
        for t in range(L_SSM):
            rows = pl.ds(t * B, B)
            nr = ar * r - ai * m + sre[k, rows, :]
            m = ar * m + ai * r + sim[k, rows, :]
            r = nr
            sre[k, rows, :] = r
            sim[k, rows, :] = m
        st_re[k] = r
        st_im[k] = m
        yk = jnp.dot(sre[k].astype(BF16), cre_ref[k], preferred_element_type=F32)
        yk = yk + jnp.dot(sim[k].astype(BF16), cim_ref[k], preferred_element_type=F32)
        ys.append(yk)
    for c in range(nslab):
        uc = usc[c]
        ysc[c] = ys[c] + d_ref[:, c * LANES:(c + 1) * LANES] * uc
    y = jnp.concatenate(
        [jnp.concatenate([ysc[c, pl.ds(b, L_SSM, stride=B), :] for c in range(nslab)], axis=1)
         for b in range(B)], axis=0)
    z = _gelu(y)
    gl = jnp.dot(z.astype(BF16), gw_ref[...], preferred_element_type=F32) + gb_ref[...]
    out = z * _sigmoid(gl)
    o = jnp.dot(out.astype(BF16), wa_ref[...], preferred_element_type=F32)
    for b in range(B):
        o_ref[b] = o[b * L_SSM:(b + 1) * L_SSM]


def _s5_call(u, bm, cre, cim, are, aim, d_skip, glu_w, glu_b, w_a):
    half = PACK_G * SSM_P
    full = lambda *shape: pl.BlockSpec(shape, lambda i: (0,) * len(shape))
    return pl.pallas_call(
        _s5_kernel,
        grid=(S // L_SSM,),
        in_specs=[pl.BlockSpec((B, L_SSM, SSM_W), lambda i: (0, i, 0)),
                  full(N_PACK, 128, 2 * half),
                  full(N_PACK, half, 128),
                  full(N_PACK, half, 128),
                  full(N_PACK, B, half),
                  full(N_PACK, B, half),
                  full(1, SSM_W),
                  full(SSM_W, SSM_W),
                  full(1, SSM_W),
                  full(SSM_W, D)],
        out_specs=pl.BlockSpec((B, L_SSM, D), lambda i: (0, i, 0)),
        out_shape=jax.ShapeDtypeStruct((B, S, D), F32),
        scratch_shapes=[pltpu.VMEM((SSM_W // LANES, R_SSM, LANES), F32),
                        pltpu.VMEM((SSM_W // LANES, R_SSM, LANES), F32),
                        pltpu.VMEM((N_PACK, R_SSM, half), F32),
                        pltpu.VMEM((N_PACK, R_SSM, half), F32),
                        pltpu.VMEM((N_PACK, B, half), F32),
                        pltpu.VMEM((N_PACK, B, half), F32)],
        compiler_params=pltpu.CompilerParams(dimension_semantics=("arbitrary",),
                                             vmem_limit_bytes=VMEM_LIMIT),
        name="s5_branch",
    )(u, bm, cre, cim, are, aim, d_skip, glu_w, glu_b, w_a)


def _s5_params(a_re, a_im, log_dt, b_re, b_im, c_re, c_im):
    dt = jnp.exp(log_dt)[:, None]
    mag = jnp.exp(a_re * dt)
    lr = mag * jnp.cos(a_im * dt)
    li = mag * jnp.sin(a_im * dt)
    den = a_re * a_re + a_im * a_im
    cr = ((lr - 1.0) * a_re + li * a_im) / den
    ci = (li * a_re - (lr - 1.0) * a_im) / den
    bbr = cr[..., None] * b_re - ci[..., None] * b_im
    bbi = cr[..., None] * b_im + ci[..., None] * b_re
    eye = jnp.eye(PACK_G, dtype=F32)
    half = PACK_G * SSM_P

    def pack_b(m):
        m4 = m.reshape(N_PACK, PACK_G, SSM_P, SSM_H)
        return jnp.einsum('kgph,gj->kghjp', m4, eye).reshape(N_PACK, PACK_G * SSM_H, half)

    def pack_c(m):
        m4 = m.reshape(N_PACK, PACK_G, SSM_H, SSM_P)
        return jnp.einsum('kghp,gj->kgpjh', m4, eye).reshape(N_PACK, half, PACK_G * SSM_H)

    bm = jnp.concatenate([pack_b(bbr), pack_b(bbi)], axis=-1).astype(BF16)
    cre = pack_c(c_re).astype(BF16)
    cim = (-pack_c(c_im)).astype(BF16)
    are = jnp.broadcast_to(lr.reshape(N_PACK, 1, half), (N_PACK, B, half))
    aim = jnp.broadcast_to(li.reshape(N_PACK, 1, half), (N_PACK, B, half))
    return bm, cre, cim, are, aim


def _transpose_exact(a):
    return a.T


def _route_tile(l, carry):
    lt = _transpose_exact(l)
    sub = lax.broadcasted_iota(jnp.int32, lt.shape, 0).astype(F32)
    sels, vals = [], []
    for _ in range(TOP_K):
        m = jnp.max(lt, axis=0, keepdims=True)
        idx = jnp.min(jnp.where(lt == m, sub, float(N_E)), axis=0, keepdims=True)
        sel = sub == idx
        sels.append(sel)
        vals.append(m)
        lt = jnp.where(sel, -jnp.inf, lt)
    member = sels[0].astype(F32)
    for k in range(1, TOP_K):
        member = member + sels[k].astype(F32)
    n_col = jnp.sum(member, axis=1, keepdims=True)

    re = lax.broadcasted_iota(jnp.int32, (N_E, N_E), 0)
    ce = lax.broadcasted_iota(jnp.int32, (N_E, N_E), 1)
    nb = jnp.broadcast_to(n_col, (N_E, 8)).astype(BF16)
    seg_col = jnp.dot((ce < re).astype(BF16), nb, preferred_element_type=F32)[:, 0:1]
    rt = lax.broadcasted_iota(jnp.int32, (TT, TT), 0)
    ct = lax.broadcasted_iota(jnp.int32, (TT, TT), 1)
    rank = jnp.dot(member.astype(BF16), (rt < ct).astype(BF16), preferred_element_type=F32)
    posb = seg_col + rank
    denom = jnp.zeros_like(vals[0])
    exps = []
    for k in range(TOP_K):
        e = jnp.exp(vals[k] - vals[0])
        exps.append(e)
        denom = denom + e
    s8 = lax.broadcasted_iota(jnp.int32, (8, TT), 0)
    pt = jnp.zeros((8, TT), F32)
    for k in range(TOP_K):
        pk = jnp.sum(jnp.where(sels[k], posb, 0.0), axis=0, keepdims=True)
        pt = jnp.where(s8 == k, pk, pt)
        pt = jnp.where(s8 == TOP_K + k, exps[k] / denom, pt)
    pc = _transpose_exact(pt)
    earlier_col = carry[...]
    carry[...] = earlier_col + n_col
    lane8 = lax.broadcasted_iota(jnp.int32, (N_E, 8), 1)
    cols = jnp.where(lane8 == 0, n_col, jnp.where(lane8 == 1, seg_col, jnp.where(lane8 == 2, earlier_col, 0.0)))
    rows = _transpose_exact(cols)
    lists, totals = _copy_lists(n_col.astype(jnp.int32), rows[0:1], rows[1:2], rows[2:3])
    return pc, pt, lists, totals


def _copy_lists(n_col, n_row_f, seg, earlier):
    shift = PIECE.bit_length() - 1

    def pieces(n, ci):
        if ci == 0:
            return jnp.right_shift(n, shift)
        return jnp.bitwise_and(jnp.right_shift(n, shift - ci), 1)

    def done(n, ci):
        if ci == 0:
            return jnp.zeros_like(n)
        return n - jnp.bitwise_and(n, (PIECE >> (ci - 1)) - 1)

    re = lax.broadcasted_iota(jnp.int32, (N_E, N_E), 0)
    ce = lax.broadcasted_iota(jnp.int32, (N_E, N_E), 1)
    n_row = n_row_f.astype(jnp.int32)
    lane8 = lax.broadcasted_iota(jnp.int32, (N_E, 8), 1)
    x = jnp.zeros((N_E, 8), F32)
    for ci in range(len(COPY_CLASSES)):
        x = jnp.where(lane8 == ci, pieces(n_col, ci).astype(F32), x)
    xb = x.astype(BF16)
    before = jnp.dot((ce < re).astype(BF16), xb, preferred_element_type=F32)
    totals = jnp.dot(jnp.ones((8, N_E), BF16), xb, preferred_element_type=F32)
    bulk_row = jnp.broadcast_to(pieces(n_row, 0).astype(F32), (8, N_E)).astype(BF16)
    before_bulk_row = jnp.dot(bulk_row, (re < ce).astype(BF16), preferred_element_type=F32)[0:1]

    lanes = lax.broadcasted_iota(jnp.int32, (N_E, LIST_W), 1)
    sub = lax.broadcasted_iota(jnp.int32, (8, N_E), 0)
    e_row = lax.broadcasted_iota(jnp.int32, (8, N_E), 1).astype(F32)
    lists = jnp.zeros((8, LIST_W), F32)
    for ci, (_, lane0) in enumerate(COPY_CLASSES):
        first = before[:, ci:ci + 1].astype(jnp.int32) + lane0
        sel = (lanes >= first) & (lanes < first + pieces(n_col, ci))
        d_row = done(n_row, ci).astype(F32)
        if ci == 0:
            d_row = d_row - PIECE * before_bulk_row
        v = jnp.where(sub == 0, seg + d_row, jnp.where(sub == 1, earlier + d_row,
                                                       jnp.where(sub == 2, e_row, 0.0)))
        lists = lists + jnp.dot(v, sel.astype(F32), preferred_element_type=F32,
                                precision=lax.Precision.HIGHEST)
    q = lax.broadcasted_iota(jnp.int32, (8, LIST_W), 1)
    s8 = lax.broadcasted_iota(jnp.int32, (8, LIST_W), 0)
    lists = lists + jnp.where((s8 < 2) & (q < RS // PIECE), (PIECE * q).astype(F32), 0.0)
    return lists, totals


def _mix_kernel(zuv_ref, ga_ref, gb_ref, ya_ref, x_ref, gate1_ref, shift2_ref, scale2_ref,
                lng_ref, lnb_ref, ws_ref, bias_ref, wbb_ref, wo_ref, n2g_ref, rw_ref, rb_ref,
                x1_ref, h2_ref, lg_ref):
    row = lax.broadcasted_iota(jnp.int32, (CHUNK, 2 * CHUNK), 0)
    col = lax.broadcasted_iota(jnp.int32, (CHUNK, 2 * CHUNK), 1)
    causal = (col % CHUNK) <= row
    lane = lax.broadcasted_iota(jnp.int32, (CHUNK, 2 * GM_HD), 1)
    first = lane < GM_HD
    wpairs = [jnp.where(causal, ws_ref[j], 0.0).astype(BF16) for j in range(GM_HEADS // 2)]

    for g in range(TS_MIX // SUB_MIX):
        rows = pl.ds(g * SUB_MIX, SUB_MIX)
        z = _gelu(zuv_ref[0, rows, :])
        u = z[:, :GM_W]
        v = z[:, GM_W:]
        mu = jnp.mean(v, axis=-1, keepdims=True)
        vc = v - mu
        var = jnp.mean(vc * vc, axis=-1, keepdims=True)
        vn = vc * lax.rsqrt(var + EPS) * lng_ref[...] + lnb_ref[...]
        chunks = []
        for n in range(SUB_MIX // CHUNK):
            cols = []
            for j in range(GM_HEADS // 2):
                vp = vn[n * CHUNK:(n + 1) * CHUNK, 2 * GM_HD * j:2 * GM_HD * (j + 1)]
                rhs = jnp.concatenate([jnp.where(first, vp, 0.0), jnp.where(first, 0.0, vp)], axis=0)
                cols.append(jnp.dot(wpairs[j], rhs.astype(BF16), preferred_element_type=F32))
            chunks.append(jnp.concatenate(cols, axis=1) + bias_ref[...])
        mixed = jnp.concatenate(chunks, axis=0)
        gm = u * mixed
        yb = jnp.dot(gm.astype(BF16), wbb_ref[...], preferred_element_type=F32)
        merged = _sigmoid(ga_ref[0, rows, :]) * ya_ref[0, rows, :] + _sigmoid(gb_ref[0, rows, :]) * yb
        o = jnp.dot(merged.astype(BF16), wo_ref[...], preferred_element_type=F32)
        x1 = x_ref[0, rows, :] + gate1_ref[0] * o
        x1_ref[0, rows, :] = x1
        h2 = _rms(x1) * (n2g_ref[...] * (1.0 + scale2_ref[0])) + shift2_ref[0]
        hb = h2.astype(BF16)
        h2_ref[0, rows, :] = hb
        lg_ref[0, rows, :] = jnp.dot(hb, rw_ref[...], preferred_element_type=F32) + rb_ref[...]


def _route_kernel(lg_ref, pc_ref, pt_ref, lists_ref, totals_ref, cnt_ref, carry):
    @pl.when(pl.program_id(0) == 0)
    def _():
        carry[...] = jnp.zeros_like(carry)

    for t in range(ROUTE_TILES):
        pc, pt, lists, totals = _route_tile(lg_ref[t * TT:(t + 1) * TT, :], carry)
        pc_ref[t * TT:(t + 1) * TT, :] = pc
        pt_ref[:, t * TT:(t + 1) * TT] = pt
        lists_ref[t] = lists.astype(jnp.int32)
        totals_ref[t] = totals.astype(jnp.int32)
    cnt_ref[...] = jnp.broadcast_to(carry[...], cnt_ref.shape)


def _route_call(logits):
    assert NT % ROUTE_TILES == 0
    return pl.pallas_call(
        _route_kernel,
        grid=(NT // ROUTE_TILES,),
        in_specs=[pl.BlockSpec((ROUTE_TILES * TT, N_E), lambda i: (i, 0))],
        out_specs=[pl.BlockSpec((ROUTE_TILES * TT, 8), lambda i: (i, 0)),
                   pl.BlockSpec((8, ROUTE_TILES * TT), lambda i: (0, i)),
                   pl.BlockSpec((ROUTE_TILES, 8, LIST_W), lambda i: (i, 0, 0)),
                   pl.BlockSpec((ROUTE_TILES, 8, 8), lambda i: (i, 0, 0)),
                   pl.BlockSpec((N_E, 8), lambda i: (0, 0))],
        out_shape=[jax.ShapeDtypeStruct((T, 8), F32),
                   jax.ShapeDtypeStruct((8, T), F32),
                   jax.ShapeDtypeStruct((NT, 8, LIST_W), jnp.int32),
                   jax.ShapeDtypeStruct((NT, 8, 8), jnp.int32),
                   jax.ShapeDtypeStruct((N_E, 8), F32)],
        scratch_shapes=[pltpu.VMEM((N_E, 1), F32)],
        compiler_params=pltpu.CompilerParams(dimension_semantics=("arbitrary",)),
        name="route",
    )(logits)


def _mix_call(zuv, ga, gb, ya2d, x, mod3, ln_g, ln_b, ws_pairs, bias_full, wbb, wo, n2g, rw, rb):
    tok_spec = pl.BlockSpec((1, TS_MIX, D), lambda b, s: (b, s, 0))
    full = lambda *shape: pl.BlockSpec(shape, lambda b, s: (0,) * len(shape))
    mod_spec = lambda j: pl.BlockSpec((1, 1, D), lambda b, s: (b, 0, j))
    return pl.pallas_call(
        _mix_kernel,
        grid=(B, S // TS_MIX),
        in_specs=[tok_spec, tok_spec, tok_spec, tok_spec, tok_spec,
                  mod_spec(2), mod_spec(3), mod_spec(4),
                  full(1, GM_W), full(1, GM_W),
                  full(GM_HEADS // 2, CHUNK, 2 * CHUNK),
                  full(CHUNK, GM_W),
                  full(GM_W, D), full(D, D), full(1, D),
                  full(D, N_E), full(1, N_E)],
        out_specs=[tok_spec, tok_spec,
                   pl.BlockSpec((1, TS_MIX, N_E), lambda b, s: (b, s, 0))],
        out_shape=[jax.ShapeDtypeStruct((B, S, D), F32),
                   jax.ShapeDtypeStruct((B, S, D), BF16),
                   jax.ShapeDtypeStruct((B, S, N_E), F32)],
        compiler_params=pltpu.CompilerParams(vmem_limit_bytes=VMEM_LIMIT),
        name="gmlp_merge_norm2",
    )(zuv, ga, gb, ya2d, x, mod3, mod3, mod3, ln_g, ln_b, ws_pairs, bias_full, wbb, wo, n2g, rw, rb)


def _pack_rows(v):
    return pltpu.pack_elementwise([v[:, :HALF], v[:, HALF:]], packed_dtype=BF16)


def _unpack_rows(w):
    halves = [pltpu.unpack_elementwise(w, index=i, packed_dtype=BF16, unpacked_dtype=F32)
              for i in range(2)]
    return jnp.concatenate(halves, axis=1)


def _load_grouped(ref, rows, first=0):
    return jnp.concatenate([ref[pl.ds(first * SUB + c, rows, stride=SUB), :] for c in range(SUB)], axis=1)


def _store_grouped(ref, w, rows, first=0):
    for c in range(SUB):
        ref[pl.ds(first * SUB + c, rows, stride=SUB), :] = w[:, c * LANES:(c + 1) * LANES]


def _start_tile_runs(tile, src_ref, dst_ref, src_tbl, dst_tbl, totals_tbl, sem):
    for ci, (rows, lane0) in enumerate(COPY_CLASSES):
        base = tile * LIST_W + lane0

        def start(i, carry, rows=rows, base=base):
            s = pl.multiple_of(src_tbl[base + i] * SUB, SUB)
            d = pl.multiple_of(dst_tbl[base + i] * SUB, SUB)
            pltpu.make_async_copy(src_ref.at[pl.ds(s, rows * SUB)],
                                  dst_ref.at[pl.ds(d, rows * SUB)], sem).start()
            return carry

        lax.fori_loop(0, totals_tbl[tile * 8 + ci], start, 0)


def _wait_tile_runs(src_ref, dst_ref, sem):
    pltpu.make_async_copy(src_ref, dst_ref, sem).wait()


def _dispatch_kernel(loc_ref, glob_ref, tot_ref, pend_ref, h_ref, pt_ref, xs_ref,
                     sbuf0, sbuf1, zbuf, sem_z, sem0, sem1):
    j = pl.program_id(0)

    @pl.when(j == 0)
    def _():
        zbuf[...] = jnp.zeros_like(zbuf)
        for e in range(N_E):
            prev = pend_ref[e - 1] if e > 0 else 0
            end = pend_ref[e]

            @pl.when(end > prev)
            def _():
                first = pl.multiple_of((end - TM) * SUB, TM * SUB)
                cp = pltpu.make_async_copy(zbuf, xs_ref.at[pl.ds(first, TM * SUB)], sem_z)
                cp.start()
                cp.wait()

    rows = lax.broadcasted_iota(jnp.int32, (RS, TT), 0)
    words = []
    for t in range(2):
        pos = pt_ref[:, t * TT:(t + 1) * TT].astype(jnp.int32)
        hit = rows == pos[0:1, :]
        for k in range(1, TOP_K):
            hit = hit | (rows == pos[k:k + 1, :])
        pm = jnp.where(hit, 1.0, 0.0).astype(BF16)
        srt = jnp.dot(pm, h_ref[t * TT:(t + 1) * TT, :], preferred_element_type=F32)
        words.append(_pack_rows(srt))
    head = xs_ref.at[pl.ds(0, RS * SUB)]

    for t, (sbuf, sem) in enumerate(((sbuf0, sem0), (sbuf1, sem1))):
        @pl.when(j >= 1)
        def _(sbuf=sbuf, sem=sem):
            _wait_tile_runs(sbuf, head, sem)
        _store_grouped(sbuf, words[t], RS)
        _start_tile_runs(2 * j + t, sbuf, xs_ref, loc_ref, glob_ref, tot_ref, sem)

    @pl.when(j == NT // 2 - 1)
    def _():
        _wait_tile_runs(sbuf0, head, sem0)
        _wait_tile_runs(sbuf1, head, sem1)


def _dispatch_call(loc_t, glob_t, tot_t, pad_end, h2, pos_t):
    assert NT % 2 == 0
    grid_spec = pltpu.PrefetchScalarGridSpec(
        num_scalar_prefetch=4,
        grid=(NT // 2,),
        in_specs=[pl.BlockSpec((2 * TT, D), lambda j, *_: (j, 0)),
                  pl.BlockSpec((8, 2 * TT), lambda j, *_: (0, j))],
        out_specs=pl.BlockSpec(memory_space=pl.ANY),
        scratch_shapes=[pltpu.VMEM((RS * SUB, LANES), U32),
                        pltpu.VMEM((RS * SUB, LANES), U32),
                        pltpu.VMEM((TM * SUB, LANES), U32),
                        pltpu.SemaphoreType.DMA,
                        pltpu.SemaphoreType.DMA,
                        pltpu.SemaphoreType.DMA],
    )
    return pl.pallas_call(
        _dispatch_kernel,
        grid_spec=grid_spec,
        out_shape=jax.ShapeDtypeStruct((N_ROWS * SUB, LANES), U32),
        compiler_params=pltpu.CompilerParams(dimension_semantics=("arbitrary",),
                                             vmem_limit_bytes=VMEM_LIMIT),
        name="dispatch",
    )(loc_t, glob_t, tot_t, pad_end, h2, pos_t)


def _moe_kernel(be_ref, bf_ref, nx_ref, nv_ref, xs_ref, wi_hbm, bi_ref, wo_hbm, bo_ref, ys_ref,
                wi_f32, wo_f32, wi_bf, wo_bf, sem_i, sem_o):
    step = pl.program_id(0)

    def fetch(e):
        return (pltpu.make_async_copy(wi_hbm.at[e], wi_f32, sem_i),
                pltpu.make_async_copy(wo_hbm.at[e], wo_f32, sem_o))

    @pl.when(step == 0)
    def _():
        for cp in fetch(be_ref[0]):
            cp.start()

    def load_weights(i):
        @pl.when(bf_ref[i] == 1)
        def _():
            for cp in fetch(be_ref[i]):
                cp.wait()
            wi_bf[...] = wi_f32[...].astype(BF16)
            wo_bf[...] = wo_f32[...].astype(BF16)

            @pl.when(nx_ref[i] >= 0)
            def _():
                for cp in fetch(nx_ref[i]):
                    cp.start()

    def ffn(i, first, rows):
        e = be_ref[i]
        xb = _unpack_rows(_load_grouped(xs_ref, rows, first)).astype(BF16)
        gu = jnp.dot(xb, wi_bf[...], preferred_element_type=F32) + bi_ref[pl.ds(e, 1), :]
        gate = jnp.minimum(gu[:, :D_E], LIMIT)
        up = jnp.clip(gu[:, D_E:], -LIMIT, LIMIT)
        act = (up + 1.0) * (gate * _sigmoid(ALPHA * gate))
        y = jnp.dot(act.astype(BF16), wo_bf[...], preferred_element_type=F32) + bo_ref[pl.ds(e, 1), :]
        _store_grouped(ys_ref, _pack_rows(y), rows, first)

    i0 = step * BPS
    last = i0 + BPS - 1
    uniform = (last < nv_ref[0]) & (be_ref[i0] == be_ref[last])

    @pl.when(uniform)
    def _():
        load_weights(i0)
        ffn(i0, 0, BPS * TM)

    @pl.when(jnp.logical_not(uniform))
    def _():
        for sub in range(BPS):
            i = i0 + sub

            @pl.when(i < nv_ref[0])
            def _(i=i, sub=sub):
                load_weights(i)
                ffn(i, sub * TM, TM)


def _moe_call(blk_e, blk_first, blk_next, n_valid, xs, w_in, b_in, w_out, b_out):
    assert N_BLOCKS % BPS == 0

    def row_map(s, be, bf, nx, nv):
        last = (nv[0] + BPS - 1) // BPS - 1
        return (jnp.maximum(jnp.minimum(s, last), 0), 0)

    grid_spec = pltpu.PrefetchScalarGridSpec(
        num_scalar_prefetch=4,
        grid=(N_BLOCKS // BPS,),
        in_specs=[pl.BlockSpec((BPS * TM * SUB, LANES), row_map),
                  pl.BlockSpec(memory_space=pl.ANY),
                  pl.BlockSpec((N_E, 2 * D_E), lambda s, *_: (0, 0)),
                  pl.BlockSpec(memory_space=pl.ANY),
                  pl.BlockSpec((N_E, D), lambda s, *_: (0, 0))],
        out_specs=pl.BlockSpec((BPS * TM * SUB, LANES), row_map),
        scratch_shapes=[pltpu.VMEM((D, 2 * D_E), F32),
                        pltpu.VMEM((D_E, D), F32),
                        pltpu.VMEM((D, 2 * D_E), BF16),
                        pltpu.VMEM((D_E, D), BF16),
                        pltpu.SemaphoreType.DMA,
                        pltpu.SemaphoreType.DMA],
    )
    return pl.pallas_call(
        _moe_kernel,
        grid_spec=grid_spec,
        out_shape=jax.ShapeDtypeStruct((N_ROWS * SUB, LANES), U32),
        compiler_params=pltpu.CompilerParams(dimension_semantics=("arbitrary",),
                                             vmem_limit_bytes=VMEM_LIMIT),
        name="moe_experts",
    )(blk_e, blk_first, blk_next, n_valid, xs, w_in, b_in, w_out, b_out)


def _combine_kernel(loc_ref, glob_ref, tot_ref, ys_ref, pc_ref, x1_ref, gate2_ref, fg_ref, o_ref,
                    buf0, buf1, buf2, buf3, sem0, sem1, sem2, sem3):
    j = pl.program_id(0)
    head = ys_ref.at[pl.ds(0, RS * SUB)]
    even = ((buf0, sem0), (buf1, sem1))
    odd = ((buf2, sem2), (buf3, sem3))

    def fetch(step, slots):
        for t, (buf, sem) in enumerate(slots):
            _start_tile_runs(2 * step + t, ys_ref, buf, glob_ref, loc_ref, tot_ref, sem)

    @pl.when(j == 0)
    def _():
        fetch(0, even)

    col = lax.broadcasted_iota(jnp.int32, (TT, RS), 1)
    wms = []
    for t in range(2):
        pc = pc_ref[t * TT:(t + 1) * TT, :]
        pos = pc.astype(jnp.int32)
        wm = jnp.zeros((TT, RS), F32)
        for k in range(TOP_K):
            wm = jnp.where(col == pos[:, k:k + 1], pc[:, TOP_K + k:TOP_K + k + 1], wm)
        wms.append(wm.astype(BF16))

    def step(cur, nxt):
        @pl.when(j + 1 < NT // 2)
        def _():
            fetch(j + 1, nxt)
        for t, (buf, sem) in enumerate(cur):
            _wait_tile_runs(head, buf, sem)
            yt = _unpack_rows(_load_grouped(buf, RS)).astype(BF16)
            acc = jnp.dot(wms[t], yt, preferred_element_type=F32)
            x2 = x1_ref[t * TT:(t + 1) * TT, :] + gate2_ref[0] * acc
            o_ref[t * TT:(t + 1) * TT, :] = _rms(x2) * fg_ref[...]

    @pl.when(j % 2 == 0)
    def _():
        step(even, odd)

    @pl.when(j % 2 == 1)
    def _():
        step(odd, even)


def _combine_call(loc_t, glob_t, tot_t, ys, pos_c, x1, mod3, final_g):
    per_b = S // (2 * TT)
    grid_spec = pltpu.PrefetchScalarGridSpec(
        num_scalar_prefetch=3,
        grid=(NT // 2,),
        in_specs=[pl.BlockSpec(memory_space=pl.ANY),
                  pl.BlockSpec((2 * TT, 8), lambda j, *_: (j, 0)),
                  pl.BlockSpec((2 * TT, D), lambda j, *_: (j, 0)),
                  pl.BlockSpec((1, 1, D), lambda j, *_: (j // per_b, 0, 5)),
                  pl.BlockSpec((1, D), lambda j, *_: (0, 0))],
        out_specs=pl.BlockSpec((2 * TT, D), lambda j, *_: (j, 0)),
        scratch_shapes=[pltpu.VMEM((RS * SUB, LANES), U32)] * 4 + [pltpu.SemaphoreType.DMA] * 4,
    )
    return pl.pallas_call(
        _combine_kernel,
        grid_spec=grid_spec,
        out_shape=jax.ShapeDtypeStruct((T, D), F32),
        compiler_params=pltpu.CompilerParams(dimension_semantics=("arbitrary",),
                                             vmem_limit_bytes=VMEM_LIMIT),
        name="combine_norm",
    )(loc_t, glob_t, tot_t, ys, pos_c, x1, mod3, final_g)


def kernel(x, c, ada_w, ada_b, norm1_g, w_in, ssm_a_re, ssm_a_im, ssm_log_dt, ssm_b_re, ssm_b_im, ssm_c_re, ssm_c_im, ssm_d, ssm_glu_w, ssm_glu_b, w_branch_a, gmlp_ln_g, gmlp_ln_b, gmlp_ws, gmlp_bs, w_branch_b, w_out, norm2_g, router_w, router_b, moe_w_in, moe_b_in, moe_w_out, moe_b_out, final_g):
    depth = ada_w.shape[0]
    assert depth == 1, "the final rms_norm is fused into the combine kernel of the only layer"
    for layer in range(depth):
        mod = _mod_call(c, ada_w[layer], ada_b[layer])
        mod3 = mod.reshape(B, 1, 6 * D)

        u, zuv, ga, gb = _proj_call(x, norm1_g[layer], mod3, w_in[layer].astype(BF16))

        bm, cre, cim, are, aim = _s5_params(ssm_a_re[layer], ssm_a_im[layer], ssm_log_dt[layer],
                                            ssm_b_re[layer], ssm_b_im[layer],
                                            ssm_c_re[layer], ssm_c_im[layer])
        ya = _s5_call(u, bm, cre, cim, are, aim,
                      ssm_d[layer].reshape(1, SSM_W), ssm_glu_w[layer].astype(BF16),
                      ssm_glu_b[layer].reshape(1, SSM_W), w_branch_a[layer].astype(BF16))

        ws = gmlp_ws[layer]
        ws_pairs = jnp.concatenate([ws[0::2], ws[1::2]], axis=-1)
        bias_full = jnp.repeat(gmlp_bs[layer].T, GM_HD, axis=1)
        x1, h2, logits = _mix_call(
            zuv, ga, gb, ya, x, mod3,
            gmlp_ln_g[layer].reshape(1, GM_W), gmlp_ln_b[layer].reshape(1, GM_W),
            ws_pairs, bias_full, w_branch_b[layer].astype(BF16), w_out[layer].astype(BF16),
            norm2_g[layer].reshape(1, D), router_w[layer].astype(BF16),
            router_b[layer].reshape(1, N_E))

        pos_c, pos_t, lists, totals, cnt = _route_call(logits.reshape(T, N_E))
        counts = cnt[:, 0].astype(jnp.int32)
        nblk = (counts + TM - 1) // TM
        blk_end = jnp.cumsum(nblk)
        pad_end = (blk_end * TM).astype(jnp.int32)
        experts = jnp.arange(N_E, dtype=jnp.int32)
        blk_ids = jnp.arange(N_BLOCKS, dtype=jnp.int32)
        blk_e = jnp.sum((blk_end[None, :] <= blk_ids[:, None]).astype(jnp.int32), axis=1)
        blk_e = jnp.minimum(blk_e, N_E - 1)
        blk_first = jnp.concatenate([jnp.ones((1,), jnp.int32),
                                     (blk_e[1:] != blk_e[:-1]).astype(jnp.int32)])
        later = (experts[None, :] > experts[:, None]) & (nblk[None, :] > 0)
        next_e = jnp.min(jnp.where(later, experts[None, :], N_E), axis=1)
        next_e = jnp.where(next_e == N_E, -1, next_e)
        blk_next = jnp.sum(jnp.where(blk_e[:, None] == experts[None, :], next_e[None, :], 0), axis=1)
        n_valid = blk_end[-1:].astype(jnp.int32)
        pad_start = pad_end - nblk * TM
        loc_t = lists[:, 0, :].reshape(NT * LIST_W)
        owner = lists[:, 2, :, None] == experts[None, None, :]
        glob_t = (lists[:, 1, :] + jnp.sum(jnp.where(owner, pad_start, 0), axis=-1)).reshape(NT * LIST_W)
        tot_t = totals[:, 0, :].reshape(NT * 8)

        xs = _dispatch_call(loc_t, glob_t, tot_t, pad_end, h2.reshape(T, D), pos_t)
        ys = _moe_call(blk_e, blk_first, blk_next.astype(jnp.int32), n_valid, xs, moe_w_in[layer],
                       moe_b_in[layer], moe_w_out[layer], moe_b_out[layer])
        x = _combine_call(loc_t, glob_t, tot_t, ys, pos_c, x1.reshape(T, D),
                          mod3, final_g.reshape(1, D)).reshape(B, S, D)
    return x
```

```python
import functools
import math

import jax
import jax.numpy as jnp
from jax import lax
from jax.experimental import pallas as pl
from jax.experimental.pallas import tpu as pltpu

F32 = jnp.float32
BF16 = jnp.bfloat16

D = 1024
B = 8
S = 2048
T = B * S
SSM_W = 512
SSM_G = 32
SSM_H = 16
SSM_P = 64
N_PACK = 4
PACK_G = SSM_G // N_PACK
GM_W = 512
GM_HEADS = 8
GM_HD = 64
CHUNK = 128
N_E = 32
TOP_K = 4
D_E = 1024
LIMIT = 7.0
ALPHA = 1.702
EPS = 1e-6

TS_PROJ = 512
L_SSM = 128
R_SSM = L_SSM * B
TS_MIX = 512
SUB_MIX = 256
TT = 256
NT = T // TT
ROUTE_TILES = 4
RS = TOP_K * TT
TM = 256
BPS = 4
N_ROWS = T * TOP_K + N_E * TM
N_BLOCKS = N_ROWS // TM
LANES = 128
HALF = D // 2
SUB = HALF // LANES
PIECE = 32
COPY_CLASSES = ((PIECE, 0),) + tuple((PIECE >> s, RS // PIECE + N_E * (s - 1))
                                     for s in range(1, PIECE.bit_length()))
LIST_W = 256
assert COPY_CLASSES[-1][1] + N_E <= LIST_W and len(COPY_CLASSES) <= 8
VMEM_LIMIT = 56 * 1024 * 1024
U32 = jnp.uint32


def _sigmoid(v):
    return 0.5 * jnp.tanh(0.5 * v) + 0.5


def _gelu(v):
    c = math.sqrt(2.0 / math.pi)
    inner = v * (c + (c * 0.044715) * (v * v))
    return v * (0.5 + 0.5 * jnp.tanh(inner))


def _rms(v):
    return v * lax.rsqrt(jnp.mean(v * v, axis=-1, keepdims=True) + EPS)


def _mod_kernel(c_ref, w_ref, b_ref, o_ref):
    cv = c_ref[...]
    sv = cv * _sigmoid(cv)
    o_ref[...] = jnp.dot(sv, w_ref[...], preferred_element_type=F32,
                         precision=lax.Precision.HIGHEST) + b_ref[...]


def _mod_call(c, ada_w, ada_b):
    n = ada_w.shape[1]
    return pl.pallas_call(
        _mod_kernel,
        grid=(n // D,),
        in_specs=[pl.BlockSpec((B, D), lambda j: (0, 0)),
                  pl.BlockSpec((D, D), lambda j: (0, j)),
                  pl.BlockSpec((1, D), lambda j: (0, j))],
        out_specs=pl.BlockSpec((B, D), lambda j: (0, j)),
        out_shape=jax.ShapeDtypeStruct((B, n), F32),
        name="adaln_mod",
    )(c, ada_w, ada_b.reshape(1, n))


def _proj_kernel(x_ref, g_ref, shift_ref, scale_ref, w_ref, u_ref, zg_ref):
    h = _rms(x_ref[0]) * (g_ref[...] * (1.0 + scale_ref[0])) + shift_ref[0]
    hb = h.astype(BF16)
    u_ref[0] = jnp.dot(hb, w_ref[:, 0:SSM_W], preferred_element_type=F32)
    zg_ref[0] = jnp.dot(hb, w_ref[:, SSM_W:], preferred_element_type=F32)


def _proj_call(x, norm_g, mod3, w_in_bf):
    pw = w_in_bf.shape[1]
    tok_spec = pl.BlockSpec((1, TS_PROJ, D), lambda b, s: (b, s, 0))
    return pl.pallas_call(
        _proj_kernel,
        grid=(B, S // TS_PROJ),
        in_specs=[tok_spec,
                  pl.BlockSpec((1, D), lambda b, s: (0, 0)),
                  pl.BlockSpec((1, 1, D), lambda b, s: (b, 0, 0)),
                  pl.BlockSpec((1, 1, D), lambda b, s: (b, 0, 1)),
                  pl.BlockSpec((D, pw), lambda b, s: (0, 0))],
        out_specs=[pl.BlockSpec((1, TS_PROJ, SSM_W), lambda b, s: (b, s, 0)),
                   pl.BlockSpec((1, TS_PROJ, pw - SSM_W), lambda b, s: (b, s, 0))],
        out_shape=[jax.ShapeDtypeStruct((B, S, SSM_W), F32),
                   jax.ShapeDtypeStruct((B, S, pw - SSM_W), F32)],
        compiler_params=pltpu.CompilerParams(vmem_limit_bytes=VMEM_LIMIT),
        name="norm_proj",
    )(x, norm_g.reshape(1, D), mod3, mod3, w_in_bf)


def _s5_kernel(u_ref, bm_ref, cre_ref, cim_ref, are_ref, aim_ref, d_ref, gw_ref, gb_ref, wa_ref,
               o_ref, usc, ysc, sre, sim, st_re, st_im):
    @pl.when(pl.program_id(0) == 0)
    def _():
        st_re[...] = jnp.zeros_like(st_re)
        st_im[...] = jnp.zeros_like(st_im)

    nslab = SSM_W // LANES
    for b in range(B):
        for c in range(nslab):
            usc[c, pl.ds(b, L_SSM, stride=B), :] = u_ref[b, :, c * LANES:(c + 1) * LANES]
    u = jnp.concatenate([usc[c] for c in range(nslab)], axis=1)
    ub = u.astype(BF16)
    half = PACK_G * SSM_P
    ys = []
    for k in range(N_PACK):
        bu = jnp.dot(ub[:, 128 * k:128 * (k + 1)], bm_ref[k], preferred_element_type=F32)
        sre[k] = bu[:, :half]
        sim[k] = bu[:, half:]
        ar = are_ref[k]
        ai = aim_ref[k]
        r = st_re[k]
        m = st_im[k]
        for t in range(L_SSM):
            rows = pl.ds(t * B, B)
            nr = ar * r - ai * m + sre[k, rows, :]
            m = ar * m + ai * r + sim[k, rows, :]
            r = nr
            sre[k, rows, :] = r
            sim[k, rows, :] = m
        st_re[k] = r
        st_im[k] = m
        yk = jnp.dot(sre[k].astype(BF16), cre_ref[k], preferred_element_type=F32)
        yk = yk + jnp.dot(sim[k].astype(BF16), cim_ref[k], preferred_element_type=F32)
        ys.append(yk)
    for c in range(nslab):
        uc = usc[c]
        ysc[c] = ys[c] + d_ref[:, c * LANES:(c + 1) * LANES] * uc
    y = jnp.concatenate(
        [jnp.concatenate([ysc[c, pl.ds(b, L_SSM, stride=B), :] for c in range(nslab)], axis=1)
         for b in range(B)], axis=0)
    z = _gelu(y)
    gl = jnp.dot(z.astype(BF16), gw_ref[...], preferred_element_type=F32) + gb_ref[...]
    out = z * _sigmoid(gl)
    o = jnp.dot(out.astype(BF16), wa_ref[...], preferred_element_type=F32)
    for b in range(B):
        o_ref[b] = o[b * L_SSM:(b + 1) * L_SSM]


def _s5_call(u, bm, cre, cim, are, aim, d_skip, glu_w, glu_b, w_a):
    half = PACK_G * SSM_P
    full = lambda *shape: pl.BlockSpec(shape, lambda i: (0,) * len(shape))
    return pl.pallas_call(
        _s5_kernel,
        grid=(S // L_SSM,),
        in_specs=[pl.BlockSpec((B, L_SSM, SSM_W), lambda i: (0, i, 0)),
                  full(N_PACK, 128, 2 * half),
                  full(N_PACK, half, 128),
                  full(N_PACK, half, 128),
                  full(N_PACK, B, half),
                  full(N_PACK, B, half),
                  full(1, SSM_W),
                  full(SSM_W, SSM_W),
                  full(1, SSM_W),
                  full(SSM_W, D)],
        out_specs=pl.BlockSpec((B, L_SSM, D), lambda i: (0, i, 0)),
        out_shape=jax.ShapeDtypeStruct((B, S, D), F32),
        scratch_shapes=[pltpu.VMEM((SSM_W // LANES, R_SSM, LANES), F32),
                        pltpu.VMEM((SSM_W // LANES, R_SSM, LANES), F32),
                        pltpu.VMEM((N_PACK, R_SSM, half), F32),
                        pltpu.VMEM((N_PACK, R_SSM, half), F32),
                        pltpu.VMEM((N_PACK, B, half), F32),
                        pltpu.VMEM((N_PACK, B, half), F32)],
        compiler_params=pltpu.CompilerParams(dimension_semantics=("arbitrary",),
                                             vmem_limit_bytes=VMEM_LIMIT),
        name="s5_branch",
    )(u, bm, cre, cim, are, aim, d_skip, glu_w, glu_b, w_a)


def _s5_params(a_re, a_im, log_dt, b_re, b_im, c_re, c_im):
    dt = jnp.exp(log_dt)[:, None]
    mag = jnp.exp(a_re * dt)
    lr = mag * jnp.cos(a_im * dt)
    li = mag * jnp.sin(a_im * dt)
    den = a_re * a_re + a_im * a_im
    cr = ((lr - 1.0) * a_re + li * a_im) / den
    ci = (li * a_re - (lr - 1.0) * a_im) / den
    bbr = cr[..., None] * b_re - ci[..., None] * b_im
    bbi = cr[..., None] * b_im + ci[..., None] * b_re
    eye = jnp.eye(PACK_G, dtype=F32)
    half = PACK_G * SSM_P

    def pack_b(m):
        m4 = m.reshape(N_PACK, PACK_G, SSM_P, SSM_H)
        return jnp.einsum('kgph,gj->kghjp', m4, eye).reshape(N_PACK, PACK_G * SSM_H, half)

    def pack_c(m):
        m4 = m.reshape(N_PACK, PACK_G, SSM_H, SSM_P)
        return jnp.einsum('kghp,gj->kgpjh', m4, eye).reshape(N_PACK, half, PACK_G * SSM_H)

    bm = jnp.concatenate([pack_b(bbr), pack_b(bbi)], axis=-1).astype(BF16)
    cre = pack_c(c_re).astype(BF16)
    cim = (-pack_c(c_im)).astype(BF16)
    are = jnp.broadcast_to(lr.reshape(N_PACK, 1, half), (N_PACK, B, half))
    aim = jnp.broadcast_to(li.reshape(N_PACK, 1, half), (N_PACK, B, half))
    return bm, cre, cim, are, aim


def _transpose_exact(a):
    return a.T


def _route_tile(lt, carry):
    sub = lax.broadcasted_iota(jnp.int32, lt.shape, 0).astype(F32)
    sels, vals = [], []
    for _ in range(TOP_K):
        m = jnp.max(lt, axis=0, keepdims=True)
        idx = jnp.min(jnp.where(lt == m, sub, float(N_E)), axis=0, keepdims=True)
        sel = sub == idx
        sels.append(sel)
        vals.append(m)
        lt = jnp.where(sel, -jnp.inf, lt)
    member = sels[0].astype(F32)
    for k in range(1, TOP_K):
        member = member + sels[k].astype(F32)
    n_col = jnp.sum(member, axis=1, keepdims=True)

    re = lax.broadcasted_iota(jnp.int32, (N_E, N_E), 0)
    ce = lax.broadcasted_iota(jnp.int32, (N_E, N_E), 1)
    nb = jnp.broadcast_to(n_col, (N_E, 8)).astype(BF16)
    seg_col = jnp.dot((ce < re).astype(BF16), nb, preferred_element_type=F32)[:, 0:1]
    rt = lax.broadcasted_iota(jnp.int32, (TT, TT), 0)
    ct = lax.broadcasted_iota(jnp.int32, (TT, TT), 1)
    rank = jnp.dot(member.astype(BF16), (rt < ct).astype(BF16), preferred_element_type=F32)
    posb = seg_col + rank
    denom = jnp.zeros_like(vals[0])
    exps = []
    for k in range(TOP_K):
        e = jnp.exp(vals[k] - vals[0])
        exps.append(e)
        denom = denom + e
    s8 = lax.broadcasted_iota(jnp.int32, (8, TT), 0)
    pt = jnp.zeros((8, TT), F32)
    for k in range(TOP_K):
        pk = jnp.sum(jnp.where(sels[k], posb, 0.0), axis=0, keepdims=True)
        pt = jnp.where(s8 == k, pk, pt)
        pt = jnp.where(s8 == TOP_K + k, exps[k] / denom, pt)
    pc = _transpose_exact(pt)
    earlier_col = carry[...]
    carry[...] = earlier_col + n_col
    lane8 = lax.broadcasted_iota(jnp.int32, (N_E, 8), 1)
    cols = jnp.where(lane8 == 0, n_col, jnp.where(lane8 == 1, seg_col, jnp.where(lane8 == 2, earlier_col, 0.0)))
    rows = _transpose_exact(cols)
    lists, totals = _copy_lists(n_col.astype(jnp.int32), rows[0:1], rows[1:2], rows[2:3])
    return pc, pt, lists, totals


def _copy_lists(n_col, n_row_f, seg, earlier):
    shift = PIECE.bit_length() - 1

    def pieces(n, ci):
        if ci == 0:
            return jnp.right_shift(n, shift)
        return jnp.bitwise_and(jnp.right_shift(n, shift - ci), 1)

    def done(n, ci):
        if ci == 0:
            return jnp.zeros_like(n)
        return n - jnp.bitwise_and(n, (PIECE >> (ci - 1)) - 1)

    re = lax.broadcasted_iota(jnp.int32, (N_E, N_E), 0)
    ce = lax.broadcasted_iota(jnp.int32, (N_E, N_E), 1)
    n_row = n_row_f.astype(jnp.int32)
    lane8 = lax.broadcasted_iota(jnp.int32, (N_E, 8), 1)
    x = jnp.zeros((N_E, 8), F32)
    for ci in range(len(COPY_CLASSES)):
        x = jnp.where(lane8 == ci, pieces(n_col, ci).astype(F32), x)
    xb = x.astype(BF16)
    before = jnp.dot((ce < re).astype(BF16), xb, preferred_element_type=F32)
    totals = jnp.dot(jnp.ones((8, N_E), BF16), xb, preferred_element_type=F32)
    bulk_row = jnp.broadcast_to(pieces(n_row, 0).astype(F32), (8, N_E)).astype(BF16)
    before_bulk_row = jnp.dot(bulk_row, (re < ce).astype(BF16), preferred_element_type=F32)[0:1]

    lanes = lax.broadcasted_iota(jnp.int32, (N_E, LIST_W), 1)
    sub = lax.broadcasted_iota(jnp.int32, (8, N_E), 0)
    e_row = lax.broadcasted_iota(jnp.int32, (8, N_E), 1).astype(F32)
    lists = jnp.zeros((8, LIST_W), F32)
    for ci, (_, lane0) in enumerate(COPY_CLASSES):
        first = before[:, ci:ci + 1].astype(jnp.int32) + lane0
        sel = (lanes >= first) & (lanes < first + pieces(n_col, ci))
        d_row = done(n_row, ci).astype(F32)
        if ci == 0:
            d_row = d_row - PIECE * before_bulk_row
        v = jnp.where(sub == 0, seg + d_row, jnp.where(sub == 1, earlier + d_row,
                                                       jnp.where(sub == 2, e_row, 0.0)))
        lists = lists + jnp.dot(v, sel.astype(F32), preferred_element_type=F32,
                                precision=lax.Precision.HIGHEST)
    q = lax.broadcasted_iota(jnp.int32, (8, LIST_W), 1)
    s8 = lax.broadcasted_iota(jnp.int32, (8, LIST_W), 0)
    lists = lists + jnp.where((s8 < 2) & (q < RS // PIECE), (PIECE * q).astype(F32), 0.0)
    return lists, totals


def _mix_kernel(zg_ref, ya_ref, x_ref, gate1_ref, shift2_ref, scale2_ref,
                lng_ref, lnb_ref, ws_ref, bias_ref, wbb_ref, wo_ref, n2g_ref, rw_ref, rb_ref,
                x1_ref, h2_ref, lg_ref):
    row = lax.broadcasted_iota(jnp.int32, (CHUNK, 2 * CHUNK), 0)
    col = lax.broadcasted_iota(jnp.int32, (CHUNK, 2 * CHUNK), 1)
    causal = (col % CHUNK) <= row
    lane = lax.broadcasted_iota(jnp.int32, (CHUNK, 2 * GM_HD), 1)
    first = lane < GM_HD
    wpairs = [jnp.where(causal, ws_ref[j], 0.0).astype(BF16) for j in range(GM_HEADS // 2)]

    for g in range(TS_MIX // SUB_MIX):
        rows = pl.ds(g * SUB_MIX, SUB_MIX)
        z = _gelu(zg_ref[0, rows, 0:2 * GM_W])
        u = z[:, :GM_W]
        v = z[:, GM_W:]
        mu = jnp.mean(v, axis=-1, keepdims=True)
        vc = v - mu
        var = jnp.mean(vc * vc, axis=-1, keepdims=True)
        vn = vc * lax.rsqrt(var + EPS) * lng_ref[...] + lnb_ref[...]
        chunks = []
        for n in range(SUB_MIX // CHUNK):
            cols = []
            for j in range(GM_HEADS // 2):
                vp = vn[n * CHUNK:(n + 1) * CHUNK, 2 * GM_HD * j:2 * GM_HD * (j + 1)]
                rhs = jnp.concatenate([jnp.where(first, vp, 0.0), jnp.where(first, 0.0, vp)], axis=0)
                cols.append(jnp.dot(wpairs[j], rhs.astype(BF16), preferred_element_type=F32))
            chunks.append(jnp.concatenate(cols, axis=1) + bias_ref[...])
        mixed = jnp.concatenate(chunks, axis=0)
        gm = u * mixed
        yb = jnp.dot(gm.astype(BF16), wbb_ref[...], preferred_element_type=F32)
        g_a = zg_ref[0, rows, 2 * GM_W:2 * GM_W + D]
        g_b = zg_ref[0, rows, 2 * GM_W + D:]
        merged = _sigmoid(g_a) * ya_ref[0, rows, :] + _sigmoid(g_b) * yb
        o = jnp.dot(merged.astype(BF16), wo_ref[...], preferred_element_type=F32)
        x1 = x_ref[0, rows, :] + gate1_ref[0] * o
        x1_ref[0, rows, :] = x1
        h2 = _rms(x1) * (n2g_ref[...] * (1.0 + scale2_ref[0])) + shift2_ref[0]
        hb = h2.astype(BF16)
        h2_ref[0, rows, :] = hb
        lg_ref[:, g * SUB_MIX:(g + 1) * SUB_MIX] = lax.dot_general(
            rw_ref[...], hb, (((1,), (1,)), ((), ())), preferred_element_type=F32) + rb_ref[...]


def _route_kernel(lg_ref, pc_ref, pt_ref, lists_ref, totals_ref, cnt_ref, carry):
    @pl.when(pl.program_id(0) == 0)
    def _():
        carry[...] = jnp.zeros_like(carry)

    for t in range(ROUTE_TILES):
        pc, pt, lists, totals = _route_tile(lg_ref[:, t * TT:(t + 1) * TT], carry)
        pc_ref[t * TT:(t + 1) * TT, :] = pc
        pt_ref[:, t * TT:(t + 1) * TT] = pt
        lists_ref[t] = lists.astype(jnp.int32)
        totals_ref[t] = totals.astype(jnp.int32)
    cnt_ref[...] = jnp.broadcast_to(carry[...], cnt_ref.shape)


def _route_call(logits):
    assert NT % ROUTE_TILES == 0
    return pl.pallas_call(
        _route_kernel,
        grid=(NT // ROUTE_TILES,),
        in_specs=[pl.BlockSpec((N_E, ROUTE_TILES * TT), lambda i: (0, i))],
        out_specs=[pl.BlockSpec((ROUTE_TILES * TT, 8), lambda i: (i, 0)),
                   pl.BlockSpec((8, ROUTE_TILES * TT), lambda i: (0, i)),
                   pl.BlockSpec((ROUTE_TILES, 8, LIST_W), lambda i: (i, 0, 0)),
                   pl.BlockSpec((ROUTE_TILES, 8, 8), lambda i: (i, 0, 0)),
                   pl.BlockSpec((N_E, 8), lambda i: (0, 0))],
        out_shape=[jax.ShapeDtypeStruct((T, 8), F32),
                   jax.ShapeDtypeStruct((8, T), F32),
                   jax.ShapeDtypeStruct((NT, 8, LIST_W), jnp.int32),
                   jax.ShapeDtypeStruct((NT, 8, 8), jnp.int32),
                   jax.ShapeDtypeStruct((N_E, 8), F32)],
        scratch_shapes=[pltpu.VMEM((N_E, 1), F32)],
        compiler_params=pltpu.CompilerParams(dimension_semantics=("arbitrary",)),
        name="route",
    )(logits)


def _mix_call(zg, ya2d, x, mod3, ln_g, ln_b, ws_pairs, bias_full, wbb, wo, n2g, rw, rb):
    tok_spec = pl.BlockSpec((1, TS_MIX, D), lambda b, s: (b, s, 0))
    full = lambda *shape: pl.BlockSpec(shape, lambda b, s: (0,) * len(shape))
    mod_spec = lambda j: pl.BlockSpec((1, 1, D), lambda b, s: (b, 0, j))
    return pl.pallas_call(
        _mix_kernel,
        grid=(B, S // TS_MIX),
        in_specs=[pl.BlockSpec((1, TS_MIX, zg.shape[-1]), lambda b, s: (b, s, 0)), tok_spec, tok_spec,
                  mod_spec(2), mod_spec(3), mod_spec(4),
                  full(1, GM_W), full(1, GM_W),
                  full(GM_HEADS // 2, CHUNK, 2 * CHUNK),
                  full(CHUNK, GM_W),
                  full(GM_W, D), full(D, D), full(1, D),
                  full(N_E, D), full(N_E, 1)],
        out_specs=[tok_spec, tok_spec,
                   pl.BlockSpec((N_E, TS_MIX), lambda b, s: (0, b * (S // TS_MIX) + s))],
        out_shape=[jax.ShapeDtypeStruct((B, S, D), F32),
                   jax.ShapeDtypeStruct((B, S, D), BF16),
                   jax.ShapeDtypeStruct((N_E, T), F32)],
        compiler_params=pltpu.CompilerParams(vmem_limit_bytes=VMEM_LIMIT),
        name="gmlp_merge_norm2",
    )(zg, ya2d, x, mod3, mod3, mod3, ln_g, ln_b, ws_pairs, bias_full, wbb, wo, n2g, rw, rb)


def _pack_rows(v):
    return pltpu.pack_elementwise([v[:, :HALF], v[:, HALF:]], packed_dtype=BF16)


def _unpack_rows(w):
    halves = [pltpu.unpack_elementwise(w, index=i, packed_dtype=BF16, unpacked_dtype=F32)
              for i in range(2)]
    return jnp.concatenate(halves, axis=1)


def _load_grouped(ref, rows, first=0):
    return jnp.concatenate([ref[pl.ds(first * SUB + c, rows, stride=SUB), :] for c in range(SUB)], axis=1)


def _store_grouped(ref, w, rows, first=0):
    for c in range(SUB):
        ref[pl.ds(first * SUB + c, rows, stride=SUB), :] = w[:, c * LANES:(c + 1) * LANES]


def _start_tile_runs(tile, src_ref, dst_ref, src_tbl, dst_tbl, totals_tbl, sem):
    for ci, (rows, lane0) in enumerate(COPY_CLASSES):
        base = tile * LIST_W + lane0

        def start(i, carry, rows=rows, base=base):
            s = pl.multiple_of(src_tbl[base + i] * SUB, SUB)
            d = pl.multiple_of(dst_tbl[base + i] * SUB, SUB)
            pltpu.make_async_copy(src_ref.at[pl.ds(s, rows * SUB)],
                                  dst_ref.at[pl.ds(d, rows * SUB)], sem).start()
            return carry

        lax.fori_loop(0, totals_tbl[tile * 8 + ci], start, 0)


def _wait_tile_runs(src_ref, dst_ref, sem):
    pltpu.make_async_copy(src_ref, dst_ref, sem).wait()


def _dispatch_kernel(loc_ref, glob_ref, tot_ref, pend_ref, h_ref, pt_ref, xs_ref,
                     sbuf0, sbuf1, zbuf, sem_z, sem0, sem1):
    j = pl.program_id(0)

    @pl.when(j == 0)
    def _():
        zbuf[...] = jnp.zeros_like(zbuf)
        for e in range(N_E):
            prev = pend_ref[e - 1] if e > 0 else 0
            end = pend_ref[e]

            @pl.when(end > prev)
            def _():
                first = pl.multiple_of((end - TM) * SUB, TM * SUB)
                cp = pltpu.make_async_copy(zbuf, xs_ref.at[pl.ds(first, TM * SUB)], sem_z)
                cp.start()
                cp.wait()

    rows = lax.broadcasted_iota(jnp.int32, (RS, TT), 0)
    words = []
    for t in range(2):
        pos = pt_ref[:, t * TT:(t + 1) * TT].astype(jnp.int32)
        hit = rows == pos[0:1, :]
        for k in range(1, TOP_K):
            hit = hit | (rows == pos[k:k + 1, :])
        pm = jnp.where(hit, 1.0, 0.0).astype(BF16)
        srt = jnp.dot(pm, h_ref[t * TT:(t + 1) * TT, :], preferred_element_type=F32)
        words.append(_pack_rows(srt))
    head = xs_ref.at[pl.ds(0, RS * SUB)]

    for t, (sbuf, sem) in enumerate(((sbuf0, sem0), (sbuf1, sem1))):
        @pl.when(j >= 1)
        def _(sbuf=sbuf, sem=sem):
            _wait_tile_runs(sbuf, head, sem)
        _store_grouped(sbuf, words[t], RS)
        _start_tile_runs(2 * j + t, sbuf, xs_ref, loc_ref, glob_ref, tot_ref, sem)

    @pl.when(j == NT // 2 - 1)
    def _():
        _wait_tile_runs(sbuf0, head, sem0)
        _wait_tile_runs(sbuf1, head, sem1)


def _dispatch_call(loc_t, glob_t, tot_t, pad_end, h2, pos_t):
    assert NT % 2 == 0
    grid_spec = pltpu.PrefetchScalarGridSpec(
        num_scalar_prefetch=4,
        grid=(NT // 2,),
        in_specs=[pl.BlockSpec((2 * TT, D), lambda j, *_: (j, 0)),
                  pl.BlockSpec((8, 2 * TT), lambda j, *_: (0, j))],
        out_specs=pl.BlockSpec(memory_space=pl.ANY),
        scratch_shapes=[pltpu.VMEM((RS * SUB, LANES), U32),
                        pltpu.VMEM((RS * SUB, LANES), U32),
                        pltpu.VMEM((TM * SUB, LANES), U32),
                        pltpu.SemaphoreType.DMA,
                        pltpu.SemaphoreType.DMA,
                        pltpu.SemaphoreType.DMA],
    )
    return pl.pallas_call(
        _dispatch_kernel,
        grid_spec=grid_spec,
        out_shape=jax.ShapeDtypeStruct((N_ROWS * SUB, LANES), U32),
        compiler_params=pltpu.CompilerParams(dimension_semantics=("arbitrary",),
                                             vmem_limit_bytes=VMEM_LIMIT),
        name="dispatch",
    )(loc_t, glob_t, tot_t, pad_end, h2, pos_t)


def _moe_kernel(be_ref, bf_ref, nx_ref, nv_ref, xs_ref, wi_hbm, bi_ref, wo_hbm, bo_ref, ys_ref,
                wi_f32, wo_f32, wi_bf, wo_bf, sem_i, sem_o):
    step = pl.program_id(0)

    def fetch(e):
        return (pltpu.make_async_copy(wi_hbm.at[e], wi_f32, sem_i),
                pltpu.make_async_copy(wo_hbm.at[e], wo_f32, sem_o))

    @pl.when(step == 0)
    def _():
        for cp in fetch(be_ref[0]):
            cp.start()

    def load_weights(i):
        @pl.when(bf_ref[i] == 1)
        def _():
            for cp in fetch(be_ref[i]):
                cp.wait()
            wi_bf[...] = wi_f32[...].astype(BF16)
            wo_bf[...] = wo_f32[...].astype(BF16)

            @pl.when(nx_ref[i] >= 0)
            def _():
                for cp in fetch(nx_ref[i]):
                    cp.start()

    def ffn(i, first, rows):
        e = be_ref[i]
        xb = _unpack_rows(_load_grouped(xs_ref, rows, first)).astype(BF16)
        gu = jnp.dot(xb, wi_bf[...], preferred_element_type=F32) + bi_ref[pl.ds(e, 1), :]
        gate = jnp.minimum(gu[:, :D_E], LIMIT)
        up = jnp.clip(gu[:, D_E:], -LIMIT, LIMIT)
        act = (up + 1.0) * (gate * _sigmoid(ALPHA * gate))
        y = jnp.dot(act.astype(BF16), wo_bf[...], preferred_element_type=F32) + bo_ref[pl.ds(e, 1), :]
        _store_grouped(ys_ref, _pack_rows(y), rows, first)

    i0 = step * BPS
    last = i0 + BPS - 1
    uniform = (last < nv_ref[0]) & (be_ref[i0] == be_ref[last])

    @pl.when(uniform)
    def _():
        load_weights(i0)
        ffn(i0, 0, BPS * TM)

    @pl.when(jnp.logical_not(uniform))
    def _():
        for sub in range(BPS):
            i = i0 + sub

            @pl.when(i < nv_ref[0])
            def _(i=i, sub=sub):
                load_weights(i)
                ffn(i, sub * TM, TM)


def _moe_call(blk_e, blk_first, blk_next, n_valid, xs, w_in, b_in, w_out, b_out):
    assert N_BLOCKS % BPS == 0

    def row_map(s, be, bf, nx, nv):
        last = (nv[0] + BPS - 1) // BPS - 1
        return (jnp.maximum(jnp.minimum(s, last), 0), 0)

    grid_spec = pltpu.PrefetchScalarGridSpec(
        num_scalar_prefetch=4,
        grid=(N_BLOCKS // BPS,),
        in_specs=[pl.BlockSpec((BPS * TM * SUB, LANES), row_map),
                  pl.BlockSpec(memory_space=pl.ANY),
                  pl.BlockSpec((N_E, 2 * D_E), lambda s, *_: (0, 0)),
                  pl.BlockSpec(memory_space=pl.ANY),
                  pl.BlockSpec((N_E, D), lambda s, *_: (0, 0))],
        out_specs=pl.BlockSpec((BPS * TM * SUB, LANES), row_map),
        scratch_shapes=[pltpu.VMEM((D, 2 * D_E), F32),
                        pltpu.VMEM((D_E, D), F32),
                        pltpu.VMEM((D, 2 * D_E), BF16),
                        pltpu.VMEM((D_E, D), BF16),
                        pltpu.SemaphoreType.DMA,
                        pltpu.SemaphoreType.DMA],
    )
    return pl.pallas_call(
        _moe_kernel,
        grid_spec=grid_spec,
        out_shape=jax.ShapeDtypeStruct((N_ROWS * SUB, LANES), U32),
        compiler_params=pltpu.CompilerParams(dimension_semantics=("arbitrary",),
                                             vmem_limit_bytes=VMEM_LIMIT),
        name="moe_experts",
    )(blk_e, blk_first, blk_next, n_valid, xs, w_in, b_in, w_out, b_out)


def _combine_kernel(loc_ref, glob_ref, tot_ref, ys_ref, pc_ref, x1_ref, gate2_ref, fg_ref, o_ref,
                    buf0, buf1, buf2, buf3, sem0, sem1, sem2, sem3):
    j = pl.program_id(0)
    head = ys_ref.at[pl.ds(0, RS * SUB)]
    even = ((buf0, sem0), (buf1, sem1))
    odd = ((buf2, sem2), (buf3, sem3))

    def fetch(step, slots):
        for t, (buf, sem) in enumerate(slots):
            _start_tile_runs(2 * step + t, ys_ref, buf, glob_ref, loc_ref, tot_ref, sem)

    @pl.when(j == 0)
    def _():
        fetch(0, even)

    col = lax.broadcasted_iota(jnp.int32, (TT, RS), 1)
    wms = []
    for t in range(2):
        pc = pc_ref[t * TT:(t + 1) * TT, :]
        pos = pc.astype(jnp.int32)
        wm = jnp.zeros((TT, RS), F32)
        for k in range(TOP_K):
            wm = jnp.where(col == pos[:, k:k + 1], pc[:, TOP_K + k:TOP_K + k + 1], wm)
        wms.append(wm.astype(BF16))

    def step(cur, nxt):
        @pl.when(j + 1 < NT // 2)
        def _():
            fetch(j + 1, nxt)
        for t, (buf, sem) in enumerate(cur):
            _wait_tile_runs(head, buf, sem)
            yt = _unpack_rows(_load_grouped(buf, RS)).astype(BF16)
            acc = jnp.dot(wms[t], yt, preferred_element_type=F32)
            x2 = x1_ref[t * TT:(t + 1) * TT, :] + gate2_ref[0] * acc
            o_ref[t * TT:(t + 1) * TT, :] = _rms(x2) * fg_ref[...]

    @pl.when(j % 2 == 0)
    def _():
        step(even, odd)

    @pl.when(j % 2 == 1)
    def _():
        step(odd, even)


def _combine_call(loc_t, glob_t, tot_t, ys, pos_c, x1, mod3, final_g):
    per_b = S // (2 * TT)
    grid_spec = pltpu.PrefetchScalarGridSpec(
        num_scalar_prefetch=3,
        grid=(NT // 2,),
        in_specs=[pl.BlockSpec(memory_space=pl.ANY),
                  pl.BlockSpec((2 * TT, 8), lambda j, *_: (j, 0)),
                  pl.BlockSpec((2 * TT, D), lambda j, *_: (j, 0)),
                  pl.BlockSpec((1, 1, D), lambda j, *_: (j // per_b, 0, 5)),
                  pl.BlockSpec((1, D), lambda j, *_: (0, 0))],
        out_specs=pl.BlockSpec((2 * TT, D), lambda j, *_: (j, 0)),
        scratch_shapes=[pltpu.VMEM((RS * SUB, LANES), U32)] * 4 + [pltpu.SemaphoreType.DMA] * 4,
    )
    return pl.pallas_call(
        _combine_kernel,
        grid_spec=grid_spec,
        out_shape=jax.ShapeDtypeStruct((T, D), F32),
        compiler_params=pltpu.CompilerParams(dimension_semantics=("arbitrary",),
                                             vmem_limit_bytes=VMEM_LIMIT),
        name="combine_norm",
    )(loc_t, glob_t, tot_t, ys, pos_c, x1, mod3, final_g)


def kernel(x, c, ada_w, ada_b, norm1_g, w_in, ssm_a_re, ssm_a_im, ssm_log_dt, ssm_b_re, ssm_b_im, ssm_c_re, ssm_c_im, ssm_d, ssm_glu_w, ssm_glu_b, w_branch_a, gmlp_ln_g, gmlp_ln_b, gmlp_ws, gmlp_bs, w_branch_b, w_out, norm2_g, router_w, router_b, moe_w_in, moe_b_in, moe_w_out, moe_b_out, final_g):
    depth = ada_w.shape[0]
    assert depth == 1, "the final rms_norm is fused into the combine kernel of the only layer"
    for layer in range(depth):
        mod = _mod_call(c, ada_w[layer], ada_b[layer])
        mod3 = mod.reshape(B, 1, 6 * D)

        u, zg = _proj_call(x, norm1_g[layer], mod3, w_in[layer].astype(BF16))

        bm, cre, cim, are, aim = _s5_params(ssm_a_re[layer], ssm_a_im[layer], ssm_log_dt[layer],
                                            ssm_b_re[layer], ssm_b_im[layer],
                                            ssm_c_re[layer], ssm_c_im[layer])
        ya = _s5_call(u, bm, cre, cim, are, aim,
                      ssm_d[layer].reshape(1, SSM_W), ssm_glu_w[layer].astype(BF16),
                      ssm_glu_b[layer].reshape(1, SSM_W), w_branch_a[layer].astype(BF16))

        ws = gmlp_ws[layer]
        ws_pairs = jnp.concatenate([ws[0::2], ws[1::2]], axis=-1)
        bias_full = jnp.repeat(gmlp_bs[layer].T, GM_HD, axis=1)
        x1, h2, logits = _mix_call(
            zg, ya, x, mod3,
            gmlp_ln_g[layer].reshape(1, GM_W), gmlp_ln_b[layer].reshape(1, GM_W),
            ws_pairs, bias_full, w_branch_b[layer].astype(BF16), w_out[layer].astype(BF16),
            norm2_g[layer].reshape(1, D), router_w[layer].T.astype(BF16),
            router_b[layer].reshape(N_E, 1))

        pos_c, pos_t, lists, totals, cnt = _route_call(logits)
        counts = cnt[:, 0].astype(jnp.int32)
        nblk = (counts + TM - 1) // TM
        blk_end = jnp.cumsum(nblk)
        pad_end = (blk_end * TM).astype(jnp.int32)
        experts = jnp.arange(N_E, dtype=jnp.int32)
        blk_ids = jnp.arange(N_BLOCKS, dtype=jnp.int32)
        blk_e = jnp.sum((blk_end[None, :] <= blk_ids[:, None]).astype(jnp.int32), axis=1)
        blk_e = jnp.minimum(blk_e, N_E - 1)
        blk_first = jnp.concatenate([jnp.ones((1,), jnp.int32),
                                     (blk_e[1:] != blk_e[:-1]).astype(jnp.int32)])
        later = (experts[None, :] > experts[:, None]) & (nblk[None, :] > 0)
        next_e = jnp.min(jnp.where(later, experts[None, :], N_E), axis=1)
        next_e = jnp.where(next_e == N_E, -1, next_e)
        blk_next = jnp.sum(jnp.where(blk_e[:, None] == experts[None, :], next_e[None, :], 0), axis=1)
        n_valid = blk_end[-1:].astype(jnp.int32)
        pad_start = pad_end - nblk * TM
        loc_t = lists[:, 0, :].reshape(NT * LIST_W)
        owner = lists[:, 2, :, None] == experts[None, None, :]
        glob_t = (lists[:, 1, :] + jnp.sum(jnp.where(owner, pad_start, 0), axis=-1)).reshape(NT * LIST_W)
        tot_t = totals[:, 0, :].reshape(NT * 8)

        xs = _dispatch_call(loc_t, glob_t, tot_t, pad_end, h2.reshape(T, D), pos_t)
        ys = _moe_call(blk_e, blk_first, blk_next.astype(jnp.int32), n_valid, xs, moe_w_in[layer],
                       moe_b_in[layer], moe_w_out[layer], moe_b_out[layer])
        x = _combine_call(loc_t, glob_t, tot_t, ys, pos_c, x1.reshape(T, D),
                          mod3, final_g.reshape(1, D)).reshape(B, S, D)
    return x
```

```python
import math

import jax
import jax.numpy as jnp
from jax import lax
from jax.experimental import pallas as pl
from jax.experimental.pallas import tpu as pltpu

F32 = jnp.float32
BF16 = jnp.bfloat16

D = 1024
B = 8
S = 2048
T = B * S
SSM_W = 512
SSM_G = 32
SSM_H = 16
SSM_P = 64
N_PACK = 4
PACK_G = SSM_G // N_PACK
GM_W = 512
GM_HEADS = 8
GM_HD = 64
CHUNK = 128
N_E = 32
TOP_K = 4
D_E = 1024
LIMIT = 7.0
ALPHA = 1.702
EPS = 1e-6

TS_PROJ = 512
L_SSM = 128
R_SSM = L_SSM * B
TS_MIX = 512
SUB_MIX = 256
TT = 256
NT = T // TT
ROUTE_TILES = 4
RS = TOP_K * TT
TM = 256
BPS = 4
N_ROWS = T * TOP_K + N_E * TM
N_BLOCKS = N_ROWS // TM
LANES = 128
HALF = D // 2
SUB = HALF // LANES
PIECE = 32
COPY_CLASSES = ((PIECE, 0),) + tuple((PIECE >> s, RS // PIECE + N_E * (s - 1))
                                     for s in range(1, PIECE.bit_length()))
LIST_W = 256
CLASS_SLOTS = 8
assert COPY_CLASSES[-1][1] + N_E <= LIST_W and len(COPY_CLASSES) <= CLASS_SLOTS
VMEM_LIMIT = 56 * 1024 * 1024
U32 = jnp.uint32


def _sigmoid(v):
    return 0.5 * jnp.tanh(0.5 * v) + 0.5


def _gelu(v):
    c = math.sqrt(2.0 / math.pi)
    inner = v * (c + (c * 0.044715) * (v * v))
    return v * (0.5 + 0.5 * jnp.tanh(inner))


def _rms(v):
    return v * lax.rsqrt(jnp.mean(v * v, axis=-1, keepdims=True) + EPS)


def _mod_kernel(c_ref, w_ref, b_ref, o_ref):
    cv = c_ref[...]
    sv = cv * _sigmoid(cv)
    o_ref[...] = jnp.dot(sv, w_ref[...], preferred_element_type=F32,
                         precision=lax.Precision.HIGHEST) + b_ref[...]


def _mod_call(c, ada_w, ada_b):
    n = ada_w.shape[1]
    return pl.pallas_call(
        _mod_kernel,
        grid=(n // D,),
        in_specs=[pl.BlockSpec((B, D), lambda j: (0, 0)),
                  pl.BlockSpec((D, D), lambda j: (0, j)),
                  pl.BlockSpec((1, D), lambda j: (0, j))],
        out_specs=pl.BlockSpec((B, D), lambda j: (0, j)),
        out_shape=jax.ShapeDtypeStruct((B, n), F32),
        name="adaln_mod",
    )(c, ada_w, ada_b.reshape(1, n))


def _proj_kernel(x_ref, g_ref, shift_ref, scale_ref, w_ref, u_ref, zg_ref):
    h = _rms(x_ref[0]) * (g_ref[...] * (1.0 + scale_ref[0])) + shift_ref[0]
    hb = h.astype(BF16)
    u_ref[0] = jnp.dot(hb, w_ref[:, 0:SSM_W], preferred_element_type=F32)
    zg_ref[0] = jnp.dot(hb, w_ref[:, SSM_W:], preferred_element_type=F32)


def _proj_call(x, norm_g, mod3, w_in_bf):
    pw = w_in_bf.shape[1]
    tok_spec = pl.BlockSpec((1, TS_PROJ, D), lambda b, s: (b, s, 0))
    return pl.pallas_call(
        _proj_kernel,
        grid=(B, S // TS_PROJ),
        in_specs=[tok_spec,
                  pl.BlockSpec((1, D), lambda b, s: (0, 0)),
                  pl.BlockSpec((1, 1, D), lambda b, s: (b, 0, 0)),
                  pl.BlockSpec((1, 1, D), lambda b, s: (b, 0, 1)),
                  pl.BlockSpec((D, pw), lambda b, s: (0, 0))],
        out_specs=[pl.BlockSpec((1, TS_PROJ, SSM_W), lambda b, s: (b, s, 0)),
                   pl.BlockSpec((1, TS_PROJ, pw - SSM_W), lambda b, s: (b, s, 0))],
        out_shape=[jax.ShapeDtypeStruct((B, S, SSM_W), F32),
                   jax.ShapeDtypeStruct((B, S, pw - SSM_W), F32)],
        compiler_params=pltpu.CompilerParams(vmem_limit_bytes=VMEM_LIMIT),
        name="norm_proj",
    )(x, norm_g.reshape(1, D), mod3, mod3, w_in_bf)


def _s5_kernel(u_ref, bm_ref, cre_ref, cim_ref, are_ref, aim_ref, d_ref, gw_ref, gb_ref, wa_ref,
               o_ref, usc, ysc, sre, sim, st_re, st_im):
    @pl.when(pl.program_id(0) == 0)
    def _():
        st_re[...] = jnp.zeros_like(st_re)
        st_im[...] = jnp.zeros_like(st_im)

    nslab = SSM_W // LANES
    for b in range(B):
        for c in range(nslab):
            usc[c, pl.ds(b, L_SSM, stride=B), :] = u_ref[b, :, c * LANES:(c + 1) * LANES]
    u = jnp.concatenate([usc[c] for c in range(nslab)], axis=1)
    ub = u.astype(BF16)
    half = PACK_G * SSM_P
    ys = []
    for k in range(N_PACK):
        pin = PACK_G * SSM_H
        bu = jnp.dot(ub[:, pin * k:pin * (k + 1)], bm_ref[k], preferred_element_type=F32)
        sre[k] = bu[:, :half]
        sim[k] = bu[:, half:]
        ar = are_ref[k]
        ai = aim_ref[k]
        r = st_re[k]
        m = st_im[k]
        for t in range(L_SSM):
            rows = pl.ds(t * B, B)
            nr = ar * r - ai * m + sre[k, rows, :]
            m = ar * m + ai * r + sim[k, rows, :]
            r = nr
            sre[k, rows, :] = r
            sim[k, rows, :] = m
        st_re[k] = r
        st_im[k] = m
        yk = jnp.dot(sre[k].astype(BF16), cre_ref[k], preferred_element_type=F32)
        yk = yk + jnp.dot(sim[k].astype(BF16), cim_ref[k], preferred_element_type=F32)
        ys.append(yk)
    for c in range(nslab):
        uc = usc[c]
        ysc[c] = ys[c] + d_ref[:, c * LANES:(c + 1) * LANES] * uc
    y = jnp.concatenate(
        [jnp.concatenate([ysc[c, pl.ds(b, L_SSM, stride=B), :] for c in range(nslab)], axis=1)
         for b in range(B)], axis=0)
    z = _gelu(y)
    gl = jnp.dot(z.astype(BF16), gw_ref[...], preferred_element_type=F32) + gb_ref[...]
    out = z * _sigmoid(gl)
    o = jnp.dot(out.astype(BF16), wa_ref[...], preferred_element_type=F32)
    for b in range(B):
        o_ref[b] = o[b * L_SSM:(b + 1) * L_SSM]


def _s5_call(u, bm, cre, cim, are, aim, d_skip, glu_w, glu_b, w_a):
    half = PACK_G * SSM_P
    full = lambda *shape: pl.BlockSpec(shape, lambda i: (0,) * len(shape))
    return pl.pallas_call(
        _s5_kernel,
        grid=(S // L_SSM,),
        in_specs=[pl.BlockSpec((B, L_SSM, SSM_W), lambda i: (0, i, 0)),
                  full(N_PACK, 128, 2 * half),
                  full(N_PACK, half, 128),
                  full(N_PACK, half, 128),
                  full(N_PACK, B, half),
                  full(N_PACK, B, half),
                  full(1, SSM_W),
                  full(SSM_W, SSM_W),
                  full(1, SSM_W),
                  full(SSM_W, D)],
        out_specs=pl.BlockSpec((B, L_SSM, D), lambda i: (0, i, 0)),
        out_shape=jax.ShapeDtypeStruct((B, S, D), F32),
        scratch_shapes=[pltpu.VMEM((SSM_W // LANES, R_SSM, LANES), F32),
                        pltpu.VMEM((SSM_W // LANES, R_SSM, LANES), F32),
                        pltpu.VMEM((N_PACK, R_SSM, half), F32),
                        pltpu.VMEM((N_PACK, R_SSM, half), F32),
                        pltpu.VMEM((N_PACK, B, half), F32),
                        pltpu.VMEM((N_PACK, B, half), F32)],
        compiler_params=pltpu.CompilerParams(dimension_semantics=("arbitrary",),
                                             vmem_limit_bytes=VMEM_LIMIT),
        name="s5_branch",
    )(u, bm, cre, cim, are, aim, d_skip, glu_w, glu_b, w_a)


def _s5_params(a_re, a_im, log_dt, b_re, b_im, c_re, c_im):
    dt = jnp.exp(log_dt)[:, None]
    mag = jnp.exp(a_re * dt)
    lr = mag * jnp.cos(a_im * dt)
    li = mag * jnp.sin(a_im * dt)
    den = a_re * a_re + a_im * a_im
    cr = ((lr - 1.0) * a_re + li * a_im) / den
    ci = (li * a_re - (lr - 1.0) * a_im) / den
    bbr = cr[..., None] * b_re - ci[..., None] * b_im
    bbi = cr[..., None] * b_im + ci[..., None] * b_re
    eye = jnp.eye(PACK_G, dtype=F32)
    half = PACK_G * SSM_P

    def pack_b(m):
        m4 = m.reshape(N_PACK, PACK_G, SSM_P, SSM_H)
        return jnp.einsum('kgph,gj->kghjp', m4, eye).reshape(N_PACK, PACK_G * SSM_H, half)

    def pack_c(m):
        m4 = m.reshape(N_PACK, PACK_G, SSM_H, SSM_P)
        return jnp.einsum('kghp,gj->kgpjh', m4, eye).reshape(N_PACK, half, PACK_G * SSM_H)

    bm = jnp.concatenate([pack_b(bbr), pack_b(bbi)], axis=-1).astype(BF16)
    cre = pack_c(c_re).astype(BF16)
    cim = (-pack_c(c_im)).astype(BF16)
    are = jnp.broadcast_to(lr.reshape(N_PACK, 1, half), (N_PACK, B, half))
    aim = jnp.broadcast_to(li.reshape(N_PACK, 1, half), (N_PACK, B, half))
    return bm, cre, cim, are, aim


def _route_tile(lt, carry):
    sub = lax.broadcasted_iota(jnp.int32, lt.shape, 0).astype(F32)
    sels, vals = [], []
    for _ in range(TOP_K):
        m = jnp.max(lt, axis=0, keepdims=True)
        idx = jnp.min(jnp.where(lt == m, sub, float(N_E)), axis=0, keepdims=True)
        sel = sub == idx
        sels.append(sel)
        vals.append(m)
        lt = jnp.where(sel, -jnp.inf, lt)
    member = sels[0].astype(F32)
    for k in range(1, TOP_K):
        member = member + sels[k].astype(F32)
    n_col = jnp.sum(member, axis=1, keepdims=True)

    re = lax.broadcasted_iota(jnp.int32, (N_E, N_E), 0)
    ce = lax.broadcasted_iota(jnp.int32, (N_E, N_E), 1)
    nb = jnp.broadcast_to(n_col, (N_E, 8)).astype(BF16)
    seg_col = jnp.dot((ce < re).astype(BF16), nb, preferred_element_type=F32)[:, 0:1]
    rt = lax.broadcasted_iota(jnp.int32, (TT, TT), 0)
    ct = lax.broadcasted_iota(jnp.int32, (TT, TT), 1)
    rank = jnp.dot(member.astype(BF16), (rt < ct).astype(BF16), preferred_element_type=F32)
    posb = seg_col + rank
    denom = jnp.zeros_like(vals[0])
    exps = []
    for k in range(TOP_K):
        e = jnp.exp(vals[k] - vals[0])
        exps.append(e)
        denom = denom + e
    s8 = lax.broadcasted_iota(jnp.int32, (8, TT), 0)
    pt = jnp.zeros((8, TT), F32)
    for k in range(TOP_K):
        pk = jnp.sum(jnp.where(sels[k], posb, 0.0), axis=0, keepdims=True)
        pt = jnp.where(s8 == k, pk, pt)
        pt = jnp.where(s8 == TOP_K + k, exps[k] / denom, pt)
    pc = pt.T
    earlier_col = carry[...]
    carry[...] = earlier_col + n_col
    lane8 = lax.broadcasted_iota(jnp.int32, (N_E, 8), 1)
    cols = jnp.where(lane8 == 0, n_col, jnp.where(lane8 == 1, seg_col, jnp.where(lane8 == 2, earlier_col, 0.0)))
    rows = cols.T
    lists, totals = _copy_lists(n_col.astype(jnp.int32), rows[0:1], rows[1:2], rows[2:3])
    return pc, pt, lists, totals


def _copy_lists(n_col, n_row_f, seg, earlier):
    shift = PIECE.bit_length() - 1

    def pieces(n, ci):
        if ci == 0:
            return jnp.right_shift(n, shift)
        return jnp.bitwise_and(jnp.right_shift(n, shift - ci), 1)

    def done(n, ci):
        if ci == 0:
            return jnp.zeros_like(n)
        return n - jnp.bitwise_and(n, (PIECE >> (ci - 1)) - 1)

    re = lax.broadcasted_iota(jnp.int32, (N_E, N_E), 0)
    ce = lax.broadcasted_iota(jnp.int32, (N_E, N_E), 1)
    n_row = n_row_f.astype(jnp.int32)
    lane8 = lax.broadcasted_iota(jnp.int32, (N_E, CLASS_SLOTS), 1)
    x = jnp.zeros((N_E, CLASS_SLOTS), F32)
    for ci in range(len(COPY_CLASSES)):
        x = jnp.where(lane8 == ci, pieces(n_col, ci).astype(F32), x)
    xb = x.astype(BF16)
    before = jnp.dot((ce < re).astype(BF16), xb, preferred_element_type=F32)
    totals = jnp.dot(jnp.ones((8, N_E), BF16), xb, preferred_element_type=F32)
    bulk_row = jnp.broadcast_to(pieces(n_row, 0).astype(F32), (8, N_E)).astype(BF16)
    before_bulk_row = jnp.dot(bulk_row, (re < ce).astype(BF16), preferred_element_type=F32)[0:1]

    lanes = lax.broadcasted_iota(jnp.int32, (N_E, LIST_W), 1)
    sub = lax.broadcasted_iota(jnp.int32, (8, N_E), 0)
    e_row = lax.broadcasted_iota(jnp.int32, (8, N_E), 1).astype(F32)
    lists = jnp.zeros((8, LIST_W), F32)
    for ci, (_, lane0) in enumerate(COPY_CLASSES):
        first = before[:, ci:ci + 1].astype(jnp.int32) + lane0
        sel = (lanes >= first) & (lanes < first + pieces(n_col, ci))
        d_row = done(n_row, ci).astype(F32)
        if ci == 0:
            d_row = d_row - PIECE * before_bulk_row
        v = jnp.where(sub == 0, seg + d_row, jnp.where(sub == 1, earlier + d_row,
                                                       jnp.where(sub == 2, e_row, 0.0)))
        lists = lists + jnp.dot(v, sel.astype(F32), preferred_element_type=F32,
                                precision=lax.Precision.HIGHEST)
    q = lax.broadcasted_iota(jnp.int32, (8, LIST_W), 1)
    s8 = lax.broadcasted_iota(jnp.int32, (8, LIST_W), 0)
    lists = lists + jnp.where((s8 < 2) & (q < RS // PIECE), (PIECE * q).astype(F32), 0.0)
    return lists, totals


def _mix_kernel(zg_ref, ya_ref, x_ref, gate1_ref, shift2_ref, scale2_ref,
                lng_ref, lnb_ref, ws_ref, bias_ref, wbb_ref, wo_ref, n2g_ref, rw_ref, rb_ref,
                x1_ref, h2_ref, lg_ref):
    row = lax.broadcasted_iota(jnp.int32, (CHUNK, 2 * CHUNK), 0)
    col = lax.broadcasted_iota(jnp.int32, (CHUNK, 2 * CHUNK), 1)
    causal = (col % CHUNK) <= row
    lane = lax.broadcasted_iota(jnp.int32, (CHUNK, 2 * GM_HD), 1)
    first = lane < GM_HD
    wpairs = [jnp.where(causal, ws_ref[j], 0.0).astype(BF16) for j in range(GM_HEADS // 2)]

    for g in range(TS_MIX // SUB_MIX):
        rows = pl.ds(g * SUB_MIX, SUB_MIX)
        z = _gelu(zg_ref[0, rows, 0:2 * GM_W])
        u = z[:, :GM_W]
        v = z[:, GM_W:]
        mu = jnp.mean(v, axis=-1, keepdims=True)
        vc = v - mu
        var = jnp.mean(vc * vc, axis=-1, keepdims=True)
        vn = vc * lax.rsqrt(var + EPS) * lng_ref[...] + lnb_ref[...]
        chunks = []
        for n in range(SUB_MIX // CHUNK):
            cols = []
            for j in range(GM_HEADS // 2):
                vp = vn[n * CHUNK:(n + 1) * CHUNK, 2 * GM_HD * j:2 * GM_HD * (j + 1)]
                rhs = jnp.concatenate([jnp.where(first, vp, 0.0), jnp.where(first, 0.0, vp)], axis=0)
                cols.append(jnp.dot(wpairs[j], rhs.astype(BF16), preferred_element_type=F32))
            chunks.append(jnp.concatenate(cols, axis=1) + bias_ref[...])
        mixed = jnp.concatenate(chunks, axis=0)
        gm = u * mixed
        yb = jnp.dot(gm.astype(BF16), wbb_ref[...], preferred_element_type=F32)
        g_a = zg_ref[0, rows, 2 * GM_W:2 * GM_W + D]
        g_b = zg_ref[0, rows, 2 * GM_W + D:]
        merged = _sigmoid(g_a) * ya_ref[0, rows, :] + _sigmoid(g_b) * yb
        o = jnp.dot(merged.astype(BF16), wo_ref[...], preferred_element_type=F32)
        x1 = x_ref[0, rows, :] + gate1_ref[0] * o
        x1_ref[0, rows, :] = x1
        h2 = _rms(x1) * (n2g_ref[...] * (1.0 + scale2_ref[0])) + shift2_ref[0]
        hb = h2.astype(BF16)
        h2_ref[0, rows, :] = hb
        lg_ref[:, g * SUB_MIX:(g + 1) * SUB_MIX] = lax.dot_general(
            rw_ref[...], hb, (((1,), (1,)), ((), ())), preferred_element_type=F32) + rb_ref[...]


def _route_kernel(lg_ref, pc_ref, pt_ref, lists_ref, totals_ref, cnt_ref, carry):
    @pl.when(pl.program_id(0) == 0)
    def _():
        carry[...] = jnp.zeros_like(carry)

    for t in range(ROUTE_TILES):
        pc, pt, lists, totals = _route_tile(lg_ref[:, t * TT:(t + 1) * TT], carry)
        pc_ref[t * TT:(t + 1) * TT, :] = pc
        pt_ref[:, t * TT:(t + 1) * TT] = pt
        lists_ref[t] = lists.astype(jnp.int32)
        totals_ref[t] = totals.astype(jnp.int32)
    cnt_ref[...] = jnp.broadcast_to(carry[...], cnt_ref.shape)


def _route_call(logits):
    assert NT % ROUTE_TILES == 0
    return pl.pallas_call(
        _route_kernel,
        grid=(NT // ROUTE_TILES,),
        in_specs=[pl.BlockSpec((N_E, ROUTE_TILES * TT), lambda i: (0, i))],
        out_specs=[pl.BlockSpec((ROUTE_TILES * TT, 8), lambda i: (i, 0)),
                   pl.BlockSpec((8, ROUTE_TILES * TT), lambda i: (0, i)),
                   pl.BlockSpec((ROUTE_TILES, 8, LIST_W), lambda i: (i, 0, 0)),
                   pl.BlockSpec((ROUTE_TILES, 8, CLASS_SLOTS), lambda i: (i, 0, 0)),
                   pl.BlockSpec((N_E, 8), lambda i: (0, 0))],
        out_shape=[jax.ShapeDtypeStruct((T, 8), F32),
                   jax.ShapeDtypeStruct((8, T), F32),
                   jax.ShapeDtypeStruct((NT, 8, LIST_W), jnp.int32),
                   jax.ShapeDtypeStruct((NT, 8, CLASS_SLOTS), jnp.int32),
                   jax.ShapeDtypeStruct((N_E, 8), F32)],
        scratch_shapes=[pltpu.VMEM((N_E, 1), F32)],
        compiler_params=pltpu.CompilerParams(dimension_semantics=("arbitrary",)),
        name="route",
    )(logits)


def _mix_call(zg, ya2d, x, mod3, ln_g, ln_b, ws_pairs, bias_full, wbb, wo, n2g, rw, rb):
    tok_spec = pl.BlockSpec((1, TS_MIX, D), lambda b, s: (b, s, 0))
    full = lambda *shape: pl.BlockSpec(shape, lambda b, s: (0,) * len(shape))
    mod_spec = lambda j: pl.BlockSpec((1, 1, D), lambda b, s: (b, 0, j))
    return pl.pallas_call(
        _mix_kernel,
        grid=(B, S // TS_MIX),
        in_specs=[pl.BlockSpec((1, TS_MIX, zg.shape[-1]), lambda b, s: (b, s, 0)), tok_spec, tok_spec,
                  mod_spec(2), mod_spec(3), mod_spec(4),
                  full(1, GM_W), full(1, GM_W),
                  full(GM_HEADS // 2, CHUNK, 2 * CHUNK),
                  full(CHUNK, GM_W),
                  full(GM_W, D), full(D, D), full(1, D),
                  full(N_E, D), full(N_E, 1)],
        out_specs=[tok_spec, tok_spec,
                   pl.BlockSpec((N_E, TS_MIX), lambda b, s: (0, b * (S // TS_MIX) + s))],
        out_shape=[jax.ShapeDtypeStruct((B, S, D), F32),
                   jax.ShapeDtypeStruct((B, S, D), BF16),
                   jax.ShapeDtypeStruct((N_E, T), F32)],
        compiler_params=pltpu.CompilerParams(vmem_limit_bytes=VMEM_LIMIT),
        name="gmlp_merge_norm2",
    )(zg, ya2d, x, mod3, mod3, mod3, ln_g, ln_b, ws_pairs, bias_full, wbb, wo, n2g, rw, rb)


def _pack_rows(v):
    return pltpu.pack_elementwise([v[:, :HALF], v[:, HALF:]], packed_dtype=BF16)


def _unpack_rows(w):
    halves = [pltpu.unpack_elementwise(w, index=i, packed_dtype=BF16, unpacked_dtype=F32)
              for i in range(2)]
    return jnp.concatenate(halves, axis=1)


def _load_grouped(ref, rows, first=0):
    return jnp.concatenate([ref[pl.ds(first * SUB + c, rows, stride=SUB), :] for c in range(SUB)], axis=1)


def _store_grouped(ref, w, rows, first=0):
    for c in range(SUB):
        ref[pl.ds(first * SUB + c, rows, stride=SUB), :] = w[:, c * LANES:(c + 1) * LANES]


def _start_tile_runs(tile, src_ref, dst_ref, src_tbl, dst_tbl, totals_tbl, sem):
    for ci, (rows, lane0) in enumerate(COPY_CLASSES):
        base = tile * LIST_W + lane0

        def start(i, carry, rows=rows, base=base):
            s = pl.multiple_of(src_tbl[base + i] * SUB, SUB)
            d = pl.multiple_of(dst_tbl[base + i] * SUB, SUB)
            pltpu.make_async_copy(src_ref.at[pl.ds(s, rows * SUB)],
                                  dst_ref.at[pl.ds(d, rows * SUB)], sem).start()
            return carry

        lax.fori_loop(0, totals_tbl[tile * CLASS_SLOTS + ci], start, 0)


def _wait_tile_runs(src_ref, dst_ref, sem):
    pltpu.make_async_copy(src_ref, dst_ref, sem).wait()


def _dispatch_kernel(loc_ref, glob_ref, tot_ref, pend_ref, h_ref, pt_ref, xs_ref,
                     sbuf0, sbuf1, zbuf, sem_z, sem0, sem1):
    j = pl.program_id(0)

    @pl.when(j == 0)
    def _():
        zbuf[...] = jnp.zeros_like(zbuf)
        for e in range(N_E):
            prev = pend_ref[e - 1] if e > 0 else 0
            end = pend_ref[e]

            @pl.when(end > prev)
            def _():
                first = pl.multiple_of((end - TM) * SUB, TM * SUB)
                cp = pltpu.make_async_copy(zbuf, xs_ref.at[pl.ds(first, TM * SUB)], sem_z)
                cp.start()
                cp.wait()

    rows = lax.broadcasted_iota(jnp.int32, (RS, TT), 0)
    words = []
    for t in range(2):
        pos = pt_ref[:, t * TT:(t + 1) * TT].astype(jnp.int32)
        hit = rows == pos[0:1, :]
        for k in range(1, TOP_K):
            hit = hit | (rows == pos[k:k + 1, :])
        pm = jnp.where(hit, 1.0, 0.0).astype(BF16)
        srt = jnp.dot(pm, h_ref[t * TT:(t + 1) * TT, :], preferred_element_type=F32)
        words.append(_pack_rows(srt))
    head = xs_ref.at[pl.ds(0, RS * SUB)]

    for t, (sbuf, sem) in enumerate(((sbuf0, sem0), (sbuf1, sem1))):
        @pl.when(j >= 1)
        def _(sbuf=sbuf, sem=sem):
            _wait_tile_runs(sbuf, head, sem)
        _store_grouped(sbuf, words[t], RS)
        _start_tile_runs(2 * j + t, sbuf, xs_ref, loc_ref, glob_ref, tot_ref, sem)

    @pl.when(j == NT // 2 - 1)
    def _():
        _wait_tile_runs(sbuf0, head, sem0)
        _wait_tile_runs(sbuf1, head, sem1)


def _dispatch_call(loc_t, glob_t, tot_t, pad_end, h2, pos_t):
    assert NT % 2 == 0
    grid_spec = pltpu.PrefetchScalarGridSpec(
        num_scalar_prefetch=4,
        grid=(NT // 2,),
        in_specs=[pl.BlockSpec((2 * TT, D), lambda j, *_: (j, 0)),
                  pl.BlockSpec((8, 2 * TT), lambda j, *_: (0, j))],
        out_specs=pl.BlockSpec(memory_space=pl.ANY),
        scratch_shapes=[pltpu.VMEM((RS * SUB, LANES), U32),
                        pltpu.VMEM((RS * SUB, LANES), U32),
                        pltpu.VMEM((TM * SUB, LANES), U32),
                        pltpu.SemaphoreType.DMA,
                        pltpu.SemaphoreType.DMA,
                        pltpu.SemaphoreType.DMA],
    )
    return pl.pallas_call(
        _dispatch_kernel,
        grid_spec=grid_spec,
        out_shape=jax.ShapeDtypeStruct((N_ROWS * SUB, LANES), U32),
        compiler_params=pltpu.CompilerParams(dimension_semantics=("arbitrary",),
                                             vmem_limit_bytes=VMEM_LIMIT),
        name="dispatch",
    )(loc_t, glob_t, tot_t, pad_end, h2, pos_t)


def _moe_kernel(be_ref, bf_ref, nx_ref, nv_ref, xs_ref, wi_hbm, bi_ref, wo_hbm, bo_ref, ys_ref,
                wi_f32, wo_f32, wi_bf, wo_bf, sem_i, sem_o):
    step = pl.program_id(0)

    def fetch(e):
        return (pltpu.make_async_copy(wi_hbm.at[e], wi_f32, sem_i),
                pltpu.make_async_copy(wo_hbm.at[e], wo_f32, sem_o))

    @pl.when(step == 0)
    def _():
        for cp in fetch(be_ref[0]):
            cp.start()

    def load_weights(i):
        @pl.when(bf_ref[i] == 1)
        def _():
            for cp in fetch(be_ref[i]):
                cp.wait()
            wi_bf[...] = wi_f32[...].astype(BF16)
            wo_bf[...] = wo_f32[...].astype(BF16)

            @pl.when(nx_ref[i] >= 0)
            def _():
                for cp in fetch(nx_ref[i]):
                    cp.start()

    def ffn(i, first, rows):
        e = be_ref[i]
        xb = _unpack_rows(_load_grouped(xs_ref, rows, first)).astype(BF16)
        gu = jnp.dot(xb, wi_bf[...], preferred_element_type=F32) + bi_ref[pl.ds(e, 1), :]
        gate = jnp.minimum(gu[:, :D_E], LIMIT)
        up = jnp.clip(gu[:, D_E:], -LIMIT, LIMIT)
        act = (up + 1.0) * (gate * _sigmoid(ALPHA * gate))
        y = jnp.dot(act.astype(BF16), wo_bf[...], preferred_element_type=F32) + bo_ref[pl.ds(e, 1), :]
        _store_grouped(ys_ref, _pack_rows(y), rows, first)

    i0 = step * BPS
    last = i0 + BPS - 1
    uniform = (last < nv_ref[0]) & (be_ref[i0] == be_ref[last])

    @pl.when(uniform)
    def _():
        load_weights(i0)
        ffn(i0, 0, BPS * TM)

    @pl.when(jnp.logical_not(uniform))
    def _():
        for sub in range(BPS):
            i = i0 + sub

            @pl.when(i < nv_ref[0])
            def _(i=i, sub=sub):
                load_weights(i)
                ffn(i, sub * TM, TM)


def _moe_call(blk_e, blk_first, blk_next, n_valid, xs, w_in, b_in, w_out, b_out):
    assert N_BLOCKS % BPS == 0

    def row_map(s, be, bf, nx, nv):
        last = (nv[0] + BPS - 1) // BPS - 1
        return (jnp.maximum(jnp.minimum(s, last), 0), 0)

    grid_spec = pltpu.PrefetchScalarGridSpec(
        num_scalar_prefetch=4,
        grid=(N_BLOCKS // BPS,),
        in_specs=[pl.BlockSpec((BPS * TM * SUB, LANES), row_map),
                  pl.BlockSpec(memory_space=pl.ANY),
                  pl.BlockSpec((N_E, 2 * D_E), lambda s, *_: (0, 0)),
                  pl.BlockSpec(memory_space=pl.ANY),
                  pl.BlockSpec((N_E, D), lambda s, *_: (0, 0))],
        out_specs=pl.BlockSpec((BPS * TM * SUB, LANES), row_map),
        scratch_shapes=[pltpu.VMEM((D, 2 * D_E), F32),
                        pltpu.VMEM((D_E, D), F32),
                        pltpu.VMEM((D, 2 * D_E), BF16),
                        pltpu.VMEM((D_E, D), BF16),
                        pltpu.SemaphoreType.DMA,
                        pltpu.SemaphoreType.DMA],
    )
    return pl.pallas_call(
        _moe_kernel,
        grid_spec=grid_spec,
        out_shape=jax.ShapeDtypeStruct((N_ROWS * SUB, LANES), U32),
        compiler_params=pltpu.CompilerParams(dimension_semantics=("arbitrary",),
                                             vmem_limit_bytes=VMEM_LIMIT),
        name="moe_experts",
    )(blk_e, blk_first, blk_next, n_valid, xs, w_in, b_in, w_out, b_out)


def _combine_kernel(loc_ref, glob_ref, tot_ref, ys_ref, pc_ref, x1_ref, gate2_ref, fg_ref, o_ref,
                    buf0, buf1, buf2, buf3, sem0, sem1, sem2, sem3):
    j = pl.program_id(0)
    head = ys_ref.at[pl.ds(0, RS * SUB)]
    even = ((buf0, sem0), (buf1, sem1))
    odd = ((buf2, sem2), (buf3, sem3))

    def fetch(step, slots):
        for t, (buf, sem) in enumerate(slots):
            _start_tile_runs(2 * step + t, ys_ref, buf, glob_ref, loc_ref, tot_ref, sem)

    @pl.when(j == 0)
    def _():
        fetch(0, even)

    col = lax.broadcasted_iota(jnp.int32, (TT, RS), 1)
    wms = []
    for t in range(2):
        pc = pc_ref[t * TT:(t + 1) * TT, :]
        pos = pc.astype(jnp.int32)
        wm = jnp.zeros((TT, RS), F32)
        for k in range(TOP_K):
            wm = jnp.where(col == pos[:, k:k + 1], pc[:, TOP_K + k:TOP_K + k + 1], wm)
        wms.append(wm.astype(BF16))

    def step(cur, nxt):
        @pl.when(j + 1 < NT // 2)
        def _():
            fetch(j + 1, nxt)
        for t, (buf, sem) in enumerate(cur):
            _wait_tile_runs(head, buf, sem)
            yt = _unpack_rows(_load_grouped(buf, RS)).astype(BF16)
            acc = jnp.dot(wms[t], yt, preferred_element_type=F32)
            x2 = x1_ref[t * TT:(t + 1) * TT, :] + gate2_ref[0] * acc
            o_ref[t * TT:(t + 1) * TT, :] = _rms(x2) * fg_ref[...]

    @pl.when(j % 2 == 0)
    def _():
        step(even, odd)

    @pl.when(j % 2 == 1)
    def _():
        step(odd, even)


def _combine_call(loc_t, glob_t, tot_t, ys, pos_c, x1, mod3, final_g):
    per_b = S // (2 * TT)
    grid_spec = pltpu.PrefetchScalarGridSpec(
        num_scalar_prefetch=3,
        grid=(NT // 2,),
        in_specs=[pl.BlockSpec(memory_space=pl.ANY),
                  pl.BlockSpec((2 * TT, 8), lambda j, *_: (j, 0)),
                  pl.BlockSpec((2 * TT, D), lambda j, *_: (j, 0)),
                  pl.BlockSpec((1, 1, D), lambda j, *_: (j // per_b, 0, 5)),
                  pl.BlockSpec((1, D), lambda j, *_: (0, 0))],
        out_specs=pl.BlockSpec((2 * TT, D), lambda j, *_: (j, 0)),
        scratch_shapes=[pltpu.VMEM((RS * SUB, LANES), U32)] * 4 + [pltpu.SemaphoreType.DMA] * 4,
    )
    return pl.pallas_call(
        _combine_kernel,
        grid_spec=grid_spec,
        out_shape=jax.ShapeDtypeStruct((T, D), F32),
        compiler_params=pltpu.CompilerParams(dimension_semantics=("arbitrary",),
                                             vmem_limit_bytes=VMEM_LIMIT),
        name="combine_norm",
    )(loc_t, glob_t, tot_t, ys, pos_c, x1, mod3, final_g)


def kernel(x, c, ada_w, ada_b, norm1_g, w_in, ssm_a_re, ssm_a_im, ssm_log_dt, ssm_b_re, ssm_b_im, ssm_c_re, ssm_c_im, ssm_d, ssm_glu_w, ssm_glu_b, w_branch_a, gmlp_ln_g, gmlp_ln_b, gmlp_ws, gmlp_bs, w_branch_b, w_out, norm2_g, router_w, router_b, moe_w_in, moe_b_in, moe_w_out, moe_b_out, final_g):
    depth = ada_w.shape[0]
    assert depth == 1, "the final rms_norm is fused into the combine kernel of the only layer"
    for layer in range(depth):
        mod = _mod_call(c, ada_w[layer], ada_b[layer])
        mod3 = mod.reshape(B, 1, 6 * D)

        u, zg = _proj_call(x, norm1_g[layer], mod3, w_in[layer].astype(BF16))

        bm, cre, cim, are, aim = _s5_params(ssm_a_re[layer], ssm_a_im[layer], ssm_log_dt[layer],
                                            ssm_b_re[layer], ssm_b_im[layer],
                                            ssm_c_re[layer], ssm_c_im[layer])
        ya = _s5_call(u, bm, cre, cim, are, aim,
                      ssm_d[layer].reshape(1, SSM_W), ssm_glu_w[layer].astype(BF16),
                      ssm_glu_b[layer].reshape(1, SSM_W), w_branch_a[layer].astype(BF16))

        ws = gmlp_ws[layer]
        ws_pairs = jnp.concatenate([ws[0::2], ws[1::2]], axis=-1)
        bias_full = jnp.repeat(gmlp_bs[layer].T, GM_HD, axis=1)
        x1, h2, logits = _mix_call(
            zg, ya, x, mod3,
            gmlp_ln_g[layer].reshape(1, GM_W), gmlp_ln_b[layer].reshape(1, GM_W),
            ws_pairs, bias_full, w_branch_b[layer].astype(BF16), w_out[layer].astype(BF16),
            norm2_g[layer].reshape(1, D), router_w[layer].T.astype(BF16),
            router_b[layer].reshape(N_E, 1))

        pos_c, pos_t, lists, totals, cnt = _route_call(logits)
        counts = cnt[:, 0].astype(jnp.int32)
        nblk = (counts + TM - 1) // TM
        blk_end = jnp.cumsum(nblk)
        pad_end = (blk_end * TM).astype(jnp.int32)
        experts = jnp.arange(N_E, dtype=jnp.int32)
        blk_ids = jnp.arange(N_BLOCKS, dtype=jnp.int32)
        blk_e = jnp.sum((blk_end[None, :] <= blk_ids[:, None]).astype(jnp.int32), axis=1)
        blk_e = jnp.minimum(blk_e, N_E - 1)
        blk_first = jnp.concatenate([jnp.ones((1,), jnp.int32),
                                     (blk_e[1:] != blk_e[:-1]).astype(jnp.int32)])
        later = (experts[None, :] > experts[:, None]) & (nblk[None, :] > 0)
        next_e = jnp.min(jnp.where(later, experts[None, :], N_E), axis=1)
        next_e = jnp.where(next_e == N_E, -1, next_e)
        blk_next = jnp.sum(jnp.where(blk_e[:, None] == experts[None, :], next_e[None, :], 0), axis=1)
        n_valid = blk_end[-1:].astype(jnp.int32)
        pad_start = pad_end - nblk * TM
        loc_t = lists[:, 0, :].reshape(NT * LIST_W)
        owner = lists[:, 2, :, None] == experts[None, None, :]
        glob_t = (lists[:, 1, :] + jnp.sum(jnp.where(owner, pad_start, 0), axis=-1)).reshape(NT * LIST_W)
        tot_t = totals[:, 0, :].reshape(NT * CLASS_SLOTS)

        xs = _dispatch_call(loc_t, glob_t, tot_t, pad_end, h2.reshape(T, D), pos_t)
        ys = _moe_call(blk_e, blk_first, blk_next.astype(jnp.int32), n_valid, xs, moe_w_in[layer],
                       moe_b_in[layer], moe_w_out[layer], moe_b_out[layer])
        x = _combine_call(loc_t, glob_t, tot_t, ys, pos_c, x1.reshape(T, D),
                          mod3, final_g.reshape(1, D)).reshape(B, S, D)
    return x
```

```python
import math

import jax
import jax.numpy as jnp
from jax import lax
from jax.experimental import pallas as pl
from jax.experimental.pallas import tpu as pltpu

F32 = jnp.float32
BF16 = jnp.bfloat16

D = 1024
B = 8
S = 2048
T = B * S
SSM_W = 512
SSM_G = 32
SSM_H = 16
SSM_P = 64
N_PACK = 4
PACK_G = SSM_G // N_PACK
GM_W = 512
GM_HEADS = 8
GM_HD = 64
CHUNK = 128
N_E = 32
TOP_K = 4
D_E = 1024
LIMIT = 7.0
ALPHA = 1.702
EPS = 1e-6

TS_PROJ = 512
L_SSM = 128
R_SSM = L_SSM * B
TS_MIX = 512
SUB_MIX = 256
TT = 256
NT = T // TT
ROUTE_TILES = 4
RS = TOP_K * TT
TM = 256
BPS = 4
N_ROWS = T * TOP_K + N_E * TM
N_BLOCKS = N_ROWS // TM
LANES = 128
HALF = D // 2
SUB = HALF // LANES
PIECE = 32
COPY_CLASSES = ((PIECE, 0),) + tuple((PIECE >> s, RS // PIECE + N_E * (s - 1))
                                     for s in range(1, PIECE.bit_length()))
LIST_W = 256
CLASS_SLOTS = 8
assert COPY_CLASSES[-1][1] + N_E <= LIST_W and len(COPY_CLASSES) <= CLASS_SLOTS
VMEM_LIMIT = 56 * 1024 * 1024
U32 = jnp.uint32
ACT = jnp.bfloat16


def _sigmoid(v):
    return 0.5 * jnp.tanh(0.5 * v) + 0.5


def _gelu(v):
    c = math.sqrt(2.0 / math.pi)
    inner = v * (c + (c * 0.044715) * (v * v))
    return v * (0.5 + 0.5 * jnp.tanh(inner))


def _rms(v):
    return v * lax.rsqrt(jnp.mean(v * v, axis=-1, keepdims=True) + EPS)


def _mod_kernel(c_ref, w_ref, b_ref, o_ref):
    cv = c_ref[...]
    sv = cv * _sigmoid(cv)
    o_ref[...] = jnp.dot(sv, w_ref[...], preferred_element_type=F32,
                         precision=lax.Precision.HIGHEST) + b_ref[...]


def _mod_call(c, ada_w, ada_b):
    n = ada_w.shape[1]
    return pl.pallas_call(
        _mod_kernel,
        grid=(n // D,),
        in_specs=[pl.BlockSpec((B, D), lambda j: (0, 0)),
                  pl.BlockSpec((D, D), lambda j: (0, j)),
                  pl.BlockSpec((1, D), lambda j: (0, j))],
        out_specs=pl.BlockSpec((B, D), lambda j: (0, j)),
        out_shape=jax.ShapeDtypeStruct((B, n), F32),
        name="adaln_mod",
    )(c, ada_w, ada_b.reshape(1, n))


def _proj_kernel(x_ref, g_ref, shift_ref, scale_ref, w_ref, u_ref, zg_ref):
    h = _rms(x_ref[0]) * (g_ref[...] * (1.0 + scale_ref[0])) + shift_ref[0]
    hb = h.astype(BF16)
    u_ref[0] = jnp.dot(hb, w_ref[:, 0:SSM_W], preferred_element_type=F32)
    zg_ref[0] = jnp.dot(hb, w_ref[:, SSM_W:], preferred_element_type=F32).astype(ACT)


def _proj_call(x, norm_g, mod3, w_in_bf):
    pw = w_in_bf.shape[1]
    tok_spec = pl.BlockSpec((1, TS_PROJ, D), lambda b, s: (b, s, 0))
    return pl.pallas_call(
        _proj_kernel,
        grid=(B, S // TS_PROJ),
        in_specs=[tok_spec,
                  pl.BlockSpec((1, D), lambda b, s: (0, 0)),
                  pl.BlockSpec((1, 1, D), lambda b, s: (b, 0, 0)),
                  pl.BlockSpec((1, 1, D), lambda b, s: (b, 0, 1)),
                  pl.BlockSpec((D, pw), lambda b, s: (0, 0))],
        out_specs=[pl.BlockSpec((1, TS_PROJ, SSM_W), lambda b, s: (b, s, 0)),
                   pl.BlockSpec((1, TS_PROJ, pw - SSM_W), lambda b, s: (b, s, 0))],
        out_shape=[jax.ShapeDtypeStruct((B, S, SSM_W), F32),
                   jax.ShapeDtypeStruct((B, S, pw - SSM_W), ACT)],
        compiler_params=pltpu.CompilerParams(vmem_limit_bytes=VMEM_LIMIT),
        name="norm_proj",
    )(x, norm_g.reshape(1, D), mod3, mod3, w_in_bf)


def _s5_kernel(u_ref, bm_ref, cre_ref, cim_ref, are_ref, aim_ref, d_ref, gw_ref, gb_ref, wa_ref,
               o_ref, usc, ysc, sre, sim, st_re, st_im):
    @pl.when(pl.program_id(0) == 0)
    def _():
        st_re[...] = jnp.zeros_like(st_re)
        st_im[...] = jnp.zeros_like(st_im)

    nslab = SSM_W // LANES
    for b in range(B):
        for c in range(nslab):
            usc[c, pl.ds(b, L_SSM, stride=B), :] = u_ref[b, :, c * LANES:(c + 1) * LANES]
    u = jnp.concatenate([usc[c] for c in range(nslab)], axis=1)
    ub = u.astype(BF16)
    half = PACK_G * SSM_P
    ys = []
    for k in range(N_PACK):
        pin = PACK_G * SSM_H
        bu = jnp.dot(ub[:, pin * k:pin * (k + 1)], bm_ref[k], preferred_element_type=F32)
        sre[k] = bu[:, :half]
        sim[k] = bu[:, half:]
        ar = are_ref[k]
        ai = aim_ref[k]
        r = st_re[k]
        m = st_im[k]
        for t in range(L_SSM):
            rows = pl.ds(t * B, B)
            nr = ar * r - ai * m + sre[k, rows, :]
            m = ar * m + ai * r + sim[k, rows, :]
            r = nr
            sre[k, rows, :] = r
            sim[k, rows, :] = m
        st_re[k] = r
        st_im[k] = m
        yk = jnp.dot(sre[k].astype(BF16), cre_ref[k], preferred_element_type=F32)
        yk = yk + jnp.dot(sim[k].astype(BF16), cim_ref[k], preferred_element_type=F32)
        ys.extend(yk[:, j * LANES:(j + 1) * LANES] for j in range(pin // LANES))
    for c in range(nslab):
        uc = usc[c]
        ysc[c] = ys[c] + d_ref[:, c * LANES:(c + 1) * LANES] * uc
    y = jnp.concatenate(
        [jnp.concatenate([ysc[c, pl.ds(b, L_SSM, stride=B), :] for c in range(nslab)], axis=1)
         for b in range(B)], axis=0)
    z = _gelu(y)
    gl = jnp.dot(z.astype(BF16), gw_ref[...], preferred_element_type=F32) + gb_ref[...]
    out = z * _sigmoid(gl)
    o = jnp.dot(out.astype(BF16), wa_ref[...], preferred_element_type=F32)
    for b in range(B):
        o_ref[b] = o[b * L_SSM:(b + 1) * L_SSM].astype(ACT)


def _s5_call(u, bm, cre, cim, are, aim, d_skip, glu_w, glu_b, w_a):
    half = PACK_G * SSM_P
    full = lambda *shape: pl.BlockSpec(shape, lambda i: (0,) * len(shape))
    return pl.pallas_call(
        _s5_kernel,
        grid=(S // L_SSM,),
        in_specs=[pl.BlockSpec((B, L_SSM, SSM_W), lambda i: (0, i, 0)),
                  full(N_PACK, PACK_G * SSM_H, 2 * half),
                  full(N_PACK, half, PACK_G * SSM_H),
                  full(N_PACK, half, PACK_G * SSM_H),
                  full(N_PACK, B, half),
                  full(N_PACK, B, half),
                  full(1, SSM_W),
                  full(SSM_W, SSM_W),
                  full(1, SSM_W),
                  full(SSM_W, D)],
        out_specs=pl.BlockSpec((B, L_SSM, D), lambda i: (0, i, 0)),
        out_shape=jax.ShapeDtypeStruct((B, S, D), ACT),
        scratch_shapes=[pltpu.VMEM((SSM_W // LANES, R_SSM, LANES), F32),
                        pltpu.VMEM((SSM_W // LANES, R_SSM, LANES), F32),
                        pltpu.VMEM((N_PACK, R_SSM, half), F32),
                        pltpu.VMEM((N_PACK, R_SSM, half), F32),
                        pltpu.VMEM((N_PACK, B, half), F32),
                        pltpu.VMEM((N_PACK, B, half), F32)],
        compiler_params=pltpu.CompilerParams(dimension_semantics=("arbitrary",),
                                             vmem_limit_bytes=VMEM_LIMIT),
        name="s5_branch",
    )(u, bm, cre, cim, are, aim, d_skip, glu_w, glu_b, w_a)


def _s5_params(a_re, a_im, log_dt, b_re, b_im, c_re, c_im):
    dt = jnp.exp(log_dt)[:, None]
    mag = jnp.exp(a_re * dt)
    lr = mag * jnp.cos(a_im * dt)
    li = mag * jnp.sin(a_im * dt)
    den = a_re * a_re + a_im * a_im
    cr = ((lr - 1.0) * a_re + li * a_im) / den
    ci = (li * a_re - (lr - 1.0) * a_im) / den
    bbr = cr[..., None] * b_re - ci[..., None] * b_im
    bbi = cr[..., None] * b_im + ci[..., None] * b_re
    eye = jnp.eye(PACK_G, dtype=F32)
    half = PACK_G * SSM_P

    def pack_b(m):
        m4 = m.reshape(N_PACK, PACK_G, SSM_P, SSM_H)
        return jnp.einsum('kgph,gj->kghjp', m4, eye).reshape(N_PACK, PACK_G * SSM_H, half)

    def pack_c(m):
        m4 = m.reshape(N_PACK, PACK_G, SSM_H, SSM_P)
        return jnp.einsum('kghp,gj->kgpjh', m4, eye).reshape(N_PACK, half, PACK_G * SSM_H)

    bm = jnp.concatenate([pack_b(bbr), pack_b(bbi)], axis=-1).astype(BF16)
    cre = pack_c(c_re).astype(BF16)
    cim = (-pack_c(c_im)).astype(BF16)
    are = jnp.broadcast_to(lr.reshape(N_PACK, 1, half), (N_PACK, B, half))
    aim = jnp.broadcast_to(li.reshape(N_PACK, 1, half), (N_PACK, B, half))
    return bm, cre, cim, are, aim


def _route_tile(lt, carry):
    sub = lax.broadcasted_iota(jnp.int32, lt.shape, 0).astype(F32)
    sels, vals = [], []
    for _ in range(TOP_K):
        m = jnp.max(lt, axis=0, keepdims=True)
        idx = jnp.min(jnp.where(lt == m, sub, float(N_E)), axis=0, keepdims=True)
        sel = sub == idx
        sels.append(sel)
        vals.append(m)
        lt = jnp.where(sel, -jnp.inf, lt)
    member = sels[0].astype(F32)
    for k in range(1, TOP_K):
        member = member + sels[k].astype(F32)
    n_col = jnp.sum(member, axis=1, keepdims=True)

    re = lax.broadcasted_iota(jnp.int32, (N_E, N_E), 0)
    ce = lax.broadcasted_iota(jnp.int32, (N_E, N_E), 1)
    nb = jnp.broadcast_to(n_col, (N_E, 8)).astype(BF16)
    seg_col = jnp.dot((ce < re).astype(BF16), nb, preferred_element_type=F32)[:, 0:1]
    rt = lax.broadcasted_iota(jnp.int32, (TT, TT), 0)
    ct = lax.broadcasted_iota(jnp.int32, (TT, TT), 1)
    rank = jnp.dot(member.astype(BF16), (rt < ct).astype(BF16), preferred_element_type=F32)
    posb = seg_col + rank
    denom = jnp.zeros_like(vals[0])
    exps = []
    for k in range(TOP_K):
        e = jnp.exp(vals[k] - vals[0])
        exps.append(e)
        denom = denom + e
    s8 = lax.broadcasted_iota(jnp.int32, (8, TT), 0)
    pt = jnp.zeros((8, TT), F32)
    for k in range(TOP_K):
        pk = jnp.sum(jnp.where(sels[k], posb, 0.0), axis=0, keepdims=True)
        pt = jnp.where(s8 == k, pk, pt)
        pt = jnp.where(s8 == TOP_K + k, exps[k] / denom, pt)
    pc = pt.T
    earlier_col = carry[...]
    carry[...] = earlier_col + n_col
    lane8 = lax.broadcasted_iota(jnp.int32, (N_E, 8), 1)
    cols = jnp.where(lane8 == 0, n_col, jnp.where(lane8 == 1, seg_col, jnp.where(lane8 == 2, earlier_col, 0.0)))
    rows = cols.T
    lists, totals = _copy_lists(n_col.astype(jnp.int32), rows[0:1], rows[1:2], rows[2:3])
    return pc, pt, lists, totals


def _copy_lists(n_col, n_row_f, seg, earlier):
    shift = PIECE.bit_length() - 1

    def pieces(n, ci):
        if ci == 0:
            return jnp.right_shift(n, shift)
        return jnp.bitwise_and(jnp.right_shift(n, shift - ci), 1)

    def done(n, ci):
        if ci == 0:
            return jnp.zeros_like(n)
        return n - jnp.bitwise_and(n, (PIECE >> (ci - 1)) - 1)

    re = lax.broadcasted_iota(jnp.int32, (N_E, N_E), 0)
    ce = lax.broadcasted_iota(jnp.int32, (N_E, N_E), 1)
    n_row = n_row_f.astype(jnp.int32)
    lane8 = lax.broadcasted_iota(jnp.int32, (N_E, CLASS_SLOTS), 1)
    x = jnp.zeros((N_E, CLASS_SLOTS), F32)
    for ci in range(len(COPY_CLASSES)):
        x = jnp.where(lane8 == ci, pieces(n_col, ci).astype(F32), x)
    xb = x.astype(BF16)
    before = jnp.dot((ce < re).astype(BF16), xb, preferred_element_type=F32)
    totals = jnp.dot(jnp.ones((8, N_E), BF16), xb, preferred_element_type=F32)
    bulk_row = jnp.broadcast_to(pieces(n_row, 0).astype(F32), (8, N_E)).astype(BF16)
    before_bulk_row = jnp.dot(bulk_row, (re < ce).astype(BF16), preferred_element_type=F32)[0:1]

    lanes = lax.broadcasted_iota(jnp.int32, (N_E, LIST_W), 1)
    sub = lax.broadcasted_iota(jnp.int32, (8, N_E), 0)
    e_row = lax.broadcasted_iota(jnp.int32, (8, N_E), 1).astype(F32)
    lists = jnp.zeros((8, LIST_W), F32)
    for ci, (_, lane0) in enumerate(COPY_CLASSES):
        first = before[:, ci:ci + 1].astype(jnp.int32) + lane0
        sel = (lanes >= first) & (lanes < first + pieces(n_col, ci))
        d_row = done(n_row, ci).astype(F32)
        if ci == 0:
            d_row = d_row - PIECE * before_bulk_row
        v = jnp.where(sub == 0, seg + d_row, jnp.where(sub == 1, earlier + d_row,
                                                       jnp.where(sub == 2, e_row, 0.0)))
        lists = lists + jnp.dot(v, sel.astype(F32), preferred_element_type=F32,
                                precision=lax.Precision.HIGHEST)
    q = lax.broadcasted_iota(jnp.int32, (8, LIST_W), 1)
    s8 = lax.broadcasted_iota(jnp.int32, (8, LIST_W), 0)
    lists = lists + jnp.where((s8 < 2) & (q < RS // PIECE), (PIECE * q).astype(F32), 0.0)
    return lists, totals


def _mix_kernel(zg_ref, ya_ref, x_ref, gate1_ref, shift2_ref, scale2_ref,
                lng_ref, lnb_ref, ws_ref, bias_ref, wbb_ref, wo_ref, n2g_ref, rw_ref, rb_ref,
                x1_ref, h2_ref, lg_ref):
    row = lax.broadcasted_iota(jnp.int32, (CHUNK, 2 * CHUNK), 0)
    col = lax.broadcasted_iota(jnp.int32, (CHUNK, 2 * CHUNK), 1)
    causal = (col % CHUNK) <= row
    lane = lax.broadcasted_iota(jnp.int32, (CHUNK, 2 * GM_HD), 1)
    first = lane < GM_HD
    wpairs = [jnp.where(causal, ws_ref[j], 0.0).astype(BF16) for j in range(GM_HEADS // 2)]

    for g in range(TS_MIX // SUB_MIX):
        rows = pl.ds(g * SUB_MIX, SUB_MIX)
        z = _gelu(zg_ref[0, rows, 0:2 * GM_W].astype(F32))
        u = z[:, :GM_W]
        v = z[:, GM_W:]
        mu = jnp.mean(v, axis=-1, keepdims=True)
        vc = v - mu
        var = jnp.mean(vc * vc, axis=-1, keepdims=True)
        vn = vc * lax.rsqrt(var + EPS) * lng_ref[...] + lnb_ref[...]
        chunks = []
        for n in range(SUB_MIX // CHUNK):
            cols = []
            for j in range(GM_HEADS // 2):
                vp = vn[n * CHUNK:(n + 1) * CHUNK, 2 * GM_HD * j:2 * GM_HD * (j + 1)]
                rhs = jnp.concatenate([jnp.where(first, vp, 0.0), jnp.where(first, 0.0, vp)], axis=0)
                cols.append(jnp.dot(wpairs[j], rhs.astype(BF16), preferred_element_type=F32))
            chunks.append(jnp.concatenate(cols, axis=1) + bias_ref[...])
        mixed = jnp.concatenate(chunks, axis=0)
        gm = u * mixed
        yb = jnp.dot(gm.astype(BF16), wbb_ref[...], preferred_element_type=F32)
        g_a = zg_ref[0, rows, 2 * GM_W:2 * GM_W + D].astype(F32)
        g_b = zg_ref[0, rows, 2 * GM_W + D:].astype(F32)
        merged = _sigmoid(g_a) * ya_ref[0, rows, :].astype(F32) + _sigmoid(g_b) * yb
        o = jnp.dot(merged.astype(BF16), wo_ref[...], preferred_element_type=F32)
        x1 = x_ref[0, rows, :] + gate1_ref[0] * o
        x1_ref[0, rows, :] = x1
        h2 = _rms(x1) * (n2g_ref[...] * (1.0 + scale2_ref[0])) + shift2_ref[0]
        hb = h2.astype(BF16)
        h2_ref[0, rows, :] = hb
        lg_ref[:, g * SUB_MIX:(g + 1) * SUB_MIX] = lax.dot_general(
            rw_ref[...], hb, (((1,), (1,)), ((), ())), preferred_element_type=F32) + rb_ref[...]


def _route_kernel(lg_ref, pc_ref, pt_ref, lists_ref, totals_ref, cnt_ref, carry):
    @pl.when(pl.program_id(0) == 0)
    def _():
        carry[...] = jnp.zeros_like(carry)

    for t in range(ROUTE_TILES):
        pc, pt, lists, totals = _route_tile(lg_ref[:, t * TT:(t + 1) * TT], carry)
        pc_ref[t * TT:(t + 1) * TT, :] = pc
        pt_ref[:, t * TT:(t + 1) * TT] = pt
        lists_ref[t] = lists.astype(jnp.int32)
        totals_ref[t] = totals.astype(jnp.int32)
    cnt_ref[...] = jnp.broadcast_to(carry[...], cnt_ref.shape)


def _route_call(logits):
    assert NT % ROUTE_TILES == 0
    return pl.pallas_call(
        _route_kernel,
        grid=(NT // ROUTE_TILES,),
        in_specs=[pl.BlockSpec((N_E, ROUTE_TILES * TT), lambda i: (0, i))],
        out_specs=[pl.BlockSpec((ROUTE_TILES * TT, 8), lambda i: (i, 0)),
                   pl.BlockSpec((8, ROUTE_TILES * TT), lambda i: (0, i)),
                   pl.BlockSpec((ROUTE_TILES, 8, LIST_W), lambda i: (i, 0, 0)),
                   pl.BlockSpec((ROUTE_TILES, 8, CLASS_SLOTS), lambda i: (i, 0, 0)),
                   pl.BlockSpec((N_E, 8), lambda i: (0, 0))],
        out_shape=[jax.ShapeDtypeStruct((T, 8), F32),
                   jax.ShapeDtypeStruct((8, T), F32),
                   jax.ShapeDtypeStruct((NT, 8, LIST_W), jnp.int32),
                   jax.ShapeDtypeStruct((NT, 8, CLASS_SLOTS), jnp.int32),
                   jax.ShapeDtypeStruct((N_E, 8), F32)],
        scratch_shapes=[pltpu.VMEM((N_E, 1), F32)],
        compiler_params=pltpu.CompilerParams(dimension_semantics=("arbitrary",)),
        name="route",
    )(logits)


def _mix_call(zg, ya2d, x, mod3, ln_g, ln_b, ws_pairs, bias_full, wbb, wo, n2g, rw, rb):
    tok_spec = pl.BlockSpec((1, TS_MIX, D), lambda b, s: (b, s, 0))
    full = lambda *shape: pl.BlockSpec(shape, lambda b, s: (0,) * len(shape))
    mod_spec = lambda j: pl.BlockSpec((1, 1, D), lambda b, s: (b, 0, j))
    return pl.pallas_call(
        _mix_kernel,
        grid=(B, S // TS_MIX),
        in_specs=[pl.BlockSpec((1, TS_MIX, zg.shape[-1]), lambda b, s: (b, s, 0)), tok_spec, tok_spec,
                  mod_spec(2), mod_spec(3), mod_spec(4),
                  full(1, GM_W), full(1, GM_W),
                  full(GM_HEADS // 2, CHUNK, 2 * CHUNK),
                  full(CHUNK, GM_W),
                  full(GM_W, D), full(D, D), full(1, D),
                  full(N_E, D), full(N_E, 1)],
        out_specs=[tok_spec, tok_spec,
                   pl.BlockSpec((N_E, TS_MIX), lambda b, s: (0, b * (S // TS_MIX) + s))],
        out_shape=[jax.ShapeDtypeStruct((B, S, D), F32),
                   jax.ShapeDtypeStruct((B, S, D), BF16),
                   jax.ShapeDtypeStruct((N_E, T), F32)],
        compiler_params=pltpu.CompilerParams(vmem_limit_bytes=VMEM_LIMIT),
        name="gmlp_merge_norm2",
    )(zg, ya2d, x, mod3, mod3, mod3, ln_g, ln_b, ws_pairs, bias_full, wbb, wo, n2g, rw, rb)


def _pack_rows(v):
    return pltpu.pack_elementwise([v[:, :HALF], v[:, HALF:]], packed_dtype=BF16)


def _unpack_rows(w):
    halves = [pltpu.unpack_elementwise(w, index=i, packed_dtype=BF16, unpacked_dtype=F32)
              for i in range(2)]
    return jnp.concatenate(halves, axis=1)


def _load_grouped(ref, rows, first=0):
    return jnp.concatenate([ref[pl.ds(first * SUB + c, rows, stride=SUB), :] for c in range(SUB)], axis=1)


def _store_grouped(ref, w, rows, first=0):
    for c in range(SUB):
        ref[pl.ds(first * SUB + c, rows, stride=SUB), :] = w[:, c * LANES:(c + 1) * LANES]


def _start_tile_runs(tile, src_ref, dst_ref, src_tbl, dst_tbl, totals_tbl, sem):
    for ci, (rows, lane0) in enumerate(COPY_CLASSES):
        base = tile * LIST_W + lane0

        def start(i, carry, rows=rows, base=base):
            s = pl.multiple_of(src_tbl[base + i] * SUB, SUB)
            d = pl.multiple_of(dst_tbl[base + i] * SUB, SUB)
            pltpu.make_async_copy(src_ref.at[pl.ds(s, rows * SUB)],
                                  dst_ref.at[pl.ds(d, rows * SUB)], sem).start()
            return carry

        lax.fori_loop(0, totals_tbl[tile * CLASS_SLOTS + ci], start, 0)


def _wait_tile_runs(src_ref, dst_ref, sem):
    pltpu.make_async_copy(src_ref, dst_ref, sem).wait()


def _dispatch_kernel(loc_ref, glob_ref, tot_ref, pend_ref, h_ref, pt_ref, xs_ref,
                     sbuf0, sbuf1, zbuf, sem_z, sem0, sem1):
    j = pl.program_id(0)

    @pl.when(j == 0)
    def _():
        zbuf[...] = jnp.zeros_like(zbuf)
        for e in range(N_E):
            prev = pend_ref[e - 1] if e > 0 else 0
            end = pend_ref[e]

            @pl.when(end > prev)
            def _():
                first = pl.multiple_of((end - TM) * SUB, TM * SUB)
                cp = pltpu.make_async_copy(zbuf, xs_ref.at[pl.ds(first, TM * SUB)], sem_z)
                cp.start()
                cp.wait()

    rows = lax.broadcasted_iota(jnp.int32, (RS, TT), 0)
    words = []
    for t in range(2):
        pos = pt_ref[:, t * TT:(t + 1) * TT].astype(jnp.int32)
        hit = rows == pos[0:1, :]
        for k in range(1, TOP_K):
            hit = hit | (rows == pos[k:k + 1, :])
        pm = jnp.where(hit, 1.0, 0.0).astype(BF16)
        srt = jnp.dot(pm, h_ref[t * TT:(t + 1) * TT, :], preferred_element_type=F32)
        words.append(_pack_rows(srt))
    head = xs_ref.at[pl.ds(0, RS * SUB)]

    for t, (sbuf, sem) in enumerate(((sbuf0, sem0), (sbuf1, sem1))):
        @pl.when(j >= 1)
        def _(sbuf=sbuf, sem=sem):
            _wait_tile_runs(sbuf, head, sem)
        _store_grouped(sbuf, words[t], RS)
        _start_tile_runs(2 * j + t, sbuf, xs_ref, loc_ref, glob_ref, tot_ref, sem)

    @pl.when(j == NT // 2 - 1)
    def _():
        _wait_tile_runs(sbuf0, head, sem0)
        _wait_tile_runs(sbuf1, head, sem1)


def _dispatch_call(loc_t, glob_t, tot_t, pad_end, h2, pos_t):
    assert NT % 2 == 0
    grid_spec = pltpu.PrefetchScalarGridSpec(
        num_scalar_prefetch=4,
        grid=(NT // 2,),
        in_specs=[pl.BlockSpec((2 * TT, D), lambda j, *_: (j, 0)),
                  pl.BlockSpec((8, 2 * TT), lambda j, *_: (0, j))],
        out_specs=pl.BlockSpec(memory_space=pl.ANY),
        scratch_shapes=[pltpu.VMEM((RS * SUB, LANES), U32),
                        pltpu.VMEM((RS * SUB, LANES), U32),
                        pltpu.VMEM((TM * SUB, LANES), U32),
                        pltpu.SemaphoreType.DMA,
                        pltpu.SemaphoreType.DMA,
                        pltpu.SemaphoreType.DMA],
    )
    return pl.pallas_call(
        _dispatch_kernel,
        grid_spec=grid_spec,
        out_shape=jax.ShapeDtypeStruct((N_ROWS * SUB, LANES), U32),
        compiler_params=pltpu.CompilerParams(dimension_semantics=("arbitrary",),
                                             vmem_limit_bytes=VMEM_LIMIT),
        name="dispatch",
    )(loc_t, glob_t, tot_t, pad_end, h2, pos_t)


def _moe_kernel(be_ref, bf_ref, nx_ref, nv_ref, xs_ref, wi_hbm, bi_ref, wo_hbm, bo_ref, ys_ref,
                wi_f32, wo_f32, wi_bf, wo_bf, sem_i, sem_o):
    step = pl.program_id(0)

    def fetch(e):
        return (pltpu.make_async_copy(wi_hbm.at[e], wi_f32, sem_i),
                pltpu.make_async_copy(wo_hbm.at[e], wo_f32, sem_o))

    @pl.when(step == 0)
    def _():
        for cp in fetch(be_ref[0]):
            cp.start()

    def load_weights(i):
        @pl.when(bf_ref[i] == 1)
        def _():
            for cp in fetch(be_ref[i]):
                cp.wait()
            wi_bf[...] = wi_f32[...].astype(BF16)
            wo_bf[...] = wo_f32[...].astype(BF16)

            @pl.when(nx_ref[i] >= 0)
            def _():
                for cp in fetch(nx_ref[i]):
                    cp.start()

    def ffn(i, first, rows):
        e = be_ref[i]
        xb = _unpack_rows(_load_grouped(xs_ref, rows, first)).astype(BF16)
        gu = jnp.dot(xb, wi_bf[...], preferred_element_type=F32) + bi_ref[pl.ds(e, 1), :]
        gate = jnp.minimum(gu[:, :D_E], LIMIT)
        up = jnp.clip(gu[:, D_E:], -LIMIT, LIMIT)
        act = (up + 1.0) * (gate * _sigmoid(ALPHA * gate))
        y = jnp.dot(act.astype(BF16), wo_bf[...], preferred_element_type=F32) + bo_ref[pl.ds(e, 1), :]
        _store_grouped(ys_ref, _pack_rows(y), rows, first)

    i0 = step * BPS
    last = i0 + BPS - 1
    uniform = (last < nv_ref[0]) & (be_ref[i0] == be_ref[last])

    @pl.when(uniform)
    def _():
        load_weights(i0)
        ffn(i0, 0, BPS * TM)

    @pl.when(jnp.logical_not(uniform))
    def _():
        for sub in range(BPS):
            i = i0 + sub

            @pl.when(i < nv_ref[0])
            def _(i=i, sub=sub):
                load_weights(i)
                ffn(i, sub * TM, TM)


def _moe_call(blk_e, blk_first, blk_next, n_valid, xs, w_in, b_in, w_out, b_out):
    assert N_BLOCKS % BPS == 0

    def row_map(s, be, bf, nx, nv):
        last = (nv[0] + BPS - 1) // BPS - 1
        return (jnp.maximum(jnp.minimum(s, last), 0), 0)

    grid_spec = pltpu.PrefetchScalarGridSpec(
        num_scalar_prefetch=4,
        grid=(N_BLOCKS // BPS,),
        in_specs=[pl.BlockSpec((BPS * TM * SUB, LANES), row_map),
                  pl.BlockSpec(memory_space=pl.ANY),
                  pl.BlockSpec((N_E, 2 * D_E), lambda s, *_: (0, 0)),
                  pl.BlockSpec(memory_space=pl.ANY),
                  pl.BlockSpec((N_E, D), lambda s, *_: (0, 0))],
        out_specs=pl.BlockSpec((BPS * TM * SUB, LANES), row_map),
        scratch_shapes=[pltpu.VMEM((D, 2 * D_E), F32),
                        pltpu.VMEM((D_E, D), F32),
                        pltpu.VMEM((D, 2 * D_E), BF16),
                        pltpu.VMEM((D_E, D), BF16),
                        pltpu.SemaphoreType.DMA,
                        pltpu.SemaphoreType.DMA],
    )
    return pl.pallas_call(
        _moe_kernel,
        grid_spec=grid_spec,
        out_shape=jax.ShapeDtypeStruct((N_ROWS * SUB, LANES), U32),
        compiler_params=pltpu.CompilerParams(dimension_semantics=("arbitrary",),
                                             vmem_limit_bytes=VMEM_LIMIT),
        name="moe_experts",
    )(blk_e, blk_first, blk_next, n_valid, xs, w_in, b_in, w_out, b_out)


def _combine_kernel(loc_ref, glob_ref, tot_ref, ys_ref, pc_ref, x1_ref, gate2_ref, fg_ref, o_ref,
                    buf0, buf1, buf2, buf3, sem0, sem1, sem2, sem3):
    j = pl.program_id(0)
    head = ys_ref.at[pl.ds(0, RS * SUB)]
    even = ((buf0, sem0), (buf1, sem1))
    odd = ((buf2, sem2), (buf3, sem3))

    def fetch(step, slots):
        for t, (buf, sem) in enumerate(slots):
            _start_tile_runs(2 * step + t, ys_ref, buf, glob_ref, loc_ref, tot_ref, sem)

    @pl.when(j == 0)
    def _():
        fetch(0, even)

    col = lax.broadcasted_iota(jnp.int32, (TT, RS), 1)
    wms = []
    for t in range(2):
        pc = pc_ref[t * TT:(t + 1) * TT, :]
        pos = pc.astype(jnp.int32)
        wm = jnp.zeros((TT, RS), F32)
        for k in range(TOP_K):
            wm = jnp.where(col == pos[:, k:k + 1], pc[:, TOP_K + k:TOP_K + k + 1], wm)
        wms.append(wm.astype(BF16))

    def step(cur, nxt):
        @pl.when(j + 1 < NT // 2)
        def _():
            fetch(j + 1, nxt)
        for t, (buf, sem) in enumerate(cur):
            _wait_tile_runs(head, buf, sem)
            yt = _unpack_rows(_load_grouped(buf, RS)).astype(BF16)
            acc = jnp.dot(wms[t], yt, preferred_element_type=F32)
            x2 = x1_ref[t * TT:(t + 1) * TT, :] + gate2_ref[0] * acc
            o_ref[t * TT:(t + 1) * TT, :] = _rms(x2) * fg_ref[...]

    @pl.when(j % 2 == 0)
    def _():
        step(even, odd)

    @pl.when(j % 2 == 1)
    def _():
        step(odd, even)


def _combine_call(loc_t, glob_t, tot_t, ys, pos_c, x1, mod3, final_g):
    per_b = S // (2 * TT)
    grid_spec = pltpu.PrefetchScalarGridSpec(
        num_scalar_prefetch=3,
        grid=(NT // 2,),
        in_specs=[pl.BlockSpec(memory_space=pl.ANY),
                  pl.BlockSpec((2 * TT, 8), lambda j, *_: (j, 0)),
                  pl.BlockSpec((2 * TT, D), lambda j, *_: (j, 0)),
                  pl.BlockSpec((1, 1, D), lambda j, *_: (j // per_b, 0, 5)),
                  pl.BlockSpec((1, D), lambda j, *_: (0, 0))],
        out_specs=pl.BlockSpec((2 * TT, D), lambda j, *_: (j, 0)),
        scratch_shapes=[pltpu.VMEM((RS * SUB, LANES), U32)] * 4 + [pltpu.SemaphoreType.DMA] * 4,
    )
    return pl.pallas_call(
        _combine_kernel,
        grid_spec=grid_spec,
        out_shape=jax.ShapeDtypeStruct((T, D), F32),
        compiler_params=pltpu.CompilerParams(dimension_semantics=("arbitrary",),
                                             vmem_limit_bytes=VMEM_LIMIT),
        name="combine_norm",
    )(loc_t, glob_t, tot_t, ys, pos_c, x1, mod3, final_g)


def kernel(x, c, ada_w, ada_b, norm1_g, w_in, ssm_a_re, ssm_a_im, ssm_log_dt, ssm_b_re, ssm_b_im, ssm_c_re, ssm_c_im, ssm_d, ssm_glu_w, ssm_glu_b, w_branch_a, gmlp_ln_g, gmlp_ln_b, gmlp_ws, gmlp_bs, w_branch_b, w_out, norm2_g, router_w, router_b, moe_w_in, moe_b_in, moe_w_out, moe_b_out, final_g):
    depth = ada_w.shape[0]
    assert depth == 1, "the final rms_norm is fused into the combine kernel of the only layer"
    for layer in range(depth):
        mod = _mod_call(c, ada_w[layer], ada_b[layer])
        mod3 = mod.reshape(B, 1, 6 * D)

        u, zg = _proj_call(x, norm1_g[layer], mod3, w_in[layer].astype(BF16))

        bm, cre, cim, are, aim = _s5_params(ssm_a_re[layer], ssm_a_im[layer], ssm_log_dt[layer],
                                            ssm_b_re[layer], ssm_b_im[layer],
                                            ssm_c_re[layer], ssm_c_im[layer])
        ya = _s5_call(u, bm, cre, cim, are, aim,
                      ssm_d[layer].reshape(1, SSM_W), ssm_glu_w[layer].astype(BF16),
                      ssm_glu_b[layer].reshape(1, SSM_W), w_branch_a[layer].astype(BF16))

        ws = gmlp_ws[layer]
        ws_pairs = jnp.concatenate([ws[0::2], ws[1::2]], axis=-1)
        bias_full = jnp.repeat(gmlp_bs[layer].T, GM_HD, axis=1)
        x1, h2, logits = _mix_call(
            zg, ya, x, mod3,
            gmlp_ln_g[layer].reshape(1, GM_W), gmlp_ln_b[layer].reshape(1, GM_W),
            ws_pairs, bias_full, w_branch_b[layer].astype(BF16), w_out[layer].astype(BF16),
            norm2_g[layer].reshape(1, D), router_w[layer].T.astype(BF16),
            router_b[layer].reshape(N_E, 1))

        pos_c, pos_t, lists, totals, cnt = _route_call(logits)
        counts = cnt[:, 0].astype(jnp.int32)
        nblk = (counts + TM - 1) // TM
        blk_end = jnp.cumsum(nblk)
        pad_end = (blk_end * TM).astype(jnp.int32)
        experts = jnp.arange(N_E, dtype=jnp.int32)
        blk_ids = jnp.arange(N_BLOCKS, dtype=jnp.int32)
        blk_e = jnp.sum((blk_end[None, :] <= blk_ids[:, None]).astype(jnp.int32), axis=1)
        blk_e = jnp.minimum(blk_e, N_E - 1)
        blk_first = jnp.concatenate([jnp.ones((1,), jnp.int32),
                                     (blk_e[1:] != blk_e[:-1]).astype(jnp.int32)])
        later = (experts[None, :] > experts[:, None]) & (nblk[None, :] > 0)
        next_e = jnp.min(jnp.where(later, experts[None, :], N_E), axis=1)
        next_e = jnp.where(next_e == N_E, -1, next_e)
        blk_next = jnp.sum(jnp.where(blk_e[:, None] == experts[None, :], next_e[None, :], 0), axis=1)
        n_valid = blk_end[-1:].astype(jnp.int32)
        pad_start = pad_end - nblk * TM
        loc_t = lists[:, 0, :].reshape(NT * LIST_W)
        owner = lists[:, 2, :, None] == experts[None, None, :]
        glob_t = (lists[:, 1, :] + jnp.sum(jnp.where(owner, pad_start, 0), axis=-1)).reshape(NT * LIST_W)
        tot_t = totals[:, 0, :].reshape(NT * CLASS_SLOTS)

        xs = _dispatch_call(loc_t, glob_t, tot_t, pad_end, h2.reshape(T, D), pos_t)
        ys = _moe_call(blk_e, blk_first, blk_next.astype(jnp.int32), n_valid, xs, moe_w_in[layer],
                       moe_b_in[layer], moe_w_out[layer], moe_b_out[layer])
        x = _combine_call(loc_t, glob_t, tot_t, ys, pos_c, x1.reshape(T, D),
                          mod3, final_g.reshape(1, D)).reshape(B, S, D)
    return x
```

```python
import math

import jax
import jax.numpy as jnp
from jax import lax
from jax.experimental import pallas as pl
from jax.experimental.pallas import tpu as pltpu

F32 = jnp.float32
BF16 = jnp.bfloat16

D = 1024
B = 8
S = 2048
T = B * S
SSM_W = 512
SSM_G = 32
SSM_H = 16
SSM_P = 64
N_PACK = 4
PACK_G = SSM_G // N_PACK
GM_W = 512
GM_HEADS = 8
GM_HD = 64
CHUNK = 128
N_E = 32
TOP_K = 4
D_E = 1024
LIMIT = 7.0
ALPHA = 1.702
EPS = 1e-6

TS_PROJ = 1024
SUB_PROJ = 512
L_SSM = 128
R_SSM = L_SSM * B
TS_MIX = 512
SUB_MIX = 256
TT = 256
NT = T // TT
ROUTE_TILES = 4
RS = TOP_K * TT
TM = 256
BPS = 4
N_ROWS = T * TOP_K + N_E * TM
N_BLOCKS = N_ROWS // TM
LANES = 128
HALF = D // 2
SUB = HALF // LANES
PIECE = 32
COPY_CLASSES = ((PIECE, 0),) + tuple((PIECE >> s, RS // PIECE + N_E * (s - 1))
                                     for s in range(1, PIECE.bit_length()))
LIST_W = 256
CLASS_SLOTS = 8
assert COPY_CLASSES[-1][1] + N_E <= LIST_W and len(COPY_CLASSES) <= CLASS_SLOTS
VMEM_LIMIT = 56 * 1024 * 1024
U32 = jnp.uint32
ACT = jnp.bfloat16


def _sigmoid(v):
    return 0.5 * jnp.tanh(0.5 * v) + 0.5


def _gelu(v):
    c = math.sqrt(2.0 / math.pi)
    inner = v * (c + (c * 0.044715) * (v * v))
    return v * (0.5 + 0.5 * jnp.tanh(inner))


def _rms(v):
    return v * lax.rsqrt(jnp.mean(v * v, axis=-1, keepdims=True) + EPS)


def _mod_kernel(c_ref, w_ref, b_ref, o_ref):
    cv = c_ref[...]
    sv = cv * _sigmoid(cv)
    o_ref[...] = jnp.dot(sv, w_ref[...], preferred_element_type=F32,
                         precision=lax.Precision.HIGHEST) + b_ref[...]


def _mod_call(c, ada_w, ada_b):
    n = ada_w.shape[1]
    return pl.pallas_call(
        _mod_kernel,
        grid=(n // D,),
        in_specs=[pl.BlockSpec((B, D), lambda j: (0, 0)),
                  pl.BlockSpec((D, D), lambda j: (0, j)),
                  pl.BlockSpec((1, D), lambda j: (0, j))],
        out_specs=pl.BlockSpec((B, D), lambda j: (0, j)),
        out_shape=jax.ShapeDtypeStruct((B, n), F32),
        name="adaln_mod",
    )(c, ada_w, ada_b.reshape(1, n))


def _proj_kernel(x_ref, g_ref, shift_ref, scale_ref, w_ref, u_ref, zg_ref):
    gain = g_ref[...] * (1.0 + scale_ref[0])
    for g in range(TS_PROJ // SUB_PROJ):
        rows = pl.ds(g * SUB_PROJ, SUB_PROJ)
        hb = (_rms(x_ref[0, rows, :]) * gain + shift_ref[0]).astype(BF16)
        u_ref[0, rows, :] = jnp.dot(hb, w_ref[:, 0:SSM_W], preferred_element_type=F32)
        zg_ref[0, rows, :] = jnp.dot(hb, w_ref[:, SSM_W:],
                                     preferred_element_type=F32).astype(ACT)


def _proj_call(x, norm_g, mod3, w_in_bf):
    pw = w_in_bf.shape[1]
    tok_spec = pl.BlockSpec((1, TS_PROJ, D), lambda b, s: (b, s, 0))
    return pl.pallas_call(
        _proj_kernel,
        grid=(B, S // TS_PROJ),
        in_specs=[tok_spec,
                  pl.BlockSpec((1, D), lambda b, s: (0, 0)),
                  pl.BlockSpec((1, 1, D), lambda b, s: (b, 0, 0)),
                  pl.BlockSpec((1, 1, D), lambda b, s: (b, 0, 1)),
                  pl.BlockSpec((D, pw), lambda b, s: (0, 0))],
        out_specs=[pl.BlockSpec((1, TS_PROJ, SSM_W), lambda b, s: (b, s, 0)),
                   pl.BlockSpec((1, TS_PROJ, pw - SSM_W), lambda b, s: (b, s, 0))],
        out_shape=[jax.ShapeDtypeStruct((B, S, SSM_W), F32),
                   jax.ShapeDtypeStruct((B, S, pw - SSM_W), ACT)],
        compiler_params=pltpu.CompilerParams(vmem_limit_bytes=VMEM_LIMIT),
        name="norm_proj",
    )(x, norm_g.reshape(1, D), mod3, mod3, w_in_bf)


def _s5_kernel(u_ref, bm_ref, cre_ref, cim_ref, are_ref, aim_ref, d_ref, gw_ref, gb_ref, wa_ref,
               o_ref, usc, ysc, sre, sim, st_re, st_im):
    @pl.when(pl.program_id(0) == 0)
    def _():
        st_re[...] = jnp.zeros_like(st_re)
        st_im[...] = jnp.zeros_like(st_im)

    nslab = SSM_W // LANES
    for b in range(B):
        for c in range(nslab):
            usc[c, pl.ds(b, L_SSM, stride=B), :] = u_ref[b, :, c * LANES:(c + 1) * LANES]
    u = jnp.concatenate([usc[c] for c in range(nslab)], axis=1)
    ub = u.astype(BF16)
    half = PACK_G * SSM_P
    ys = []
    for k in range(N_PACK):
        pin = PACK_G * SSM_H
        bu = jnp.dot(ub[:, pin * k:pin * (k + 1)], bm_ref[k], preferred_element_type=F32)
        sre[k] = bu[:, :half]
        sim[k] = bu[:, half:]
        ar = are_ref[k]
        ai = aim_ref[k]
        r = st_re[k]
        m = st_im[k]
        for t in range(L_SSM):
            rows = pl.ds(t * B, B)
            nr = ar * r - ai * m + sre[k, rows, :]
            m = ar * m + ai * r + sim[k, rows, :]
            r = nr
            sre[k, rows, :] = r
            sim[k, rows, :] = m
        st_re[k] = r
        st_im[k] = m
        yk = jnp.dot(sre[k].astype(BF16), cre_ref[k], preferred_element_type=F32)
        yk = yk + jnp.dot(sim[k].astype(BF16), cim_ref[k], preferred_element_type=F32)
        ys.extend(yk[:, j * LANES:(j + 1) * LANES] for j in range(pin // LANES))
    for c in range(nslab):
        uc = usc[c]
        ysc[c] = ys[c] + d_ref[:, c * LANES:(c + 1) * LANES] * uc
    y = jnp.concatenate(
        [jnp.concatenate([ysc[c, pl.ds(b, L_SSM, stride=B), :] for c in range(nslab)], axis=1)
         for b in range(B)], axis=0)
    z = _gelu(y)
    gl = jnp.dot(z.astype(BF16), gw_ref[...], preferred_element_type=F32) + gb_ref[...]
    out = z * _sigmoid(gl)
    o = jnp.dot(out.astype(BF16), wa_ref[...], preferred_element_type=F32)
    for b in range(B):
        o_ref[b] = o[b * L_SSM:(b + 1) * L_SSM].astype(ACT)


def _s5_call(u, bm, cre, cim, are, aim, d_skip, glu_w, glu_b, w_a):
    half = PACK_G * SSM_P
    full = lambda *shape: pl.BlockSpec(shape, lambda i: (0,) * len(shape))
    return pl.pallas_call(
        _s5_kernel,
        grid=(S // L_SSM,),
        in_specs=[pl.BlockSpec((B, L_SSM, SSM_W), lambda i: (0, i, 0)),
                  full(N_PACK, PACK_G * SSM_H, 2 * half),
                  full(N_PACK, half, PACK_G * SSM_H),
                  full(N_PACK, half, PACK_G * SSM_H),
                  full(N_PACK, B, half),
                  full(N_PACK, B, half),
                  full(1, SSM_W),
                  full(SSM_W, SSM_W),
                  full(1, SSM_W),
                  full(SSM_W, D)],
        out_specs=pl.BlockSpec((B, L_SSM, D), lambda i: (0, i, 0)),
        out_shape=jax.ShapeDtypeStruct((B, S, D), ACT),
        scratch_shapes=[pltpu.VMEM((SSM_W // LANES, R_SSM, LANES), F32),
                        pltpu.VMEM((SSM_W // LANES, R_SSM, LANES), F32),
                        pltpu.VMEM((N_PACK, R_SSM, half), F32),
                        pltpu.VMEM((N_PACK, R_SSM, half), F32),
                        pltpu.VMEM((N_PACK, B, half), F32),
                        pltpu.VMEM((N_PACK, B, half), F32)],
        compiler_params=pltpu.CompilerParams(dimension_semantics=("arbitrary",),
                                             vmem_limit_bytes=VMEM_LIMIT),
        name="s5_branch",
    )(u, bm, cre, cim, are, aim, d_skip, glu_w, glu_b, w_a)


def _s5_params(a_re, a_im, log_dt, b_re, b_im, c_re, c_im):
    dt = jnp.exp(log_dt)[:, None]
    mag = jnp.exp(a_re * dt)
    lr = mag * jnp.cos(a_im * dt)
    li = mag * jnp.sin(a_im * dt)
    den = a_re * a_re + a_im * a_im
    cr = ((lr - 1.0) * a_re + li * a_im) / den
    ci = (li * a_re - (lr - 1.0) * a_im) / den
    bbr = cr[..., None] * b_re - ci[..., None] * b_im
    bbi = cr[..., None] * b_im + ci[..., None] * b_re
    eye = jnp.eye(PACK_G, dtype=F32)
    half = PACK_G * SSM_P

    def pack_b(m):
        m4 = m.reshape(N_PACK, PACK_G, SSM_P, SSM_H)
        return jnp.einsum('kgph,gj->kghjp', m4, eye).reshape(N_PACK, PACK_G * SSM_H, half)

    def pack_c(m):
        m4 = m.reshape(N_PACK, PACK_G, SSM_H, SSM_P)
        return jnp.einsum('kghp,gj->kgpjh', m4, eye).reshape(N_PACK, half, PACK_G * SSM_H)

    bm = jnp.concatenate([pack_b(bbr), pack_b(bbi)], axis=-1).astype(BF16)
    cre = pack_c(c_re).astype(BF16)
    cim = (-pack_c(c_im)).astype(BF16)
    are = jnp.broadcast_to(lr.reshape(N_PACK, 1, half), (N_PACK, B, half))
    aim = jnp.broadcast_to(li.reshape(N_PACK, 1, half), (N_PACK, B, half))
    return bm, cre, cim, are, aim


def _route_tile(lt, carry):
    sub = lax.broadcasted_iota(jnp.int32, lt.shape, 0).astype(F32)
    sels, vals = [], []
    for _ in range(TOP_K):
        m = jnp.max(lt, axis=0, keepdims=True)
        idx = jnp.min(jnp.where(lt == m, sub, float(N_E)), axis=0, keepdims=True)
        sel = sub == idx
        sels.append(sel)
        vals.append(m)
        lt = jnp.where(sel, -jnp.inf, lt)
    member = sels[0].astype(F32)
    for k in range(1, TOP_K):
        member = member + sels[k].astype(F32)
    n_col = jnp.sum(member, axis=1, keepdims=True)

    re = lax.broadcasted_iota(jnp.int32, (N_E, N_E), 0)
    ce = lax.broadcasted_iota(jnp.int32, (N_E, N_E), 1)
    nb = jnp.broadcast_to(n_col, (N_E, 8)).astype(BF16)
    seg_col = jnp.dot((ce < re).astype(BF16), nb, preferred_element_type=F32)[:, 0:1]
    rt = lax.broadcasted_iota(jnp.int32, (TT, TT), 0)
    ct = lax.broadcasted_iota(jnp.int32, (TT, TT), 1)
    rank = jnp.dot(member.astype(BF16), (rt < ct).astype(BF16), preferred_element_type=F32)
    posb = seg_col + rank
    denom = jnp.zeros_like(vals[0])
    exps = []
    for k in range(TOP_K):
        e = jnp.exp(vals[k] - vals[0])
        exps.append(e)
        denom = denom + e
    s8 = lax.broadcasted_iota(jnp.int32, (8, TT), 0)
    pt = jnp.zeros((8, TT), F32)
    for k in range(TOP_K):
        pk = jnp.sum(jnp.where(sels[k], posb, 0.0), axis=0, keepdims=True)
        pt = jnp.where(s8 == k, pk, pt)
        pt = jnp.where(s8 == TOP_K + k, exps[k] / denom, pt)
    pc = pt.T
    earlier_col = carry[...]
    carry[...] = earlier_col + n_col
    lane8 = lax.broadcasted_iota(jnp.int32, (N_E, 8), 1)
    cols = jnp.where(lane8 == 0, n_col, jnp.where(lane8 == 1, seg_col, jnp.where(lane8 == 2, earlier_col, 0.0)))
    rows = cols.T
    lists, totals = _copy_lists(n_col.astype(jnp.int32), rows[0:1], rows[1:2], rows[2:3])
    return pc, pt, lists, totals


def _copy_lists(n_col, n_row_f, seg, earlier):
    shift = PIECE.bit_length() - 1

    def pieces(n, ci):
        if ci == 0:
            return jnp.right_shift(n, shift)
        return jnp.bitwise_and(jnp.right_shift(n, shift - ci), 1)

    def done(n, ci):
        if ci == 0:
            return jnp.zeros_like(n)
        return n - jnp.bitwise_and(n, (PIECE >> (ci - 1)) - 1)

    re = lax.broadcasted_iota(jnp.int32, (N_E, N_E), 0)
    ce = lax.broadcasted_iota(jnp.int32, (N_E, N_E), 1)
    n_row = n_row_f.astype(jnp.int32)
    lane8 = lax.broadcasted_iota(jnp.int32, (N_E, CLASS_SLOTS), 1)
    x = jnp.zeros((N_E, CLASS_SLOTS), F32)
    for ci in range(len(COPY_CLASSES)):
        x = jnp.where(lane8 == ci, pieces(n_col, ci).astype(F32), x)
    xb = x.astype(BF16)
    before = jnp.dot((ce < re).astype(BF16), xb, preferred_element_type=F32)
    totals = jnp.dot(jnp.ones((8, N_E), BF16), xb, preferred_element_type=F32)
    bulk_row = jnp.broadcast_to(pieces(n_row, 0).astype(F32), (8, N_E)).astype(BF16)
    before_bulk_row = jnp.dot(bulk_row, (re < ce).astype(BF16), preferred_element_type=F32)[0:1]

    lanes = lax.broadcasted_iota(jnp.int32, (N_E, LIST_W), 1)
    sub = lax.broadcasted_iota(jnp.int32, (8, N_E), 0)
    e_row = lax.broadcasted_iota(jnp.int32, (8, N_E), 1).astype(F32)
    lists = jnp.zeros((8, LIST_W), F32)
    for ci, (_, lane0) in enumerate(COPY_CLASSES):
        first = before[:, ci:ci + 1].astype(jnp.int32) + lane0
        sel = (lanes >= first) & (lanes < first + pieces(n_col, ci))
        d_row = done(n_row, ci).astype(F32)
        if ci == 0:
            d_row = d_row - PIECE * before_bulk_row
        v = jnp.where(sub == 0, seg + d_row, jnp.where(sub == 1, earlier + d_row,
                                                       jnp.where(sub == 2, e_row, 0.0)))
        lists = lists + jnp.dot(v, sel.astype(F32), preferred_element_type=F32,
                                precision=lax.Precision.HIGHEST)
    q = lax.broadcasted_iota(jnp.int32, (8, LIST_W), 1)
    s8 = lax.broadcasted_iota(jnp.int32, (8, LIST_W), 0)
    lists = lists + jnp.where((s8 < 2) & (q < RS // PIECE), (PIECE * q).astype(F32), 0.0)
    return lists, totals


def _mix_kernel(zg_ref, ya_ref, x_ref, gate1_ref, shift2_ref, scale2_ref,
                lng_ref, lnb_ref, ws_ref, bias_ref, wbb_ref, wo_ref, n2g_ref, rw_ref, rb_ref,
                x1_ref, h2_ref, lg_ref):
    row = lax.broadcasted_iota(jnp.int32, (CHUNK, 2 * CHUNK), 0)
    col = lax.broadcasted_iota(jnp.int32, (CHUNK, 2 * CHUNK), 1)
    causal = (col % CHUNK) <= row
    lane = lax.broadcasted_iota(jnp.int32, (CHUNK, 2 * GM_HD), 1)
    first = lane < GM_HD
    wpairs = [jnp.where(causal, ws_ref[j], 0.0).astype(BF16) for j in range(GM_HEADS // 2)]

    for g in range(TS_MIX // SUB_MIX):
        rows = pl.ds(g * SUB_MIX, SUB_MIX)
        z = _gelu(zg_ref[0, rows, 0:2 * GM_W].astype(F32))
        u = z[:, :GM_W]
        v = z[:, GM_W:]
        mu = jnp.mean(v, axis=-1, keepdims=True)
        vc = v - mu
        var = jnp.mean(vc * vc, axis=-1, keepdims=True)
        vn = vc * lax.rsqrt(var + EPS) * lng_ref[...] + lnb_ref[...]
        chunks = []
        for n in range(SUB_MIX // CHUNK):
            cols = []
            for j in range(GM_HEADS // 2):
                vp = vn[n * CHUNK:(n + 1) * CHUNK, 2 * GM_HD * j:2 * GM_HD * (j + 1)]
                rhs = jnp.concatenate([jnp.where(first, vp, 0.0), jnp.where(first, 0.0, vp)], axis=0)
                cols.append(jnp.dot(wpairs[j], rhs.astype(BF16), preferred_element_type=F32))
            chunks.append(jnp.concatenate(cols, axis=1) + bias_ref[...])
        mixed = jnp.concatenate(chunks, axis=0)
        gm = u * mixed
        yb = jnp.dot(gm.astype(BF16), wbb_ref[...], preferred_element_type=F32)
        g_a = zg_ref[0, rows, 2 * GM_W:2 * GM_W + D].astype(F32)
        g_b = zg_ref[0, rows, 2 * GM_W + D:].astype(F32)
        merged = _sigmoid(g_a) * ya_ref[0, rows, :].astype(F32) + _sigmoid(g_b) * yb
        o = jnp.dot(merged.astype(BF16), wo_ref[...], preferred_element_type=F32)
        x1 = x_ref[0, rows, :] + gate1_ref[0] * o
        x1_ref[0, rows, :] = x1
        h2 = _rms(x1) * (n2g_ref[...] * (1.0 + scale2_ref[0])) + shift2_ref[0]
        hb = h2.astype(BF16)
        h2_ref[0, rows, :] = hb
        lg_ref[:, g * SUB_MIX:(g + 1) * SUB_MIX] = lax.dot_general(
            rw_ref[...], hb, (((1,), (1,)), ((), ())), preferred_element_type=F32) + rb_ref[...]


def _route_kernel(lg_ref, pc_ref, pt_ref, lists_ref, totals_ref, cnt_ref, carry):
    @pl.when(pl.program_id(0) == 0)
    def _():
        carry[...] = jnp.zeros_like(carry)

    for t in range(ROUTE_TILES):
        pc, pt, lists, totals = _route_tile(lg_ref[:, t * TT:(t + 1) * TT], carry)
        pc_ref[t * TT:(t + 1) * TT, :] = pc
        pt_ref[:, t * TT:(t + 1) * TT] = pt
        lists_ref[t] = lists.astype(jnp.int32)
        totals_ref[t] = totals.astype(jnp.int32)
    cnt_ref[...] = jnp.broadcast_to(carry[...], cnt_ref.shape)


def _route_call(logits):
    assert NT % ROUTE_TILES == 0
    return pl.pallas_call(
        _route_kernel,
        grid=(NT // ROUTE_TILES,),
        in_specs=[pl.BlockSpec((N_E, ROUTE_TILES * TT), lambda i: (0, i))],
        out_specs=[pl.BlockSpec((ROUTE_TILES * TT, 8), lambda i: (i, 0)),
                   pl.BlockSpec((8, ROUTE_TILES * TT), lambda i: (0, i)),
                   pl.BlockSpec((ROUTE_TILES, 8, LIST_W), lambda i: (i, 0, 0)),
                   pl.BlockSpec((ROUTE_TILES, 8, CLASS_SLOTS), lambda i: (i, 0, 0)),
                   pl.BlockSpec((N_E, 8), lambda i: (0, 0))],
        out_shape=[jax.ShapeDtypeStruct((T, 8), F32),
                   jax.ShapeDtypeStruct((8, T), F32),
                   jax.ShapeDtypeStruct((NT, 8, LIST_W), jnp.int32),
                   jax.ShapeDtypeStruct((NT, 8, CLASS_SLOTS), jnp.int32),
                   jax.ShapeDtypeStruct((N_E, 8), F32)],
        scratch_shapes=[pltpu.VMEM((N_E, 1), F32)],
        compiler_params=pltpu.CompilerParams(dimension_semantics=("arbitrary",)),
        name="route",
    )(logits)


def _mix_call(zg, ya2d, x, mod3, ln_g, ln_b, ws_pairs, bias_full, wbb, wo, n2g, rw, rb):
    tok_spec = pl.BlockSpec((1, TS_MIX, D), lambda b, s: (b, s, 0))
    full = lambda *shape: pl.BlockSpec(shape, lambda b, s: (0,) * len(shape))
    mod_spec = lambda j: pl.BlockSpec((1, 1, D), lambda b, s: (b, 0, j))
    return pl.pallas_call(
        _mix_kernel,
        grid=(B, S // TS_MIX),
        in_specs=[pl.BlockSpec((1, TS_MIX, zg.shape[-1]), lambda b, s: (b, s, 0)), tok_spec, tok_spec,
                  mod_spec(2), mod_spec(3), mod_spec(4),
                  full(1, GM_W), full(1, GM_W),
                  full(GM_HEADS // 2, CHUNK, 2 * CHUNK),
                  full(CHUNK, GM_W),
                  full(GM_W, D), full(D, D), full(1, D),
                  full(N_E, D), full(N_E, 1)],
        out_specs=[tok_spec, tok_spec,
                   pl.BlockSpec((N_E, TS_MIX), lambda b, s: (0, b * (S // TS_MIX) + s))],
        out_shape=[jax.ShapeDtypeStruct((B, S, D), F32),
                   jax.ShapeDtypeStruct((B, S, D), BF16),
                   jax.ShapeDtypeStruct((N_E, T), F32)],
        compiler_params=pltpu.CompilerParams(vmem_limit_bytes=VMEM_LIMIT),
        name="gmlp_merge_norm2",
    )(zg, ya2d, x, mod3, mod3, mod3, ln_g, ln_b, ws_pairs, bias_full, wbb, wo, n2g, rw, rb)


def _pack_rows(v):
    return pltpu.pack_elementwise([v[:, :HALF], v[:, HALF:]], packed_dtype=BF16)


def _unpack_rows(w):
    halves = [pltpu.unpack_elementwise(w, index=i, packed_dtype=BF16, unpacked_dtype=F32)
              for i in range(2)]
    return jnp.concatenate(halves, axis=1)


def _load_grouped(ref, rows, first=0):
    return jnp.concatenate([ref[pl.ds(first * SUB + c, rows, stride=SUB), :] for c in range(SUB)], axis=1)


def _store_grouped(ref, w, rows, first=0):
    for c in range(SUB):
        ref[pl.ds(first * SUB + c, rows, stride=SUB), :] = w[:, c * LANES:(c + 1) * LANES]


def _start_tile_runs(tile, src_ref, dst_ref, src_tbl, dst_tbl, totals_tbl, sem):
    for ci, (rows, lane0) in enumerate(COPY_CLASSES):
        base = tile * LIST_W + lane0

        def start(i, carry, rows=rows, base=base):
            s = pl.multiple_of(src_tbl[base + i] * SUB, SUB)
            d = pl.multiple_of(dst_tbl[base + i] * SUB, SUB)
            pltpu.make_async_copy(src_ref.at[pl.ds(s, rows * SUB)],
                                  dst_ref.at[pl.ds(d, rows * SUB)], sem).start()
            return carry

        lax.fori_loop(0, totals_tbl[tile * CLASS_SLOTS + ci], start, 0)


def _wait_tile_runs(src_ref, dst_ref, sem):
    pltpu.make_async_copy(src_ref, dst_ref, sem).wait()


def _dispatch_kernel(loc_ref, glob_ref, tot_ref, pend_ref, h_ref, pt_ref, xs_ref,
                     sbuf0, sbuf1, zbuf, sem_z, sem0, sem1):
    j = pl.program_id(0)

    @pl.when(j == 0)
    def _():
        zbuf[...] = jnp.zeros_like(zbuf)
        for e in range(N_E):
            prev = pend_ref[e - 1] if e > 0 else 0
            end = pend_ref[e]

            @pl.when(end > prev)
            def _():
                first = pl.multiple_of((end - TM) * SUB, TM * SUB)
                cp = pltpu.make_async_copy(zbuf, xs_ref.at[pl.ds(first, TM * SUB)], sem_z)
                cp.start()
                cp.wait()

    rows = lax.broadcasted_iota(jnp.int32, (RS, TT), 0)
    words = []
    for t in range(2):
        pos = pt_ref[:, t * TT:(t + 1) * TT].astype(jnp.int32)
        hit = rows == pos[0:1, :]
        for k in range(1, TOP_K):
            hit = hit | (rows == pos[k:k + 1, :])
        pm = jnp.where(hit, 1.0, 0.0).astype(BF16)
        srt = jnp.dot(pm, h_ref[t * TT:(t + 1) * TT, :], preferred_element_type=F32)
        words.append(_pack_rows(srt))
    head = xs_ref.at[pl.ds(0, RS * SUB)]

    for t, (sbuf, sem) in enumerate(((sbuf0, sem0), (sbuf1, sem1))):
        @pl.when(j >= 1)
        def _(sbuf=sbuf, sem=sem):
            _wait_tile_runs(sbuf, head, sem)
        _store_grouped(sbuf, words[t], RS)
        _start_tile_runs(2 * j + t, sbuf, xs_ref, loc_ref, glob_ref, tot_ref, sem)

    @pl.when(j == NT // 2 - 1)
    def _():
        _wait_tile_runs(sbuf0, head, sem0)
        _wait_tile_runs(sbuf1, head, sem1)


def _dispatch_call(loc_t, glob_t, tot_t, pad_end, h2, pos_t):
    assert NT % 2 == 0
    grid_spec = pltpu.PrefetchScalarGridSpec(
        num_scalar_prefetch=4,
        grid=(NT // 2,),
        in_specs=[pl.BlockSpec((2 * TT, D), lambda j, *_: (j, 0)),
                  pl.BlockSpec((8, 2 * TT), lambda j, *_: (0, j))],
        out_specs=pl.BlockSpec(memory_space=pl.ANY),
        scratch_shapes=[pltpu.VMEM((RS * SUB, LANES), U32),
                        pltpu.VMEM((RS * SUB, LANES), U32),
                        pltpu.VMEM((TM * SUB, LANES), U32),
                        pltpu.SemaphoreType.DMA,
                        pltpu.SemaphoreType.DMA,
                        pltpu.SemaphoreType.DMA],
    )
    return pl.pallas_call(
        _dispatch_kernel,
        grid_spec=grid_spec,
        out_shape=jax.ShapeDtypeStruct((N_ROWS * SUB, LANES), U32),
        compiler_params=pltpu.CompilerParams(dimension_semantics=("arbitrary",),
                                             vmem_limit_bytes=VMEM_LIMIT),
        name="dispatch",
    )(loc_t, glob_t, tot_t, pad_end, h2, pos_t)


def _moe_kernel(be_ref, bf_ref, nx_ref, nv_ref, xs_ref, wi_hbm, bi_ref, wo_hbm, bo_ref, ys_ref,
                wi_f32, wo_f32, wi_bf, wo_bf, sem_i, sem_o):
    step = pl.program_id(0)

    def fetch(e):
        return (pltpu.make_async_copy(wi_hbm.at[e], wi_f32, sem_i),
                pltpu.make_async_copy(wo_hbm.at[e], wo_f32, sem_o))

    @pl.when(step == 0)
    def _():
        for cp in fetch(be_ref[0]):
            cp.start()

    def load_weights(i):
        @pl.when(bf_ref[i] == 1)
        def _():
            for cp in fetch(be_ref[i]):
                cp.wait()
            wi_bf[...] = wi_f32[...].astype(BF16)
            wo_bf[...] = wo_f32[...].astype(BF16)

            @pl.when(nx_ref[i] >= 0)
            def _():
                for cp in fetch(nx_ref[i]):
                    cp.start()

    def ffn(i, first, rows):
        e = be_ref[i]
        xb = _unpack_rows(_load_grouped(xs_ref, rows, first)).astype(BF16)
        gu = jnp.dot(xb, wi_bf[...], preferred_element_type=F32) + bi_ref[pl.ds(e, 1), :]
        gate = jnp.minimum(gu[:, :D_E], LIMIT)
        up = jnp.clip(gu[:, D_E:], -LIMIT, LIMIT)
        act = (up + 1.0) * (gate * _sigmoid(ALPHA * gate))
        y = jnp.dot(act.astype(BF16), wo_bf[...], preferred_element_type=F32) + bo_ref[pl.ds(e, 1), :]
        _store_grouped(ys_ref, _pack_rows(y), rows, first)

    i0 = step * BPS
    last = i0 + BPS - 1
    uniform = (last < nv_ref[0]) & (be_ref[i0] == be_ref[last])

    @pl.when(uniform)
    def _():
        load_weights(i0)
        ffn(i0, 0, BPS * TM)

    @pl.when(jnp.logical_not(uniform))
    def _():
        for sub in range(BPS):
            i = i0 + sub

            @pl.when(i < nv_ref[0])
            def _(i=i, sub=sub):
                load_weights(i)
                ffn(i, sub * TM, TM)


def _moe_call(blk_e, blk_first, blk_next, n_valid, xs, w_in, b_in, w_out, b_out):
    assert N_BLOCKS % BPS == 0

    def row_map(s, be, bf, nx, nv):
        last = (nv[0] + BPS - 1) // BPS - 1
        return (jnp.maximum(jnp.minimum(s, last), 0), 0)

    grid_spec = pltpu.PrefetchScalarGridSpec(
        num_scalar_prefetch=4,
        grid=(N_BLOCKS // BPS,),
        in_specs=[pl.BlockSpec((BPS * TM * SUB, LANES), row_map),
                  pl.BlockSpec(memory_space=pl.ANY),
                  pl.BlockSpec((N_E, 2 * D_E), lambda s, *_: (0, 0)),
                  pl.BlockSpec(memory_space=pl.ANY),
                  pl.BlockSpec((N_E, D), lambda s, *_: (0, 0))],
        out_specs=pl.BlockSpec((BPS * TM * SUB, LANES), row_map),
        scratch_shapes=[pltpu.VMEM((D, 2 * D_E), F32),
                        pltpu.VMEM((D_E, D), F32),
                        pltpu.VMEM((D, 2 * D_E), BF16),
                        pltpu.VMEM((D_E, D), BF16),
                        pltpu.SemaphoreType.DMA,
                        pltpu.SemaphoreType.DMA],
    )
    return pl.pallas_call(
        _moe_kernel,
        grid_spec=grid_spec,
        out_shape=jax.ShapeDtypeStruct((N_ROWS * SUB, LANES), U32),
        compiler_params=pltpu.CompilerParams(dimension_semantics=("arbitrary",),
                                             vmem_limit_bytes=VMEM_LIMIT),
        name="moe_experts",
    )(blk_e, blk_first, blk_next, n_valid, xs, w_in, b_in, w_out, b_out)


def _combine_kernel(loc_ref, glob_ref, tot_ref, ys_ref, pc_ref, x1_ref, gate2_ref, fg_ref, o_ref,
                    buf0, buf1, buf2, buf3, sem0, sem1, sem2, sem3):
    j = pl.program_id(0)
    head = ys_ref.at[pl.ds(0, RS * SUB)]
    even = ((buf0, sem0), (buf1, sem1))
    odd = ((buf2, sem2), (buf3, sem3))

    def fetch(step, slots):
        for t, (buf, sem) in enumerate(slots):
            _start_tile_runs(2 * step + t, ys_ref, buf, glob_ref, loc_ref, tot_ref, sem)

    @pl.when(j == 0)
    def _():
        fetch(0, even)

    col = lax.broadcasted_iota(jnp.int32, (TT, RS), 1)
    wms = []
    for t in range(2):
        pc = pc_ref[t * TT:(t + 1) * TT, :]
        pos = pc.astype(jnp.int32)
        wm = jnp.zeros((TT, RS), F32)
        for k in range(TOP_K):
            wm = jnp.where(col == pos[:, k:k + 1], pc[:, TOP_K + k:TOP_K + k + 1], wm)
        wms.append(wm.astype(BF16))

    def step(cur, nxt):
        @pl.when(j + 1 < NT // 2)
        def _():
            fetch(j + 1, nxt)
        for t, (buf, sem) in enumerate(cur):
            _wait_tile_runs(head, buf, sem)
            yt = _unpack_rows(_load_grouped(buf, RS)).astype(BF16)
            acc = jnp.dot(wms[t], yt, preferred_element_type=F32)
            x2 = x1_ref[t * TT:(t + 1) * TT, :] + gate2_ref[0] * acc
            o_ref[t * TT:(t + 1) * TT, :] = _rms(x2) * fg_ref[...]

    @pl.when(j % 2 == 0)
    def _():
        step(even, odd)

    @pl.when(j % 2 == 1)
    def _():
        step(odd, even)


def _combine_call(loc_t, glob_t, tot_t, ys, pos_c, x1, mod3, final_g):
    per_b = S // (2 * TT)
    grid_spec = pltpu.PrefetchScalarGridSpec(
        num_scalar_prefetch=3,
        grid=(NT // 2,),
        in_specs=[pl.BlockSpec(memory_space=pl.ANY),
                  pl.BlockSpec((2 * TT, 8), lambda j, *_: (j, 0)),
                  pl.BlockSpec((2 * TT, D), lambda j, *_: (j, 0)),
                  pl.BlockSpec((1, 1, D), lambda j, *_: (j // per_b, 0, 5)),
                  pl.BlockSpec((1, D), lambda j, *_: (0, 0))],
        out_specs=pl.BlockSpec((2 * TT, D), lambda j, *_: (j, 0)),
        scratch_shapes=[pltpu.VMEM((RS * SUB, LANES), U32)] * 4 + [pltpu.SemaphoreType.DMA] * 4,
    )
    return pl.pallas_call(
        _combine_kernel,
        grid_spec=grid_spec,
        out_shape=jax.ShapeDtypeStruct((T, D), F32),
        compiler_params=pltpu.CompilerParams(dimension_semantics=("arbitrary",),
                                             vmem_limit_bytes=VMEM_LIMIT),
        name="combine_norm",
    )(loc_t, glob_t, tot_t, ys, pos_c, x1, mod3, final_g)


def kernel(x, c, ada_w, ada_b, norm1_g, w_in, ssm_a_re, ssm_a_im, ssm_log_dt, ssm_b_re, ssm_b_im, ssm_c_re, ssm_c_im, ssm_d, ssm_glu_w, ssm_glu_b, w_branch_a, gmlp_ln_g, gmlp_ln_b, gmlp_ws, gmlp_bs, w_branch_b, w_out, norm2_g, router_w, router_b, moe_w_in, moe_b_in, moe_w_out, moe_b_out, final_g):
    depth = ada_w.shape[0]
    assert depth == 1, "the final rms_norm is fused into the combine kernel of the only layer"
    for layer in range(depth):
        mod = _mod_call(c, ada_w[layer], ada_b[layer])
        mod3 = mod.reshape(B, 1, 6 * D)

        u, zg = _proj_call(x, norm1_g[layer], mod3, w_in[layer].astype(BF16))

        bm, cre, cim, are, aim = _s5_params(ssm_a_re[layer], ssm_a_im[layer], ssm_log_dt[layer],
                                            ssm_b_re[layer], ssm_b_im[layer],
                                            ssm_c_re[layer], ssm_c_im[layer])
        ya = _s5_call(u, bm, cre, cim, are, aim,
                      ssm_d[layer].reshape(1, SSM_W), ssm_glu_w[layer].astype(BF16),
                      ssm_glu_b[layer].reshape(1, SSM_W), w_branch_a[layer].astype(BF16))

        ws = gmlp_ws[layer]
        ws_pairs = jnp.concatenate([ws[0::2], ws[1::2]], axis=-1)
        bias_full = jnp.repeat(gmlp_bs[layer].T, GM_HD, axis=1)
        x1, h2, logits = _mix_call(
            zg, ya, x, mod3,
            gmlp_ln_g[layer].reshape(1, GM_W), gmlp_ln_b[layer].reshape(1, GM_W),
            ws_pairs, bias_full, w_branch_b[layer].astype(BF16), w_out[layer].astype(BF16),
            norm2_g[layer].reshape(1, D), router_w[layer].T.astype(BF16),
            router_b[layer].reshape(N_E, 1))

        pos_c, pos_t, lists, totals, cnt = _route_call(logits)
        counts = cnt[:, 0].astype(jnp.int32)
        nblk = (counts + TM - 1) // TM
        blk_end = jnp.cumsum(nblk)
        pad_end = (blk_end * TM).astype(jnp.int32)
        experts = jnp.arange(N_E, dtype=jnp.int32)
        blk_ids = jnp.arange(N_BLOCKS, dtype=jnp.int32)
        blk_e = jnp.sum((blk_end[None, :] <= blk_ids[:, None]).astype(jnp.int32), axis=1)
        blk_e = jnp.minimum(blk_e, N_E - 1)
        blk_first = jnp.concatenate([jnp.ones((1,), jnp.int32),
                                     (blk_e[1:] != blk_e[:-1]).astype(jnp.int32)])
        later = (experts[None, :] > experts[:, None]) & (nblk[None, :] > 0)
        next_e = jnp.min(jnp.where(later, experts[None, :], N_E), axis=1)
        next_e = jnp.where(next_e == N_E, -1, next_e)
        blk_next = jnp.sum(jnp.where(blk_e[:, None] == experts[None, :], next_e[None, :], 0), axis=1)
        n_valid = blk_end[-1:].astype(jnp.int32)
        pad_start = pad_end - nblk * TM
        loc_t = lists[:, 0, :].reshape(NT * LIST_W)
        owner = lists[:, 2, :, None] == experts[None, None, :]
        glob_t = (lists[:, 1, :] + jnp.sum(jnp.where(owner, pad_start, 0), axis=-1)).reshape(NT * LIST_W)
        tot_t = totals[:, 0, :].reshape(NT * CLASS_SLOTS)

        xs = _dispatch_call(loc_t, glob_t, tot_t, pad_end, h2.reshape(T, D), pos_t)
        ys = _moe_call(blk_e, blk_first, blk_next.astype(jnp.int32), n_valid, xs, moe_w_in[layer],
                       moe_b_in[layer], moe_w_out[layer], moe_b_out[layer])
        x = _combine_call(loc_t, glob_t, tot_t, ys, pos_c, x1.reshape(T, D),
                          mod3, final_g.reshape(1, D)).reshape(B, S, D)
    return x
```

```python
import math

import jax
import jax.numpy as jnp
from jax import lax
from jax.experimental import pallas as pl
from jax.experimental.pallas import tpu as pltpu

F32 = jnp.float32
BF16 = jnp.bfloat16

D = 1024
B = 8
S = 2048
T = B * S
SSM_W = 512
SSM_G = 32
SSM_H = 16
SSM_P = 64
N_PACK = 4
PACK_G = SSM_G // N_PACK
GM_W = 512
GM_HEADS = 8
GM_HD = 64
CHUNK = 128
N_E = 32
TOP_K = 4
D_E = 1024
LIMIT = 7.0
ALPHA = 1.702
EPS = 1e-6

TS_PROJ = 1024
SUB_PROJ = 512
L_SSM = 128
R_SSM = L_SSM * B
TS_MIX = 1024
SUB_MIX = 256
TT = 256
NT = T // TT
ROUTE_TILES = 8
RS = TOP_K * TT
TM = 256
BPS = 4
N_ROWS = T * TOP_K + N_E * TM
N_BLOCKS = N_ROWS // TM
LANES = 128
HALF = D // 2
SUB = HALF // LANES
PIECE = 32
COPY_CLASSES = ((PIECE, 0),) + tuple((PIECE >> s, RS // PIECE + N_E * (s - 1))
                                     for s in range(1, PIECE.bit_length()))
LIST_W = 256
CLASS_SLOTS = 8
assert COPY_CLASSES[-1][1] + N_E <= LIST_W and len(COPY_CLASSES) <= CLASS_SLOTS
VMEM_LIMIT = 56 * 1024 * 1024
U32 = jnp.uint32
ACT = jnp.bfloat16


def _sigmoid(v):
    return 0.5 * jnp.tanh(0.5 * v) + 0.5


def _gelu(v):
    c = math.sqrt(2.0 / math.pi)
    inner = v * (c + (c * 0.044715) * (v * v))
    return v * (0.5 + 0.5 * jnp.tanh(inner))


def _rms(v):
    return v * lax.rsqrt(jnp.mean(v * v, axis=-1, keepdims=True) + EPS)


def _mod_kernel(c_ref, w_ref, b_ref, o_ref):
    cv = c_ref[...]
    sv = cv * _sigmoid(cv)
    o_ref[...] = jnp.dot(sv, w_ref[...], preferred_element_type=F32,
                         precision=lax.Precision.HIGHEST) + b_ref[...]


def _mod_call(c, ada_w, ada_b):
    n = ada_w.shape[1]
    return pl.pallas_call(
        _mod_kernel,
        grid=(n // D,),
        in_specs=[pl.BlockSpec((B, D), lambda j: (0, 0)),
                  pl.BlockSpec((D, D), lambda j: (0, j)),
                  pl.BlockSpec((1, D), lambda j: (0, j))],
        out_specs=pl.BlockSpec((B, D), lambda j: (0, j)),
        out_shape=jax.ShapeDtypeStruct((B, n), F32),
        name="adaln_mod",
    )(c, ada_w, ada_b.reshape(1, n))


def _proj_kernel(x_ref, g_ref, shift_ref, scale_ref, w_ref, u_ref, zg_ref):
    gain = g_ref[...] * (1.0 + scale_ref[0])
    for g in range(TS_PROJ // SUB_PROJ):
        rows = pl.ds(g * SUB_PROJ, SUB_PROJ)
        hb = (_rms(x_ref[0, rows, :]) * gain + shift_ref[0]).astype(BF16)
        u_ref[0, rows, :] = jnp.dot(hb, w_ref[:, 0:SSM_W], preferred_element_type=F32)
        zg_ref[0, rows, :] = jnp.dot(hb, w_ref[:, SSM_W:],
                                     preferred_element_type=F32).astype(ACT)


def _proj_call(x, norm_g, mod3, w_in_bf):
    pw = w_in_bf.shape[1]
    tok_spec = pl.BlockSpec((1, TS_PROJ, D), lambda b, s: (b, s, 0))
    return pl.pallas_call(
        _proj_kernel,
        grid=(B, S // TS_PROJ),
        in_specs=[tok_spec,
                  pl.BlockSpec((1, D), lambda b, s: (0, 0)),
                  pl.BlockSpec((1, 1, D), lambda b, s: (b, 0, 0)),
                  pl.BlockSpec((1, 1, D), lambda b, s: (b, 0, 1)),
                  pl.BlockSpec((D, pw), lambda b, s: (0, 0))],
        out_specs=[pl.BlockSpec((1, TS_PROJ, SSM_W), lambda b, s: (b, s, 0)),
                   pl.BlockSpec((1, TS_PROJ, pw - SSM_W), lambda b, s: (b, s, 0))],
        out_shape=[jax.ShapeDtypeStruct((B, S, SSM_W), F32),
                   jax.ShapeDtypeStruct((B, S, pw - SSM_W), ACT)],
        compiler_params=pltpu.CompilerParams(vmem_limit_bytes=VMEM_LIMIT),
        name="norm_proj",
    )(x, norm_g.reshape(1, D), mod3, mod3, w_in_bf)


def _s5_kernel(u_ref, bm_ref, cre_ref, cim_ref, are_ref, aim_ref, d_ref, gw_ref, gb_ref, wa_ref,
               o_ref, usc, ysc, sre, sim, st_re, st_im):
    @pl.when(pl.program_id(0) == 0)
    def _():
        st_re[...] = jnp.zeros_like(st_re)
        st_im[...] = jnp.zeros_like(st_im)

    nslab = SSM_W // LANES
    for b in range(B):
        for c in range(nslab):
            usc[c, pl.ds(b, L_SSM, stride=B), :] = u_ref[b, :, c * LANES:(c + 1) * LANES]
    u = jnp.concatenate([usc[c] for c in range(nslab)], axis=1)
    ub = u.astype(BF16)
    half = PACK_G * SSM_P
    ys = []
    for k in range(N_PACK):
        pin = PACK_G * SSM_H
        bu = jnp.dot(ub[:, pin * k:pin * (k + 1)], bm_ref[k], preferred_element_type=F32)
        sre[k] = bu[:, :half]
        sim[k] = bu[:, half:]
        ar = are_ref[k]
        ai = aim_ref[k]
        r = st_re[k]
        m = st_im[k]
        for t in range(L_SSM):
            rows = pl.ds(t * B, B)
            nr = ar * r - ai * m + sre[k, rows, :]
            m = ar * m + ai * r + sim[k, rows, :]
            r = nr
            sre[k, rows, :] = r
            sim[k, rows, :] = m
        st_re[k] = r
        st_im[k] = m
        yk = jnp.dot(sre[k].astype(BF16), cre_ref[k], preferred_element_type=F32)
        yk = yk + jnp.dot(sim[k].astype(BF16), cim_ref[k], preferred_element_type=F32)
        ys.extend(yk[:, j * LANES:(j + 1) * LANES] for j in range(pin // LANES))
    for c in range(nslab):
        uc = usc[c]
        ysc[c] = ys[c] + d_ref[:, c * LANES:(c + 1) * LANES] * uc
    y = jnp.concatenate(
        [jnp.concatenate([ysc[c, pl.ds(b, L_SSM, stride=B), :] for c in range(nslab)], axis=1)
         for b in range(B)], axis=0)
    z = _gelu(y)
    gl = jnp.dot(z.astype(BF16), gw_ref[...], preferred_element_type=F32) + gb_ref[...]
    out = z * _sigmoid(gl)
    o = jnp.dot(out.astype(BF16), wa_ref[...], preferred_element_type=F32)
    for b in range(B):
        o_ref[b] = o[b * L_SSM:(b + 1) * L_SSM].astype(ACT)


def _s5_call(u, bm, cre, cim, are, aim, d_skip, glu_w, glu_b, w_a):
    half = PACK_G * SSM_P
    full = lambda *shape: pl.BlockSpec(shape, lambda i: (0,) * len(shape))
    return pl.pallas_call(
        _s5_kernel,
        grid=(S // L_SSM,),
        in_specs=[pl.BlockSpec((B, L_SSM, SSM_W), lambda i: (0, i, 0)),
                  full(N_PACK, PACK_G * SSM_H, 2 * half),
                  full(N_PACK, half, PACK_G * SSM_H),
                  full(N_PACK, half, PACK_G * SSM_H),
                  full(N_PACK, B, half),
                  full(N_PACK, B, half),
                  full(1, SSM_W),
                  full(SSM_W, SSM_W),
                  full(1, SSM_W),
                  full(SSM_W, D)],
        out_specs=pl.BlockSpec((B, L_SSM, D), lambda i: (0, i, 0)),
        out_shape=jax.ShapeDtypeStruct((B, S, D), ACT),
        scratch_shapes=[pltpu.VMEM((SSM_W // LANES, R_SSM, LANES), F32),
                        pltpu.VMEM((SSM_W // LANES, R_SSM, LANES), F32),
                        pltpu.VMEM((N_PACK, R_SSM, half), F32),
                        pltpu.VMEM((N_PACK, R_SSM, half), F32),
                        pltpu.VMEM((N_PACK, B, half), F32),
                        pltpu.VMEM((N_PACK, B, half), F32)],
        compiler_params=pltpu.CompilerParams(dimension_semantics=("arbitrary",),
                                             vmem_limit_bytes=VMEM_LIMIT),
        name="s5_branch",
    )(u, bm, cre, cim, are, aim, d_skip, glu_w, glu_b, w_a)


def _s5_params(a_re, a_im, log_dt, b_re, b_im, c_re, c_im):
    dt = jnp.exp(log_dt)[:, None]
    mag = jnp.exp(a_re * dt)
    lr = mag * jnp.cos(a_im * dt)
    li = mag * jnp.sin(a_im * dt)
    den = a_re * a_re + a_im * a_im
    cr = ((lr - 1.0) * a_re + li * a_im) / den
    ci = (li * a_re - (lr - 1.0) * a_im) / den
    bbr = cr[..., None] * b_re - ci[..., None] * b_im
    bbi = cr[..., None] * b_im + ci[..., None] * b_re
    eye = jnp.eye(PACK_G, dtype=F32)
    half = PACK_G * SSM_P

    def pack_b(m):
        m4 = m.reshape(N_PACK, PACK_G, SSM_P, SSM_H)
        return jnp.einsum('kgph,gj->kghjp', m4, eye).reshape(N_PACK, PACK_G * SSM_H, half)

    def pack_c(m):
        m4 = m.reshape(N_PACK, PACK_G, SSM_H, SSM_P)
        return jnp.einsum('kghp,gj->kgpjh', m4, eye).reshape(N_PACK, half, PACK_G * SSM_H)

    bm = jnp.concatenate([pack_b(bbr), pack_b(bbi)], axis=-1).astype(BF16)
    cre = pack_c(c_re).astype(BF16)
    cim = (-pack_c(c_im)).astype(BF16)
    are = jnp.broadcast_to(lr.reshape(N_PACK, 1, half), (N_PACK, B, half))
    aim = jnp.broadcast_to(li.reshape(N_PACK, 1, half), (N_PACK, B, half))
    return bm, cre, cim, are, aim


def _route_tile(lt, carry):
    sub = lax.broadcasted_iota(jnp.int32, lt.shape, 0).astype(F32)
    sels, vals = [], []
    for _ in range(TOP_K):
        m = jnp.max(lt, axis=0, keepdims=True)
        idx = jnp.min(jnp.where(lt == m, sub, float(N_E)), axis=0, keepdims=True)
        sel = sub == idx
        sels.append(sel)
        vals.append(m)
        lt = jnp.where(sel, -jnp.inf, lt)
    member = sels[0].astype(F32)
    for k in range(1, TOP_K):
        member = member + sels[k].astype(F32)
    n_col = jnp.sum(member, axis=1, keepdims=True)

    re = lax.broadcasted_iota(jnp.int32, (N_E, N_E), 0)
    ce = lax.broadcasted_iota(jnp.int32, (N_E, N_E), 1)
    nb = jnp.broadcast_to(n_col, (N_E, 8)).astype(BF16)
    seg_col = jnp.dot((ce < re).astype(BF16), nb, preferred_element_type=F32)[:, 0:1]
    rt = lax.broadcasted_iota(jnp.int32, (TT, TT), 0)
    ct = lax.broadcasted_iota(jnp.int32, (TT, TT), 1)
    rank = jnp.dot(member.astype(BF16), (rt < ct).astype(BF16), preferred_element_type=F32)
    posb = seg_col + rank
    denom = jnp.zeros_like(vals[0])
    exps = []
    for k in range(TOP_K):
        e = jnp.exp(vals[k] - vals[0])
        exps.append(e)
        denom = denom + e
    s8 = lax.broadcasted_iota(jnp.int32, (8, TT), 0)
    pt = jnp.zeros((8, TT), F32)
    for k in range(TOP_K):
        pk = jnp.sum(jnp.where(sels[k], posb, 0.0), axis=0, keepdims=True)
        pt = jnp.where(s8 == k, pk, pt)
        pt = jnp.where(s8 == TOP_K + k, exps[k] / denom, pt)
    pc = pt.T
    earlier_col = carry[...]
    carry[...] = earlier_col + n_col
    lane8 = lax.broadcasted_iota(jnp.int32, (N_E, 8), 1)
    cols = jnp.where(lane8 == 0, n_col, jnp.where(lane8 == 1, seg_col, jnp.where(lane8 == 2, earlier_col, 0.0)))
    rows = cols.T
    lists, totals = _copy_lists(n_col.astype(jnp.int32), rows[0:1], rows[1:2], rows[2:3])
    return pc, pt, lists, totals


def _copy_lists(n_col, n_row_f, seg, earlier):
    shift = PIECE.bit_length() - 1

    def pieces(n, ci):
        if ci == 0:
            return jnp.right_shift(n, shift)
        return jnp.bitwise_and(jnp.right_shift(n, shift - ci), 1)

    def done(n, ci):
        if ci == 0:
            return jnp.zeros_like(n)
        return n - jnp.bitwise_and(n, (PIECE >> (ci - 1)) - 1)

    re = lax.broadcasted_iota(jnp.int32, (N_E, N_E), 0)
    ce = lax.broadcasted_iota(jnp.int32, (N_E, N_E), 1)
    n_row = n_row_f.astype(jnp.int32)
    lane8 = lax.broadcasted_iota(jnp.int32, (N_E, CLASS_SLOTS), 1)
    x = jnp.zeros((N_E, CLASS_SLOTS), F32)
    for ci in range(len(COPY_CLASSES)):
        x = jnp.where(lane8 == ci, pieces(n_col, ci).astype(F32), x)
    xb = x.astype(BF16)
    before = jnp.dot((ce < re).astype(BF16), xb, preferred_element_type=F32)
    totals = jnp.dot(jnp.ones((8, N_E), BF16), xb, preferred_element_type=F32)
    bulk_row = jnp.broadcast_to(pieces(n_row, 0).astype(F32), (8, N_E)).astype(BF16)
    before_bulk_row = jnp.dot(bulk_row, (re < ce).astype(BF16), preferred_element_type=F32)[0:1]

    lanes = lax.broadcasted_iota(jnp.int32, (N_E, LIST_W), 1)
    sub = lax.broadcasted_iota(jnp.int32, (8, N_E), 0)
    e_row = lax.broadcasted_iota(jnp.int32, (8, N_E), 1).astype(F32)
    lists = jnp.zeros((8, LIST_W), F32)
    for ci, (_, lane0) in enumerate(COPY_CLASSES):
        first = before[:, ci:ci + 1].astype(jnp.int32) + lane0
        sel = (lanes >= first) & (lanes < first + pieces(n_col, ci))
        d_row = done(n_row, ci).astype(F32)
        if ci == 0:
            d_row = d_row - PIECE * before_bulk_row
        v = jnp.where(sub == 0, seg + d_row, jnp.where(sub == 1, earlier + d_row,
                                                       jnp.where(sub == 2, e_row, 0.0)))
        lists = lists + jnp.dot(v, sel.astype(F32), preferred_element_type=F32,
                                precision=lax.Precision.HIGHEST)
    q = lax.broadcasted_iota(jnp.int32, (8, LIST_W), 1)
    s8 = lax.broadcasted_iota(jnp.int32, (8, LIST_W), 0)
    lists = lists + jnp.where((s8 < 2) & (q < RS // PIECE), (PIECE * q).astype(F32), 0.0)
    return lists, totals


def _mix_kernel(zg_ref, ya_ref, x_ref, gate1_ref, shift2_ref, scale2_ref,
                lng_ref, lnb_ref, ws_ref, bias_ref, wbb_ref, wo_ref, n2g_ref, rw_ref, rb_ref,
                x1_ref, h2_ref, lg_ref):
    row = lax.broadcasted_iota(jnp.int32, (CHUNK, 2 * CHUNK), 0)
    col = lax.broadcasted_iota(jnp.int32, (CHUNK, 2 * CHUNK), 1)
    causal = (col % CHUNK) <= row
    lane = lax.broadcasted_iota(jnp.int32, (CHUNK, 2 * GM_HD), 1)
    first = lane < GM_HD
    wpairs = [jnp.where(causal, ws_ref[j], 0.0).astype(BF16) for j in range(GM_HEADS // 2)]

    for g in range(TS_MIX // SUB_MIX):
        rows = pl.ds(g * SUB_MIX, SUB_MIX)
        z = _gelu(zg_ref[0, rows, 0:2 * GM_W].astype(F32))
        u = z[:, :GM_W]
        v = z[:, GM_W:]
        mu = jnp.mean(v, axis=-1, keepdims=True)
        vc = v - mu
        var = jnp.mean(vc * vc, axis=-1, keepdims=True)
        vn = vc * lax.rsqrt(var + EPS) * lng_ref[...] + lnb_ref[...]
        chunks = []
        for n in range(SUB_MIX // CHUNK):
            cols = []
            for j in range(GM_HEADS // 2):
                vp = vn[n * CHUNK:(n + 1) * CHUNK, 2 * GM_HD * j:2 * GM_HD * (j + 1)]
                rhs = jnp.concatenate([jnp.where(first, vp, 0.0), jnp.where(first, 0.0, vp)], axis=0)
                cols.append(jnp.dot(wpairs[j], rhs.astype(BF16), preferred_element_type=F32))
            chunks.append(jnp.concatenate(cols, axis=1) + bias_ref[...])
        mixed = jnp.concatenate(chunks, axis=0)
        gm = u * mixed
        yb = jnp.dot(gm.astype(BF16), wbb_ref[...], preferred_element_type=F32)
        g_a = zg_ref[0, rows, 2 * GM_W:2 * GM_W + D].astype(F32)
        g_b = zg_ref[0, rows, 2 * GM_W + D:].astype(F32)
        merged = _sigmoid(g_a) * ya_ref[0, rows, :].astype(F32) + _sigmoid(g_b) * yb
        o = jnp.dot(merged.astype(BF16), wo_ref[...], preferred_element_type=F32)
        x1 = x_ref[0, rows, :] + gate1_ref[0] * o
        x1_ref[0, rows, :] = x1
        h2 = _rms(x1) * (n2g_ref[...] * (1.0 + scale2_ref[0])) + shift2_ref[0]
        hb = h2.astype(BF16)
        h2_ref[0, rows, :] = hb
        lg_ref[:, g * SUB_MIX:(g + 1) * SUB_MIX] = lax.dot_general(
            rw_ref[...], hb, (((1,), (1,)), ((), ())), preferred_element_type=F32) + rb_ref[...]


def _route_kernel(lg_ref, pc_ref, pt_ref, lists_ref, totals_ref, cnt_ref, carry):
    @pl.when(pl.program_id(0) == 0)
    def _():
        carry[...] = jnp.zeros_like(carry)

    for t in range(ROUTE_TILES):
        pc, pt, lists, totals = _route_tile(lg_ref[:, t * TT:(t + 1) * TT], carry)
        pc_ref[t * TT:(t + 1) * TT, :] = pc
        pt_ref[:, t * TT:(t + 1) * TT] = pt
        lists_ref[t] = lists.astype(jnp.int32)
        totals_ref[t] = totals.astype(jnp.int32)
    cnt_ref[...] = jnp.broadcast_to(carry[...], cnt_ref.shape)


def _route_call(logits):
    assert NT % ROUTE_TILES == 0
    return pl.pallas_call(
        _route_kernel,
        grid=(NT // ROUTE_TILES,),
        in_specs=[pl.BlockSpec((N_E, ROUTE_TILES * TT), lambda i: (0, i))],
        out_specs=[pl.BlockSpec((ROUTE_TILES * TT, 8), lambda i: (i, 0)),
                   pl.BlockSpec((8, ROUTE_TILES * TT), lambda i: (0, i)),
                   pl.BlockSpec((ROUTE_TILES, 8, LIST_W), lambda i: (i, 0, 0)),
                   pl.BlockSpec((ROUTE_TILES, 8, CLASS_SLOTS), lambda i: (i, 0, 0)),
                   pl.BlockSpec((N_E, 8), lambda i: (0, 0))],
        out_shape=[jax.ShapeDtypeStruct((T, 8), F32),
                   jax.ShapeDtypeStruct((8, T), F32),
                   jax.ShapeDtypeStruct((NT, 8, LIST_W), jnp.int32),
                   jax.ShapeDtypeStruct((NT, 8, CLASS_SLOTS), jnp.int32),
                   jax.ShapeDtypeStruct((N_E, 8), F32)],
        scratch_shapes=[pltpu.VMEM((N_E, 1), F32)],
        compiler_params=pltpu.CompilerParams(dimension_semantics=("arbitrary",)),
        name="route",
    )(logits)


def _mix_call(zg, ya2d, x, mod3, ln_g, ln_b, ws_pairs, bias_full, wbb, wo, n2g, rw, rb):
    tok_spec = pl.BlockSpec((1, TS_MIX, D), lambda b, s: (b, s, 0))
    full = lambda *shape: pl.BlockSpec(shape, lambda b, s: (0,) * len(shape))
    mod_spec = lambda j: pl.BlockSpec((1, 1, D), lambda b, s: (b, 0, j))
    return pl.pallas_call(
        _mix_kernel,
        grid=(B, S // TS_MIX),
        in_specs=[pl.BlockSpec((1, TS_MIX, zg.shape[-1]), lambda b, s: (b, s, 0)), tok_spec, tok_spec,
                  mod_spec(2), mod_spec(3), mod_spec(4),
                  full(1, GM_W), full(1, GM_W),
                  full(GM_HEADS // 2, CHUNK, 2 * CHUNK),
                  full(CHUNK, GM_W),
                  full(GM_W, D), full(D, D), full(1, D),
                  full(N_E, D), full(N_E, 1)],
        out_specs=[tok_spec, tok_spec,
                   pl.BlockSpec((N_E, TS_MIX), lambda b, s: (0, b * (S // TS_MIX) + s))],
        out_shape=[jax.ShapeDtypeStruct((B, S, D), F32),
                   jax.ShapeDtypeStruct((B, S, D), BF16),
                   jax.ShapeDtypeStruct((N_E, T), F32)],
        compiler_params=pltpu.CompilerParams(vmem_limit_bytes=VMEM_LIMIT),
        name="gmlp_merge_norm2",
    )(zg, ya2d, x, mod3, mod3, mod3, ln_g, ln_b, ws_pairs, bias_full, wbb, wo, n2g, rw, rb)


def _pack_rows(v):
    return pltpu.pack_elementwise([v[:, :HALF], v[:, HALF:]], packed_dtype=BF16)


def _unpack_rows(w):
    halves = [pltpu.unpack_elementwise(w, index=i, packed_dtype=BF16, unpacked_dtype=F32)
              for i in range(2)]
    return jnp.concatenate(halves, axis=1)


def _load_grouped(ref, rows, first=0):
    return jnp.concatenate([ref[pl.ds(first * SUB + c, rows, stride=SUB), :] for c in range(SUB)], axis=1)


def _store_grouped(ref, w, rows, first=0):
    for c in range(SUB):
        ref[pl.ds(first * SUB + c, rows, stride=SUB), :] = w[:, c * LANES:(c + 1) * LANES]


def _start_tile_runs(tile, src_ref, dst_ref, src_tbl, dst_tbl, totals_tbl, sem):
    for ci, (rows, lane0) in enumerate(COPY_CLASSES):
        base = tile * LIST_W + lane0

        def start(i, carry, rows=rows, base=base):
            s = pl.multiple_of(src_tbl[base + i] * SUB, SUB)
            d = pl.multiple_of(dst_tbl[base + i] * SUB, SUB)
            pltpu.make_async_copy(src_ref.at[pl.ds(s, rows * SUB)],
                                  dst_ref.at[pl.ds(d, rows * SUB)], sem).start()
            return carry

        lax.fori_loop(0, totals_tbl[tile * CLASS_SLOTS + ci], start, 0)


def _wait_tile_runs(src_ref, dst_ref, sem):
    pltpu.make_async_copy(src_ref, dst_ref, sem).wait()


def _dispatch_kernel(loc_ref, glob_ref, tot_ref, pend_ref, h_ref, pt_ref, xs_ref,
                     sbuf0, sbuf1, zbuf, sem_z, sem0, sem1):
    j = pl.program_id(0)

    @pl.when(j == 0)
    def _():
        zbuf[...] = jnp.zeros_like(zbuf)
        for e in range(N_E):
            prev = pend_ref[e - 1] if e > 0 else 0
            end = pend_ref[e]

            @pl.when(end > prev)
            def _():
                first = pl.multiple_of((end - TM) * SUB, TM * SUB)
                cp = pltpu.make_async_copy(zbuf, xs_ref.at[pl.ds(first, TM * SUB)], sem_z)
                cp.start()
                cp.wait()

    rows = lax.broadcasted_iota(jnp.int32, (RS, TT), 0)
    words = []
    for t in range(2):
        pos = pt_ref[:, t * TT:(t + 1) * TT].astype(jnp.int32)
        hit = rows == pos[0:1, :]
        for k in range(1, TOP_K):
            hit = hit | (rows == pos[k:k + 1, :])
        pm = jnp.where(hit, 1.0, 0.0).astype(BF16)
        srt = jnp.dot(pm, h_ref[t * TT:(t + 1) * TT, :], preferred_element_type=F32)
        words.append(_pack_rows(srt))
    head = xs_ref.at[pl.ds(0, RS * SUB)]

    for t, (sbuf, sem) in enumerate(((sbuf0, sem0), (sbuf1, sem1))):
        @pl.when(j >= 1)
        def _(sbuf=sbuf, sem=sem):
            _wait_tile_runs(sbuf, head, sem)
        _store_grouped(sbuf, words[t], RS)
        _start_tile_runs(2 * j + t, sbuf, xs_ref, loc_ref, glob_ref, tot_ref, sem)

    @pl.when(j == NT // 2 - 1)
    def _():
        _wait_tile_runs(sbuf0, head, sem0)
        _wait_tile_runs(sbuf1, head, sem1)


def _dispatch_call(loc_t, glob_t, tot_t, pad_end, h2, pos_t):
    assert NT % 2 == 0
    grid_spec = pltpu.PrefetchScalarGridSpec(
        num_scalar_prefetch=4,
        grid=(NT // 2,),
        in_specs=[pl.BlockSpec((2 * TT, D), lambda j, *_: (j, 0)),
                  pl.BlockSpec((8, 2 * TT), lambda j, *_: (0, j))],
        out_specs=pl.BlockSpec(memory_space=pl.ANY),
        scratch_shapes=[pltpu.VMEM((RS * SUB, LANES), U32),
                        pltpu.VMEM((RS * SUB, LANES), U32),
                        pltpu.VMEM((TM * SUB, LANES), U32),
                        pltpu.SemaphoreType.DMA,
                        pltpu.SemaphoreType.DMA,
                        pltpu.SemaphoreType.DMA],
    )
    return pl.pallas_call(
        _dispatch_kernel,
        grid_spec=grid_spec,
        out_shape=jax.ShapeDtypeStruct((N_ROWS * SUB, LANES), U32),
        compiler_params=pltpu.CompilerParams(dimension_semantics=("arbitrary",),
                                             vmem_limit_bytes=VMEM_LIMIT),
        name="dispatch",
    )(loc_t, glob_t, tot_t, pad_end, h2, pos_t)


def _moe_kernel(be_ref, bf_ref, nx_ref, nv_ref, xs_ref, wi_hbm, bi_ref, wo_hbm, bo_ref, ys_ref,
                wi_f32, wo_f32, wi_bf, wo_bf, sem_i, sem_o):
    step = pl.program_id(0)

    def fetch(e):
        return (pltpu.make_async_copy(wi_hbm.at[e], wi_f32, sem_i),
                pltpu.make_async_copy(wo_hbm.at[e], wo_f32, sem_o))

    @pl.when(step == 0)
    def _():
        for cp in fetch(be_ref[0]):
            cp.start()

    def load_weights(i):
        @pl.when(bf_ref[i] == 1)
        def _():
            for cp in fetch(be_ref[i]):
                cp.wait()
            wi_bf[...] = wi_f32[...].astype(BF16)
            wo_bf[...] = wo_f32[...].astype(BF16)

            @pl.when(nx_ref[i] >= 0)
            def _():
                for cp in fetch(nx_ref[i]):
                    cp.start()

    def ffn(i, first, rows):
        e = be_ref[i]
        xb = _unpack_rows(_load_grouped(xs_ref, rows, first)).astype(BF16)
        gu = jnp.dot(xb, wi_bf[...], preferred_element_type=F32) + bi_ref[pl.ds(e, 1), :]
        gate = jnp.minimum(gu[:, :D_E], LIMIT)
        up = jnp.clip(gu[:, D_E:], -LIMIT, LIMIT)
        act = (up + 1.0) * (gate * _sigmoid(ALPHA * gate))
        y = jnp.dot(act.astype(BF16), wo_bf[...], preferred_element_type=F32) + bo_ref[pl.ds(e, 1), :]
        _store_grouped(ys_ref, _pack_rows(y), rows, first)

    i0 = step * BPS
    last = i0 + BPS - 1
    uniform = (last < nv_ref[0]) & (be_ref[i0] == be_ref[last])

    @pl.when(uniform)
    def _():
        load_weights(i0)
        ffn(i0, 0, BPS * TM)

    @pl.when(jnp.logical_not(uniform))
    def _():
        for sub in range(BPS):
            i = i0 + sub

            @pl.when(i < nv_ref[0])
            def _(i=i, sub=sub):
                load_weights(i)
                ffn(i, sub * TM, TM)


def _moe_call(blk_e, blk_first, blk_next, n_valid, xs, w_in, b_in, w_out, b_out):
    assert N_BLOCKS % BPS == 0

    def row_map(s, be, bf, nx, nv):
        last = (nv[0] + BPS - 1) // BPS - 1
        return (jnp.maximum(jnp.minimum(s, last), 0), 0)

    grid_spec = pltpu.PrefetchScalarGridSpec(
        num_scalar_prefetch=4,
        grid=(N_BLOCKS // BPS,),
        in_specs=[pl.BlockSpec((BPS * TM * SUB, LANES), row_map),
                  pl.BlockSpec(memory_space=pl.ANY),
                  pl.BlockSpec((N_E, 2 * D_E), lambda s, *_: (0, 0)),
                  pl.BlockSpec(memory_space=pl.ANY),
                  pl.BlockSpec((N_E, D), lambda s, *_: (0, 0))],
        out_specs=pl.BlockSpec((BPS * TM * SUB, LANES), row_map),
        scratch_shapes=[pltpu.VMEM((D, 2 * D_E), F32),
                        pltpu.VMEM((D_E, D), F32),
                        pltpu.VMEM((D, 2 * D_E), BF16),
                        pltpu.VMEM((D_E, D), BF16),
                        pltpu.SemaphoreType.DMA,
                        pltpu.SemaphoreType.DMA],
    )
    return pl.pallas_call(
        _moe_kernel,
        grid_spec=grid_spec,
        out_shape=jax.ShapeDtypeStruct((N_ROWS * SUB, LANES), U32),
        compiler_params=pltpu.CompilerParams(dimension_semantics=("arbitrary",),
                                             vmem_limit_bytes=VMEM_LIMIT),
        name="moe_experts",
    )(blk_e, blk_first, blk_next, n_valid, xs, w_in, b_in, w_out, b_out)


def _combine_kernel(loc_ref, glob_ref, tot_ref, ys_ref, pc_ref, x1_ref, gate2_ref, fg_ref, o_ref,
                    buf0, buf1, buf2, buf3, sem0, sem1, sem2, sem3):
    j = pl.program_id(0)
    head = ys_ref.at[pl.ds(0, RS * SUB)]
    even = ((buf0, sem0), (buf1, sem1))
    odd = ((buf2, sem2), (buf3, sem3))

    def fetch(step, slots):
        for t, (buf, sem) in enumerate(slots):
            _start_tile_runs(2 * step + t, ys_ref, buf, glob_ref, loc_ref, tot_ref, sem)

    @pl.when(j == 0)
    def _():
        fetch(0, even)

    col = lax.broadcasted_iota(jnp.int32, (TT, RS), 1)
    wms = []
    for t in range(2):
        pc = pc_ref[t * TT:(t + 1) * TT, :]
        pos = pc.astype(jnp.int32)
        wm = jnp.zeros((TT, RS), F32)
        for k in range(TOP_K):
            wm = jnp.where(col == pos[:, k:k + 1], pc[:, TOP_K + k:TOP_K + k + 1], wm)
        wms.append(wm.astype(BF16))

    def step(cur, nxt):
        @pl.when(j + 1 < NT // 2)
        def _():
            fetch(j + 1, nxt)
        for t, (buf, sem) in enumerate(cur):
            _wait_tile_runs(head, buf, sem)
            yt = _unpack_rows(_load_grouped(buf, RS)).astype(BF16)
            acc = jnp.dot(wms[t], yt, preferred_element_type=F32)
            x2 = x1_ref[t * TT:(t + 1) * TT, :] + gate2_ref[0] * acc
            o_ref[t * TT:(t + 1) * TT, :] = _rms(x2) * fg_ref[...]

    @pl.when(j % 2 == 0)
    def _():
        step(even, odd)

    @pl.when(j % 2 == 1)
    def _():
        step(odd, even)


def _combine_call(loc_t, glob_t, tot_t, ys, pos_c, x1, mod3, final_g):
    per_b = S // (2 * TT)
    grid_spec = pltpu.PrefetchScalarGridSpec(
        num_scalar_prefetch=3,
        grid=(NT // 2,),
        in_specs=[pl.BlockSpec(memory_space=pl.ANY),
                  pl.BlockSpec((2 * TT, 8), lambda j, *_: (j, 0)),
                  pl.BlockSpec((2 * TT, D), lambda j, *_: (j, 0)),
                  pl.BlockSpec((1, 1, D), lambda j, *_: (j // per_b, 0, 5)),
                  pl.BlockSpec((1, D), lambda j, *_: (0, 0))],
        out_specs=pl.BlockSpec((2 * TT, D), lambda j, *_: (j, 0)),
        scratch_shapes=[pltpu.VMEM((RS * SUB, LANES), U32)] * 4 + [pltpu.SemaphoreType.DMA] * 4,
    )
    return pl.pallas_call(
        _combine_kernel,
        grid_spec=grid_spec,
        out_shape=jax.ShapeDtypeStruct((T, D), F32),
        compiler_params=pltpu.CompilerParams(dimension_semantics=("arbitrary",),
                                             vmem_limit_bytes=VMEM_LIMIT),
        name="combine_norm",
    )(loc_t, glob_t, tot_t, ys, pos_c, x1, mod3, final_g)


def kernel(x, c, ada_w, ada_b, norm1_g, w_in, ssm_a_re, ssm_a_im, ssm_log_dt, ssm_b_re, ssm_b_im, ssm_c_re, ssm_c_im, ssm_d, ssm_glu_w, ssm_glu_b, w_branch_a, gmlp_ln_g, gmlp_ln_b, gmlp_ws, gmlp_bs, w_branch_b, w_out, norm2_g, router_w, router_b, moe_w_in, moe_b_in, moe_w_out, moe_b_out, final_g):
    depth = ada_w.shape[0]
    assert depth == 1, "the final rms_norm is fused into the combine kernel of the only layer"
    for layer in range(depth):
        mod = _mod_call(c, ada_w[layer], ada_b[layer])
        mod3 = mod.reshape(B, 1, 6 * D)

        u, zg = _proj_call(x, norm1_g[layer], mod3, w_in[layer].astype(BF16))

        bm, cre, cim, are, aim = _s5_params(ssm_a_re[layer], ssm_a_im[layer], ssm_log_dt[layer],
                                            ssm_b_re[layer], ssm_b_im[layer],
                                            ssm_c_re[layer], ssm_c_im[layer])
        ya = _s5_call(u, bm, cre, cim, are, aim,
                      ssm_d[layer].reshape(1, SSM_W), ssm_glu_w[layer].astype(BF16),
                      ssm_glu_b[layer].reshape(1, SSM_W), w_branch_a[layer].astype(BF16))

        ws = gmlp_ws[layer]
        ws_pairs = jnp.concatenate([ws[0::2], ws[1::2]], axis=-1)
        bias_full = jnp.repeat(gmlp_bs[layer].T, GM_HD, axis=1)
        x1, h2, logits = _mix_call(
            zg, ya, x, mod3,
            gmlp_ln_g[layer].reshape(1, GM_W), gmlp_ln_b[layer].reshape(1, GM_W),
            ws_pairs, bias_full, w_branch_b[layer].astype(BF16), w_out[layer].astype(BF16),
            norm2_g[layer].reshape(1, D), router_w[layer].T.astype(BF16),
            router_b[layer].reshape(N_E, 1))

        pos_c, pos_t, lists, totals, cnt = _route_call(logits)
        counts = cnt[:, 0].astype(jnp.int32)
        nblk = (counts + TM - 1) // TM
        blk_end = jnp.cumsum(nblk)
        pad_end = (blk_end * TM).astype(jnp.int32)
        experts = jnp.arange(N_E, dtype=jnp.int32)
        blk_ids = jnp.arange(N_BLOCKS, dtype=jnp.int32)
        blk_e = jnp.sum((blk_end[None, :] <= blk_ids[:, None]).astype(jnp.int32), axis=1)
        blk_e = jnp.minimum(blk_e, N_E - 1)
        blk_first = jnp.concatenate([jnp.ones((1,), jnp.int32),
                                     (blk_e[1:] != blk_e[:-1]).astype(jnp.int32)])
        later = (experts[None, :] > experts[:, None]) & (nblk[None, :] > 0)
        next_e = jnp.min(jnp.where(later, experts[None, :], N_E), axis=1)
        next_e = jnp.where(next_e == N_E, -1, next_e)
        blk_next = jnp.sum(jnp.where(blk_e[:, None] == experts[None, :], next_e[None, :], 0), axis=1)
        n_valid = blk_end[-1:].astype(jnp.int32)
        pad_start = pad_end - nblk * TM
        loc_t = lists[:, 0, :].reshape(NT * LIST_W)
        owner = lists[:, 2, :, None] == experts[None, None, :]
        glob_t = (lists[:, 1, :] + jnp.sum(jnp.where(owner, pad_start, 0), axis=-1)).reshape(NT * LIST_W)
        tot_t = totals[:, 0, :].reshape(NT * CLASS_SLOTS)

        xs = _dispatch_call(loc_t, glob_t, tot_t, pad_end, h2.reshape(T, D), pos_t)
        ys = _moe_call(blk_e, blk_first, blk_next.astype(jnp.int32), n_valid, xs, moe_w_in[layer],
                       moe_b_in[layer], moe_w_out[layer], moe_b_out[layer])
        x = _combine_call(loc_t, glob_t, tot_t, ys, pos_c, x1.reshape(T, D),
                          mod3, final_g.reshape(1, D)).reshape(B, S, D)
    return x
```

```python
import math

import jax
import jax.numpy as jnp
from jax import lax
from jax.experimental import pallas as pl
from jax.experimental.pallas import tpu as pltpu

F32 = jnp.float32
BF16 = jnp.bfloat16

D = 1024
B = 8
S = 2048
T = B * S
SSM_W = 512
SSM_G = 32
SSM_H = 16
SSM_P = 64
N_PACK = 4
PACK_G = SSM_G // N_PACK
GM_W = 512
GM_HEADS = 8
GM_HD = 64
CHUNK = 128
N_E = 32
TOP_K = 4
D_E = 1024
LIMIT = 7.0
ALPHA = 1.702
EPS = 1e-6

TS_PROJ = 1024
SUB_PROJ = 512
L_SSM = 128
R_SSM = L_SSM * B
TS_MIX = 1024
SUB_MIX = 256
TT = 256
NT = T // TT
ROUTE_TILES = 8
RS = TOP_K * TT
TM = 256
BPS = 4
N_ROWS = T * TOP_K + N_E * TM
N_BLOCKS = N_ROWS // TM
LANES = 128
HALF = D // 2
SUB = HALF // LANES
PIECE = 32
COPY_CLASSES = ((PIECE, 0),) + tuple((PIECE >> s, RS // PIECE + N_E * (s - 1))
                                     for s in range(1, PIECE.bit_length()))
LIST_W = 256
CLASS_SLOTS = 8
assert COPY_CLASSES[-1][1] + N_E <= LIST_W and len(COPY_CLASSES) <= CLASS_SLOTS
VMEM_LIMIT = 56 * 1024 * 1024
U32 = jnp.uint32
ACT = jnp.bfloat16


def _sigmoid(v):
    return 0.5 * jnp.tanh(0.5 * v) + 0.5


def _gelu(v):
    c = math.sqrt(2.0 / math.pi)
    inner = v * (c + (c * 0.044715) * (v * v))
    return v * (0.5 + 0.5 * jnp.tanh(inner))


def _rms(v):
    return v * lax.rsqrt(jnp.mean(v * v, axis=-1, keepdims=True) + EPS)


def _mod_kernel(c_ref, w_ref, b_ref, o_ref):
    cv = c_ref[...]
    sv = cv * _sigmoid(cv)
    o_ref[...] = jnp.dot(sv, w_ref[...], preferred_element_type=F32,
                         precision=lax.Precision.HIGHEST) + b_ref[...]


def _mod_call(c, ada_w, ada_b):
    n = ada_w.shape[1]
    return pl.pallas_call(
        _mod_kernel,
        grid=(n // D,),
        in_specs=[pl.BlockSpec((B, D), lambda j: (0, 0)),
                  pl.BlockSpec((D, D), lambda j: (0, j)),
                  pl.BlockSpec((1, D), lambda j: (0, j))],
        out_specs=pl.BlockSpec((B, D), lambda j: (0, j)),
        out_shape=jax.ShapeDtypeStruct((B, n), F32),
        name="adaln_mod",
    )(c, ada_w, ada_b.reshape(1, n))


def _proj_kernel(x_ref, g_ref, shift_ref, scale_ref, w_ref, u_ref, zg_ref):
    gain = g_ref[...] * (1.0 + scale_ref[0])
    for g in range(TS_PROJ // SUB_PROJ):
        rows = pl.ds(g * SUB_PROJ, SUB_PROJ)
        hb = (_rms(x_ref[0, rows, :]) * gain + shift_ref[0]).astype(BF16)
        u_ref[0, rows, :] = jnp.dot(hb, w_ref[:, 0:SSM_W], preferred_element_type=F32)
        zg_ref[0, rows, :] = jnp.dot(hb, w_ref[:, SSM_W:],
                                     preferred_element_type=F32).astype(ACT)


def _proj_call(x, norm_g, mod3, w_in_bf):
    pw = w_in_bf.shape[1]
    tok_spec = pl.BlockSpec((1, TS_PROJ, D), lambda b, s: (b, s, 0))
    return pl.pallas_call(
        _proj_kernel,
        grid=(B, S // TS_PROJ),
        in_specs=[tok_spec,
                  pl.BlockSpec((1, D), lambda b, s: (0, 0)),
                  pl.BlockSpec((1, 1, D), lambda b, s: (b, 0, 0)),
                  pl.BlockSpec((1, 1, D), lambda b, s: (b, 0, 1)),
                  pl.BlockSpec((D, pw), lambda b, s: (0, 0))],
        out_specs=[pl.BlockSpec((1, TS_PROJ, SSM_W), lambda b, s: (b, s, 0)),
                   pl.BlockSpec((1, TS_PROJ, pw - SSM_W), lambda b, s: (b, s, 0))],
        out_shape=[jax.ShapeDtypeStruct((B, S, SSM_W), F32),
                   jax.ShapeDtypeStruct((B, S, pw - SSM_W), ACT)],
        compiler_params=pltpu.CompilerParams(vmem_limit_bytes=VMEM_LIMIT),
        name="norm_proj",
    )(x, norm_g.reshape(1, D), mod3, mod3, w_in_bf)


def _s5_kernel(u_ref, bm_ref, cre_ref, cim_ref, are_ref, aim_ref, d_ref, gw_ref, gb_ref, wa_ref,
               o_ref, usc, ysc, sre, sim, st_re, st_im):
    @pl.when(pl.program_id(0) == 0)
    def _():
        st_re[...] = jnp.zeros_like(st_re)
        st_im[...] = jnp.zeros_like(st_im)

    nslab = SSM_W // LANES
    for b in range(B):
        for c in range(nslab):
            usc[c, pl.ds(b, L_SSM, stride=B), :] = u_ref[b, :, c * LANES:(c + 1) * LANES]
    u = jnp.concatenate([usc[c] for c in range(nslab)], axis=1)
    ub = u.astype(BF16)
    half = PACK_G * SSM_P
    ys = []
    for k in range(N_PACK):
        pin = PACK_G * SSM_H
        bu = jnp.dot(ub[:, pin * k:pin * (k + 1)], bm_ref[k], preferred_element_type=F32)
        sre[k] = bu[:, :half]
        sim[k] = bu[:, half:]
        ar = are_ref[k]
        ai = aim_ref[k]
        r = st_re[k]
        m = st_im[k]
        for t in range(L_SSM):
            rows = pl.ds(t * B, B)
            nr = ar * r - ai * m + sre[k, rows, :]
            m = ar * m + ai * r + sim[k, rows, :]
            r = nr
            sre[k, rows, :] = r
            sim[k, rows, :] = m
        st_re[k] = r
        st_im[k] = m
        yk = jnp.dot(sre[k].astype(BF16), cre_ref[k], preferred_element_type=F32)
        yk = yk + jnp.dot(sim[k].astype(BF16), cim_ref[k], preferred_element_type=F32)
        ys.extend(yk[:, j * LANES:(j + 1) * LANES] for j in range(pin // LANES))
    for c in range(nslab):
        uc = usc[c]
        ysc[c] = ys[c] + d_ref[:, c * LANES:(c + 1) * LANES] * uc
    y = jnp.concatenate(
        [jnp.concatenate([ysc[c, pl.ds(b, L_SSM, stride=B), :] for c in range(nslab)], axis=1)
         for b in range(B)], axis=0)
    z = _gelu(y)
    gl = jnp.dot(z.astype(BF16), gw_ref[...], preferred_element_type=F32) + gb_ref[...]
    out = z * _sigmoid(gl)
    o = jnp.dot(out.astype(BF16), wa_ref[...], preferred_element_type=F32)
    for b in range(B):
        o_ref[b] = o[b * L_SSM:(b + 1) * L_SSM].astype(ACT)


def _s5_call(u, bm, cre, cim, are, aim, d_skip, glu_w, glu_b, w_a):
    half = PACK_G * SSM_P
    full = lambda *shape: pl.BlockSpec(shape, lambda i: (0,) * len(shape))
    return pl.pallas_call(
        _s5_kernel,
        grid=(S // L_SSM,),
        in_specs=[pl.BlockSpec((B, L_SSM, SSM_W), lambda i: (0, i, 0)),
                  full(N_PACK, PACK_G * SSM_H, 2 * half),
                  full(N_PACK, half, PACK_G * SSM_H),
                  full(N_PACK, half, PACK_G * SSM_H),
                  full(N_PACK, B, half),
                  full(N_PACK, B, half),
                  full(1, SSM_W),
                  full(SSM_W, SSM_W),
                  full(1, SSM_W),
                  full(SSM_W, D)],
        out_specs=pl.BlockSpec((B, L_SSM, D), lambda i: (0, i, 0)),
        out_shape=jax.ShapeDtypeStruct((B, S, D), ACT),
        scratch_shapes=[pltpu.VMEM((SSM_W // LANES, R_SSM, LANES), F32),
                        pltpu.VMEM((SSM_W // LANES, R_SSM, LANES), F32),
                        pltpu.VMEM((N_PACK, R_SSM, half), F32),
                        pltpu.VMEM((N_PACK, R_SSM, half), F32),
                        pltpu.VMEM((N_PACK, B, half), F32),
                        pltpu.VMEM((N_PACK, B, half), F32)],
        compiler_params=pltpu.CompilerParams(dimension_semantics=("arbitrary",),
                                             vmem_limit_bytes=VMEM_LIMIT),
        name="s5_branch",
    )(u, bm, cre, cim, are, aim, d_skip, glu_w, glu_b, w_a)


def _s5_params(a_re, a_im, log_dt, b_re, b_im, c_re, c_im):
    dt = jnp.exp(log_dt)[:, None]
    mag = jnp.exp(a_re * dt)
    lr = mag * jnp.cos(a_im * dt)
    li = mag * jnp.sin(a_im * dt)
    den = a_re * a_re + a_im * a_im
    cr = ((lr - 1.0) * a_re + li * a_im) / den
    ci = (li * a_re - (lr - 1.0) * a_im) / den
    bbr = cr[..., None] * b_re - ci[..., None] * b_im
    bbi = cr[..., None] * b_im + ci[..., None] * b_re
    eye = jnp.eye(PACK_G, dtype=F32)
    half = PACK_G * SSM_P

    def pack_b(m):
        m4 = m.reshape(N_PACK, PACK_G, SSM_P, SSM_H)
        return jnp.einsum('kgph,gj->kghjp', m4, eye).reshape(N_PACK, PACK_G * SSM_H, half)

    def pack_c(m):
        m4 = m.reshape(N_PACK, PACK_G, SSM_H, SSM_P)
        return jnp.einsum('kghp,gj->kgpjh', m4, eye).reshape(N_PACK, half, PACK_G * SSM_H)

    bm = jnp.concatenate([pack_b(bbr), pack_b(bbi)], axis=-1).astype(BF16)
    cre = pack_c(c_re).astype(BF16)
    cim = (-pack_c(c_im)).astype(BF16)
    are = jnp.broadcast_to(lr.reshape(N_PACK, 1, half), (N_PACK, B, half))
    aim = jnp.broadcast_to(li.reshape(N_PACK, 1, half), (N_PACK, B, half))
    return bm, cre, cim, are, aim


def _route_tile(lt, carry):
    sub = lax.broadcasted_iota(jnp.int32, lt.shape, 0).astype(F32)
    sels, vals = [], []
    for _ in range(TOP_K):
        m = jnp.max(lt, axis=0, keepdims=True)
        idx = jnp.min(jnp.where(lt == m, sub, float(N_E)), axis=0, keepdims=True)
        sel = sub == idx
        sels.append(sel)
        vals.append(m)
        lt = jnp.where(sel, -jnp.inf, lt)
    member = sels[0].astype(F32)
    for k in range(1, TOP_K):
        member = member + sels[k].astype(F32)
    n_col = jnp.sum(member, axis=1, keepdims=True)

    re = lax.broadcasted_iota(jnp.int32, (N_E, N_E), 0)
    ce = lax.broadcasted_iota(jnp.int32, (N_E, N_E), 1)
    nb = jnp.broadcast_to(n_col, (N_E, 8)).astype(BF16)
    seg_col = jnp.dot((ce < re).astype(BF16), nb, preferred_element_type=F32)[:, 0:1]
    rt = lax.broadcasted_iota(jnp.int32, (TT, TT), 0)
    ct = lax.broadcasted_iota(jnp.int32, (TT, TT), 1)
    rank = jnp.dot(member.astype(BF16), (rt < ct).astype(BF16), preferred_element_type=F32)
    posb = seg_col + rank
    denom = jnp.zeros_like(vals[0])
    exps = []
    for k in range(TOP_K):
        e = jnp.exp(vals[k] - vals[0])
        exps.append(e)
        denom = denom + e
    s8 = lax.broadcasted_iota(jnp.int32, (8, TT), 0)
    pt = jnp.zeros((8, TT), F32)
    for k in range(TOP_K):
        pk = jnp.sum(jnp.where(sels[k], posb, 0.0), axis=0, keepdims=True)
        pt = jnp.where(s8 == k, pk, pt)
        pt = jnp.where(s8 == TOP_K + k, exps[k] / denom, pt)
    pc = pt.T
    earlier_col = carry[...]
    carry[...] = earlier_col + n_col
    lane8 = lax.broadcasted_iota(jnp.int32, (N_E, 8), 1)
    cols = jnp.where(lane8 == 0, n_col, jnp.where(lane8 == 1, seg_col, jnp.where(lane8 == 2, earlier_col, 0.0)))
    rows = cols.T
    lists, totals = _copy_lists(n_col.astype(jnp.int32), rows[0:1], rows[1:2], rows[2:3])
    return pc, pt, lists, totals


def _copy_lists(n_col, n_row_f, seg, earlier):
    shift = PIECE.bit_length() - 1

    def pieces(n, ci):
        if ci == 0:
            return jnp.right_shift(n, shift)
        return jnp.bitwise_and(jnp.right_shift(n, shift - ci), 1)

    def done(n, ci):
        if ci == 0:
            return jnp.zeros_like(n)
        return n - jnp.bitwise_and(n, (PIECE >> (ci - 1)) - 1)

    re = lax.broadcasted_iota(jnp.int32, (N_E, N_E), 0)
    ce = lax.broadcasted_iota(jnp.int32, (N_E, N_E), 1)
    n_row = n_row_f.astype(jnp.int32)
    lane8 = lax.broadcasted_iota(jnp.int32, (N_E, CLASS_SLOTS), 1)
    x = jnp.zeros((N_E, CLASS_SLOTS), F32)
    for ci in range(len(COPY_CLASSES)):
        x = jnp.where(lane8 == ci, pieces(n_col, ci).astype(F32), x)
    xb = x.astype(BF16)
    before = jnp.dot((ce < re).astype(BF16), xb, preferred_element_type=F32)
    totals = jnp.dot(jnp.ones((8, N_E), BF16), xb, preferred_element_type=F32)
    bulk_row = jnp.broadcast_to(pieces(n_row, 0).astype(F32), (8, N_E)).astype(BF16)
    before_bulk_row = jnp.dot(bulk_row, (re < ce).astype(BF16), preferred_element_type=F32)[0:1]

    lanes = lax.broadcasted_iota(jnp.int32, (N_E, LIST_W), 1)
    sub = lax.broadcasted_iota(jnp.int32, (8, N_E), 0)
    e_row = lax.broadcasted_iota(jnp.int32, (8, N_E), 1).astype(F32)
    lists = jnp.zeros((8, LIST_W), F32)
    for ci, (_, lane0) in enumerate(COPY_CLASSES):
        first = before[:, ci:ci + 1].astype(jnp.int32) + lane0
        sel = (lanes >= first) & (lanes < first + pieces(n_col, ci))
        d_row = done(n_row, ci).astype(F32)
        if ci == 0:
            d_row = d_row - PIECE * before_bulk_row
        v = jnp.where(sub == 0, seg + d_row, jnp.where(sub == 1, earlier + d_row,
                                                       jnp.where(sub == 2, e_row, 0.0)))
        lists = lists + jnp.dot(v, sel.astype(F32), preferred_element_type=F32,
                                precision=lax.Precision.HIGHEST)
    q = lax.broadcasted_iota(jnp.int32, (8, LIST_W), 1)
    s8 = lax.broadcasted_iota(jnp.int32, (8, LIST_W), 0)
    lists = lists + jnp.where((s8 < 2) & (q < RS // PIECE), (PIECE * q).astype(F32), 0.0)
    return lists, totals


def _mix_kernel(zg_ref, ya_ref, x_ref, gate1_ref, shift2_ref, scale2_ref,
                lng_ref, lnb_ref, ws_ref, bias_ref, wbb_ref, wo_ref, n2g_ref, rw_ref, rb_ref,
                x1_ref, h2_ref, lg_ref):
    row = lax.broadcasted_iota(jnp.int32, (CHUNK, 2 * CHUNK), 0)
    col = lax.broadcasted_iota(jnp.int32, (CHUNK, 2 * CHUNK), 1)
    causal = (col % CHUNK) <= row
    lane = lax.broadcasted_iota(jnp.int32, (CHUNK, 2 * GM_HD), 1)
    first = lane < GM_HD
    wpairs = [jnp.where(causal, ws_ref[j], 0.0).astype(BF16) for j in range(GM_HEADS // 2)]

    for g in range(TS_MIX // SUB_MIX):
        rows = pl.ds(g * SUB_MIX, SUB_MIX)
        z = _gelu(zg_ref[0, rows, 0:2 * GM_W].astype(F32))
        u = z[:, :GM_W]
        v = z[:, GM_W:]
        mu = jnp.mean(v, axis=-1, keepdims=True)
        vc = v - mu
        var = jnp.mean(vc * vc, axis=-1, keepdims=True)
        vn = vc * lax.rsqrt(var + EPS) * lng_ref[...] + lnb_ref[...]
        chunks = []
        for n in range(SUB_MIX // CHUNK):
            cols = []
            for j in range(GM_HEADS // 2):
                vp = vn[n * CHUNK:(n + 1) * CHUNK, 2 * GM_HD * j:2 * GM_HD * (j + 1)]
                rhs = jnp.concatenate([jnp.where(first, vp, 0.0), jnp.where(first, 0.0, vp)], axis=0)
                cols.append(jnp.dot(wpairs[j], rhs.astype(BF16), preferred_element_type=F32))
            chunks.append(jnp.concatenate(cols, axis=1) + bias_ref[...])
        mixed = jnp.concatenate(chunks, axis=0)
        gm = u * mixed
        yb = jnp.dot(gm.astype(BF16), wbb_ref[...], preferred_element_type=F32)
        g_a = zg_ref[0, rows, 2 * GM_W:2 * GM_W + D].astype(F32)
        g_b = zg_ref[0, rows, 2 * GM_W + D:].astype(F32)
        merged = _sigmoid(g_a) * ya_ref[0, rows, :].astype(F32) + _sigmoid(g_b) * yb
        o = jnp.dot(merged.astype(BF16), wo_ref[...], preferred_element_type=F32)
        x1 = x_ref[0, rows, :] + gate1_ref[0] * o
        x1_ref[0, rows, :] = x1
        h2 = _rms(x1) * (n2g_ref[...] * (1.0 + scale2_ref[0])) + shift2_ref[0]
        hb = h2.astype(BF16)
        h2_ref[0, rows, :] = hb
        lg_ref[:, g * SUB_MIX:(g + 1) * SUB_MIX] = lax.dot_general(
            rw_ref[...], hb, (((1,), (1,)), ((), ())), preferred_element_type=F32) + rb_ref[...]


def _route_kernel(lg_ref, pc_ref, pt_ref, lists_ref, totals_ref, cnt_ref, carry):
    @pl.when(pl.program_id(0) == 0)
    def _():
        carry[...] = jnp.zeros_like(carry)

    for t in range(ROUTE_TILES):
        pc, pt, lists, totals = _route_tile(lg_ref[:, t * TT:(t + 1) * TT], carry)
        pc_ref[t * TT:(t + 1) * TT, :] = pc
        pt_ref[:, t * TT:(t + 1) * TT] = pt
        lists_ref[t] = lists.astype(jnp.int32)
        totals_ref[t] = totals.astype(jnp.int32)
    cnt_ref[...] = jnp.broadcast_to(carry[...], cnt_ref.shape)


def _route_call(logits):
    assert NT % ROUTE_TILES == 0
    return pl.pallas_call(
        _route_kernel,
        grid=(NT // ROUTE_TILES,),
        in_specs=[pl.BlockSpec((N_E, ROUTE_TILES * TT), lambda i: (0, i))],
        out_specs=[pl.BlockSpec((ROUTE_TILES * TT, 8), lambda i: (i, 0)),
                   pl.BlockSpec((8, ROUTE_TILES * TT), lambda i: (0, i)),
                   pl.BlockSpec((ROUTE_TILES, 8, LIST_W), lambda i: (i, 0, 0)),
                   pl.BlockSpec((ROUTE_TILES, 8, CLASS_SLOTS), lambda i: (i, 0, 0)),
                   pl.BlockSpec((N_E, 8), lambda i: (0, 0))],
        out_shape=[jax.ShapeDtypeStruct((T, 8), F32),
                   jax.ShapeDtypeStruct((8, T), F32),
                   jax.ShapeDtypeStruct((NT, 8, LIST_W), jnp.int32),
                   jax.ShapeDtypeStruct((NT, 8, CLASS_SLOTS), jnp.int32),
                   jax.ShapeDtypeStruct((N_E, 8), F32)],
        scratch_shapes=[pltpu.VMEM((N_E, 1), F32)],
        compiler_params=pltpu.CompilerParams(dimension_semantics=("arbitrary",)),
        name="route",
    )(logits)


def _mix_call(zg, ya2d, x, mod3, ln_g, ln_b, ws_pairs, bias_full, wbb, wo, n2g, rw, rb):
    tok_spec = pl.BlockSpec((1, TS_MIX, D), lambda b, s: (b, s, 0))
    full = lambda *shape: pl.BlockSpec(shape, lambda b, s: (0,) * len(shape))
    mod_spec = lambda j: pl.BlockSpec((1, 1, D), lambda b, s: (b, 0, j))
    return pl.pallas_call(
        _mix_kernel,
        grid=(B, S // TS_MIX),
        in_specs=[pl.BlockSpec((1, TS_MIX, zg.shape[-1]), lambda b, s: (b, s, 0)), tok_spec, tok_spec,
                  mod_spec(2), mod_spec(3), mod_spec(4),
                  full(1, GM_W), full(1, GM_W),
                  full(GM_HEADS // 2, CHUNK, 2 * CHUNK),
                  full(CHUNK, GM_W),
                  full(GM_W, D), full(D, D), full(1, D),
                  full(N_E, D), full(N_E, 1)],
        out_specs=[tok_spec, tok_spec,
                   pl.BlockSpec((N_E, TS_MIX), lambda b, s: (0, b * (S // TS_MIX) + s))],
        out_shape=[jax.ShapeDtypeStruct((B, S, D), F32),
                   jax.ShapeDtypeStruct((B, S, D), BF16),
                   jax.ShapeDtypeStruct((N_E, T), F32)],
        compiler_params=pltpu.CompilerParams(vmem_limit_bytes=VMEM_LIMIT),
        name="gmlp_merge_norm2",
    )(zg, ya2d, x, mod3, mod3, mod3, ln_g, ln_b, ws_pairs, bias_full, wbb, wo, n2g, rw, rb)


def _pack_rows(v):
    return pltpu.pack_elementwise([v[:, :HALF], v[:, HALF:]], packed_dtype=BF16)


def _unpack_rows(w):
    halves = [pltpu.unpack_elementwise(w, index=i, packed_dtype=BF16, unpacked_dtype=F32)
              for i in range(2)]
    return jnp.concatenate(halves, axis=1)


def _load_grouped(ref, rows, first=0):
    return jnp.concatenate([ref[pl.ds(first * SUB + c, rows, stride=SUB), :] for c in range(SUB)], axis=1)


def _store_grouped(ref, w, rows, first=0):
    for c in range(SUB):
        ref[pl.ds(first * SUB + c, rows, stride=SUB), :] = w[:, c * LANES:(c + 1) * LANES]


def _start_tile_runs(tile, src_ref, dst_ref, src_tbl, dst_tbl, totals_tbl, sem):
    for ci, (rows, lane0) in enumerate(COPY_CLASSES):
        base = tile * LIST_W + lane0

        def start(i, carry, rows=rows, base=base):
            s = pl.multiple_of(src_tbl[base + i] * SUB, SUB)
            d = pl.multiple_of(dst_tbl[base + i] * SUB, SUB)
            pltpu.make_async_copy(src_ref.at[pl.ds(s, rows * SUB)],
                                  dst_ref.at[pl.ds(d, rows * SUB)], sem).start()
            return carry

        lax.fori_loop(0, totals_tbl[tile * CLASS_SLOTS + ci], start, 0)


def _wait_tile_runs(src_ref, dst_ref, sem):
    pltpu.make_async_copy(src_ref, dst_ref, sem).wait()


def _dispatch_kernel(loc_ref, glob_ref, tot_ref, pend_ref, h_ref, pt_ref, xs_ref,
                     sbuf0, sbuf1, zbuf, sem_z, sem0, sem1):
    j = pl.program_id(0)

    def zero_last_blocks(start):
        for e in range(N_E):
            prev = pend_ref[e - 1] if e > 0 else 0
            end = pend_ref[e]

            @pl.when(end > prev)
            def _():
                first = pl.multiple_of((end - TM) * SUB, TM * SUB)
                cp = pltpu.make_async_copy(zbuf, xs_ref.at[pl.ds(first, TM * SUB)], sem_z)
                if start:
                    cp.start()
                else:
                    cp.wait()

    @pl.when(j == 0)
    def _():
        zbuf[...] = jnp.zeros_like(zbuf)
        zero_last_blocks(True)

    rows = lax.broadcasted_iota(jnp.int32, (RS, TT), 0)
    words = []
    for t in range(2):
        pos = pt_ref[:, t * TT:(t + 1) * TT].astype(jnp.int32)
        hit = rows == pos[0:1, :]
        for k in range(1, TOP_K):
            hit = hit | (rows == pos[k:k + 1, :])
        pm = jnp.where(hit, 1.0, 0.0).astype(BF16)
        srt = jnp.dot(pm, h_ref[t * TT:(t + 1) * TT, :], preferred_element_type=F32)
        words.append(_pack_rows(srt))
    head = xs_ref.at[pl.ds(0, RS * SUB)]

    @pl.when(j == 0)
    def _():
        zero_last_blocks(False)

    for t, (sbuf, sem) in enumerate(((sbuf0, sem0), (sbuf1, sem1))):
        @pl.when(j >= 1)
        def _(sbuf=sbuf, sem=sem):
            _wait_tile_runs(sbuf, head, sem)
        _store_grouped(sbuf, words[t], RS)
        _start_tile_runs(2 * j + t, sbuf, xs_ref, loc_ref, glob_ref, tot_ref, sem)

    @pl.when(j == NT // 2 - 1)
    def _():
        _wait_tile_runs(sbuf0, head, sem0)
        _wait_tile_runs(sbuf1, head, sem1)


def _dispatch_call(loc_t, glob_t, tot_t, pad_end, h2, pos_t):
    assert NT % 2 == 0
    grid_spec = pltpu.PrefetchScalarGridSpec(
        num_scalar_prefetch=4,
        grid=(NT // 2,),
        in_specs=[pl.BlockSpec((2 * TT, D), lambda j, *_: (j, 0)),
                  pl.BlockSpec((8, 2 * TT), lambda j, *_: (0, j))],
        out_specs=pl.BlockSpec(memory_space=pl.ANY),
        scratch_shapes=[pltpu.VMEM((RS * SUB, LANES), U32),
                        pltpu.VMEM((RS * SUB, LANES), U32),
                        pltpu.VMEM((TM * SUB, LANES), U32),
                        pltpu.SemaphoreType.DMA,
                        pltpu.SemaphoreType.DMA,
                        pltpu.SemaphoreType.DMA],
    )
    return pl.pallas_call(
        _dispatch_kernel,
        grid_spec=grid_spec,
        out_shape=jax.ShapeDtypeStruct((N_ROWS * SUB, LANES), U32),
        compiler_params=pltpu.CompilerParams(dimension_semantics=("arbitrary",),
                                             vmem_limit_bytes=VMEM_LIMIT),
        name="dispatch",
    )(loc_t, glob_t, tot_t, pad_end, h2, pos_t)


def _moe_kernel(be_ref, bf_ref, nx_ref, nv_ref, xs_ref, wi_hbm, bi_ref, wo_hbm, bo_ref, ys_ref,
                wi_f32, wo_f32, wi_bf, wo_bf, sem_i, sem_o):
    step = pl.program_id(0)

    def fetch(e):
        return (pltpu.make_async_copy(wi_hbm.at[e], wi_f32, sem_i),
                pltpu.make_async_copy(wo_hbm.at[e], wo_f32, sem_o))

    @pl.when(step == 0)
    def _():
        for cp in fetch(be_ref[0]):
            cp.start()

    def load_weights(i):
        @pl.when(bf_ref[i] == 1)
        def _():
            for cp in fetch(be_ref[i]):
                cp.wait()
            wi_bf[...] = wi_f32[...].astype(BF16)
            wo_bf[...] = wo_f32[...].astype(BF16)

            @pl.when(nx_ref[i] >= 0)
            def _():
                for cp in fetch(nx_ref[i]):
                    cp.start()

    def ffn(i, first, rows):
        e = be_ref[i]
        xb = _unpack_rows(_load_grouped(xs_ref, rows, first)).astype(BF16)
        gu = jnp.dot(xb, wi_bf[...], preferred_element_type=F32) + bi_ref[pl.ds(e, 1), :]
        gate = jnp.minimum(gu[:, :D_E], LIMIT)
        up = jnp.clip(gu[:, D_E:], -LIMIT, LIMIT)
        act = (up + 1.0) * (gate * _sigmoid(ALPHA * gate))
        y = jnp.dot(act.astype(BF16), wo_bf[...], preferred_element_type=F32) + bo_ref[pl.ds(e, 1), :]
        _store_grouped(ys_ref, _pack_rows(y), rows, first)

    i0 = step * BPS
    last = i0 + BPS - 1
    uniform = (last < nv_ref[0]) & (be_ref[i0] == be_ref[last])

    @pl.when(uniform)
    def _():
        load_weights(i0)
        ffn(i0, 0, BPS * TM)

    @pl.when(jnp.logical_not(uniform))
    def _():
        for sub in range(BPS):
            i = i0 + sub

            @pl.when(i < nv_ref[0])
            def _(i=i, sub=sub):
                load_weights(i)
                ffn(i, sub * TM, TM)


def _moe_call(blk_e, blk_first, blk_next, n_valid, xs, w_in, b_in, w_out, b_out):
    assert N_BLOCKS % BPS == 0

    def row_map(s, be, bf, nx, nv):
        last = (nv[0] + BPS - 1) // BPS - 1
        return (jnp.maximum(jnp.minimum(s, last), 0), 0)

    grid_spec = pltpu.PrefetchScalarGridSpec(
        num_scalar_prefetch=4,
        grid=(N_BLOCKS // BPS,),
        in_specs=[pl.BlockSpec((BPS * TM * SUB, LANES), row_map),
                  pl.BlockSpec(memory_space=pl.ANY),
                  pl.BlockSpec((N_E, 2 * D_E), lambda s, *_: (0, 0)),
                  pl.BlockSpec(memory_space=pl.ANY),
                  pl.BlockSpec((N_E, D), lambda s, *_: (0, 0))],
        out_specs=pl.BlockSpec((BPS * TM * SUB, LANES), row_map),
        scratch_shapes=[pltpu.VMEM((D, 2 * D_E), F32),
                        pltpu.VMEM((D_E, D), F32),
                        pltpu.VMEM((D, 2 * D_E), BF16),
                        pltpu.VMEM((D_E, D), BF16),
                        pltpu.SemaphoreType.DMA,
                        pltpu.SemaphoreType.DMA],
    )
    return pl.pallas_call(
        _moe_kernel,
        grid_spec=grid_spec,
        out_shape=jax.ShapeDtypeStruct((N_ROWS * SUB, LANES), U32),
        compiler_params=pltpu.CompilerParams(dimension_semantics=("arbitrary",),
                                             vmem_limit_bytes=VMEM_LIMIT),
        name="moe_experts",
    )(blk_e, blk_first, blk_next, n_valid, xs, w_in, b_in, w_out, b_out)


def _combine_kernel(loc_ref, glob_ref, tot_ref, ys_ref, pc_ref, x1_ref, gate2_ref, fg_ref, o_ref,
                    buf0, buf1, buf2, buf3, sem0, sem1, sem2, sem3):
    j = pl.program_id(0)
    head = ys_ref.at[pl.ds(0, RS * SUB)]
    even = ((buf0, sem0), (buf1, sem1))
    odd = ((buf2, sem2), (buf3, sem3))

    def fetch(step, slots):
        for t, (buf, sem) in enumerate(slots):
            _start_tile_runs(2 * step + t, ys_ref, buf, glob_ref, loc_ref, tot_ref, sem)

    @pl.when(j == 0)
    def _():
        fetch(0, even)

    col = lax.broadcasted_iota(jnp.int32, (TT, RS), 1)
    wms = []
    for t in range(2):
        pc = pc_ref[t * TT:(t + 1) * TT, :]
        pos = pc.astype(jnp.int32)
        wm = jnp.zeros((TT, RS), F32)
        for k in range(TOP_K):
            wm = jnp.where(col == pos[:, k:k + 1], pc[:, TOP_K + k:TOP_K + k + 1], wm)
        wms.append(wm.astype(BF16))

    def step(cur, nxt):
        @pl.when(j + 1 < NT // 2)
        def _():
            fetch(j + 1, nxt)
        for t, (buf, sem) in enumerate(cur):
            _wait_tile_runs(head, buf, sem)
            yt = _unpack_rows(_load_grouped(buf, RS)).astype(BF16)
            acc = jnp.dot(wms[t], yt, preferred_element_type=F32)
            x2 = x1_ref[t * TT:(t + 1) * TT, :] + gate2_ref[0] * acc
            o_ref[t * TT:(t + 1) * TT, :] = _rms(x2) * fg_ref[...]

    @pl.when(j % 2 == 0)
    def _():
        step(even, odd)

    @pl.when(j % 2 == 1)
    def _():
        step(odd, even)


def _combine_call(loc_t, glob_t, tot_t, ys, pos_c, x1, mod3, final_g):
    per_b = S // (2 * TT)
    grid_spec = pltpu.PrefetchScalarGridSpec(
        num_scalar_prefetch=3,
        grid=(NT // 2,),
        in_specs=[pl.BlockSpec(memory_space=pl.ANY),
                  pl.BlockSpec((2 * TT, 8), lambda j, *_: (j, 0)),
                  pl.BlockSpec((2 * TT, D), lambda j, *_: (j, 0)),
                  pl.BlockSpec((1, 1, D), lambda j, *_: (j // per_b, 0, 5)),
                  pl.BlockSpec((1, D), lambda j, *_: (0, 0))],
        out_specs=pl.BlockSpec((2 * TT, D), lambda j, *_: (j, 0)),
        scratch_shapes=[pltpu.VMEM((RS * SUB, LANES), U32)] * 4 + [pltpu.SemaphoreType.DMA] * 4,
    )
    return pl.pallas_call(
        _combine_kernel,
        grid_spec=grid_spec,
        out_shape=jax.ShapeDtypeStruct((T, D), F32),
        compiler_params=pltpu.CompilerParams(dimension_semantics=("arbitrary",),
                                             vmem_limit_bytes=VMEM_LIMIT),
        name="combine_norm",
    )(loc_t, glob_t, tot_t, ys, pos_c, x1, mod3, final_g)


def kernel(x, c, ada_w, ada_b, norm1_g, w_in, ssm_a_re, ssm_a_im, ssm_log_dt, ssm_b_re, ssm_b_im, ssm_c_re, ssm_c_im, ssm_d, ssm_glu_w, ssm_glu_b, w_branch_a, gmlp_ln_g, gmlp_ln_b, gmlp_ws, gmlp_bs, w_branch_b, w_out, norm2_g, router_w, router_b, moe_w_in, moe_b_in, moe_w_out, moe_b_out, final_g):
    depth = ada_w.shape[0]
    assert depth == 1, "the final rms_norm is fused into the combine kernel of the only layer"
    for layer in range(depth):
        mod = _mod_call(c, ada_w[layer], ada_b[layer])
        mod3 = mod.reshape(B, 1, 6 * D)

        u, zg = _proj_call(x, norm1_g[layer], mod3, w_in[layer].astype(BF16))

        bm, cre, cim, are, aim = _s5_params(ssm_a_re[layer], ssm_a_im[layer], ssm_log_dt[layer],
                                            ssm_b_re[layer], ssm_b_im[layer],
                                            ssm_c_re[layer], ssm_c_im[layer])
        ya = _s5_call(u, bm, cre, cim, are, aim,
                      ssm_d[layer].reshape(1, SSM_W), ssm_glu_w[layer].astype(BF16),
                      ssm_glu_b[layer].reshape(1, SSM_W), w_branch_a[layer].astype(BF16))

        ws = gmlp_ws[layer]
        ws_pairs = jnp.concatenate([ws[0::2], ws[1::2]], axis=-1)
        bias_full = jnp.repeat(gmlp_bs[layer].T, GM_HD, axis=1)
        x1, h2, logits = _mix_call(
            zg, ya, x, mod3,
            gmlp_ln_g[layer].reshape(1, GM_W), gmlp_ln_b[layer].reshape(1, GM_W),
            ws_pairs, bias_full, w_branch_b[layer].astype(BF16), w_out[layer].astype(BF16),
            norm2_g[layer].reshape(1, D), router_w[layer].T.astype(BF16),
            router_b[layer].reshape(N_E, 1))

        pos_c, pos_t, lists, totals, cnt = _route_call(logits)
        counts = cnt[:, 0].astype(jnp.int32)
        nblk = (counts + TM - 1) // TM
        blk_end = jnp.cumsum(nblk)
        pad_end = (blk_end * TM).astype(jnp.int32)
        experts = jnp.arange(N_E, dtype=jnp.int32)
        blk_ids = jnp.arange(N_BLOCKS, dtype=jnp.int32)
        blk_e = jnp.sum((blk_end[None, :] <= blk_ids[:, None]).astype(jnp.int32), axis=1)
        blk_e = jnp.minimum(blk_e, N_E - 1)
        blk_first = jnp.concatenate([jnp.ones((1,), jnp.int32),
                                     (blk_e[1:] != blk_e[:-1]).astype(jnp.int32)])
        later = (experts[None, :] > experts[:, None]) & (nblk[None, :] > 0)
        next_e = jnp.min(jnp.where(later, experts[None, :], N_E), axis=1)
        next_e = jnp.where(next_e == N_E, -1, next_e)
        blk_next = jnp.sum(jnp.where(blk_e[:, None] == experts[None, :], next_e[None, :], 0), axis=1)
        n_valid = blk_end[-1:].astype(jnp.int32)
        pad_start = pad_end - nblk * TM
        loc_t = lists[:, 0, :].reshape(NT * LIST_W)
        owner = lists[:, 2, :, None] == experts[None, None, :]
        glob_t = (lists[:, 1, :] + jnp.sum(jnp.where(owner, pad_start, 0), axis=-1)).reshape(NT * LIST_W)
        tot_t = totals[:, 0, :].reshape(NT * CLASS_SLOTS)

        xs = _dispatch_call(loc_t, glob_t, tot_t, pad_end, h2.reshape(T, D), pos_t)
        ys = _moe_call(blk_e, blk_first, blk_next.astype(jnp.int32), n_valid, xs, moe_w_in[layer],
                       moe_b_in[layer], moe_w_out[layer], moe_b_out[layer])
        x = _combine_call(loc_t, glob_t, tot_t, ys, pos_c, x1.reshape(T, D),
                          mod3, final_g.reshape(1, D)).reshape(B, S, D)
    return x
```

```python
import math

import jax
import jax.numpy as jnp
from jax import lax
from jax.experimental import pallas as pl
from jax.experimental.pallas import tpu as pltpu

F32 = jnp.float32
BF16 = jnp.bfloat16

D = 1024
B = 8
S = 2048
T = B * S
SSM_W = 512
SSM_G = 32
SSM_H = 16
SSM_P = 64
N_PACK = 4
PACK_G = SSM_G // N_PACK
GM_W = 512
GM_HEADS = 8
GM_HD = 64
CHUNK = 128
N_E = 32
TOP_K = 4
D_E = 1024
LIMIT = 7.0
ALPHA = 1.702
EPS = 1e-6

TS_PROJ = 1024
SUB_PROJ = 512
L_SSM = 128
R_SSM = L_SSM * B
TS_MIX = 1024
SUB_MIX = 512
TT = 256
NT = T // TT
ROUTE_TILES = 8
RS = TOP_K * TT
TM = 256
BPS = 4
N_ROWS = T * TOP_K + N_E * TM
N_BLOCKS = N_ROWS // TM
LANES = 128
HALF = D // 2
SUB = HALF // LANES
PIECE = 32
COPY_CLASSES = ((PIECE, 0),) + tuple((PIECE >> s, RS // PIECE + N_E * (s - 1))
                                     for s in range(1, PIECE.bit_length()))
LIST_W = 256
CLASS_SLOTS = 8
assert COPY_CLASSES[-1][1] + N_E <= LIST_W and len(COPY_CLASSES) <= CLASS_SLOTS
VMEM_LIMIT = 56 * 1024 * 1024
U32 = jnp.uint32
ACT = jnp.bfloat16


def _sigmoid(v):
    return 0.5 * jnp.tanh(0.5 * v) + 0.5


def _gelu(v):
    c = math.sqrt(2.0 / math.pi)
    inner = v * (c + (c * 0.044715) * (v * v))
    return v * (0.5 + 0.5 * jnp.tanh(inner))


def _rms(v):
    return v * lax.rsqrt(jnp.mean(v * v, axis=-1, keepdims=True) + EPS)


def _mod_kernel(c_ref, w_ref, b_ref, o_ref):
    cv = c_ref[...]
    sv = cv * _sigmoid(cv)
    o_ref[...] = jnp.dot(sv, w_ref[...], preferred_element_type=F32,
                         precision=lax.Precision.HIGHEST) + b_ref[...]


def _mod_call(c, ada_w, ada_b):
    n = ada_w.shape[1]
    return pl.pallas_call(
        _mod_kernel,
        grid=(n // D,),
        in_specs=[pl.BlockSpec((B, D), lambda j: (0, 0)),
                  pl.BlockSpec((D, D), lambda j: (0, j)),
                  pl.BlockSpec((1, D), lambda j: (0, j))],
        out_specs=pl.BlockSpec((B, D), lambda j: (0, j)),
        out_shape=jax.ShapeDtypeStruct((B, n), F32),
        name="adaln_mod",
    )(c, ada_w, ada_b.reshape(1, n))


def _proj_kernel(x_ref, g_ref, shift_ref, scale_ref, w_ref, u_ref, zg_ref):
    gain = g_ref[...] * (1.0 + scale_ref[0])
    for g in range(TS_PROJ // SUB_PROJ):
        rows = pl.ds(g * SUB_PROJ, SUB_PROJ)
        hb = (_rms(x_ref[0, rows, :]) * gain + shift_ref[0]).astype(BF16)
        u_ref[0, rows, :] = jnp.dot(hb, w_ref[:, 0:SSM_W], preferred_element_type=F32)
        zg_ref[0, rows, :] = jnp.dot(hb, w_ref[:, SSM_W:],
                                     preferred_element_type=F32).astype(ACT)


def _proj_call(x, norm_g, mod3, w_in_bf):
    pw = w_in_bf.shape[1]
    tok_spec = pl.BlockSpec((1, TS_PROJ, D), lambda b, s: (b, s, 0))
    return pl.pallas_call(
        _proj_kernel,
        grid=(B, S // TS_PROJ),
        in_specs=[tok_spec,
                  pl.BlockSpec((1, D), lambda b, s: (0, 0)),
                  pl.BlockSpec((1, 1, D), lambda b, s: (b, 0, 0)),
                  pl.BlockSpec((1, 1, D), lambda b, s: (b, 0, 1)),
                  pl.BlockSpec((D, pw), lambda b, s: (0, 0))],
        out_specs=[pl.BlockSpec((1, TS_PROJ, SSM_W), lambda b, s: (b, s, 0)),
                   pl.BlockSpec((1, TS_PROJ, pw - SSM_W), lambda b, s: (b, s, 0))],
        out_shape=[jax.ShapeDtypeStruct((B, S, SSM_W), F32),
                   jax.ShapeDtypeStruct((B, S, pw - SSM_W), ACT)],
        compiler_params=pltpu.CompilerParams(vmem_limit_bytes=VMEM_LIMIT),
        name="norm_proj",
    )(x, norm_g.reshape(1, D), mod3, mod3, w_in_bf)


def _s5_kernel(u_ref, bm_ref, cre_ref, cim_ref, are_ref, aim_ref, d_ref, gw_ref, gb_ref, wa_ref,
               o_ref, usc, ysc, sre, sim, st_re, st_im):
    @pl.when(pl.program_id(0) == 0)
    def _():
        st_re[...] = jnp.zeros_like(st_re)
        st_im[...] = jnp.zeros_like(st_im)

    nslab = SSM_W // LANES
    for b in range(B):
        for c in range(nslab):
            usc[c, pl.ds(b, L_SSM, stride=B), :] = u_ref[b, :, c * LANES:(c + 1) * LANES]
    u = jnp.concatenate([usc[c] for c in range(nslab)], axis=1)
    ub = u.astype(BF16)
    half = PACK_G * SSM_P
    ys = []
    for k in range(N_PACK):
        pin = PACK_G * SSM_H
        bu = jnp.dot(ub[:, pin * k:pin * (k + 1)], bm_ref[k], preferred_element_type=F32)
        sre[k] = bu[:, :half]
        sim[k] = bu[:, half:]
        ar = are_ref[k]
        ai = aim_ref[k]
        r = st_re[k]
        m = st_im[k]
        for t in range(L_SSM):
            rows = pl.ds(t * B, B)
            nr = ar * r - ai * m + sre[k, rows, :]
            m = ar * m + ai * r + sim[k, rows, :]
            r = nr
            sre[k, rows, :] = r
            sim[k, rows, :] = m
        st_re[k] = r
        st_im[k] = m
        yk = jnp.dot(sre[k].astype(BF16), cre_ref[k], preferred_element_type=F32)
        yk = yk + jnp.dot(sim[k].astype(BF16), cim_ref[k], preferred_element_type=F32)
        ys.extend(yk[:, j * LANES:(j + 1) * LANES] for j in range(pin // LANES))
    for c in range(nslab):
        uc = usc[c]
        ysc[c] = ys[c] + d_ref[:, c * LANES:(c + 1) * LANES] * uc
    y = jnp.concatenate(
        [jnp.concatenate([ysc[c, pl.ds(b, L_SSM, stride=B), :] for c in range(nslab)], axis=1)
         for b in range(B)], axis=0)
    z = _gelu(y)
    gl = jnp.dot(z.astype(BF16), gw_ref[...], preferred_element_type=F32) + gb_ref[...]
    out = z * _sigmoid(gl)
    o = jnp.dot(out.astype(BF16), wa_ref[...], preferred_element_type=F32)
    for b in range(B):
        o_ref[b] = o[b * L_SSM:(b + 1) * L_SSM].astype(ACT)


def _s5_call(u, bm, cre, cim, are, aim, d_skip, glu_w, glu_b, w_a):
    half = PACK_G * SSM_P
    full = lambda *shape: pl.BlockSpec(shape, lambda i: (0,) * len(shape))
    return pl.pallas_call(
        _s5_kernel,
        grid=(S // L_SSM,),
        in_specs=[pl.BlockSpec((B, L_SSM, SSM_W), lambda i: (0, i, 0)),
                  full(N_PACK, PACK_G * SSM_H, 2 * half),
                  full(N_PACK, half, PACK_G * SSM_H),
                  full(N_PACK, half, PACK_G * SSM_H),
                  full(N_PACK, B, half),
                  full(N_PACK, B, half),
                  full(1, SSM_W),
                  full(SSM_W, SSM_W),
                  full(1, SSM_W),
                  full(SSM_W, D)],
        out_specs=pl.BlockSpec((B, L_SSM, D), lambda i: (0, i, 0)),
        out_shape=jax.ShapeDtypeStruct((B, S, D), ACT),
        scratch_shapes=[pltpu.VMEM((SSM_W // LANES, R_SSM, LANES), F32),
                        pltpu.VMEM((SSM_W // LANES, R_SSM, LANES), F32),
                        pltpu.VMEM((N_PACK, R_SSM, half), F32),
                        pltpu.VMEM((N_PACK, R_SSM, half), F32),
                        pltpu.VMEM((N_PACK, B, half), F32),
                        pltpu.VMEM((N_PACK, B, half), F32)],
        compiler_params=pltpu.CompilerParams(dimension_semantics=("arbitrary",),
                                             vmem_limit_bytes=VMEM_LIMIT),
        name="s5_branch",
    )(u, bm, cre, cim, are, aim, d_skip, glu_w, glu_b, w_a)


def _s5_params(a_re, a_im, log_dt, b_re, b_im, c_re, c_im):
    dt = jnp.exp(log_dt)[:, None]
    mag = jnp.exp(a_re * dt)
    lr = mag * jnp.cos(a_im * dt)
    li = mag * jnp.sin(a_im * dt)
    den = a_re * a_re + a_im * a_im
    cr = ((lr - 1.0) * a_re + li * a_im) / den
    ci = (li * a_re - (lr - 1.0) * a_im) / den
    bbr = cr[..., None] * b_re - ci[..., None] * b_im
    bbi = cr[..., None] * b_im + ci[..., None] * b_re
    eye = jnp.eye(PACK_G, dtype=F32)
    half = PACK_G * SSM_P

    def pack_b(m):
        m4 = m.reshape(N_PACK, PACK_G, SSM_P, SSM_H)
        return jnp.einsum('kgph,gj->kghjp', m4, eye).reshape(N_PACK, PACK_G * SSM_H, half)

    def pack_c(m):
        m4 = m.reshape(N_PACK, PACK_G, SSM_H, SSM_P)
        return jnp.einsum('kghp,gj->kgpjh', m4, eye).reshape(N_PACK, half, PACK_G * SSM_H)

    bm = jnp.concatenate([pack_b(bbr), pack_b(bbi)], axis=-1).astype(BF16)
    cre = pack_c(c_re).astype(BF16)
    cim = (-pack_c(c_im)).astype(BF16)
    are = jnp.broadcast_to(lr.reshape(N_PACK, 1, half), (N_PACK, B, half))
    aim = jnp.broadcast_to(li.reshape(N_PACK, 1, half), (N_PACK, B, half))
    return bm, cre, cim, are, aim


def _route_tile(lt, carry):
    sub = lax.broadcasted_iota(jnp.int32, lt.shape, 0).astype(F32)
    sels, vals = [], []
    for _ in range(TOP_K):
        m = jnp.max(lt, axis=0, keepdims=True)
        idx = jnp.min(jnp.where(lt == m, sub, float(N_E)), axis=0, keepdims=True)
        sel = sub == idx
        sels.append(sel)
        vals.append(m)
        lt = jnp.where(sel, -jnp.inf, lt)
    member = sels[0].astype(F32)
    for k in range(1, TOP_K):
        member = member + sels[k].astype(F32)
    n_col = jnp.sum(member, axis=1, keepdims=True)

    re = lax.broadcasted_iota(jnp.int32, (N_E, N_E), 0)
    ce = lax.broadcasted_iota(jnp.int32, (N_E, N_E), 1)
    nb = jnp.broadcast_to(n_col, (N_E, 8)).astype(BF16)
    seg_col = jnp.dot((ce < re).astype(BF16), nb, preferred_element_type=F32)[:, 0:1]
    rt = lax.broadcasted_iota(jnp.int32, (TT, TT), 0)
    ct = lax.broadcasted_iota(jnp.int32, (TT, TT), 1)
    rank = jnp.dot(member.astype(BF16), (rt < ct).astype(BF16), preferred_element_type=F32)
    posb = seg_col + rank
    denom = jnp.zeros_like(vals[0])
    exps = []
    for k in range(TOP_K):
        e = jnp.exp(vals[k] - vals[0])
        exps.append(e)
        denom = denom + e
    s8 = lax.broadcasted_iota(jnp.int32, (8, TT), 0)
    pt = jnp.zeros((8, TT), F32)
    for k in range(TOP_K):
        pk = jnp.sum(jnp.where(sels[k], posb, 0.0), axis=0, keepdims=True)
        pt = jnp.where(s8 == k, pk, pt)
        pt = jnp.where(s8 == TOP_K + k, exps[k] / denom, pt)
    pc = pt.T
    earlier_col = carry[...]
    carry[...] = earlier_col + n_col
    lane8 = lax.broadcasted_iota(jnp.int32, (N_E, 8), 1)
    cols = jnp.where(lane8 == 0, n_col, jnp.where(lane8 == 1, seg_col, jnp.where(lane8 == 2, earlier_col, 0.0)))
    rows = cols.T
    lists, totals = _copy_lists(n_col.astype(jnp.int32), rows[0:1], rows[1:2], rows[2:3])
    return pc, pt, lists, totals


def _copy_lists(n_col, n_row_f, seg, earlier):
    shift = PIECE.bit_length() - 1

    def pieces(n, ci):
        if ci == 0:
            return jnp.right_shift(n, shift)
        return jnp.bitwise_and(jnp.right_shift(n, shift - ci), 1)

    def done(n, ci):
        if ci == 0:
            return jnp.zeros_like(n)
        return n - jnp.bitwise_and(n, (PIECE >> (ci - 1)) - 1)

    re = lax.broadcasted_iota(jnp.int32, (N_E, N_E), 0)
    ce = lax.broadcasted_iota(jnp.int32, (N_E, N_E), 1)
    n_row = n_row_f.astype(jnp.int32)
    lane8 = lax.broadcasted_iota(jnp.int32, (N_E, CLASS_SLOTS), 1)
    x = jnp.zeros((N_E, CLASS_SLOTS), F32)
    for ci in range(len(COPY_CLASSES)):
        x = jnp.where(lane8 == ci, pieces(n_col, ci).astype(F32), x)
    xb = x.astype(BF16)
    before = jnp.dot((ce < re).astype(BF16), xb, preferred_element_type=F32)
    totals = jnp.dot(jnp.ones((8, N_E), BF16), xb, preferred_element_type=F32)
    bulk_row = jnp.broadcast_to(pieces(n_row, 0).astype(F32), (8, N_E)).astype(BF16)
    before_bulk_row = jnp.dot(bulk_row, (re < ce).astype(BF16), preferred_element_type=F32)[0:1]

    lanes = lax.broadcasted_iota(jnp.int32, (N_E, LIST_W), 1)
    sub = lax.broadcasted_iota(jnp.int32, (8, N_E), 0)
    e_row = lax.broadcasted_iota(jnp.int32, (8, N_E), 1).astype(F32)
    lists = jnp.zeros((8, LIST_W), F32)
    for ci, (_, lane0) in enumerate(COPY_CLASSES):
        first = before[:, ci:ci + 1].astype(jnp.int32) + lane0
        sel = (lanes >= first) & (lanes < first + pieces(n_col, ci))
        d_row = done(n_row, ci).astype(F32)
        if ci == 0:
            d_row = d_row - PIECE * before_bulk_row
        v = jnp.where(sub == 0, seg + d_row, jnp.where(sub == 1, earlier + d_row,
                                                       jnp.where(sub == 2, e_row, 0.0)))
        lists = lists + jnp.dot(v, sel.astype(F32), preferred_element_type=F32,
                                precision=lax.Precision.HIGHEST)
    q = lax.broadcasted_iota(jnp.int32, (8, LIST_W), 1)
    s8 = lax.broadcasted_iota(jnp.int32, (8, LIST_W), 0)
    lists = lists + jnp.where((s8 < 2) & (q < RS // PIECE), (PIECE * q).astype(F32), 0.0)
    return lists, totals


def _mix_kernel(zg_ref, ya_ref, x_ref, gate1_ref, shift2_ref, scale2_ref,
                lng_ref, lnb_ref, ws_ref, bias_ref, wbb_ref, wo_ref, n2g_ref, rw_ref, rb_ref,
                x1_ref, h2_ref, lg_ref):
    row = lax.broadcasted_iota(jnp.int32, (CHUNK, 2 * CHUNK), 0)
    col = lax.broadcasted_iota(jnp.int32, (CHUNK, 2 * CHUNK), 1)
    causal = (col % CHUNK) <= row
    lane = lax.broadcasted_iota(jnp.int32, (CHUNK, 2 * GM_HD), 1)
    first = lane < GM_HD
    wpairs = [jnp.where(causal, ws_ref[j], 0.0).astype(BF16) for j in range(GM_HEADS // 2)]

    for g in range(TS_MIX // SUB_MIX):
        rows = pl.ds(g * SUB_MIX, SUB_MIX)
        z = _gelu(zg_ref[0, rows, 0:2 * GM_W].astype(F32))
        u = z[:, :GM_W]
        v = z[:, GM_W:]
        mu = jnp.mean(v, axis=-1, keepdims=True)
        vc = v - mu
        var = jnp.mean(vc * vc, axis=-1, keepdims=True)
        vn = vc * lax.rsqrt(var + EPS) * lng_ref[...] + lnb_ref[...]
        chunks = []
        for n in range(SUB_MIX // CHUNK):
            cols = []
            for j in range(GM_HEADS // 2):
                vp = vn[n * CHUNK:(n + 1) * CHUNK, 2 * GM_HD * j:2 * GM_HD * (j + 1)]
                rhs = jnp.concatenate([jnp.where(first, vp, 0.0), jnp.where(first, 0.0, vp)], axis=0)
                cols.append(jnp.dot(wpairs[j], rhs.astype(BF16), preferred_element_type=F32))
            chunks.append(jnp.concatenate(cols, axis=1) + bias_ref[...])
        mixed = jnp.concatenate(chunks, axis=0)
        gm = u * mixed
        yb = jnp.dot(gm.astype(BF16), wbb_ref[...], preferred_element_type=F32)
        g_a = zg_ref[0, rows, 2 * GM_W:2 * GM_W + D].astype(F32)
        g_b = zg_ref[0, rows, 2 * GM_W + D:].astype(F32)
        merged = _sigmoid(g_a) * ya_ref[0, rows, :].astype(F32) + _sigmoid(g_b) * yb
        o = jnp.dot(merged.astype(BF16), wo_ref[...], preferred_element_type=F32)
        x1 = x_ref[0, rows, :] + gate1_ref[0] * o
        x1_ref[0, rows, :] = x1
        h2 = _rms(x1) * (n2g_ref[...] * (1.0 + scale2_ref[0])) + shift2_ref[0]
        hb = h2.astype(BF16)
        h2_ref[0, rows, :] = hb
        lg_ref[:, g * SUB_MIX:(g + 1) * SUB_MIX] = lax.dot_general(
            rw_ref[...], hb, (((1,), (1,)), ((), ())), preferred_element_type=F32) + rb_ref[...]


def _route_kernel(lg_ref, pc_ref, pt_ref, lists_ref, totals_ref, cnt_ref, carry):
    @pl.when(pl.program_id(0) == 0)
    def _():
        carry[...] = jnp.zeros_like(carry)

    for t in range(ROUTE_TILES):
        pc, pt, lists, totals = _route_tile(lg_ref[:, t * TT:(t + 1) * TT], carry)
        pc_ref[t * TT:(t + 1) * TT, :] = pc
        pt_ref[:, t * TT:(t + 1) * TT] = pt
        lists_ref[t] = lists.astype(jnp.int32)
        totals_ref[t] = totals.astype(jnp.int32)
    cnt_ref[...] = jnp.broadcast_to(carry[...], cnt_ref.shape)


def _route_call(logits):
    assert NT % ROUTE_TILES == 0
    return pl.pallas_call(
        _route_kernel,
        grid=(NT // ROUTE_TILES,),
        in_specs=[pl.BlockSpec((N_E, ROUTE_TILES * TT), lambda i: (0, i))],
        out_specs=[pl.BlockSpec((ROUTE_TILES * TT, 8), lambda i: (i, 0)),
                   pl.BlockSpec((8, ROUTE_TILES * TT), lambda i: (0, i)),
                   pl.BlockSpec((ROUTE_TILES, 8, LIST_W), lambda i: (i, 0, 0)),
                   pl.BlockSpec((ROUTE_TILES, 8, CLASS_SLOTS), lambda i: (i, 0, 0)),
                   pl.BlockSpec((N_E, 8), lambda i: (0, 0))],
        out_shape=[jax.ShapeDtypeStruct((T, 8), F32),
                   jax.ShapeDtypeStruct((8, T), F32),
                   jax.ShapeDtypeStruct((NT, 8, LIST_W), jnp.int32),
                   jax.ShapeDtypeStruct((NT, 8, CLASS_SLOTS), jnp.int32),
                   jax.ShapeDtypeStruct((N_E, 8), F32)],
        scratch_shapes=[pltpu.VMEM((N_E, 1), F32)],
        compiler_params=pltpu.CompilerParams(dimension_semantics=("arbitrary",)),
        name="route",
    )(logits)


def _mix_call(zg, ya2d, x, mod3, ln_g, ln_b, ws_pairs, bias_full, wbb, wo, n2g, rw, rb):
    tok_spec = pl.BlockSpec((1, TS_MIX, D), lambda b, s: (b, s, 0))
    full = lambda *shape: pl.BlockSpec(shape, lambda b, s: (0,) * len(shape))
    mod_spec = lambda j: pl.BlockSpec((1, 1, D), lambda b, s: (b, 0, j))
    return pl.pallas_call(
        _mix_kernel,
        grid=(B, S // TS_MIX),
        in_specs=[pl.BlockSpec((1, TS_MIX, zg.shape[-1]), lambda b, s: (b, s, 0)), tok_spec, tok_spec,
                  mod_spec(2), mod_spec(3), mod_spec(4),
                  full(1, GM_W), full(1, GM_W),
                  full(GM_HEADS // 2, CHUNK, 2 * CHUNK),
                  full(CHUNK, GM_W),
                  full(GM_W, D), full(D, D), full(1, D),
                  full(N_E, D), full(N_E, 1)],
        out_specs=[tok_spec, tok_spec,
                   pl.BlockSpec((N_E, TS_MIX), lambda b, s: (0, b * (S // TS_MIX) + s))],
        out_shape=[jax.ShapeDtypeStruct((B, S, D), F32),
                   jax.ShapeDtypeStruct((B, S, D), BF16),
                   jax.ShapeDtypeStruct((N_E, T), F32)],
        compiler_params=pltpu.CompilerParams(vmem_limit_bytes=VMEM_LIMIT),
        name="gmlp_merge_norm2",
    )(zg, ya2d, x, mod3, mod3, mod3, ln_g, ln_b, ws_pairs, bias_full, wbb, wo, n2g, rw, rb)


def _pack_rows(v):
    return pltpu.pack_elementwise([v[:, :HALF], v[:, HALF:]], packed_dtype=BF16)


def _unpack_rows(w):
    halves = [pltpu.unpack_elementwise(w, index=i, packed_dtype=BF16, unpacked_dtype=F32)
              for i in range(2)]
    return jnp.concatenate(halves, axis=1)


def _load_grouped(ref, rows, first=0):
    return jnp.concatenate([ref[pl.ds(first * SUB + c, rows, stride=SUB), :] for c in range(SUB)], axis=1)


def _store_grouped(ref, w, rows, first=0):
    for c in range(SUB):
        ref[pl.ds(first * SUB + c, rows, stride=SUB), :] = w[:, c * LANES:(c + 1) * LANES]


def _start_tile_runs(tile, src_ref, dst_ref, src_tbl, dst_tbl, totals_tbl, sem):
    for ci, (rows, lane0) in enumerate(COPY_CLASSES):
        base = tile * LIST_W + lane0

        def start(i, carry, rows=rows, base=base):
            s = pl.multiple_of(src_tbl[base + i] * SUB, SUB)
            d = pl.multiple_of(dst_tbl[base + i] * SUB, SUB)
            pltpu.make_async_copy(src_ref.at[pl.ds(s, rows * SUB)],
                                  dst_ref.at[pl.ds(d, rows * SUB)], sem).start()
            return carry

        lax.fori_loop(0, totals_tbl[tile * CLASS_SLOTS + ci], start, 0)


def _wait_tile_runs(src_ref, dst_ref, sem):
    pltpu.make_async_copy(src_ref, dst_ref, sem).wait()


def _dispatch_kernel(loc_ref, glob_ref, tot_ref, pend_ref, h_ref, pt_ref, xs_ref,
                     sbuf0, sbuf1, zbuf, sem_z, sem0, sem1):
    j = pl.program_id(0)

    def zero_last_blocks(start):
        for e in range(N_E):
            prev = pend_ref[e - 1] if e > 0 else 0
            end = pend_ref[e]

            @pl.when(end > prev)
            def _():
                first = pl.multiple_of((end - TM) * SUB, TM * SUB)
                cp = pltpu.make_async_copy(zbuf, xs_ref.at[pl.ds(first, TM * SUB)], sem_z)
                if start:
                    cp.start()
                else:
                    cp.wait()

    @pl.when(j == 0)
    def _():
        zbuf[...] = jnp.zeros_like(zbuf)
        zero_last_blocks(True)

    rows = lax.broadcasted_iota(jnp.int32, (RS, TT), 0)
    words = []
    for t in range(2):
        pos = pt_ref[:, t * TT:(t + 1) * TT].astype(jnp.int32)
        hit = rows == pos[0:1, :]
        for k in range(1, TOP_K):
            hit = hit | (rows == pos[k:k + 1, :])
        pm = jnp.where(hit, 1.0, 0.0).astype(BF16)
        srt = jnp.dot(pm, h_ref[t * TT:(t + 1) * TT, :], preferred_element_type=F32)
        words.append(_pack_rows(srt))
    head = xs_ref.at[pl.ds(0, RS * SUB)]

    @pl.when(j == 0)
    def _():
        zero_last_blocks(False)

    slots = ((sbuf0, sem0), (sbuf1, sem1))
    for t, (sbuf, sem) in enumerate(slots):
        @pl.when(j >= 1)
        def _(sbuf=sbuf, sem=sem):
            _wait_tile_runs(sbuf, head, sem)
        _store_grouped(sbuf, words[t], RS)
    for t, (sbuf, sem) in enumerate(slots):
        _start_tile_runs(2 * j + t, sbuf, xs_ref, loc_ref, glob_ref, tot_ref, sem)

    @pl.when(j == NT // 2 - 1)
    def _():
        _wait_tile_runs(sbuf0, head, sem0)
        _wait_tile_runs(sbuf1, head, sem1)


def _dispatch_call(loc_t, glob_t, tot_t, pad_end, h2, pos_t):
    assert NT % 2 == 0
    grid_spec = pltpu.PrefetchScalarGridSpec(
        num_scalar_prefetch=4,
        grid=(NT // 2,),
        in_specs=[pl.BlockSpec((2 * TT, D), lambda j, *_: (j, 0)),
                  pl.BlockSpec((8, 2 * TT), lambda j, *_: (0, j))],
        out_specs=pl.BlockSpec(memory_space=pl.ANY),
        scratch_shapes=[pltpu.VMEM((RS * SUB, LANES), U32),
                        pltpu.VMEM((RS * SUB, LANES), U32),
                        pltpu.VMEM((TM * SUB, LANES), U32),
                        pltpu.SemaphoreType.DMA,
                        pltpu.SemaphoreType.DMA,
                        pltpu.SemaphoreType.DMA],
    )
    return pl.pallas_call(
        _dispatch_kernel,
        grid_spec=grid_spec,
        out_shape=jax.ShapeDtypeStruct((N_ROWS * SUB, LANES), U32),
        compiler_params=pltpu.CompilerParams(dimension_semantics=("arbitrary",),
                                             vmem_limit_bytes=VMEM_LIMIT),
        name="dispatch",
    )(loc_t, glob_t, tot_t, pad_end, h2, pos_t)


def _moe_kernel(be_ref, bf_ref, nx_ref, nv_ref, xs_ref, wi_hbm, bi_ref, wo_hbm, bo_ref, ys_ref,
                wi_f32, wo_f32, wi_bf, wo_bf, sem_i, sem_o):
    step = pl.program_id(0)

    def fetch(e):
        return (pltpu.make_async_copy(wi_hbm.at[e], wi_f32, sem_i),
                pltpu.make_async_copy(wo_hbm.at[e], wo_f32, sem_o))

    @pl.when(step == 0)
    def _():
        for cp in fetch(be_ref[0]):
            cp.start()

    def load_weights(i):
        @pl.when(bf_ref[i] == 1)
        def _():
            for cp in fetch(be_ref[i]):
                cp.wait()
            wi_bf[...] = wi_f32[...].astype(BF16)
            wo_bf[...] = wo_f32[...].astype(BF16)

            @pl.when(nx_ref[i] >= 0)
            def _():
                for cp in fetch(nx_ref[i]):
                    cp.start()

    def ffn(i, first, rows):
        e = be_ref[i]
        xb = _unpack_rows(_load_grouped(xs_ref, rows, first)).astype(BF16)
        gu = jnp.dot(xb, wi_bf[...], preferred_element_type=F32) + bi_ref[pl.ds(e, 1), :]
        gate = jnp.minimum(gu[:, :D_E], LIMIT)
        up = jnp.clip(gu[:, D_E:], -LIMIT, LIMIT)
        act = (up + 1.0) * (gate * _sigmoid(ALPHA * gate))
        y = jnp.dot(act.astype(BF16), wo_bf[...], preferred_element_type=F32) + bo_ref[pl.ds(e, 1), :]
        _store_grouped(ys_ref, _pack_rows(y), rows, first)

    i0 = step * BPS
    last = i0 + BPS - 1
    uniform = (last < nv_ref[0]) & (be_ref[i0] == be_ref[last])

    @pl.when(uniform)
    def _():
        load_weights(i0)
        ffn(i0, 0, BPS * TM)

    @pl.when(jnp.logical_not(uniform))
    def _():
        for sub in range(BPS):
            i = i0 + sub

            @pl.when(i < nv_ref[0])
            def _(i=i, sub=sub):
                load_weights(i)
                ffn(i, sub * TM, TM)


def _moe_call(blk_e, blk_first, blk_next, n_valid, xs, w_in, b_in, w_out, b_out):
    assert N_BLOCKS % BPS == 0

    def row_map(s, be, bf, nx, nv):
        last = (nv[0] + BPS - 1) // BPS - 1
        return (jnp.maximum(jnp.minimum(s, last), 0), 0)

    grid_spec = pltpu.PrefetchScalarGridSpec(
        num_scalar_prefetch=4,
        grid=(N_BLOCKS // BPS,),
        in_specs=[pl.BlockSpec((BPS * TM * SUB, LANES), row_map),
                  pl.BlockSpec(memory_space=pl.ANY),
                  pl.BlockSpec((N_E, 2 * D_E), lambda s, *_: (0, 0)),
                  pl.BlockSpec(memory_space=pl.ANY),
                  pl.BlockSpec((N_E, D), lambda s, *_: (0, 0))],
        out_specs=pl.BlockSpec((BPS * TM * SUB, LANES), row_map),
        scratch_shapes=[pltpu.VMEM((D, 2 * D_E), F32),
                        pltpu.VMEM((D_E, D), F32),
                        pltpu.VMEM((D, 2 * D_E), BF16),
                        pltpu.VMEM((D_E, D), BF16),
                        pltpu.SemaphoreType.DMA,
                        pltpu.SemaphoreType.DMA],
    )
    return pl.pallas_call(
        _moe_kernel,
        grid_spec=grid_spec,
        out_shape=jax.ShapeDtypeStruct((N_ROWS * SUB, LANES), U32),
        compiler_params=pltpu.CompilerParams(dimension_semantics=("arbitrary",),
                                             vmem_limit_bytes=VMEM_LIMIT),
        name="moe_experts",
    )(blk_e, blk_first, blk_next, n_valid, xs, w_in, b_in, w_out, b_out)


def _combine_kernel(loc_ref, glob_ref, tot_ref, ys_ref, pc_ref, x1_ref, gate2_ref, fg_ref, o_ref,
                    buf0, buf1, buf2, buf3, sem0, sem1, sem2, sem3):
    j = pl.program_id(0)
    head = ys_ref.at[pl.ds(0, RS * SUB)]
    even = ((buf0, sem0), (buf1, sem1))
    odd = ((buf2, sem2), (buf3, sem3))

    def fetch(step, slots):
        for t, (buf, sem) in enumerate(slots):
            _start_tile_runs(2 * step + t, ys_ref, buf, glob_ref, loc_ref, tot_ref, sem)

    @pl.when(j == 0)
    def _():
        fetch(0, even)

    col = lax.broadcasted_iota(jnp.int32, (TT, RS), 1)
    wms = []
    for t in range(2):
        pc = pc_ref[t * TT:(t + 1) * TT, :]
        pos = pc.astype(jnp.int32)
        wm = jnp.zeros((TT, RS), F32)
        for k in range(TOP_K):
            wm = jnp.where(col == pos[:, k:k + 1], pc[:, TOP_K + k:TOP_K + k + 1], wm)
        wms.append(wm.astype(BF16))

    def step(cur, nxt):
        @pl.when(j + 1 < NT // 2)
        def _():
            fetch(j + 1, nxt)
        for t, (buf, sem) in enumerate(cur):
            _wait_tile_runs(head, buf, sem)
            yt = _unpack_rows(_load_grouped(buf, RS)).astype(BF16)
            acc = jnp.dot(wms[t], yt, preferred_element_type=F32)
            x2 = x1_ref[t * TT:(t + 1) * TT, :] + gate2_ref[0] * acc
            o_ref[t * TT:(t + 1) * TT, :] = _rms(x2) * fg_ref[...]

    @pl.when(j % 2 == 0)
    def _():
        step(even, odd)

    @pl.when(j % 2 == 1)
    def _():
        step(odd, even)


def _combine_call(loc_t, glob_t, tot_t, ys, pos_c, x1, mod3, final_g):
    per_b = S // (2 * TT)
    grid_spec = pltpu.PrefetchScalarGridSpec(
        num_scalar_prefetch=3,
        grid=(NT // 2,),
        in_specs=[pl.BlockSpec(memory_space=pl.ANY),
                  pl.BlockSpec((2 * TT, 8), lambda j, *_: (j, 0)),
                  pl.BlockSpec((2 * TT, D), lambda j, *_: (j, 0)),
                  pl.BlockSpec((1, 1, D), lambda j, *_: (j // per_b, 0, 5)),
                  pl.BlockSpec((1, D), lambda j, *_: (0, 0))],
        out_specs=pl.BlockSpec((2 * TT, D), lambda j, *_: (j, 0)),
        scratch_shapes=[pltpu.VMEM((RS * SUB, LANES), U32)] * 4 + [pltpu.SemaphoreType.DMA] * 4,
    )
    return pl.pallas_call(
        _combine_kernel,
        grid_spec=grid_spec,
        out_shape=jax.ShapeDtypeStruct((T, D), F32),
        compiler_params=pltpu.CompilerParams(dimension_semantics=("arbitrary",),
                                             vmem_limit_bytes=VMEM_LIMIT),
        name="combine_norm",
    )(loc_t, glob_t, tot_t, ys, pos_c, x1, mod3, final_g)


def kernel(x, c, ada_w, ada_b, norm1_g, w_in, ssm_a_re, ssm_a_im, ssm_log_dt, ssm_b_re, ssm_b_im, ssm_c_re, ssm_c_im, ssm_d, ssm_glu_w, ssm_glu_b, w_branch_a, gmlp_ln_g, gmlp_ln_b, gmlp_ws, gmlp_bs, w_branch_b, w_out, norm2_g, router_w, router_b, moe_w_in, moe_b_in, moe_w_out, moe_b_out, final_g):
    depth = ada_w.shape[0]
    assert depth == 1, "the final rms_norm is fused into the combine kernel of the only layer"
    for layer in range(depth):
        mod = _mod_call(c, ada_w[layer], ada_b[layer])
        mod3 = mod.reshape(B, 1, 6 * D)

        u, zg = _proj_call(x, norm1_g[layer], mod3, w_in[layer].astype(BF16))

        bm, cre, cim, are, aim = _s5_params(ssm_a_re[layer], ssm_a_im[layer], ssm_log_dt[layer],
                                            ssm_b_re[layer], ssm_b_im[layer],
                                            ssm_c_re[layer], ssm_c_im[layer])
        ya = _s5_call(u, bm, cre, cim, are, aim,
                      ssm_d[layer].reshape(1, SSM_W), ssm_glu_w[layer].astype(BF16),
                      ssm_glu_b[layer].reshape(1, SSM_W), w_branch_a[layer].astype(BF16))

        ws = gmlp_ws[layer]
        ws_pairs = jnp.concatenate([ws[0::2], ws[1::2]], axis=-1)
        bias_full = jnp.repeat(gmlp_bs[layer].T, GM_HD, axis=1)
        x1, h2, logits = _mix_call(
            zg, ya, x, mod3,
            gmlp_ln_g[layer].reshape(1, GM_W), gmlp_ln_b[layer].reshape(1, GM_W),
            ws_pairs, bias_full, w_branch_b[layer].astype(BF16), w_out[layer].astype(BF16),
            norm2_g[layer].reshape(1, D), router_w[layer].T.astype(BF16),
            router_b[layer].reshape(N_E, 1))

        pos_c, pos_t, lists, totals, cnt = _route_call(logits)
        counts = cnt[:, 0].astype(jnp.int32)
        nblk = (counts + TM - 1) // TM
        blk_end = jnp.cumsum(nblk)
        pad_end = (blk_end * TM).astype(jnp.int32)
        experts = jnp.arange(N_E, dtype=jnp.int32)
        blk_ids = jnp.arange(N_BLOCKS, dtype=jnp.int32)
        blk_e = jnp.sum((blk_end[None, :] <= blk_ids[:, None]).astype(jnp.int32), axis=1)
        blk_e = jnp.minimum(blk_e, N_E - 1)
        blk_first = jnp.concatenate([jnp.ones((1,), jnp.int32),
                                     (blk_e[1:] != blk_e[:-1]).astype(jnp.int32)])
        later = (experts[None, :] > experts[:, None]) & (nblk[None, :] > 0)
        next_e = jnp.min(jnp.where(later, experts[None, :], N_E), axis=1)
        next_e = jnp.where(next_e == N_E, -1, next_e)
        blk_next = jnp.sum(jnp.where(blk_e[:, None] == experts[None, :], next_e[None, :], 0), axis=1)
        n_valid = blk_end[-1:].astype(jnp.int32)
        pad_start = pad_end - nblk * TM
        loc_t = lists[:, 0, :].reshape(NT * LIST_W)
        owner = lists[:, 2, :, None] == experts[None, None, :]
        glob_t = (lists[:, 1, :] + jnp.sum(jnp.where(owner, pad_start, 0), axis=-1)).reshape(NT * LIST_W)
        tot_t = totals[:, 0, :].reshape(NT * CLASS_SLOTS)

        xs = _dispatch_call(loc_t, glob_t, tot_t, pad_end, h2.reshape(T, D), pos_t)
        ys = _moe_call(blk_e, blk_first, blk_next.astype(jnp.int32), n_valid, xs, moe_w_in[layer],
                       moe_b_in[layer], moe_w_out[layer], moe_b_out[layer])
        x = _combine_call(loc_t, glob_t, tot_t, ys, pos_c, x1.reshape(T, D),
                          mod3, final_g.reshape(1, D)).reshape(B, S, D)
    return x
```
